```python
import jax, jax.numpy as jnp
from jax import lax
import numpy as np

D_MODEL = 1024
BATCH = 2
SEQ = 8192
DEPTH = 1

HG_HEADS = 4
HG_KEY = 128
HG_VAL = 128
HG_KW = HG_HEADS * HG_KEY
HG_VW = HG_HEADS * HG_VAL
DN_HEADS = 4
DN_KEY = 128
DN_VAL = 128
DN_KW = DN_HEADS * DN_KEY
DN_VW = DN_HEADS * DN_VAL
CONV_W = 4
CHUNK = 64
D_FF = 4 * D_MODEL
ALPHA = (2 * DEPTH) ** 0.25
BETA_INIT = (8 * DEPTH) ** -0.25
LN_EPS = 1e-5
RMS_EPS = 1e-6
L2_EPS = 1e-6
PROJ_SIZES = (HG_KW, HG_KW, HG_VW, HG_VW, 2 * DN_KW + DN_VW, DN_VW, DN_HEADS, DN_HEADS, D_MODEL, D_MODEL)
PROJ_WIDTH = sum(PROJ_SIZES)

kernel_name = "hybrid_hgrn2_gated_deltanet_deepnorm"


def layer_norm(x, g, b):
    xf = x.astype(jnp.float32)
    mu = jnp.mean(xf, -1, keepdims=True)
    var = jnp.mean(jnp.square(xf - mu), -1, keepdims=True)
    return ((xf - mu) * lax.rsqrt(var + LN_EPS) * g.astype(jnp.float32) + b.astype(jnp.float32)).astype(x.dtype)


def gated_rms_norm(o, gate, w):
    of = o.astype(jnp.float32)
    of = of * lax.rsqrt(jnp.mean(of * of, -1, keepdims=True) + RMS_EPS) * w.astype(jnp.float32)
    return (of * jax.nn.silu(gate.astype(jnp.float32))).astype(gate.dtype)


def l2_normalize(x):
    xf = x.astype(jnp.float32)
    return (xf * lax.rsqrt(jnp.sum(xf * xf, -1, keepdims=True) + L2_EPS)).astype(x.dtype)


def causal_depthwise_conv(x, w):
    return lax.conv_general_dilated(x, w[:, None, :].astype(x.dtype), window_strides=(1,),
                                    padding=[(CONV_W - 1, 0)],
                                    dimension_numbers=('NWC', 'WIO', 'NWC'),
                                    feature_group_count=x.shape[-1])


def to_chunks(x, heads):
    B, T, W = x.shape
    return x.reshape(B, T // CHUNK, CHUNK, heads, W // heads).transpose(0, 3, 1, 2, 4)


def to_chunks_scalar(x):
    B, T, H = x.shape
    return x.reshape(B, T // CHUNK, CHUNK, H).transpose(0, 3, 1, 2)


def from_chunks(o):
    B, H, N, C, d = o.shape
    return o.transpose(0, 2, 3, 1, 4).reshape(B, N * C, H, d)


def hgrn2_chunked(q, k, v, log_f):
    b = jnp.cumsum(log_f, axis=3)
    causal = jnp.tril(jnp.ones((CHUNK, CHUNK), dtype=bool))

    def step(S, inp):
        qc, kc, vc, bc = inp
        o_inter = jnp.einsum('bhtk,bhkv->bhtv', qc * jnp.exp(bc), S)
        diff = bc[:, :, :, None, :] - bc[:, :, None, :, :]
        decay = jnp.exp(jnp.where(causal[:, :, None], diff, -jnp.inf))
        scores = jnp.einsum('bhtsk,bhsk->bhts', qc[:, :, :, None, :] * decay, kc)
        o_intra = jnp.einsum('bhts,bhsv->bhtv', scores, vc)
        b_last = bc[:, :, -1:, :]
        S_new = jnp.exp(b_last[:, :, 0, :])[..., None] * S + jnp.einsum(
            'bhsk,bhsv->bhkv', kc * jnp.exp(b_last - bc), vc)
        return S_new.astype(S.dtype), o_inter + o_intra

    xs = tuple(jnp.moveaxis(a, 2, 0) for a in (q, k, v, b))
    S0 = jnp.zeros(q.shape[:2] + (q.shape[-1], v.shape[-1]), dtype=v.dtype)
    _, o = lax.scan(step, S0, xs)
    return jnp.moveaxis(o, 0, 2)


def gated_delta_chunked(q, k, v, g, beta):
    dv = v.shape[-1]
    gc = jnp.cumsum(g, axis=-1)
    causal = jnp.tril(jnp.ones((CHUNK, CHUNK), dtype=bool))
    strict = jnp.tril(jnp.ones((CHUNK, CHUNK), dtype=bool), k=-1)
    decay = jnp.exp(jnp.where(causal, gc[..., :, None] - gc[..., None, :], -jnp.inf))
    k_beta = k * beta[..., None]
    L = jnp.where(strict, jnp.einsum('bhntk,bhnsk->bhnts', k_beta, k) * decay, 0.0)
    eye = jnp.eye(CHUNK, dtype=jnp.float32)
    rhs = jnp.concatenate([v * beta[..., None], k_beta * jnp.exp(gc)[..., None]], axis=-1)
    sol = lax.linalg.triangular_solve((L + eye).astype(jnp.float32), rhs.astype(jnp.float32),
                                      left_side=True, lower=True, unit_diagonal=True).astype(v.dtype)
    u, w = sol[..., :dv], sol[..., dv:]
    qk = jnp.einsum('bhntk,bhnsk->bhnts', q, k) * decay

    def step(S, inp):
        qc, kc, uc, wc, qkc, gcc = inp
        v_new = uc - jnp.einsum('bhck,bhkv->bhcv', wc, S)
        o = jnp.einsum('bhck,bhkv->bhcv', qc * jnp.exp(gcc)[..., None], S) + \
            jnp.einsum('bhts,bhsv->bhtv', qkc, v_new)
        g_last = gcc[..., -1]
        S_new = S * jnp.exp(g_last)[..., None, None] + jnp.einsum(
            'bhsk,bhsv->bhkv', kc * jnp.exp(g_last[..., None] - gcc)[..., None], v_new)
        return S_new.astype(S.dtype), o

    xs = tuple(jnp.moveaxis(a, 2, 0) for a in (q, k, u, w, qk, gc))
    S0 = jnp.zeros(q.shape[:2] + (q.shape[-1], dv), dtype=v.dtype)
    _, o = lax.scan(step, S0, xs)
    return jnp.moveaxis(o, 0, 2)


def token_mixer(h, lb, w_in, conv_w, hg_norm_w, dn_A_log, dn_dt_bias, dn_norm_w, w_branch_a, w_branch_b, w_o):
    B, T, _ = h.shape
    proj = h @ w_in
    split_idx = np.cumsum(PROJ_SIZES)[:-1].tolist()
    hg_q, hg_f, hg_i, hg_g, dn_qkv, dn_z, dn_b, dn_a, gate_a, gate_b = jnp.split(proj, split_idx, axis=-1)

    f = lb + (1.0 - lb) * jax.nn.sigmoid(hg_f.astype(jnp.float32))
    log_f = jnp.log(f)
    k_hg = (1.0 - f).astype(h.dtype)
    q_hg = jax.nn.silu(hg_q) * (HG_KEY ** -0.5)
    o_a = hgrn2_chunked(to_chunks(q_hg, HG_HEADS), to_chunks(k_hg, HG_HEADS),
                        to_chunks(hg_i, HG_HEADS), to_chunks(log_f, HG_HEADS))
    o_a = from_chunks(o_a)
    o_a = gated_rms_norm(o_a, hg_g.reshape(B, T, HG_HEADS, HG_VAL),
                         hg_norm_w.reshape(HG_HEADS, HG_VAL)).reshape(B, T, HG_VW)

    qkv = jax.nn.silu(causal_depthwise_conv(dn_qkv, conv_w))
    dq, dk, dvv = jnp.split(qkv, [DN_KW, 2 * DN_KW], axis=-1)
    dq = l2_normalize(dq.reshape(B, T, DN_HEADS, DN_KEY)).reshape(B, T, DN_KW) * (DN_KEY ** -0.5)
    dk = l2_normalize(dk.reshape(B, T, DN_HEADS, DN_KEY)).reshape(B, T, DN_KW)
    beta = jax.nn.sigmoid(dn_b)
    g = -jnp.exp(dn_A_log.astype(jnp.float32)) * jax.nn.softplus(dn_a.astype(jnp.float32) + dn_dt_bias.astype(jnp.float32))
    o_b = gated_delta_chunked(to_chunks(dq, DN_HEADS), to_chunks(dk, DN_HEADS), to_chunks(dvv, DN_HEADS),
                              to_chunks_scalar(g), to_chunks_scalar(beta))
    o_b = from_chunks(o_b)
    o_b = gated_rms_norm(o_b, dn_z.reshape(B, T, DN_HEADS, DN_VAL), dn_norm_w).reshape(B, T, DN_VW)

    merged = jax.nn.sigmoid(gate_a) * (o_a @ w_branch_a) + jax.nn.sigmoid(gate_b) * (o_b @ w_branch_b)
    return merged @ w_o


def setup_inputs(seed: int = 0) -> dict:
    key = jax.random.key(seed)
    ks = jax.random.split(key, 20)
    f32 = jnp.float32
    x = jax.random.normal(ks[0], (BATCH, SEQ, D_MODEL), f32)
    w_in = jax.random.normal(ks[1], (DEPTH, D_MODEL, PROJ_WIDTH), f32) * D_MODEL ** -0.5
    conv_w = jax.random.normal(ks[2], (DEPTH, CONV_W, 2 * DN_KW + DN_VW), f32) * CONV_W ** -0.5
    hg_lb_logits = jax.random.normal(ks[3], (DEPTH + 1, HG_KW), f32) * 0.1
    hg_norm_w = 1.0 + 0.02 * jax.random.normal(ks[4], (DEPTH, HG_VW), f32)
    dn_A_log = jnp.log(jax.random.uniform(ks[5], (DEPTH, DN_HEADS), f32, 1.0, 16.0))
    dt = jnp.exp(jax.random.uniform(ks[6], (DEPTH, DN_HEADS), f32, np.log(1e-3), np.log(1e-1)))
    dn_dt_bias = dt + jnp.log(-jnp.expm1(-dt))
    dn_norm_w = 1.0 + 0.02 * jax.random.normal(ks[7], (DEPTH, DN_VAL), f32)
    w_branch_a = jax.random.normal(ks[8], (DEPTH, HG_VW, D_MODEL), f32) * HG_VW ** -0.5
    w_branch_b = jax.random.normal(ks[9], (DEPTH, DN_VW, D_MODEL), f32) * DN_VW ** -0.5
    w_o = jax.random.normal(ks[10], (DEPTH, D_MODEL, D_MODEL), f32) * (2.0 / (2 * D_MODEL)) ** 0.5 * BETA_INIT
    ln1_g = 1.0 + 0.02 * jax.random.normal(ks[11], (DEPTH, D_MODEL), f32)
    ln1_b = 0.02 * jax.random.normal(ks[12], (DEPTH, D_MODEL), f32)
    w_up = jax.random.normal(ks[13], (DEPTH, D_MODEL, D_FF), f32) * (2.0 / (D_MODEL + D_FF)) ** 0.5
    w_down = jax.random.normal(ks[14], (DEPTH, D_FF, D_MODEL), f32) * (2.0 / (D_MODEL + D_FF)) ** 0.5 * BETA_INIT
    ln2_g = 1.0 + 0.02 * jax.random.normal(ks[15], (DEPTH, D_MODEL), f32)
    ln2_b = 0.02 * jax.random.normal(ks[16], (DEPTH, D_MODEL), f32)
    return {"x": x, "hg_lb_logits": hg_lb_logits, "w_in": w_in, "conv_w": conv_w, "hg_norm_w": hg_norm_w,
            "dn_A_log": dn_A_log, "dn_dt_bias": dn_dt_bias, "dn_norm_w": dn_norm_w,
            "w_branch_a": w_branch_a, "w_branch_b": w_branch_b, "w_o": w_o,
            "ln1_g": ln1_g, "ln1_b": ln1_b, "w_up": w_up, "w_down": w_down, "ln2_g": ln2_g, "ln2_b": ln2_b}


def reference(x, hg_lb_logits, w_in, conv_w, hg_norm_w, dn_A_log, dn_dt_bias, dn_norm_w,
              w_branch_a, w_branch_b, w_o, ln1_g, ln1_b, w_up, w_down, ln2_g, ln2_b):
    lower_bounds = jnp.cumsum(jax.nn.softmax(hg_lb_logits.astype(jnp.float32), axis=0), axis=0)
    h = x
    for l in range(DEPTH):
        mix = token_mixer(h, lower_bounds[l], w_in[l], conv_w[l], hg_norm_w[l], dn_A_log[l], dn_dt_bias[l],
                          dn_norm_w[l], w_branch_a[l], w_branch_b[l], w_o[l])
        h = layer_norm(ALPHA * h + mix, ln1_g[l], ln1_b[l])
        mlp = jnp.square(jax.nn.relu(h @ w_up[l])) @ w_down[l]
        h = layer_norm(ALPHA * h + mlp, ln2_g[l], ln2_b[l])
    return h
```

```python
import functools

import numpy as np
import jax
import jax.numpy as jnp
from jax import lax
from jax.experimental import pallas as pl
from jax.experimental.pallas import tpu as pltpu

D_MODEL = 1024
HEADS = 4
HEAD_DIM = 128
KW = HEADS * HEAD_DIM
CHUNK = 64
CONV_W = 4
D_FF = 4 * D_MODEL
DEPTH = 1
ALPHA = (2 * DEPTH) ** 0.25
LN_EPS = 1e-5
RMS_EPS = 1e-6
L2_EPS = 1e-6
QK_SCALE = HEAD_DIM ** -0.5

TOKEN_BLOCK = 256
MLP_BLOCK = 512
FF_BLOCK = 1024
CONV_PAD = 8
VMEM_LIMIT_BYTES = 56 * 1024 * 1024

_SIZES = (KW, KW, KW, KW, 3 * KW, KW, HEADS, HEADS, D_MODEL, D_MODEL)
_OFFS = np.concatenate([[0], np.cumsum(_SIZES)]).tolist()

_LEVELS = (32, 16, 8, 4, 2, 1)


def _build_constants():
    t = np.arange(CHUNK)[:, None]
    r = np.arange(CHUNK)[None, :]
    rows = [r <= t, r > t]
    masks = []
    for m in _LEVELS:
        last_key = (t // (2 * m)) * (2 * m) + m - 1
        is_query = (t % (2 * m)) >= m
        rows.append((is_query & (r > last_key) & (r <= t)) | (~is_query & (r > t) & (r <= last_key)))
        masks.append(((t // (2 * m)) == (r // (2 * m))) & is_query & ((r % (2 * m)) < m))
    masks.append(t == r)
    wexp = np.concatenate(rows, 0).astype(np.float32)
    masks = np.concatenate(masks, 0).astype(np.float32)
    return wexp, masks


_WEXP, _MASKS = _build_constants()


def _dot(a, b):
    return jnp.dot(a, b, preferred_element_type=jnp.float32)


def _dot_nt(a, b):
    return lax.dot_general(a, b, (((1,), (1,)), ((), ())), preferred_element_type=jnp.float32)


def _dot_tn(a, b):
    return lax.dot_general(a, b, (((0,), (0,)), ((), ())), preferred_element_type=jnp.float32)


def _bf(x):
    return x.astype(jnp.bfloat16)


def _split(x):
    hi = _bf(x)
    return hi, _bf(x - hi.astype(jnp.float32))


def _sigmoid(x):
    return 1.0 / (1.0 + jnp.exp(-x))


def _silu(x):
    return x * _sigmoid(x)


def _softplus(x):
    return jnp.maximum(x, 0.0) + jnp.log1p(jnp.exp(-jnp.abs(x)))


def _layer_norm(x, g, b):
    mu = jnp.mean(x, axis=-1, keepdims=True)
    xc = x - mu
    var = jnp.mean(xc * xc, axis=-1, keepdims=True)
    return xc * lax.rsqrt(var + LN_EPS) * g + b


def _mixer_kernel(x_ref, lbl_ref, whg_ref, wdn_ref, wab_ref, wabt_ref, wgate_ref, convw_ref, hgnw_ref,
                  alogr_ref, dtbr_ref, alogc_ref, dtbc_ref, dnnw_ref, wa_ref, wb_ref, wo_ref,
                  g1_ref, b1_ref, wexp_ref, masks_ref, bdtriu_ref, out_ref,
                  hq, hk, hv, hlf, oa, dq, dk, dv, ob, gbuf, gcrow, convbuf, shg, sdn):
    tb = x_ref.shape[0]
    n_chunks = tb // CHUNK

    @pl.when(pl.program_id(1) == 0)
    def _reset_carries():
        shg[...] = jnp.zeros_like(shg)
        sdn[...] = jnp.zeros_like(sdn)
        convbuf[0:CONV_PAD, :] = jnp.zeros((CONV_PAD, 3 * KW), jnp.float32)

    x = x_ref[...]
    xb = _bf(x)

    lg = lbl_ref[...]
    e = jnp.exp(lg - jnp.max(lg, axis=0, keepdims=True))
    lb = e[0:1, :] / jnp.sum(e, axis=0, keepdims=True)
    hq[...] = _silu(_dot(xb, whg_ref[:, 0:KW])) * QK_SCALE
    f = lb + (1.0 - lb) * _sigmoid(_dot(xb, whg_ref[:, KW:2 * KW]))
    hlf[...] = jnp.log(f)
    hk[...] = 1.0 - f
    hv[...] = _dot(xb, whg_ref[:, 2 * KW:3 * KW])

    convbuf[CONV_PAD:CONV_PAD + tb, :] = _dot(xb, wdn_ref[:, 0:3 * KW])
    cw = convw_ref[...]
    conv = convbuf[CONV_PAD - 3:CONV_PAD - 3 + tb, :] * cw[0:1, :]
    for j in range(1, CONV_W):
        conv = conv + convbuf[CONV_PAD - 3 + j:CONV_PAD - 3 + j + tb, :] * cw[j:j + 1, :]
    convbuf[0:CONV_PAD, :] = convbuf[tb:tb + CONV_PAD, :]
    qkv = _silu(conv)
    for h in range(HEADS):
        hs = slice(h * HEAD_DIM, (h + 1) * HEAD_DIM)
        qh = qkv[:, h * HEAD_DIM:(h + 1) * HEAD_DIM]
        kh = qkv[:, KW + h * HEAD_DIM:KW + (h + 1) * HEAD_DIM]
        dq[:, hs] = qh * (lax.rsqrt(jnp.sum(qh * qh, axis=-1, keepdims=True) + L2_EPS) * QK_SCALE)
        dk[:, hs] = kh * lax.rsqrt(jnp.sum(kh * kh, axis=-1, keepdims=True) + L2_EPS)
    dv[...] = qkv[:, 2 * KW:3 * KW]

    pab = _dot(xb, wab_ref[...])
    g_col = -jnp.exp(alogr_ref[...]) * _softplus(pab + dtbr_ref[...])
    lane = lax.broadcasted_iota(jnp.int32, pab.shape, 1)
    gbuf[...] = jnp.where(lane < HEADS, _sigmoid(pab), g_col)
    pabt = _dot_nt(wabt_ref[...], xb)
    g_row = -jnp.exp(alogc_ref[...]) * _softplus(pabt + dtbc_ref[...])
    gr_hi, gr_lo = _split(g_row)
    gc_row = _dot(gr_hi, bdtriu_ref[...]) + _dot(gr_lo, bdtriu_ref[...])
    for c in range(n_chunks):
        gcrow[c] = gc_row[:, c * CHUNK:(c + 1) * CHUNK]

    ri = lax.broadcasted_iota(jnp.int32, (CHUNK, CHUNK), 0)
    ci = lax.broadcasted_iota(jnp.int32, (CHUNK, CHUNK), 1)
    causal = ri >= ci
    strict = ri > ci

    def chunk_body(c, carry):
        r0 = pl.multiple_of(c * CHUNK, CHUNK)
        rows = pl.ds(r0, CHUNK)
        wexp = wexp_ref[...]
        level_masks = [masks_ref[l * CHUNK:(l + 1) * CHUNK, :] for l in range(len(_LEVELS) + 1)]

        for h in range(HEADS):
            hs = slice(h * HEAD_DIM, (h + 1) * HEAD_DIM)
            q = hq[rows, hs]
            k = hk[rows, hs]
            vb = _bf(hv[rows, hs])
            lf_hi, lf_lo = _split(hlf[rows, hs])
            ex = _dot(wexp, lf_hi) + _dot(wexp, lf_lo)
            b = ex[0:CHUNK]
            st = shg[h]
            o = _dot_nt(_bf(q * jnp.exp(b)), _bf(st))
            sc = level_masks[-1] * _dot_nt(_bf(q), _bf(k))
            for l in range(len(_LEVELS)):
                z = jnp.exp(ex[(2 + l) * CHUNK:(3 + l) * CHUNK])
                sc = sc + level_masks[l] * _dot_nt(_bf(q * z), _bf(k * z))
            o = o + _dot(_bf(sc), vb)
            oa[rows, hs] = o
            kd = k * jnp.exp(ex[CHUNK:2 * CHUNK])
            shg[h] = st * jnp.exp(b[CHUNK - 1:CHUNK, :]) + _dot_tn(vb, _bf(kd))

        gb = gbuf[rows, :]
        g_hi, g_lo = _split(gb)
        eg = _dot(wexp[0:2 * CHUNK], g_hi) + _dot(wexp[0:2 * CHUNK], g_lo)
        gcr = gcrow[c]
        for h in range(HEADS):
            hs = slice(h * HEAD_DIM, (h + 1) * HEAD_DIM)
            q = dq[rows, hs]
            k = dk[rows, hs]
            v = dv[rows, hs]
            beta = gb[:, h:h + 1]
            gcol = eg[0:CHUNK, HEADS + h:HEADS + h + 1]
            gsfx = eg[CHUNK:2 * CHUNK, HEADS + h:HEADS + h + 1]
            grow = gcr[HEADS + h:HEADS + h + 1, :]
            decay = jnp.where(causal, jnp.exp(jnp.minimum(gcol - grow, 0.0)), 0.0)
            kb = k * beta
            kbf = _bf(k)
            lmat = jnp.where(strict, _dot_nt(_bf(kb), kbf) * decay, 0.0)
            qk = _dot_nt(_bf(q), kbf) * decay
            n = -(level_masks[len(_LEVELS) - 1] * lmat)
            for l in range(len(_LEVELS) - 2, -1, -1):
                cm = level_masks[l] * lmat
                nb = _bf(n)
                xm = cm + _dot(nb, _bf(cm))
                n = n - (xm + _dot(_bf(xm), nb))
            eg_col = jnp.exp(gcol)
            rhs = jnp.concatenate([v * beta, kb * eg_col], axis=1)
            uw = rhs + _dot(_bf(n), _bf(rhs))
            u = uw[:, 0:HEAD_DIM]
            w = uw[:, HEAD_DIM:2 * HEAD_DIM]
            s = sdn[h]
            ws = _dot(_bf(jnp.concatenate([w, q * eg_col], axis=0)), _bf(s))
            v_new = u - ws[0:CHUNK]
            vnb = _bf(v_new)
            ob[rows, hs] = ws[CHUNK:2 * CHUNK] + _dot(_bf(qk), vnb)
            g_last = gcol[CHUNK - 1:CHUNK, :]
            sdn[h] = s * jnp.exp(g_last) + _dot_tn(_bf(k * jnp.exp(gsfx)), vnb)
        return carry

    lax.fori_loop(0, n_chunks, chunk_body, 0)

    def gated_norm(o_ref, w_row, gate):
        parts = []
        for h in range(HEADS):
            oh = o_ref[:, h * HEAD_DIM:(h + 1) * HEAD_DIM]
            parts.append(oh * lax.rsqrt(jnp.mean(oh * oh, axis=-1, keepdims=True) + RMS_EPS))
        return jnp.concatenate(parts, axis=1) * w_row * _silu(gate)

    na = gated_norm(oa, hgnw_ref[...], _dot(xb, whg_ref[:, 3 * KW:4 * KW]))
    nb_ = gated_norm(ob, dnnw_ref[...], _dot(xb, wdn_ref[:, 3 * KW:4 * KW]))
    merged = (_sigmoid(_dot(xb, wgate_ref[:, 0:D_MODEL])) * _dot(_bf(na), wa_ref[...])
              + _sigmoid(_dot(xb, wgate_ref[:, D_MODEL:2 * D_MODEL])) * _dot(_bf(nb_), wb_ref[...]))
    mix = _dot(_bf(merged), wo_ref[...])
    out_ref[...] = _layer_norm(ALPHA * x + mix, g1_ref[...], b1_ref[...])


def _mlp_kernel(h_ref, wup_ref, wdown_ref, g2_ref, b2_ref, out_ref):
    h = h_ref[...]
    hb = _bf(h)
    acc = jnp.zeros(h.shape, jnp.float32)
    for j in range(D_FF // FF_BLOCK):
        up = _dot(hb, wup_ref[:, j * FF_BLOCK:(j + 1) * FF_BLOCK])
        act = jnp.square(jnp.maximum(up, 0.0))
        acc = acc + _dot(_bf(act), wdown_ref[j * FF_BLOCK:(j + 1) * FF_BLOCK, :])
    out_ref[...] = _layer_norm(ALPHA * h + acc, g2_ref[...], b2_ref[...])


def _resident(shape):
    nd = len(shape)
    return pl.BlockSpec(shape, lambda *_: (0,) * nd, pipeline_mode=pl.Buffered(1))


def _mixer_call(x, lbl, whg, wdn, wab, wabt, wgate, convw, hgnw, alogr, dtbr, alogc, dtbc, dnnw,
                wa, wb, wo, g1, b1):
    bsz, seq, _ = x.shape
    tb = min(TOKEN_BLOCK, seq)
    assert seq % tb == 0 and tb % CHUNK == 0
    n_chunks = tb // CHUNK
    wexp = jnp.asarray(_WEXP, jnp.bfloat16)
    masks = jnp.asarray(_MASKS, jnp.float32)
    tt = np.arange(tb)
    bdtriu = jnp.asarray(((tt[:, None] <= tt[None, :]) & ((tt[:, None] // CHUNK) == (tt[None, :] // CHUNK))),
                         jnp.bfloat16)
    consts = (lbl, whg, wdn, wab, wabt, wgate, convw, hgnw, alogr, dtbr, alogc, dtbc, dnnw, wa, wb, wo,
              g1, b1, wexp, masks, bdtriu)
    f32 = jnp.float32
    blk = functools.partial(pltpu.VMEM, (tb, KW))
    scratch = [blk(f32) for _ in range(9)]
    scratch += [pltpu.VMEM((tb, 128), f32),
                pltpu.VMEM((n_chunks, 8, CHUNK), f32),
                pltpu.VMEM((tb + CONV_PAD, 3 * KW), f32),
                pltpu.VMEM((HEADS, HEAD_DIM, HEAD_DIM), f32),
                pltpu.VMEM((HEADS, HEAD_DIM, HEAD_DIM), f32)]
    return pl.pallas_call(
        _mixer_kernel,
        grid=(bsz, seq // tb),
        in_specs=[pl.BlockSpec((None, tb, D_MODEL), lambda b, i: (b, i, 0))]
                 + [_resident(c.shape) for c in consts],
        out_specs=pl.BlockSpec((None, tb, D_MODEL), lambda b, i: (b, i, 0)),
        out_shape=jax.ShapeDtypeStruct(x.shape, f32),
        scratch_shapes=scratch,
        compiler_params=pltpu.CompilerParams(dimension_semantics=("arbitrary", "arbitrary"),
                                             vmem_limit_bytes=VMEM_LIMIT_BYTES),
        name="token_mixer",
    )(x, *consts)


def _mlp_call(h, wup, wdown, g2, b2):
    m = h.shape[0]
    tm = min(MLP_BLOCK, m)
    assert m % tm == 0
    consts = (wup, wdown, g2, b2)
    return pl.pallas_call(
        _mlp_kernel,
        grid=(m // tm,),
        in_specs=[pl.BlockSpec((tm, D_MODEL), lambda i: (i, 0))] + [_resident(c.shape) for c in consts],
        out_specs=pl.BlockSpec((tm, D_MODEL), lambda i: (i, 0)),
        out_shape=jax.ShapeDtypeStruct(h.shape, jnp.float32),
        compiler_params=pltpu.CompilerParams(dimension_semantics=("arbitrary",),
                                             vmem_limit_bytes=VMEM_LIMIT_BYTES),
        name="relu2_mlp",
    )(h, *consts)


def kernel(x, hg_lb_logits, w_in, conv_w, hg_norm_w, dn_A_log, dn_dt_bias, dn_norm_w, w_branch_a, w_branch_b, w_o, ln1_g, ln1_b, w_up, w_down, ln2_g, ln2_b):
    bsz, seq, _ = x.shape
    f32 = jnp.float32
    bf16 = jnp.bfloat16
    o = _OFFS
    l = 0
    w = w_in[l]
    whg = w[:, o[0]:o[4]].astype(bf16)
    wdn = w[:, o[4]:o[6]].astype(bf16)
    wab_f = w[:, o[6]:o[8]]
    wab = jnp.pad(wab_f, ((0, 0), (0, 128 - 2 * HEADS))).astype(bf16)
    wabt = wab_f.T.astype(bf16)
    wgate = w[:, o[8]:o[10]].astype(bf16)
    zeros4 = jnp.zeros((HEADS,), f32)
    alog8 = jnp.concatenate([zeros4, dn_A_log[l].astype(f32)])
    dtb8 = jnp.concatenate([zeros4, dn_dt_bias[l].astype(f32)])
    alogr = jnp.pad(alog8, (0, 128 - 2 * HEADS)).reshape(1, 128)
    dtbr = jnp.pad(dtb8, (0, 128 - 2 * HEADS)).reshape(1, 128)
    h1 = _mixer_call(
        x, hg_lb_logits.astype(f32), whg, wdn, wab, wabt, wgate, conv_w[l].astype(f32),
        hg_norm_w[l].reshape(1, KW), alogr, dtbr, alog8.reshape(8, 1), dtb8.reshape(8, 1),
        jnp.tile(dn_norm_w[l], HEADS).reshape(1, KW),
        w_branch_a[l].astype(bf16), w_branch_b[l].astype(bf16), w_o[l].astype(bf16),
        ln1_g[l].reshape(1, D_MODEL), ln1_b[l].reshape(1, D_MODEL))
    out = _mlp_call(h1.reshape(bsz * seq, D_MODEL), w_up[l].astype(bf16), w_down[l].astype(bf16),
                    ln2_g[l].reshape(1, D_MODEL), ln2_b[l].reshape(1, D_MODEL))
    return out.reshape(bsz, seq, D_MODEL)
```

```python
import functools

import numpy as np
import jax
import jax.numpy as jnp
from jax import lax
from jax.experimental import pallas as pl
from jax.experimental.pallas import tpu as pltpu

D_MODEL = 1024
HEADS = 4
HEAD_DIM = 128
KW = HEADS * HEAD_DIM
CHUNK = 64
CONV_W = 4
D_FF = 4 * D_MODEL
DEPTH = 1
ALPHA = (2 * DEPTH) ** 0.25
LN_EPS = 1e-5
RMS_EPS = 1e-6
L2_EPS = 1e-6
QK_SCALE = HEAD_DIM ** -0.5

TOKEN_BLOCK = 256
MLP_BLOCK = 512
FF_BLOCK = 1024
CONV_PAD = 8
VMEM_LIMIT_BYTES = 56 * 1024 * 1024

_SIZES = (KW, KW, KW, KW, 3 * KW, KW, HEADS, HEADS, D_MODEL, D_MODEL)
_OFFS = np.concatenate([[0], np.cumsum(_SIZES)]).tolist()

_LEVELS = (32, 16, 8, 4, 2, 1)


def _build_constants():
    t = np.arange(CHUNK)[:, None]
    r = np.arange(CHUNK)[None, :]
    rows = [r <= t, r > t]
    masks = []
    for m in _LEVELS:
        last_key = (t // (2 * m)) * (2 * m) + m - 1
        is_query = (t % (2 * m)) >= m
        rows.append((is_query & (r > last_key) & (r <= t)) | (~is_query & (r > t) & (r <= last_key)))
        masks.append(((t // (2 * m)) == (r // (2 * m))) & is_query & ((r % (2 * m)) < m))
    masks.append(t == r)
    wexp = np.concatenate(rows, 0).astype(np.float32)
    masks = np.concatenate(masks, 0).astype(np.float32)
    return wexp, masks


_WEXP, _MASKS = _build_constants()


def _dot(a, b):
    return jnp.dot(a, b, preferred_element_type=jnp.float32)


def _dot_nt(a, b):
    return lax.dot_general(a, b, (((1,), (1,)), ((), ())), preferred_element_type=jnp.float32)


def _dot_tn(a, b):
    return lax.dot_general(a, b, (((0,), (0,)), ((), ())), preferred_element_type=jnp.float32)


def _bf(x):
    return x.astype(jnp.bfloat16)


def _split(x):
    hi = _bf(x)
    return hi, _bf(x - hi.astype(jnp.float32))


def _sigmoid(x):
    return 1.0 / (1.0 + jnp.exp(-x))


def _silu(x):
    return x * _sigmoid(x)


def _softplus(x):
    return jnp.maximum(x, 0.0) + jnp.log1p(jnp.exp(-jnp.abs(x)))


def _layer_norm(x, g, b):
    mu = jnp.mean(x, axis=-1, keepdims=True)
    xc = x - mu
    var = jnp.mean(xc * xc, axis=-1, keepdims=True)
    return xc * lax.rsqrt(var + LN_EPS) * g + b


def _mixer_kernel(x_ref, lbl_ref, whg_ref, wdn_ref, wab_ref, wabt_ref, wgate_ref, convw_ref, hgnw_ref,
                  alogr_ref, dtbr_ref, alogc_ref, dtbc_ref, dnnw_ref, wa_ref, wb_ref, wo_ref,
                  g1_ref, b1_ref, wexp_ref, masks_ref, bdtriu_ref, out_ref,
                  hq, hk, hv, hlf, oa, dq, dk, dv, ob, du, gbuf, gcrow, convbuf, exbuf, qin, kdec, wq, qkb,
                  kvinc, hdec, gl, shg, sdn):
    tb = x_ref.shape[0]
    n_chunks = tb // CHUNK

    @pl.when(pl.program_id(1) == 0)
    def _reset_carries():
        shg[...] = jnp.zeros_like(shg)
        sdn[...] = jnp.zeros_like(sdn)
        convbuf[0:CONV_PAD, :] = jnp.zeros((CONV_PAD, 3 * KW), jnp.float32)

    x = x_ref[...]
    xb = _bf(x)

    lg = lbl_ref[...]
    e = jnp.exp(lg - jnp.max(lg, axis=0, keepdims=True))
    lb = e[0:1, :] / jnp.sum(e, axis=0, keepdims=True)
    hq[...] = _silu(_dot(xb, whg_ref[:, 0:KW])) * QK_SCALE
    f = lb + (1.0 - lb) * _sigmoid(_dot(xb, whg_ref[:, KW:2 * KW]))
    hlf[...] = jnp.log(f)
    hk[...] = 1.0 - f
    hv[...] = _dot(xb, whg_ref[:, 2 * KW:3 * KW])

    convbuf[CONV_PAD:CONV_PAD + tb, :] = _dot(xb, wdn_ref[:, 0:3 * KW])
    cw = convw_ref[...]
    conv = convbuf[CONV_PAD - 3:CONV_PAD - 3 + tb, :] * cw[0:1, :]
    for j in range(1, CONV_W):
        conv = conv + convbuf[CONV_PAD - 3 + j:CONV_PAD - 3 + j + tb, :] * cw[j:j + 1, :]
    convbuf[0:CONV_PAD, :] = convbuf[tb:tb + CONV_PAD, :]
    qkv = _silu(conv)
    for h in range(HEADS):
        hs = slice(h * HEAD_DIM, (h + 1) * HEAD_DIM)
        qh = qkv[:, h * HEAD_DIM:(h + 1) * HEAD_DIM]
        kh = qkv[:, KW + h * HEAD_DIM:KW + (h + 1) * HEAD_DIM]
        dq[:, hs] = qh * (lax.rsqrt(jnp.sum(qh * qh, axis=-1, keepdims=True) + L2_EPS) * QK_SCALE)
        dk[:, hs] = kh * lax.rsqrt(jnp.sum(kh * kh, axis=-1, keepdims=True) + L2_EPS)
    dv[...] = qkv[:, 2 * KW:3 * KW]

    pab = _dot(xb, wab_ref[...])
    g_col = -jnp.exp(alogr_ref[...]) * _softplus(pab + dtbr_ref[...])
    lane = lax.broadcasted_iota(jnp.int32, pab.shape, 1)
    gbuf[...] = jnp.where(lane < HEADS, _sigmoid(pab), g_col)
    pabt = _dot_nt(wabt_ref[...], xb)
    g_row = -jnp.exp(alogc_ref[...]) * _softplus(pabt + dtbc_ref[...])
    gr_hi, gr_lo = _split(g_row)
    gc_row = _dot(gr_hi, bdtriu_ref[...]) + _dot(gr_lo, bdtriu_ref[...])
    for c in range(n_chunks):
        gcrow[c] = gc_row[:, c * CHUNK:(c + 1) * CHUNK]

    ri = lax.broadcasted_iota(jnp.int32, (CHUNK, CHUNK), 0)
    ci = lax.broadcasted_iota(jnp.int32, (CHUNK, CHUNK), 1)
    causal = ri >= ci
    strict = ri > ci

    n_lev = len(_LEVELS)
    heads = range(HEADS)
    hsl = [slice(h * HEAD_DIM, (h + 1) * HEAD_DIM) for h in heads]

    def level_mask(l):
        return masks_ref[l * CHUNK:(l + 1) * CHUNK, :]

    def prepare_chunk(c, carry):
        r0 = pl.multiple_of(c * CHUNK, CHUNK)
        rows = pl.ds(r0, CHUNK)
        wexp = wexp_ref[...]
        lf_hi, lf_lo = _split(hlf[rows, :])
        exbuf[...] = _dot(wexp, lf_hi) + _dot(wexp, lf_lo)
        gb = gbuf[rows, :]
        g_hi, g_lo = _split(gb)
        eg = _dot(wexp[0:2 * CHUNK], g_hi) + _dot(wexp[0:2 * CHUNK], g_lo)
        gcr = gcrow[c]

        dkk = [dk[rows, hsl[h]] for h in heads]
        dkb = [_bf(k) for k in dkk]
        beta = [gb[:, h:h + 1] for h in heads]
        gcol = [eg[0:CHUNK, HEADS + h:HEADS + h + 1] for h in heads]
        decay = [jnp.where(causal, jnp.exp(jnp.minimum(gcol[h] - gcr[HEADS + h:HEADS + h + 1, :], 0.0)), 0.0)
                 for h in heads]
        kbeta = [dkk[h] * beta[h] for h in heads]
        lmat = [jnp.where(strict, _dot_nt(_bf(kbeta[h]), dkb[h]) * decay[h], 0.0) for h in heads]
        for h in heads:
            qkb[c * HEADS + h] = _bf(_dot_nt(_bf(dq[rows, hsl[h]]), dkb[h]) * decay[h])

        hqv = [hq[rows, hsl[h]] for h in heads]
        hkv = [hk[rows, hsl[h]] for h in heads]
        sc = [level_mask(n_lev) * _dot_nt(_bf(hqv[h]), _bf(hkv[h])) for h in heads]
        nmat = [-(level_mask(n_lev - 1) * lmat[h]) for h in heads]

        def score_level(l):
            for h in heads:
                z = jnp.exp(exbuf[(2 + l) * CHUNK:(3 + l) * CHUNK, hsl[h]])
                sc[h] = sc[h] + level_mask(l) * _dot_nt(_bf(hqv[h] * z), _bf(hkv[h] * z))

        score_level(n_lev - 1)
        for l in range(n_lev - 2, -1, -1):
            cm = [level_mask(l) * lmat[h] for h in heads]
            nb = [_bf(nmat[h]) for h in heads]
            xm = [cm[h] + _dot(nb[h], _bf(cm[h])) for h in heads]
            score_level(l)
            for h in heads:
                nmat[h] = nmat[h] - (xm[h] + _dot(_bf(xm[h]), nb[h]))

        hvb = [_bf(hv[rows, hsl[h]]) for h in heads]
        for h in heads:
            oa[rows, hsl[h]] = _dot(_bf(sc[h]), hvb[h])
        for h in heads:
            kd = hkv[h] * jnp.exp(exbuf[CHUNK:2 * CHUNK, hsl[h]])
            kvinc[c * HEADS + h] = _dot_tn(hvb[h], _bf(kd))
        qin[rows, :] = _bf(hq[rows, :] * jnp.exp(exbuf[0:CHUNK, :]))
        hdec[c] = jnp.exp(exbuf[CHUNK - 1:CHUNK, :])

        for h in heads:
            eg_col = jnp.exp(gcol[h])
            rhs = jnp.concatenate([dv[rows, hsl[h]] * beta[h], kbeta[h] * eg_col], axis=1)
            uw = rhs + _dot(_bf(nmat[h]), _bf(rhs))
            du[rows, hsl[h]] = uw[:, 0:HEAD_DIM]
            wq[c, 0:CHUNK, hsl[h]] = _bf(uw[:, HEAD_DIM:2 * HEAD_DIM])
            wq[c, CHUNK:2 * CHUNK, hsl[h]] = _bf(dq[rows, hsl[h]] * eg_col)
            kdec[rows, hsl[h]] = _bf(dkk[h] * jnp.exp(eg[CHUNK:2 * CHUNK, HEADS + h:HEADS + h + 1]))
        gl[c] = jnp.exp(eg[CHUNK - 1:CHUNK, :])
        return carry

    lax.fori_loop(0, n_chunks, prepare_chunk, 0)

    def recur_chunk(c, carry):
        r0 = pl.multiple_of(c * CHUNK, CHUNK)
        rows = pl.ds(r0, CHUNK)
        s = [sdn[h] for h in heads]
        wqc = wq[c]
        ws = [_dot(wqc[:, hsl[h]], _bf(s[h])) for h in heads]
        st = [shg[h] for h in heads]
        dec = hdec[c]
        for h in heads:
            oa[rows, hsl[h]] = oa[rows, hsl[h]] + _dot_nt(qin[rows, hsl[h]], _bf(st[h]))
            shg[h] = st[h] * dec[:, hsl[h]] + kvinc[c * HEADS + h]
        vnb = [_bf(du[rows, hsl[h]] - ws[h][0:CHUNK]) for h in heads]
        glc = gl[c]
        for h in heads:
            ob[rows, hsl[h]] = ws[h][CHUNK:2 * CHUNK] + _dot(qkb[c * HEADS + h], vnb[h])
        for h in heads:
            sdn[h] = s[h] * glc[:, HEADS + h:HEADS + h + 1] + _dot_tn(kdec[rows, hsl[h]], vnb[h])
        return carry

    lax.fori_loop(0, n_chunks, recur_chunk, 0)

    def gated_norm(o_ref, w_row, gate):
        parts = []
        for h in range(HEADS):
            oh = o_ref[:, h * HEAD_DIM:(h + 1) * HEAD_DIM]
            parts.append(oh * lax.rsqrt(jnp.mean(oh * oh, axis=-1, keepdims=True) + RMS_EPS))
        return jnp.concatenate(parts, axis=1) * w_row * _silu(gate)

    na = gated_norm(oa, hgnw_ref[...], _dot(xb, whg_ref[:, 3 * KW:4 * KW]))
    nb_ = gated_norm(ob, dnnw_ref[...], _dot(xb, wdn_ref[:, 3 * KW:4 * KW]))
    merged = (_sigmoid(_dot(xb, wgate_ref[:, 0:D_MODEL])) * _dot(_bf(na), wa_ref[...])
              + _sigmoid(_dot(xb, wgate_ref[:, D_MODEL:2 * D_MODEL])) * _dot(_bf(nb_), wb_ref[...]))
    mix = _dot(_bf(merged), wo_ref[...])
    out_ref[...] = _layer_norm(ALPHA * x + mix, g1_ref[...], b1_ref[...])


def _mlp_kernel(h_ref, wup_ref, wdown_ref, g2_ref, b2_ref, out_ref):
    h = h_ref[...]
    hb = _bf(h)
    acc = jnp.zeros(h.shape, jnp.float32)
    for j in range(D_FF // FF_BLOCK):
        up = _dot(hb, wup_ref[:, j * FF_BLOCK:(j + 1) * FF_BLOCK])
        act = jnp.square(jnp.maximum(up, 0.0))
        acc = acc + _dot(_bf(act), wdown_ref[j * FF_BLOCK:(j + 1) * FF_BLOCK, :])
    out_ref[...] = _layer_norm(ALPHA * h + acc, g2_ref[...], b2_ref[...])


def _resident(shape):
    nd = len(shape)
    return pl.BlockSpec(shape, lambda *_: (0,) * nd, pipeline_mode=pl.Buffered(1))


def _mixer_call(x, lbl, whg, wdn, wab, wabt, wgate, convw, hgnw, alogr, dtbr, alogc, dtbc, dnnw,
                wa, wb, wo, g1, b1):
    bsz, seq, _ = x.shape
    tb = min(TOKEN_BLOCK, seq)
    assert seq % tb == 0 and tb % CHUNK == 0
    n_chunks = tb // CHUNK
    wexp = jnp.asarray(_WEXP, jnp.bfloat16)
    masks = jnp.asarray(_MASKS, jnp.float32)
    tt = np.arange(tb)
    bdtriu = jnp.asarray(((tt[:, None] <= tt[None, :]) & ((tt[:, None] // CHUNK) == (tt[None, :] // CHUNK))),
                         jnp.bfloat16)
    consts = (lbl, whg, wdn, wab, wabt, wgate, convw, hgnw, alogr, dtbr, alogc, dtbc, dnnw, wa, wb, wo,
              g1, b1, wexp, masks, bdtriu)
    f32 = jnp.float32
    blk = functools.partial(pltpu.VMEM, (tb, KW))
    bf16 = jnp.bfloat16
    scratch = [blk(f32) for _ in range(10)]
    scratch += [pltpu.VMEM((tb, 128), f32),
                pltpu.VMEM((n_chunks, 8, CHUNK), f32),
                pltpu.VMEM((tb + CONV_PAD, 3 * KW), f32),
                pltpu.VMEM((8 * CHUNK, KW), f32),
                blk(bf16), blk(bf16),
                pltpu.VMEM((n_chunks, 2 * CHUNK, KW), bf16),
                pltpu.VMEM((n_chunks * HEADS, CHUNK, CHUNK), bf16),
                pltpu.VMEM((n_chunks * HEADS, HEAD_DIM, HEAD_DIM), f32),
                pltpu.VMEM((n_chunks, 1, KW), f32),
                pltpu.VMEM((n_chunks, 1, 128), f32),
                pltpu.VMEM((HEADS, HEAD_DIM, HEAD_DIM), f32),
                pltpu.VMEM((HEADS, HEAD_DIM, HEAD_DIM), f32)]
    return pl.pallas_call(
        _mixer_kernel,
        grid=(bsz, seq // tb),
        in_specs=[pl.BlockSpec((None, tb, D_MODEL), lambda b, i: (b, i, 0))]
                 + [_resident(c.shape) for c in consts],
        out_specs=pl.BlockSpec((None, tb, D_MODEL), lambda b, i: (b, i, 0)),
        out_shape=jax.ShapeDtypeStruct(x.shape, f32),
        scratch_shapes=scratch,
        compiler_params=pltpu.CompilerParams(dimension_semantics=("arbitrary", "arbitrary"),
                                             vmem_limit_bytes=VMEM_LIMIT_BYTES),
        name="token_mixer",
    )(x, *consts)


def _mlp_call(h, wup, wdown, g2, b2):
    m = h.shape[0]
    tm = min(MLP_BLOCK, m)
    assert m % tm == 0
    consts = (wup, wdown, g2, b2)
    return pl.pallas_call(
        _mlp_kernel,
        grid=(m // tm,),
        in_specs=[pl.BlockSpec((tm, D_MODEL), lambda i: (i, 0))] + [_resident(c.shape) for c in consts],
        out_specs=pl.BlockSpec((tm, D_MODEL), lambda i: (i, 0)),
        out_shape=jax.ShapeDtypeStruct(h.shape, jnp.float32),
        compiler_params=pltpu.CompilerParams(dimension_semantics=("arbitrary",),
                                             vmem_limit_bytes=VMEM_LIMIT_BYTES),
        name="relu2_mlp",
    )(h, *consts)


def kernel(x, hg_lb_logits, w_in, conv_w, hg_norm_w, dn_A_log, dn_dt_bias, dn_norm_w, w_branch_a, w_branch_b, w_o, ln1_g, ln1_b, w_up, w_down, ln2_g, ln2_b):
    bsz, seq, _ = x.shape
    f32 = jnp.float32
    bf16 = jnp.bfloat16
    o = _OFFS
    l = 0
    w = w_in[l]
    whg = w[:, o[0]:o[4]].astype(bf16)
    wdn = w[:, o[4]:o[6]].astype(bf16)
    wab_f = w[:, o[6]:o[8]]
    wab = jnp.pad(wab_f, ((0, 0), (0, 128 - 2 * HEADS))).astype(bf16)
    wabt = wab_f.T.astype(bf16)
    wgate = w[:, o[8]:o[10]].astype(bf16)
    zeros4 = jnp.zeros((HEADS,), f32)
    alog8 = jnp.concatenate([zeros4, dn_A_log[l].astype(f32)])
    dtb8 = jnp.concatenate([zeros4, dn_dt_bias[l].astype(f32)])
    alogr = jnp.pad(alog8, (0, 128 - 2 * HEADS)).reshape(1, 128)
    dtbr = jnp.pad(dtb8, (0, 128 - 2 * HEADS)).reshape(1, 128)
    h1 = _mixer_call(
        x, hg_lb_logits.astype(f32), whg, wdn, wab, wabt, wgate, conv_w[l].astype(f32),
        hg_norm_w[l].reshape(1, KW), alogr, dtbr, alog8.reshape(8, 1), dtb8.reshape(8, 1),
        jnp.tile(dn_norm_w[l], HEADS).reshape(1, KW),
        w_branch_a[l].astype(bf16), w_branch_b[l].astype(bf16), w_o[l].astype(bf16),
        ln1_g[l].reshape(1, D_MODEL), ln1_b[l].reshape(1, D_MODEL))
    out = _mlp_call(h1.reshape(bsz * seq, D_MODEL), w_up[l].astype(bf16), w_down[l].astype(bf16),
                    ln2_g[l].reshape(1, D_MODEL), ln2_b[l].reshape(1, D_MODEL))
    return out.reshape(bsz, seq, D_MODEL)
```

```python
import functools

import numpy as np
import jax
import jax.numpy as jnp
from jax import lax
from jax.experimental import pallas as pl
from jax.experimental.pallas import tpu as pltpu

D_MODEL = 1024
HEADS = 4
HEAD_DIM = 128
KW = HEADS * HEAD_DIM
CHUNK = 64
CONV_W = 4
D_FF = 4 * D_MODEL
DEPTH = 1
ALPHA = (2 * DEPTH) ** 0.25
LN_EPS = 1e-5
RMS_EPS = 1e-6
L2_EPS = 1e-6
QK_SCALE = HEAD_DIM ** -0.5

TOKEN_BLOCK = 256
PREP_UNROLL = 4
MLP_BLOCK = 512
FF_BLOCK = 1024
CONV_PAD = 8
VMEM_LIMIT_BYTES = 56 * 1024 * 1024

_SIZES = (KW, KW, KW, KW, 3 * KW, KW, HEADS, HEADS, D_MODEL, D_MODEL)
_OFFS = np.concatenate([[0], np.cumsum(_SIZES)]).tolist()

_LEVELS = (32, 16, 8, 4, 2, 1)


def _build_constants():
    t = np.arange(CHUNK)[:, None]
    r = np.arange(CHUNK)[None, :]
    rows = [r <= t, r > t]
    masks = []
    for m in _LEVELS:
        last_key = (t // (2 * m)) * (2 * m) + m - 1
        is_query = (t % (2 * m)) >= m
        rows.append((is_query & (r > last_key) & (r <= t)) | (~is_query & (r > t) & (r <= last_key)))
        masks.append(((t // (2 * m)) == (r // (2 * m))) & is_query & ((r % (2 * m)) < m))
    masks.append(t == r)
    wexp = np.concatenate(rows, 0).astype(np.float32)
    masks = np.concatenate(masks, 0).astype(np.float32)
    return wexp, masks


_WEXP, _MASKS = _build_constants()


def _dot(a, b):
    return jnp.dot(a, b, preferred_element_type=jnp.float32)


def _dot_nt(a, b):
    return lax.dot_general(a, b, (((1,), (1,)), ((), ())), preferred_element_type=jnp.float32)


def _dot_tn(a, b):
    return lax.dot_general(a, b, (((0,), (0,)), ((), ())), preferred_element_type=jnp.float32)


def _bf(x):
    return x.astype(jnp.bfloat16)


def _split(x):
    hi = _bf(x)
    return hi, _bf(x - hi.astype(jnp.float32))


def _sigmoid(x):
    return 1.0 / (1.0 + jnp.exp(-x))


def _silu(x):
    return x * _sigmoid(x)


def _softplus(x):
    return jnp.maximum(x, 0.0) + jnp.log1p(jnp.exp(-jnp.abs(x)))


def _layer_norm(x, g, b):
    mu = jnp.mean(x, axis=-1, keepdims=True)
    xc = x - mu
    var = jnp.mean(xc * xc, axis=-1, keepdims=True)
    return xc * lax.rsqrt(var + LN_EPS) * g + b


def _mixer_kernel(x_ref, lbl_ref, whg_ref, wdn_ref, wab_ref, wabt_ref, wgate_ref, convw_ref, hgnw_ref,
                  alogr_ref, dtbr_ref, alogc_ref, dtbc_ref, dnnw_ref, wa_ref, wb_ref, wo_ref,
                  g1_ref, b1_ref, wexp_ref, masks_ref, bdtriu_ref, out_ref,
                  hq, hk, hv, hlf, oa, dq, dk, dv, ob, du, gbuf, gcrow, convbuf, exbuf, qin, kdec, wq, qkb,
                  kvinc, hdec, gl, shg, sdn):
    tb = x_ref.shape[0]
    n_chunks = tb // CHUNK

    @pl.when(pl.program_id(1) == 0)
    def _reset_carries():
        shg[...] = jnp.zeros_like(shg)
        sdn[...] = jnp.zeros_like(sdn)
        convbuf[0:CONV_PAD, :] = jnp.zeros((CONV_PAD, 3 * KW), jnp.float32)

    x = x_ref[...]
    xb = _bf(x)

    lg = lbl_ref[...]
    e = jnp.exp(lg - jnp.max(lg, axis=0, keepdims=True))
    lb = e[0:1, :] / jnp.sum(e, axis=0, keepdims=True)
    hq[...] = _silu(_dot(xb, whg_ref[:, 0:KW])) * QK_SCALE
    f = lb + (1.0 - lb) * _sigmoid(_dot(xb, whg_ref[:, KW:2 * KW]))
    hlf[...] = jnp.log(f)
    hk[...] = 1.0 - f
    hv[...] = _dot(xb, whg_ref[:, 2 * KW:3 * KW])

    convbuf[CONV_PAD:CONV_PAD + tb, :] = _dot(xb, wdn_ref[:, 0:3 * KW])
    cw = convw_ref[...]
    conv = convbuf[CONV_PAD - 3:CONV_PAD - 3 + tb, :] * cw[0:1, :]
    for j in range(1, CONV_W):
        conv = conv + convbuf[CONV_PAD - 3 + j:CONV_PAD - 3 + j + tb, :] * cw[j:j + 1, :]
    convbuf[0:CONV_PAD, :] = convbuf[tb:tb + CONV_PAD, :]
    qkv = _silu(conv)
    for h in range(HEADS):
        hs = slice(h * HEAD_DIM, (h + 1) * HEAD_DIM)
        qh = qkv[:, h * HEAD_DIM:(h + 1) * HEAD_DIM]
        kh = qkv[:, KW + h * HEAD_DIM:KW + (h + 1) * HEAD_DIM]
        dq[:, hs] = qh * (lax.rsqrt(jnp.sum(qh * qh, axis=-1, keepdims=True) + L2_EPS) * QK_SCALE)
        dk[:, hs] = kh * lax.rsqrt(jnp.sum(kh * kh, axis=-1, keepdims=True) + L2_EPS)
    dv[...] = qkv[:, 2 * KW:3 * KW]

    pab = _dot(xb, wab_ref[...])
    g_col = -jnp.exp(alogr_ref[...]) * _softplus(pab + dtbr_ref[...])
    lane = lax.broadcasted_iota(jnp.int32, pab.shape, 1)
    gbuf[...] = jnp.where(lane < HEADS, _sigmoid(pab), g_col)
    pabt = _dot_nt(wabt_ref[...], xb)
    g_row = -jnp.exp(alogc_ref[...]) * _softplus(pabt + dtbc_ref[...])
    gr_hi, gr_lo = _split(g_row)
    gc_row = _dot(gr_hi, bdtriu_ref[...]) + _dot(gr_lo, bdtriu_ref[...])
    for c in range(n_chunks):
        gcrow[c] = gc_row[:, c * CHUNK:(c + 1) * CHUNK]

    ri = lax.broadcasted_iota(jnp.int32, (CHUNK, CHUNK), 0)
    ci = lax.broadcasted_iota(jnp.int32, (CHUNK, CHUNK), 1)
    causal = ri >= ci
    strict = ri > ci

    n_lev = len(_LEVELS)
    heads = range(HEADS)
    hsl = [slice(h * HEAD_DIM, (h + 1) * HEAD_DIM) for h in heads]

    def level_mask(l):
        return masks_ref[l * CHUNK:(l + 1) * CHUNK, :]

    units = [(u, h) for u in range(PREP_UNROLL) for h in heads]
    n_units = range(len(units))

    def prepare_chunks(j, carry):
        cs = [j * PREP_UNROLL + u for u in range(PREP_UNROLL)]
        rows = [pl.ds(pl.multiple_of(c * CHUNK, CHUNK), CHUNK) for c in cs]
        wexp = wexp_ref[...]
        gb, eg, gcr = [], [], []
        for u in range(PREP_UNROLL):
            lf_hi, lf_lo = _split(hlf[rows[u], :])
            exbuf[u] = _dot(wexp, lf_hi) + _dot(wexp, lf_lo)
            gb.append(gbuf[rows[u], :])
            g_hi, g_lo = _split(gb[u])
            eg.append(_dot(wexp[0:2 * CHUNK], g_hi) + _dot(wexp[0:2 * CHUNK], g_lo))
            gcr.append(gcrow[cs[u]])

        def ld(ref, i):
            u, h = units[i]
            return ref[rows[u], hsl[h]]

        dkk = [ld(dk, i) for i in n_units]
        dkb = [_bf(k) for k in dkk]
        beta = [gb[u][:, h:h + 1] for u, h in units]
        gcol = [eg[u][0:CHUNK, HEADS + h:HEADS + h + 1] for u, h in units]
        decay = [jnp.where(causal, jnp.exp(jnp.minimum(gcol[i] - gcr[u][HEADS + h:HEADS + h + 1, :], 0.0)), 0.0)
                 for i, (u, h) in enumerate(units)]
        kbeta = [dkk[i] * beta[i] for i in n_units]
        lmat = [jnp.where(strict, _dot_nt(_bf(kbeta[i]), dkb[i]) * decay[i], 0.0) for i in n_units]
        for i, (u, h) in enumerate(units):
            qkb[cs[u] * HEADS + h] = _bf(_dot_nt(_bf(ld(dq, i)), dkb[i]) * decay[i])

        hqv = [ld(hq, i) for i in n_units]
        hkv = [ld(hk, i) for i in n_units]
        sc = [level_mask(n_lev) * _dot_nt(_bf(hqv[i]), _bf(hkv[i])) for i in n_units]
        nmat = [-(level_mask(n_lev - 1) * lmat[i]) for i in n_units]

        def score_level(l):
            for i, (u, h) in enumerate(units):
                z = jnp.exp(exbuf[u, (2 + l) * CHUNK:(3 + l) * CHUNK, hsl[h]])
                sc[i] = sc[i] + level_mask(l) * _dot_nt(_bf(hqv[i] * z), _bf(hkv[i] * z))

        score_level(n_lev - 1)
        for l in range(n_lev - 2, -1, -1):
            cm = [level_mask(l) * lmat[i] for i in n_units]
            nb = [_bf(nmat[i]) for i in n_units]
            xm = [cm[i] + _dot(nb[i], _bf(cm[i])) for i in n_units]
            score_level(l)
            for i in n_units:
                nmat[i] = nmat[i] - (xm[i] + _dot(_bf(xm[i]), nb[i]))

        hvb = [_bf(ld(hv, i)) for i in n_units]
        for i, (u, h) in enumerate(units):
            oa[rows[u], hsl[h]] = _dot(_bf(sc[i]), hvb[i])
        for i, (u, h) in enumerate(units):
            kd = hkv[i] * jnp.exp(exbuf[u, CHUNK:2 * CHUNK, hsl[h]])
            kvinc[cs[u] * HEADS + h] = _dot_tn(hvb[i], _bf(kd))
        for u in range(PREP_UNROLL):
            qin[rows[u], :] = _bf(hq[rows[u], :] * jnp.exp(exbuf[u, 0:CHUNK, :]))
            hdec[cs[u]] = jnp.exp(exbuf[u, CHUNK - 1:CHUNK, :])

        for i, (u, h) in enumerate(units):
            eg_col = jnp.exp(gcol[i])
            rhs = jnp.concatenate([ld(dv, i) * beta[i], kbeta[i] * eg_col], axis=1)
            uw = rhs + _dot(_bf(nmat[i]), _bf(rhs))
            du[rows[u], hsl[h]] = uw[:, 0:HEAD_DIM]
            wq[cs[u], 0:CHUNK, hsl[h]] = _bf(uw[:, HEAD_DIM:2 * HEAD_DIM])
            wq[cs[u], CHUNK:2 * CHUNK, hsl[h]] = _bf(ld(dq, i) * eg_col)
            kdec[rows[u], hsl[h]] = _bf(dkk[i] * jnp.exp(eg[u][CHUNK:2 * CHUNK, HEADS + h:HEADS + h + 1]))
        for u in range(PREP_UNROLL):
            gl[cs[u]] = jnp.exp(eg[u][CHUNK - 1:CHUNK, :])
        return carry

    lax.fori_loop(0, n_chunks // PREP_UNROLL, prepare_chunks, 0)

    def recur_chunk(c, carry):
        r0 = pl.multiple_of(c * CHUNK, CHUNK)
        rows = pl.ds(r0, CHUNK)
        s = [sdn[h] for h in heads]
        wqc = wq[c]
        ws = [_dot(wqc[:, hsl[h]], _bf(s[h])) for h in heads]
        st = [shg[h] for h in heads]
        dec = hdec[c]
        for h in heads:
            oa[rows, hsl[h]] = oa[rows, hsl[h]] + _dot_nt(qin[rows, hsl[h]], _bf(st[h]))
            shg[h] = st[h] * dec[:, hsl[h]] + kvinc[c * HEADS + h]
        vnb = [_bf(du[rows, hsl[h]] - ws[h][0:CHUNK]) for h in heads]
        glc = gl[c]
        for h in heads:
            ob[rows, hsl[h]] = ws[h][CHUNK:2 * CHUNK] + _dot(qkb[c * HEADS + h], vnb[h])
        for h in heads:
            sdn[h] = s[h] * glc[:, HEADS + h:HEADS + h + 1] + _dot_tn(kdec[rows, hsl[h]], vnb[h])
        return carry

    lax.fori_loop(0, n_chunks, recur_chunk, 0)

    def gated_norm(o_ref, w_row, gate):
        parts = []
        for h in range(HEADS):
            oh = o_ref[:, h * HEAD_DIM:(h + 1) * HEAD_DIM]
            parts.append(oh * lax.rsqrt(jnp.mean(oh * oh, axis=-1, keepdims=True) + RMS_EPS))
        return jnp.concatenate(parts, axis=1) * w_row * _silu(gate)

    na = gated_norm(oa, hgnw_ref[...], _dot(xb, whg_ref[:, 3 * KW:4 * KW]))
    nb_ = gated_norm(ob, dnnw_ref[...], _dot(xb, wdn_ref[:, 3 * KW:4 * KW]))
    merged = (_sigmoid(_dot(xb, wgate_ref[:, 0:D_MODEL])) * _dot(_bf(na), wa_ref[...])
              + _sigmoid(_dot(xb, wgate_ref[:, D_MODEL:2 * D_MODEL])) * _dot(_bf(nb_), wb_ref[...]))
    mix = _dot(_bf(merged), wo_ref[...])
    out_ref[...] = _layer_norm(ALPHA * x + mix, g1_ref[...], b1_ref[...])


def _mlp_kernel(h_ref, wup_ref, wdown_ref, g2_ref, b2_ref, out_ref):
    h = h_ref[...]
    hb = _bf(h)
    acc = jnp.zeros(h.shape, jnp.float32)
    for j in range(D_FF // FF_BLOCK):
        up = _dot(hb, wup_ref[:, j * FF_BLOCK:(j + 1) * FF_BLOCK])
        act = jnp.square(jnp.maximum(up, 0.0))
        acc = acc + _dot(_bf(act), wdown_ref[j * FF_BLOCK:(j + 1) * FF_BLOCK, :])
    out_ref[...] = _layer_norm(ALPHA * h + acc, g2_ref[...], b2_ref[...])


def _resident(shape):
    nd = len(shape)
    return pl.BlockSpec(shape, lambda *_: (0,) * nd, pipeline_mode=pl.Buffered(1))


def _mixer_call(x, lbl, whg, wdn, wab, wabt, wgate, convw, hgnw, alogr, dtbr, alogc, dtbc, dnnw,
                wa, wb, wo, g1, b1):
    bsz, seq, _ = x.shape
    tb = min(TOKEN_BLOCK, seq)
    assert seq % tb == 0 and tb % (CHUNK * PREP_UNROLL) == 0
    n_chunks = tb // CHUNK
    wexp = jnp.asarray(_WEXP, jnp.bfloat16)
    masks = jnp.asarray(_MASKS, jnp.float32)
    tt = np.arange(tb)
    bdtriu = jnp.asarray(((tt[:, None] <= tt[None, :]) & ((tt[:, None] // CHUNK) == (tt[None, :] // CHUNK))),
                         jnp.bfloat16)
    consts = (lbl, whg, wdn, wab, wabt, wgate, convw, hgnw, alogr, dtbr, alogc, dtbc, dnnw, wa, wb, wo,
              g1, b1, wexp, masks, bdtriu)
    f32 = jnp.float32
    blk = functools.partial(pltpu.VMEM, (tb, KW))
    bf16 = jnp.bfloat16
    scratch = [blk(f32) for _ in range(10)]
    scratch += [pltpu.VMEM((tb, 128), f32),
                pltpu.VMEM((n_chunks, 8, CHUNK), f32),
                pltpu.VMEM((tb + CONV_PAD, 3 * KW), f32),
                pltpu.VMEM((PREP_UNROLL, 8 * CHUNK, KW), f32),
                blk(bf16), blk(bf16),
                pltpu.VMEM((n_chunks, 2 * CHUNK, KW), bf16),
                pltpu.VMEM((n_chunks * HEADS, CHUNK, CHUNK), bf16),
                pltpu.VMEM((n_chunks * HEADS, HEAD_DIM, HEAD_DIM), f32),
                pltpu.VMEM((n_chunks, 1, KW), f32),
                pltpu.VMEM((n_chunks, 1, 128), f32),
                pltpu.VMEM((HEADS, HEAD_DIM, HEAD_DIM), f32),
                pltpu.VMEM((HEADS, HEAD_DIM, HEAD_DIM), f32)]
    return pl.pallas_call(
        _mixer_kernel,
        grid=(bsz, seq // tb),
        in_specs=[pl.BlockSpec((None, tb, D_MODEL), lambda b, i: (b, i, 0))]
                 + [_resident(c.shape) for c in consts],
        out_specs=pl.BlockSpec((None, tb, D_MODEL), lambda b, i: (b, i, 0)),
        out_shape=jax.ShapeDtypeStruct(x.shape, f32),
        scratch_shapes=scratch,
        compiler_params=pltpu.CompilerParams(dimension_semantics=("arbitrary", "arbitrary"),
                                             vmem_limit_bytes=VMEM_LIMIT_BYTES),
        name="token_mixer",
    )(x, *consts)


def _mlp_call(h, wup, wdown, g2, b2):
    m = h.shape[0]
    tm = min(MLP_BLOCK, m)
    assert m % tm == 0
    consts = (wup, wdown, g2, b2)
    return pl.pallas_call(
        _mlp_kernel,
        grid=(m // tm,),
        in_specs=[pl.BlockSpec((tm, D_MODEL), lambda i: (i, 0))] + [_resident(c.shape) for c in consts],
        out_specs=pl.BlockSpec((tm, D_MODEL), lambda i: (i, 0)),
        out_shape=jax.ShapeDtypeStruct(h.shape, jnp.float32),
        compiler_params=pltpu.CompilerParams(dimension_semantics=("arbitrary",),
                                             vmem_limit_bytes=VMEM_LIMIT_BYTES),
        name="relu2_mlp",
    )(h, *consts)


def kernel(x, hg_lb_logits, w_in, conv_w, hg_norm_w, dn_A_log, dn_dt_bias, dn_norm_w, w_branch_a, w_branch_b, w_o, ln1_g, ln1_b, w_up, w_down, ln2_g, ln2_b):
    bsz, seq, _ = x.shape
    f32 = jnp.float32
    bf16 = jnp.bfloat16
    o = _OFFS
    l = 0
    w = w_in[l]
    whg = w[:, o[0]:o[4]].astype(bf16)
    wdn = w[:, o[4]:o[6]].astype(bf16)
    wab_f = w[:, o[6]:o[8]]
    wab = jnp.pad(wab_f, ((0, 0), (0, 128 - 2 * HEADS))).astype(bf16)
    wabt = wab_f.T.astype(bf16)
    wgate = w[:, o[8]:o[10]].astype(bf16)
    zeros4 = jnp.zeros((HEADS,), f32)
    alog8 = jnp.concatenate([zeros4, dn_A_log[l].astype(f32)])
    dtb8 = jnp.concatenate([zeros4, dn_dt_bias[l].astype(f32)])
    alogr = jnp.pad(alog8, (0, 128 - 2 * HEADS)).reshape(1, 128)
    dtbr = jnp.pad(dtb8, (0, 128 - 2 * HEADS)).reshape(1, 128)
    h1 = _mixer_call(
        x, hg_lb_logits.astype(f32), whg, wdn, wab, wabt, wgate, conv_w[l].astype(f32),
        hg_norm_w[l].reshape(1, KW), alogr, dtbr, alog8.reshape(8, 1), dtb8.reshape(8, 1),
        jnp.tile(dn_norm_w[l], HEADS).reshape(1, KW),
        w_branch_a[l].astype(bf16), w_branch_b[l].astype(bf16), w_o[l].astype(bf16),
        ln1_g[l].reshape(1, D_MODEL), ln1_b[l].reshape(1, D_MODEL))
    out = _mlp_call(h1.reshape(bsz * seq, D_MODEL), w_up[l].astype(bf16), w_down[l].astype(bf16),
                    ln2_g[l].reshape(1, D_MODEL), ln2_b[l].reshape(1, D_MODEL))
    return out.reshape(bsz, seq, D_MODEL)
```

```python
import functools

import numpy as np
import jax
import jax.numpy as jnp
from jax import lax
from jax.experimental import pallas as pl
from jax.experimental.pallas import tpu as pltpu

D_MODEL = 1024
HEADS = 4
HEAD_DIM = 128
KW = HEADS * HEAD_DIM
CHUNK = 64
CONV_W = 4
D_FF = 4 * D_MODEL
DEPTH = 1
ALPHA = (2 * DEPTH) ** 0.25
LN_EPS = 1e-5
RMS_EPS = 1e-6
L2_EPS = 1e-6
QK_SCALE = HEAD_DIM ** -0.5
NEG_LOG2E = -1.4426950408889634

TOKEN_BLOCK = 256
PREP_UNROLL = 4
GATE_TILE = 256
MLP_BLOCK = 512
FF_BLOCK = 1024
CONV_PAD = 8
VMEM_LIMIT_BYTES = 56 * 1024 * 1024

_SIZES = (KW, KW, KW, KW, 3 * KW, KW, HEADS, HEADS, D_MODEL, D_MODEL)
_OFFS = np.concatenate([[0], np.cumsum(_SIZES)]).tolist()

_LEVELS = (32, 16, 8, 4, 2, 1)


def _build_constants():
    t = np.arange(CHUNK)[:, None]
    r = np.arange(CHUNK)[None, :]
    rows = [r <= t, r > t]
    masks = []
    for m in _LEVELS:
        last_key = (t // (2 * m)) * (2 * m) + m - 1
        is_query = (t % (2 * m)) >= m
        rows.append((is_query & (r > last_key) & (r <= t)) | (~is_query & (r > t) & (r <= last_key)))
        masks.append(((t // (2 * m)) == (r // (2 * m))) & is_query & ((r % (2 * m)) < m))
    masks.append(t == r)
    wexp = np.concatenate(rows, 0).astype(np.float32)
    masks = np.concatenate(masks, 0).astype(np.float32)
    return wexp, masks


_WEXP, _MASKS = _build_constants()


def _dot(a, b):
    return jnp.dot(a, b, preferred_element_type=jnp.float32)


def _dot_nt(a, b):
    return lax.dot_general(a, b, (((1,), (1,)), ((), ())), preferred_element_type=jnp.float32)


def _dot_tn(a, b):
    return lax.dot_general(a, b, (((0,), (0,)), ((), ())), preferred_element_type=jnp.float32)


def _bf(x):
    return x.astype(jnp.bfloat16)


def _split(x):
    hi = _bf(x)
    return hi, _bf(x - hi.astype(jnp.float32))


def _sigmoid(x):
    return 1.0 / (1.0 + jnp.exp2(x * NEG_LOG2E))


def _silu(x):
    return x * _sigmoid(x)


def _softplus(x):
    return jnp.maximum(x, 0.0) + jnp.log1p(jnp.exp(-jnp.abs(x)))


def _layer_norm(x, g, b):
    mu = jnp.mean(x, axis=-1, keepdims=True)
    xc = x - mu
    var = jnp.mean(xc * xc, axis=-1, keepdims=True)
    return xc * lax.rsqrt(var + LN_EPS) * g + b


def _mixer_kernel(x_ref, lbl_ref, whg_ref, wdn_ref, wab_ref, wabt_ref, wgate_ref, convw_ref, hgnw_ref,
                  alogr_ref, dtbr_ref, alogc_ref, dtbc_ref, dnnw_ref, wa_ref, wb_ref, wo_ref,
                  g1_ref, b1_ref, wexp_ref, masks_ref, bdtriu_ref, out_ref,
                  hq, hk, hv, hlf, oa, dq, dk, dv, ob, du, gbuf, gcrow, convbuf, exbuf, qin, kdec, wq, qkb,
                  kvinc, hdec, gl, sg, sz, sga, sgb, shg, sdn):
    tb = x_ref.shape[0]
    n_chunks = tb // CHUNK

    @pl.when(pl.program_id(1) == 0)
    def _reset_carries():
        shg[...] = jnp.zeros_like(shg)
        sdn[...] = jnp.zeros_like(sdn)
        convbuf[0:CONV_PAD, :] = jnp.zeros((CONV_PAD, 3 * KW), jnp.float32)

    x = x_ref[...]
    xb = _bf(x)

    def project_conv(part):
        cols = slice(part * KW, (part + 1) * KW)
        convbuf[CONV_PAD:CONV_PAD + tb, cols] = _dot(xb, wdn_ref[:, cols])

    def conv_silu(part):
        cols = slice(part * KW, (part + 1) * KW)
        cw = convw_ref[:, cols]
        acc = convbuf[CONV_PAD - 3:CONV_PAD - 3 + tb, cols] * cw[0:1, :]
        for j in range(1, CONV_W):
            acc = acc + convbuf[CONV_PAD - 3 + j:CONV_PAD - 3 + j + tb, cols] * cw[j:j + 1, :]
        return _silu(acc)

    def l2_normalize_heads(val, dst, scale):
        for h in range(HEADS):
            hs = slice(h * HEAD_DIM, (h + 1) * HEAD_DIM)
            vh = val[:, hs]
            dst[:, hs] = vh * (lax.rsqrt(jnp.sum(vh * vh, axis=-1, keepdims=True) + L2_EPS) * scale)

    project_conv(0)
    project_conv(1)
    l2_normalize_heads(conv_silu(0), dq, QK_SCALE)
    project_conv(2)
    l2_normalize_heads(conv_silu(1), dk, 1.0)
    p_hq = _dot(xb, whg_ref[:, 0:KW])
    dv[...] = conv_silu(2)
    convbuf[0:CONV_PAD, :] = convbuf[tb:tb + CONV_PAD, :]

    p_hf = _dot(xb, whg_ref[:, KW:2 * KW])
    hq[...] = _silu(p_hq) * QK_SCALE
    hv[...] = _dot(xb, whg_ref[:, 2 * KW:3 * KW])
    lg = lbl_ref[...]
    e = jnp.exp(lg - jnp.max(lg, axis=0, keepdims=True))
    lb = e[0:1, :] / jnp.sum(e, axis=0, keepdims=True)
    f = lb + (1.0 - lb) * _sigmoid(p_hf)
    hlf[...] = jnp.log(f)
    hk[...] = 1.0 - f

    pab = _dot(xb, wab_ref[...])
    g_col = -jnp.exp(alogr_ref[...]) * _softplus(pab + dtbr_ref[...])
    lane = lax.broadcasted_iota(jnp.int32, pab.shape, 1)
    gbuf[...] = jnp.where(lane < HEADS, _sigmoid(pab), g_col)
    pabt = _dot_nt(wabt_ref[...], xb)
    g_row = -jnp.exp(alogc_ref[...]) * _softplus(pabt + dtbc_ref[...])
    gr_hi, gr_lo = _split(g_row)
    gc_row = _dot(gr_hi, bdtriu_ref[...]) + _dot(gr_lo, bdtriu_ref[...])
    for c in range(n_chunks):
        gcrow[c] = gc_row[:, c * CHUNK:(c + 1) * CHUNK]

    ri = lax.broadcasted_iota(jnp.int32, (CHUNK, CHUNK), 0)
    ci = lax.broadcasted_iota(jnp.int32, (CHUNK, CHUNK), 1)
    causal = ri >= ci
    strict = ri > ci

    n_lev = len(_LEVELS)
    heads = range(HEADS)
    hsl = [slice(h * HEAD_DIM, (h + 1) * HEAD_DIM) for h in heads]

    def level_mask(l):
        return masks_ref[l * CHUNK:(l + 1) * CHUNK, :]

    units = [(u, h) for u in range(PREP_UNROLL) for h in heads]
    n_units = range(len(units))

    def prepare_chunks(j, carry):
        cs = [j * PREP_UNROLL + u for u in range(PREP_UNROLL)]
        rows = [pl.ds(pl.multiple_of(c * CHUNK, CHUNK), CHUNK) for c in cs]
        wexp = wexp_ref[...]
        gb, eg, gcr = [], [], []
        for u in range(PREP_UNROLL):
            lf_hi, lf_lo = _split(hlf[rows[u], :])
            exbuf[u] = _dot(wexp, lf_hi) + _dot(wexp, lf_lo)
            gb.append(gbuf[rows[u], :])
            g_hi, g_lo = _split(gb[u])
            eg.append(_dot(wexp[0:2 * CHUNK], g_hi) + _dot(wexp[0:2 * CHUNK], g_lo))
            gcr.append(gcrow[cs[u]])

        def ld(ref, i):
            u, h = units[i]
            return ref[rows[u], hsl[h]]

        dkk = [ld(dk, i) for i in n_units]
        dkb = [_bf(k) for k in dkk]
        beta = [gb[u][:, h:h + 1] for u, h in units]
        gcol = [eg[u][0:CHUNK, HEADS + h:HEADS + h + 1] for u, h in units]
        decay = [jnp.where(causal, jnp.exp(jnp.minimum(gcol[i] - gcr[u][HEADS + h:HEADS + h + 1, :], 0.0)), 0.0)
                 for i, (u, h) in enumerate(units)]
        kbeta = [dkk[i] * beta[i] for i in n_units]
        lmat = [jnp.where(strict, _dot_nt(_bf(kbeta[i]), dkb[i]) * decay[i], 0.0) for i in n_units]
        for i, (u, h) in enumerate(units):
            qkb[cs[u] * HEADS + h] = _bf(_dot_nt(_bf(ld(dq, i)), dkb[i]) * decay[i])

        hqv = [ld(hq, i) for i in n_units]
        hkv = [ld(hk, i) for i in n_units]
        sc = [level_mask(n_lev) * _dot_nt(_bf(hqv[i]), _bf(hkv[i])) for i in n_units]
        nmat = [-(level_mask(n_lev - 1) * lmat[i]) for i in n_units]

        def score_level(l):
            for i, (u, h) in enumerate(units):
                z = jnp.exp(exbuf[u, (2 + l) * CHUNK:(3 + l) * CHUNK, hsl[h]])
                sc[i] = sc[i] + level_mask(l) * _dot_nt(_bf(hqv[i] * z), _bf(hkv[i] * z))

        score_level(n_lev - 1)
        for l in range(n_lev - 2, -1, -1):
            cm = [level_mask(l) * lmat[i] for i in n_units]
            nb = [_bf(nmat[i]) for i in n_units]
            xm = [cm[i] + _dot(nb[i], _bf(cm[i])) for i in n_units]
            score_level(l)
            for i in n_units:
                nmat[i] = nmat[i] - (xm[i] + _dot(_bf(xm[i]), nb[i]))

        hvb = [_bf(ld(hv, i)) for i in n_units]
        for i, (u, h) in enumerate(units):
            oa[rows[u], hsl[h]] = _dot(_bf(sc[i]), hvb[i])
        for i, (u, h) in enumerate(units):
            kd = hkv[i] * jnp.exp(exbuf[u, CHUNK:2 * CHUNK, hsl[h]])
            kvinc[cs[u] * HEADS + h] = _dot_tn(hvb[i], _bf(kd))
        for u in range(PREP_UNROLL):
            qin[rows[u], :] = _bf(hq[rows[u], :] * jnp.exp(exbuf[u, 0:CHUNK, :]))
            hdec[cs[u]] = jnp.exp(exbuf[u, CHUNK - 1:CHUNK, :])

        for i, (u, h) in enumerate(units):
            eg_col = jnp.exp(gcol[i])
            rhs = jnp.concatenate([ld(dv, i) * beta[i], kbeta[i] * eg_col], axis=1)
            uw = rhs + _dot(_bf(nmat[i]), _bf(rhs))
            du[rows[u], hsl[h]] = uw[:, 0:HEAD_DIM]
            wq[cs[u], 0:CHUNK, hsl[h]] = _bf(uw[:, HEAD_DIM:2 * HEAD_DIM])
            wq[cs[u], CHUNK:2 * CHUNK, hsl[h]] = _bf(ld(dq, i) * eg_col)
            kdec[rows[u], hsl[h]] = _bf(dkk[i] * jnp.exp(eg[u][CHUNK:2 * CHUNK, HEADS + h:HEADS + h + 1]))
        for u in range(PREP_UNROLL):
            gl[cs[u]] = jnp.exp(eg[u][CHUNK - 1:CHUNK, :])
        return carry

    lax.fori_loop(0, n_chunks // PREP_UNROLL, prepare_chunks, 0)

    gate_tiles = ([(whg_ref, 3 * KW + t, sg, t, _silu) for t in range(0, KW, GATE_TILE)]
                  + [(wdn_ref, 3 * KW + t, sz, t, _silu) for t in range(0, KW, GATE_TILE)]
                  + [(wgate_ref, t, sga, t, _sigmoid) for t in range(0, D_MODEL, GATE_TILE)]
                  + [(wgate_ref, D_MODEL + t, sgb, t, _sigmoid) for t in range(0, D_MODEL, GATE_TILE)])
    n_slots = 2 * n_chunks

    def gate_slot(slot):
        for k in range(slot * len(gate_tiles) // n_slots, (slot + 1) * len(gate_tiles) // n_slots):
            w_ref, src, dst, off, act = gate_tiles[k]
            dst[:, off:off + GATE_TILE] = act(_dot(xb, w_ref[:, src:src + GATE_TILE]))

    for c in range(n_chunks):
        rows = slice(c * CHUNK, (c + 1) * CHUNK)
        s = [sdn[h] for h in heads]
        wqc = wq[c]
        ws = [_dot(wqc[:, hsl[h]], _bf(s[h])) for h in heads]
        st = [shg[h] for h in heads]
        dec = hdec[c]
        for h in heads:
            oa[rows, hsl[h]] = oa[rows, hsl[h]] + _dot_nt(qin[rows, hsl[h]], _bf(st[h]))
            shg[h] = st[h] * dec[:, hsl[h]] + kvinc[c * HEADS + h]
        gate_slot(2 * c)
        vnb = [_bf(du[rows, hsl[h]] - ws[h][0:CHUNK]) for h in heads]
        glc = gl[c]
        for h in heads:
            ob[rows, hsl[h]] = ws[h][CHUNK:2 * CHUNK] + _dot(qkb[c * HEADS + h], vnb[h])
        for h in heads:
            sdn[h] = s[h] * glc[:, HEADS + h:HEADS + h + 1] + _dot_tn(kdec[rows, hsl[h]], vnb[h])
        gate_slot(2 * c + 1)

    def gated_norm(o_ref, w_row, gate):
        parts = []
        for h in range(HEADS):
            oh = o_ref[:, h * HEAD_DIM:(h + 1) * HEAD_DIM]
            parts.append(oh * lax.rsqrt(jnp.mean(oh * oh, axis=-1, keepdims=True) + RMS_EPS))
        return jnp.concatenate(parts, axis=1) * w_row * gate

    na = gated_norm(oa, hgnw_ref[...], sg[...])
    nb_ = gated_norm(ob, dnnw_ref[...], sz[...])
    merged = sga[...] * _dot(_bf(na), wa_ref[...]) + sgb[...] * _dot(_bf(nb_), wb_ref[...])
    mix = _dot(_bf(merged), wo_ref[...])
    out_ref[...] = _layer_norm(ALPHA * x + mix, g1_ref[...], b1_ref[...])


def _mlp_kernel(h_ref, wup_ref, wdown_ref, g2_ref, b2_ref, out_ref):
    h = h_ref[...]
    hb = _bf(h)
    acc = jnp.zeros(h.shape, jnp.float32)
    for j in range(D_FF // FF_BLOCK):
        up = _dot(hb, wup_ref[:, j * FF_BLOCK:(j + 1) * FF_BLOCK])
        act = jnp.square(jnp.maximum(up, 0.0))
        acc = acc + _dot(_bf(act), wdown_ref[j * FF_BLOCK:(j + 1) * FF_BLOCK, :])
    out_ref[...] = _layer_norm(ALPHA * h + acc, g2_ref[...], b2_ref[...])


def _resident(shape):
    nd = len(shape)
    return pl.BlockSpec(shape, lambda *_: (0,) * nd, pipeline_mode=pl.Buffered(1))


def _mixer_call(x, lbl, whg, wdn, wab, wabt, wgate, convw, hgnw, alogr, dtbr, alogc, dtbc, dnnw,
                wa, wb, wo, g1, b1):
    bsz, seq, _ = x.shape
    tb = min(TOKEN_BLOCK, seq)
    assert seq % tb == 0 and tb % (CHUNK * PREP_UNROLL) == 0
    n_chunks = tb // CHUNK
    wexp = jnp.asarray(_WEXP, jnp.bfloat16)
    masks = jnp.asarray(_MASKS, jnp.float32)
    tt = np.arange(tb)
    bdtriu = jnp.asarray(((tt[:, None] <= tt[None, :]) & ((tt[:, None] // CHUNK) == (tt[None, :] // CHUNK))),
                         jnp.bfloat16)
    consts = (lbl, whg, wdn, wab, wabt, wgate, convw, hgnw, alogr, dtbr, alogc, dtbc, dnnw, wa, wb, wo,
              g1, b1, wexp, masks, bdtriu)
    f32 = jnp.float32
    blk = functools.partial(pltpu.VMEM, (tb, KW))
    bf16 = jnp.bfloat16
    scratch = [blk(f32) for _ in range(10)]
    scratch += [pltpu.VMEM((tb, 128), f32),
                pltpu.VMEM((n_chunks, 8, CHUNK), f32),
                pltpu.VMEM((tb + CONV_PAD, 3 * KW), f32),
                pltpu.VMEM((PREP_UNROLL, 8 * CHUNK, KW), f32),
                blk(bf16), blk(bf16),
                pltpu.VMEM((n_chunks, 2 * CHUNK, KW), bf16),
                pltpu.VMEM((n_chunks * HEADS, CHUNK, CHUNK), bf16),
                pltpu.VMEM((n_chunks * HEADS, HEAD_DIM, HEAD_DIM), f32),
                pltpu.VMEM((n_chunks, 1, KW), f32),
                pltpu.VMEM((n_chunks, 1, 128), f32),
                blk(f32), blk(f32),
                pltpu.VMEM((tb, D_MODEL), f32),
                pltpu.VMEM((tb, D_MODEL), f32),
                pltpu.VMEM((HEADS, HEAD_DIM, HEAD_DIM), f32),
                pltpu.VMEM((HEADS, HEAD_DIM, HEAD_DIM), f32)]
    return pl.pallas_call(
        _mixer_kernel,
        grid=(bsz, seq // tb),
        in_specs=[pl.BlockSpec((None, tb, D_MODEL), lambda b, i: (b, i, 0))]
                 + [_resident(c.shape) for c in consts],
        out_specs=pl.BlockSpec((None, tb, D_MODEL), lambda b, i: (b, i, 0)),
        out_shape=jax.ShapeDtypeStruct(x.shape, f32),
        scratch_shapes=scratch,
        compiler_params=pltpu.CompilerParams(dimension_semantics=("arbitrary", "arbitrary"),
                                             vmem_limit_bytes=VMEM_LIMIT_BYTES),
        name="token_mixer",
    )(x, *consts)


def _mlp_call(h, wup, wdown, g2, b2):
    m = h.shape[0]
    tm = min(MLP_BLOCK, m)
    assert m % tm == 0
    consts = (wup, wdown, g2, b2)
    return pl.pallas_call(
        _mlp_kernel,
        grid=(m // tm,),
        in_specs=[pl.BlockSpec((tm, D_MODEL), lambda i: (i, 0))] + [_resident(c.shape) for c in consts],
        out_specs=pl.BlockSpec((tm, D_MODEL), lambda i: (i, 0)),
        out_shape=jax.ShapeDtypeStruct(h.shape, jnp.float32),
        compiler_params=pltpu.CompilerParams(dimension_semantics=("arbitrary",),
                                             vmem_limit_bytes=VMEM_LIMIT_BYTES),
        name="relu2_mlp",
    )(h, *consts)


def kernel(x, hg_lb_logits, w_in, conv_w, hg_norm_w, dn_A_log, dn_dt_bias, dn_norm_w, w_branch_a, w_branch_b, w_o, ln1_g, ln1_b, w_up, w_down, ln2_g, ln2_b):
    bsz, seq, _ = x.shape
    f32 = jnp.float32
    bf16 = jnp.bfloat16
    o = _OFFS
    l = 0
    w = w_in[l]
    whg = w[:, o[0]:o[4]].astype(bf16)
    wdn = w[:, o[4]:o[6]].astype(bf16)
    wab_f = w[:, o[6]:o[8]]
    wab = jnp.pad(wab_f, ((0, 0), (0, 128 - 2 * HEADS))).astype(bf16)
    wabt = wab_f.T.astype(bf16)
    wgate = w[:, o[8]:o[10]].astype(bf16)
    zeros4 = jnp.zeros((HEADS,), f32)
    alog8 = jnp.concatenate([zeros4, dn_A_log[l].astype(f32)])
    dtb8 = jnp.concatenate([zeros4, dn_dt_bias[l].astype(f32)])
    alogr = jnp.pad(alog8, (0, 128 - 2 * HEADS)).reshape(1, 128)
    dtbr = jnp.pad(dtb8, (0, 128 - 2 * HEADS)).reshape(1, 128)
    h1 = _mixer_call(
        x, hg_lb_logits.astype(f32), whg, wdn, wab, wabt, wgate, conv_w[l].astype(f32),
        hg_norm_w[l].reshape(1, KW), alogr, dtbr, alog8.reshape(8, 1), dtb8.reshape(8, 1),
        jnp.tile(dn_norm_w[l], HEADS).reshape(1, KW),
        w_branch_a[l].astype(bf16), w_branch_b[l].astype(bf16), w_o[l].astype(bf16),
        ln1_g[l].reshape(1, D_MODEL), ln1_b[l].reshape(1, D_MODEL))
    out = _mlp_call(h1.reshape(bsz * seq, D_MODEL), w_up[l].astype(bf16), w_down[l].astype(bf16),
                    ln2_g[l].reshape(1, D_MODEL), ln2_b[l].reshape(1, D_MODEL))
    return out.reshape(bsz, seq, D_MODEL)
```

```python
import functools

import numpy as np
import jax
import jax.numpy as jnp
from jax import lax
from jax.experimental import pallas as pl
from jax.experimental.pallas import tpu as pltpu

D_MODEL = 1024
HEADS = 4
HEAD_DIM = 128
KW = HEADS * HEAD_DIM
CHUNK = 64
CONV_W = 4
D_FF = 4 * D_MODEL
DEPTH = 1
ALPHA = (2 * DEPTH) ** 0.25
LN_EPS = 1e-5
RMS_EPS = 1e-6
L2_EPS = 1e-6
QK_SCALE = HEAD_DIM ** -0.5
NEG_LOG2E = -1.4426950408889634

TOKEN_BLOCK = 256
PREP_UNROLL = 4
GATE_TILE = 256
MLP_BLOCK = 512
FF_BLOCK = 1024
SUBLANES = 8
CONV_PAD = 8
VMEM_LIMIT_BYTES = 56 * 1024 * 1024

_SIZES = (KW, KW, KW, KW, 3 * KW, KW, HEADS, HEADS, D_MODEL, D_MODEL)
_OFFS = np.concatenate([[0], np.cumsum(_SIZES)]).tolist()

_LEVELS = (32, 16, 8, 4, 2, 1)


def _build_constants():
    t = np.arange(CHUNK)[:, None]
    r = np.arange(CHUNK)[None, :]
    rows = [r <= t, r > t]
    masks = []
    for m in _LEVELS:
        last_key = (t // (2 * m)) * (2 * m) + m - 1
        is_query = (t % (2 * m)) >= m
        rows.append((is_query & (r > last_key) & (r <= t)) | (~is_query & (r > t) & (r <= last_key)))
        masks.append(((t // (2 * m)) == (r // (2 * m))) & is_query & ((r % (2 * m)) < m))
    masks.append(t == r)
    wexp = np.concatenate(rows, 0).astype(np.float32)
    masks = np.concatenate(masks, 0).astype(np.float32)
    return wexp, masks


_WEXP, _MASKS = _build_constants()


def _dot(a, b):
    return jnp.dot(a, b, preferred_element_type=jnp.float32)


def _dot_nt(a, b):
    return lax.dot_general(a, b, (((1,), (1,)), ((), ())), preferred_element_type=jnp.float32)


def _dot_tn(a, b):
    return lax.dot_general(a, b, (((0,), (0,)), ((), ())), preferred_element_type=jnp.float32)


def _bf(x):
    return x.astype(jnp.bfloat16)


def _split(x):
    hi = _bf(x)
    return hi, _bf(x - hi.astype(jnp.float32))


def _sigmoid(x):
    return 1.0 / (1.0 + jnp.exp2(x * NEG_LOG2E))


def _silu(x):
    return x * _sigmoid(x)


def _softplus(x):
    return jnp.maximum(x, 0.0) + jnp.log1p(jnp.exp(-jnp.abs(x)))


def _layer_norm(x, g, b):
    mu = jnp.mean(x, axis=-1, keepdims=True)
    xc = x - mu
    var = jnp.mean(xc * xc, axis=-1, keepdims=True)
    return xc * lax.rsqrt(var + LN_EPS) * g + b


def _mixer_kernel(x_ref, lbl_ref, wheads_ref, wgates_ref, wab_ref, wabt_ref, convw_ref, hgnw_ref,
                  alogr_ref, dtbr_ref, alogc_ref, dtbc_ref, dnnw_ref, wa_ref, wb_ref, wo_ref,
                  g1_ref, b1_ref, wexp_ref, masks_ref, bdtriu_ref, bdtril_ref, out_ref,
                  xbuf, hq, hk, hv, hlf, hf, hb, dq, dk, dv, oa, ob, du, gbuf, gcrow, convbuf, rawhg, qin, kdec, wq, qkb,
                  kvinc, hdec, gl, sg, sz, sga, sgb, shg, sdn):
    tb = x_ref.shape[0]
    n_chunks = tb // CHUNK

    @pl.when(pl.program_id(1) == 0)
    def _reset_carries():
        shg[...] = jnp.zeros_like(shg)
        sdn[...] = jnp.zeros_like(sdn)
        convbuf[:, 0:CONV_PAD, :] = jnp.zeros((HEADS, CONV_PAD, 3 * HEAD_DIM), jnp.float32)

    xbuf[...] = _bf(x_ref[...])

    def project_head(c):
        raw = _dot(xbuf[...],wheads_ref[c])
        convbuf[c, CONV_PAD:CONV_PAD + tb, :] = raw[:, 0:3 * HEAD_DIM]
        rawhg[c] = raw[:, 3 * HEAD_DIM:6 * HEAD_DIM]

    def head_epilogue(c):
        cw = convw_ref[c]
        base = CONV_PAD - (CONV_W - 1)
        acc = convbuf[c, base:base + tb, :] * cw[0:1, :]
        for j in range(1, CONV_W):
            acc = acc + convbuf[c, base + j:base + j + tb, :] * cw[j:j + 1, :]
        convbuf[c, 0:CONV_PAD, :] = convbuf[c, tb:tb + CONV_PAD, :]
        qkv = _silu(acc)
        qh = qkv[:, 0:HEAD_DIM]
        kh = qkv[:, HEAD_DIM:2 * HEAD_DIM]
        dq[c] = qh * (lax.rsqrt(jnp.sum(qh * qh, axis=-1, keepdims=True) + L2_EPS) * QK_SCALE)
        dk[c] = kh * lax.rsqrt(jnp.sum(kh * kh, axis=-1, keepdims=True) + L2_EPS)
        dv[c] = qkv[:, 2 * HEAD_DIM:3 * HEAD_DIM]
        lg = lbl_ref[c]
        e = jnp.exp(lg - jnp.max(lg, axis=0, keepdims=True))
        lb = e[0:1, :] / jnp.sum(e, axis=0, keepdims=True)
        f = lb + (1.0 - lb) * _sigmoid(rawhg[c, :, HEAD_DIM:2 * HEAD_DIM])
        hlf[c] = jnp.log(f)
        hf[c] = f
        hk[c] = 1.0 - f
        hq[c] = _silu(rawhg[c, :, 0:HEAD_DIM]) * QK_SCALE
        hv[c] = rawhg[c, :, 2 * HEAD_DIM:3 * HEAD_DIM]

    project_head(0)

    def cumulate_forget(h0):
        lf_hi, lf_lo = _split(jnp.concatenate([hlf[h0], hlf[h0 + 1]], axis=1))
        bc = _dot(bdtril_ref[...], lf_hi) + _dot(bdtril_ref[...], lf_lo)
        hb[h0] = bc[:, 0:HEAD_DIM]
        hb[h0 + 1] = bc[:, HEAD_DIM:2 * HEAD_DIM]

    for c in range(1, HEADS):
        head_epilogue(c - 1)
        project_head(c)
        if c % 2 == 0:
            cumulate_forget(c - 2)
    head_epilogue(HEADS - 1)
    cumulate_forget(HEADS - 2)

    pab = _dot(xbuf[...],wab_ref[...])
    g_col = -jnp.exp(alogr_ref[...]) * _softplus(pab + dtbr_ref[...])
    lane = lax.broadcasted_iota(jnp.int32, pab.shape, 1)
    gbuf[...] = jnp.where(lane < HEADS, _sigmoid(pab), g_col)
    pabt = _dot_nt(wabt_ref[...], xbuf[...])
    g_row = -jnp.exp(alogc_ref[...]) * _softplus(pabt + dtbc_ref[...])
    gr_hi, gr_lo = _split(g_row)
    gc_row = _dot(gr_hi, bdtriu_ref[...]) + _dot(gr_lo, bdtriu_ref[...])
    for c in range(n_chunks):
        gcrow[c] = gc_row[:, c * CHUNK:(c + 1) * CHUNK]

    ri = lax.broadcasted_iota(jnp.int32, (CHUNK, CHUNK), 0)
    ci = lax.broadcasted_iota(jnp.int32, (CHUNK, CHUNK), 1)
    causal = ri >= ci
    strict = ri > ci
    trow = lax.broadcasted_iota(jnp.int32, (CHUNK, HEAD_DIM), 0)

    n_lev = len(_LEVELS)
    heads = range(HEADS)
    hsl = [slice(h * HEAD_DIM, (h + 1) * HEAD_DIM) for h in heads]

    def level_mask(l):
        return masks_ref[l * CHUNK:(l + 1) * CHUNK, :]

    units = [(u, h) for u in range(PREP_UNROLL) for h in heads]
    n_units = range(len(units))

    def prepare_chunks(j, carry):
        cs = [j * PREP_UNROLL + u for u in range(PREP_UNROLL)]
        rows = [pl.ds(pl.multiple_of(c * CHUNK, CHUNK), CHUNK) for c in cs]
        wexp = wexp_ref[...]
        gb, eg, gcr = [], [], []
        for u in range(PREP_UNROLL):
            gb.append(gbuf[rows[u], :])
            g_hi, g_lo = _split(gb[u])
            eg.append(_dot(wexp[0:2 * CHUNK], g_hi) + _dot(wexp[0:2 * CHUNK], g_lo))
            gcr.append(gcrow[cs[u]])

        def ld(ref, i):
            u, h = units[i]
            return ref[h, rows[u], :]

        dkk = [ld(dk, i) for i in n_units]
        dkb = [_bf(k) for k in dkk]
        beta = [gb[u][:, h:h + 1] for u, h in units]
        gcol = [eg[u][0:CHUNK, HEADS + h:HEADS + h + 1] for u, h in units]
        decay = [jnp.where(causal, jnp.exp(jnp.minimum(gcol[i] - gcr[u][HEADS + h:HEADS + h + 1, :], 0.0)), 0.0)
                 for i, (u, h) in enumerate(units)]
        kbeta = [dkk[i] * beta[i] for i in n_units]
        lmat = [jnp.where(strict, _dot_nt(_bf(kbeta[i]), dkb[i]) * decay[i], 0.0) for i in n_units]
        for i, (u, h) in enumerate(units):
            qkb[cs[u] * HEADS + h] = _bf(_dot_nt(_bf(ld(dq, i)), dkb[i]) * decay[i])

        hqv = [ld(hq, i) for i in n_units]
        hkv = [ld(hk, i) for i in n_units]
        sc = [level_mask(n_lev) * _dot_nt(_bf(hqv[i]), _bf(hkv[i])) for i in n_units]
        nmat = [-(level_mask(n_lev - 1) * lmat[i]) for i in n_units]

        hbv = [ld(hb, i) for i in n_units]

        def level_factor(i, l):
            m = _LEVELS[l]
            if m >= SUBLANES // 2:
                ref = jnp.concatenate(
                    [jnp.broadcast_to(hbv[i][r:r + 1, :], (SUBLANES, HEAD_DIM))
                     for r in [(g * SUBLANES // (2 * m)) * (2 * m) + m - 1 for g in range(CHUNK // SUBLANES)]],
                    axis=0)
                return jnp.exp(-jnp.abs(hbv[i] - ref))
            f = ld(hf, i)
            if m == 1:
                return jnp.where(trow % 2 == 1, f, 1.0)
            f_prev = pltpu.roll(f, 1, 0)
            f_next = pltpu.roll(f, CHUNK - 1, 0)
            return jnp.where(trow % 4 == 0, f_next,
                             jnp.where(trow % 4 == 1, 1.0, jnp.where(trow % 4 == 2, f, f_prev * f)))

        def score_level(l):
            for i in n_units:
                z = level_factor(i, l)
                sc[i] = sc[i] + level_mask(l) * _dot_nt(_bf(hqv[i] * z), _bf(hkv[i] * z))

        score_level(n_lev - 1)
        for l in range(n_lev - 2, -1, -1):
            cm = [level_mask(l) * lmat[i] for i in n_units]
            nb = [_bf(nmat[i]) for i in n_units]
            xm = [cm[i] + _dot(nb[i], _bf(cm[i])) for i in n_units]
            score_level(l)
            for i in n_units:
                nmat[i] = nmat[i] - (xm[i] + _dot(_bf(xm[i]), nb[i]))

        hvb = [_bf(ld(hv, i)) for i in n_units]
        for i, (u, h) in enumerate(units):
            oa[rows[u], hsl[h]] = _dot(_bf(sc[i]), hvb[i])
        for i, (u, h) in enumerate(units):
            kd = hkv[i] * jnp.exp(hbv[i][CHUNK - 1:CHUNK, :] - hbv[i])
            kvinc[cs[u] * HEADS + h] = _dot_tn(hvb[i], _bf(kd))
        for i, (u, h) in enumerate(units):
            qin[rows[u], hsl[h]] = _bf(hqv[i] * jnp.exp(hbv[i]))
            hdec[cs[u], :, hsl[h]] = jnp.exp(hbv[i][CHUNK - 1:CHUNK, :])

        for i, (u, h) in enumerate(units):
            eg_col = jnp.exp(gcol[i])
            rhs = jnp.concatenate([ld(dv, i) * beta[i], kbeta[i] * eg_col], axis=1)
            uw = rhs + _dot(_bf(nmat[i]), _bf(rhs))
            du[rows[u], hsl[h]] = uw[:, 0:HEAD_DIM]
            wq[cs[u], 0:CHUNK, hsl[h]] = _bf(uw[:, HEAD_DIM:2 * HEAD_DIM])
            wq[cs[u], CHUNK:2 * CHUNK, hsl[h]] = _bf(ld(dq, i) * eg_col)
            kdec[rows[u], hsl[h]] = _bf(dkk[i] * jnp.exp(eg[u][CHUNK:2 * CHUNK, HEADS + h:HEADS + h + 1]))
        for u in range(PREP_UNROLL):
            gl[cs[u]] = jnp.exp(eg[u][CHUNK - 1:CHUNK, :])
        return carry

    lax.fori_loop(0, n_chunks // PREP_UNROLL, prepare_chunks, 0)

    gate_tiles = ([(t, sg, t, _silu) for t in range(0, KW, GATE_TILE)]
                  + [(KW + t, sz, t, _silu) for t in range(0, KW, GATE_TILE)]
                  + [(2 * KW + t, sga, t, _sigmoid) for t in range(0, D_MODEL, GATE_TILE)]
                  + [(2 * KW + D_MODEL + t, sgb, t, _sigmoid) for t in range(0, D_MODEL, GATE_TILE)])
    n_slots = 2 * n_chunks

    def gate_slot(slot):
        for k in range(slot * len(gate_tiles) // n_slots, (slot + 1) * len(gate_tiles) // n_slots):
            src, dst, off, act = gate_tiles[k]
            dst[:, off:off + GATE_TILE] = act(_dot(xbuf[...],wgates_ref[:, src:src + GATE_TILE]))

    for c in range(n_chunks):
        rows = slice(c * CHUNK, (c + 1) * CHUNK)
        s = [sdn[h] for h in heads]
        wqc = wq[c]
        ws = [_dot(wqc[:, hsl[h]], _bf(s[h])) for h in heads]
        st = [shg[h] for h in heads]
        dec = hdec[c]
        for h in heads:
            oa[rows, hsl[h]] = oa[rows, hsl[h]] + _dot_nt(qin[rows, hsl[h]], _bf(st[h]))
            shg[h] = st[h] * dec[:, hsl[h]] + kvinc[c * HEADS + h]
        gate_slot(2 * c)
        vnb = [_bf(du[rows, hsl[h]] - ws[h][0:CHUNK]) for h in heads]
        glc = gl[c]
        for h in heads:
            ob[rows, hsl[h]] = ws[h][CHUNK:2 * CHUNK] + _dot(qkb[c * HEADS + h], vnb[h])
        for h in heads:
            sdn[h] = s[h] * glc[:, HEADS + h:HEADS + h + 1] + _dot_tn(kdec[rows, hsl[h]], vnb[h])
        gate_slot(2 * c + 1)

    def gated_norm(o_ref, w_row, gate):
        parts = []
        for h in range(HEADS):
            oh = o_ref[:, h * HEAD_DIM:(h + 1) * HEAD_DIM]
            parts.append(oh * lax.rsqrt(jnp.mean(oh * oh, axis=-1, keepdims=True) + RMS_EPS))
        return jnp.concatenate(parts, axis=1) * w_row * gate

    na = gated_norm(oa, hgnw_ref[...], sg[...])
    nb_ = gated_norm(ob, dnnw_ref[...], sz[...])
    merged = sga[...] * _dot(_bf(na), wa_ref[...]) + sgb[...] * _dot(_bf(nb_), wb_ref[...])
    mix = _dot(_bf(merged), wo_ref[...])
    out_ref[...] = _layer_norm(ALPHA * x_ref[...] + mix, g1_ref[...], b1_ref[...])


def _mlp_kernel(h_ref, wup_ref, wdown_ref, g2_ref, b2_ref, out_ref):
    h = h_ref[...]
    hb = _bf(h)
    acc = jnp.zeros(h.shape, jnp.float32)
    for j in range(D_FF // FF_BLOCK):
        up = _dot(hb, wup_ref[:, j * FF_BLOCK:(j + 1) * FF_BLOCK])
        act = jnp.square(jnp.maximum(up, 0.0))
        acc = acc + _dot(_bf(act), wdown_ref[j * FF_BLOCK:(j + 1) * FF_BLOCK, :])
    out_ref[...] = _layer_norm(ALPHA * h + acc, g2_ref[...], b2_ref[...])


def _resident(shape):
    nd = len(shape)
    return pl.BlockSpec(shape, lambda *_: (0,) * nd, pipeline_mode=pl.Buffered(1))


def _mixer_call(x, lbl, wheads, wgates, wab, wabt, convw, hgnw, alogr, dtbr, alogc, dtbc, dnnw,
                wa, wb, wo, g1, b1):
    bsz, seq, _ = x.shape
    tb = min(TOKEN_BLOCK, seq)
    assert seq % tb == 0 and tb % (CHUNK * PREP_UNROLL) == 0
    n_chunks = tb // CHUNK
    wexp = jnp.asarray(_WEXP, jnp.bfloat16)
    masks = jnp.asarray(_MASKS, jnp.float32)
    tt = np.arange(tb)
    bdtriu = jnp.asarray(((tt[:, None] <= tt[None, :]) & ((tt[:, None] // CHUNK) == (tt[None, :] // CHUNK))),
                         jnp.bfloat16)
    consts = (lbl, wheads, wgates, wab, wabt, convw, hgnw, alogr, dtbr, alogc, dtbc, dnnw, wa, wb, wo,
              g1, b1, wexp, masks, bdtriu, bdtriu.T)
    f32 = jnp.float32
    blk = functools.partial(pltpu.VMEM, (tb, KW))
    bf16 = jnp.bfloat16
    per_head = functools.partial(pltpu.VMEM, (HEADS, tb, HEAD_DIM))
    scratch = [pltpu.VMEM((tb, D_MODEL), bf16)]
    scratch += [per_head(f32) for _ in range(9)]
    scratch += [blk(f32) for _ in range(3)]
    scratch += [pltpu.VMEM((tb, 128), f32),
                pltpu.VMEM((n_chunks, 8, CHUNK), f32),
                pltpu.VMEM((HEADS, tb + CONV_PAD, 3 * HEAD_DIM), f32),
                pltpu.VMEM((HEADS, tb, 3 * HEAD_DIM), f32),
                blk(bf16), blk(bf16),
                pltpu.VMEM((n_chunks, 2 * CHUNK, KW), bf16),
                pltpu.VMEM((n_chunks * HEADS, CHUNK, CHUNK), bf16),
                pltpu.VMEM((n_chunks * HEADS, HEAD_DIM, HEAD_DIM), f32),
                pltpu.VMEM((n_chunks, 1, KW), f32),
                pltpu.VMEM((n_chunks, 1, 128), f32),
                blk(f32), blk(f32),
                pltpu.VMEM((tb, D_MODEL), f32),
                pltpu.VMEM((tb, D_MODEL), f32),
                pltpu.VMEM((HEADS, HEAD_DIM, HEAD_DIM), f32),
                pltpu.VMEM((HEADS, HEAD_DIM, HEAD_DIM), f32)]
    return pl.pallas_call(
        _mixer_kernel,
        grid=(bsz, seq // tb),
        in_specs=[pl.BlockSpec((None, tb, D_MODEL), lambda b, i: (b, i, 0))]
                 + [_resident(c.shape) for c in consts],
        out_specs=pl.BlockSpec((None, tb, D_MODEL), lambda b, i: (b, i, 0)),
        out_shape=jax.ShapeDtypeStruct(x.shape, f32),
        scratch_shapes=scratch,
        compiler_params=pltpu.CompilerParams(dimension_semantics=("arbitrary", "arbitrary"),
                                             vmem_limit_bytes=VMEM_LIMIT_BYTES),
        name="token_mixer",
    )(x, *consts)


def _mlp_call(h, wup, wdown, g2, b2):
    m = h.shape[0]
    tm = min(MLP_BLOCK, m)
    assert m % tm == 0
    consts = (wup, wdown, g2, b2)
    return pl.pallas_call(
        _mlp_kernel,
        grid=(m // tm,),
        in_specs=[pl.BlockSpec((tm, D_MODEL), lambda i: (i, 0))] + [_resident(c.shape) for c in consts],
        out_specs=pl.BlockSpec((tm, D_MODEL), lambda i: (i, 0)),
        out_shape=jax.ShapeDtypeStruct(h.shape, jnp.float32),
        compiler_params=pltpu.CompilerParams(dimension_semantics=("arbitrary",),
                                             vmem_limit_bytes=VMEM_LIMIT_BYTES),
        name="relu2_mlp",
    )(h, *consts)


def kernel(x, hg_lb_logits, w_in, conv_w, hg_norm_w, dn_A_log, dn_dt_bias, dn_norm_w, w_branch_a, w_branch_b, w_o, ln1_g, ln1_b, w_up, w_down, ln2_g, ln2_b):
    bsz, seq, _ = x.shape
    f32 = jnp.float32
    bf16 = jnp.bfloat16
    o = _OFFS
    l = 0
    w = w_in[l]
    def by_head(cols):
        return cols.reshape(D_MODEL, 3, HEADS, HEAD_DIM)
    wheads = jnp.concatenate([by_head(w[:, o[4]:o[4] + 3 * KW]), by_head(w[:, o[0]:o[3]])], axis=1)
    wheads = wheads.transpose(2, 0, 1, 3).reshape(HEADS, D_MODEL, 6 * HEAD_DIM).astype(bf16)
    wgates = jnp.concatenate([w[:, o[3]:o[4]], w[:, o[5]:o[6]], w[:, o[8]:o[10]]], axis=1).astype(bf16)
    wab_f = w[:, o[6]:o[8]]
    wab = jnp.pad(wab_f, ((0, 0), (0, 128 - 2 * HEADS))).astype(bf16)
    wabt = wab_f.T.astype(bf16)
    convw = conv_w[l].astype(f32).reshape(CONV_W, 3, HEADS, HEAD_DIM).transpose(2, 0, 1, 3)
    convw = convw.reshape(HEADS, CONV_W, 3 * HEAD_DIM)
    lbl = hg_lb_logits.astype(f32).reshape(DEPTH + 1, HEADS, HEAD_DIM).transpose(1, 0, 2)
    zeros4 = jnp.zeros((HEADS,), f32)
    alog8 = jnp.concatenate([zeros4, dn_A_log[l].astype(f32)])
    dtb8 = jnp.concatenate([zeros4, dn_dt_bias[l].astype(f32)])
    alogr = jnp.pad(alog8, (0, 128 - 2 * HEADS)).reshape(1, 128)
    dtbr = jnp.pad(dtb8, (0, 128 - 2 * HEADS)).reshape(1, 128)
    h1 = _mixer_call(
        x, lbl, wheads, wgates, wab, wabt, convw,
        hg_norm_w[l].reshape(1, KW), alogr, dtbr, alog8.reshape(8, 1), dtb8.reshape(8, 1),
        jnp.tile(dn_norm_w[l], HEADS).reshape(1, KW),
        w_branch_a[l].astype(bf16), w_branch_b[l].astype(bf16), w_o[l].astype(bf16),
        ln1_g[l].reshape(1, D_MODEL), ln1_b[l].reshape(1, D_MODEL))
    out = _mlp_call(h1.reshape(bsz * seq, D_MODEL), w_up[l].astype(bf16), w_down[l].astype(bf16),
                    ln2_g[l].reshape(1, D_MODEL), ln2_b[l].reshape(1, D_MODEL))
    return out.reshape(bsz, seq, D_MODEL)
```

```python
import functools

import numpy as np
import jax
import jax.numpy as jnp
from jax import lax
from jax.experimental import pallas as pl
from jax.experimental.pallas import tpu as pltpu

D_MODEL = 1024
HEADS = 4
HEAD_DIM = 128
KW = HEADS * HEAD_DIM
CHUNK = 64
CONV_W = 4
D_FF = 4 * D_MODEL
DEPTH = 1
ALPHA = (2 * DEPTH) ** 0.25
LN_EPS = 1e-5
RMS_EPS = 1e-6
L2_EPS = 1e-6
QK_SCALE = HEAD_DIM ** -0.5
NEG_LOG2E = -1.4426950408889634

TOKEN_BLOCK = 256
PREP_UNROLL = 4
GATE_TILE = 256
MLP_BLOCK = 1024
FF_BLOCK = 1024
SUBLANES = 8
CONV_PAD = 8
VMEM_LIMIT_BYTES = 56 * 1024 * 1024

_SIZES = (KW, KW, KW, KW, 3 * KW, KW, HEADS, HEADS, D_MODEL, D_MODEL)
_OFFS = np.concatenate([[0], np.cumsum(_SIZES)]).tolist()

COL_HG_QFI = _OFFS[0]
COL_HG_G = _OFFS[3]
COL_DN_QKV = _OFFS[4]
COL_DN_Z = _OFFS[5]
COL_GATE_A = _OFFS[6]
COL_GATE_B = _OFFS[6] + D_MODEL
COL_AB = _OFFS[6] + 2 * D_MODEL

_LEVELS = (32, 16, 8, 4, 2, 1)


def _build_constants():
    t = np.arange(CHUNK)[:, None]
    r = np.arange(CHUNK)[None, :]
    rows = [r <= t, r > t]
    masks = []
    for m in _LEVELS:
        last_key = (t // (2 * m)) * (2 * m) + m - 1
        is_query = (t % (2 * m)) >= m
        rows.append((is_query & (r > last_key) & (r <= t)) | (~is_query & (r > t) & (r <= last_key)))
        masks.append(((t // (2 * m)) == (r // (2 * m))) & is_query & ((r % (2 * m)) < m))
    masks.append(t == r)
    wexp = np.concatenate(rows, 0).astype(np.float32)
    masks = np.concatenate(masks, 0).astype(np.float32)
    return wexp, masks


_WEXP, _MASKS = _build_constants()


def _dot(a, b):
    return jnp.dot(a, b, preferred_element_type=jnp.float32)


def _dot_nt(a, b):
    return lax.dot_general(a, b, (((1,), (1,)), ((), ())), preferred_element_type=jnp.float32)


def _dot_tn(a, b):
    return lax.dot_general(a, b, (((0,), (0,)), ((), ())), preferred_element_type=jnp.float32)


def _bf(x):
    return x.astype(jnp.bfloat16)


def _split(x):
    hi = _bf(x)
    return hi, _bf(x - hi.astype(jnp.float32))


def _sigmoid(x):
    return 1.0 / (1.0 + jnp.exp2(x * NEG_LOG2E))


def _silu(x):
    return x * _sigmoid(x)


def _softplus(x):
    return jnp.maximum(x, 0.0) + jnp.log1p(jnp.exp(-jnp.abs(x)))


def _layer_norm(x, g, b):
    mu = jnp.mean(x, axis=-1, keepdims=True)
    xc = x - mu
    var = jnp.mean(xc * xc, axis=-1, keepdims=True)
    return xc * lax.rsqrt(var + LN_EPS) * g + b


def _mixer_kernel(x_ref, lbl_ref, win_ref, wabt_ref, convw_ref, hgnw_ref,
                  alogr_ref, dtbr_ref, alogc_ref, dtbc_ref, dnnw_ref, wa_ref, wb_ref, wo_ref,
                  g1_ref, b1_ref, wexp_ref, masks_ref, bdtriu_ref, bdtril_ref, out_ref,
                  xbuf, hq, hk, hv, hlf, hf, hb, dq, dk, dv, oa, ob, du, gbuf, gcrow, convbuf, rawhg, qin, kdec, wq, qkb,
                  kvinc, hdec, gl, sg, sz, sga, sgb, shg, sdn):
    tb = x_ref.shape[0]
    n_chunks = tb // CHUNK

    @pl.when(pl.program_id(1) == 0)
    def _reset_carries():
        shg[...] = jnp.zeros_like(shg)
        sdn[...] = jnp.zeros_like(sdn)
        convbuf[:, 0:CONV_PAD, :] = jnp.zeros((HEADS, CONV_PAD, 3 * HEAD_DIM), jnp.float32)

    xbuf[...] = _bf(x_ref[...])

    def project_head(c):
        cols = ([COL_DN_QKV + k * KW + c * HEAD_DIM for k in range(3)]
                + [COL_HG_QFI + k * KW + c * HEAD_DIM for k in range(3)])
        w_head = jnp.concatenate([win_ref[:, o:o + HEAD_DIM] for o in cols], axis=1)
        raw = _dot(xbuf[...], w_head)
        convbuf[c, CONV_PAD:CONV_PAD + tb, :] = raw[:, 0:3 * HEAD_DIM]
        rawhg[c] = raw[:, 3 * HEAD_DIM:6 * HEAD_DIM]

    def head_epilogue(c):
        cw = jnp.concatenate([convw_ref[:, k * KW + c * HEAD_DIM:k * KW + (c + 1) * HEAD_DIM]
                              for k in range(3)], axis=1)
        base = CONV_PAD - (CONV_W - 1)
        acc = convbuf[c, base:base + tb, :] * cw[0:1, :]
        for j in range(1, CONV_W):
            acc = acc + convbuf[c, base + j:base + j + tb, :] * cw[j:j + 1, :]
        convbuf[c, 0:CONV_PAD, :] = convbuf[c, tb:tb + CONV_PAD, :]
        qkv = _silu(acc)
        qh = qkv[:, 0:HEAD_DIM]
        kh = qkv[:, HEAD_DIM:2 * HEAD_DIM]
        dq[c] = qh * (lax.rsqrt(jnp.sum(qh * qh, axis=-1, keepdims=True) + L2_EPS) * QK_SCALE)
        dk[c] = kh * lax.rsqrt(jnp.sum(kh * kh, axis=-1, keepdims=True) + L2_EPS)
        dv[c] = qkv[:, 2 * HEAD_DIM:3 * HEAD_DIM]
        lg = lbl_ref[:, c * HEAD_DIM:(c + 1) * HEAD_DIM]
        e = jnp.exp(lg - jnp.max(lg, axis=0, keepdims=True))
        lb = e[0:1, :] / jnp.sum(e, axis=0, keepdims=True)
        f = lb + (1.0 - lb) * _sigmoid(rawhg[c, :, HEAD_DIM:2 * HEAD_DIM])
        hlf[c] = jnp.log(f)
        hf[c] = f
        hk[c] = 1.0 - f
        hq[c] = _silu(rawhg[c, :, 0:HEAD_DIM]) * QK_SCALE
        hv[c] = rawhg[c, :, 2 * HEAD_DIM:3 * HEAD_DIM]

    project_head(0)

    def cumulate_forget(h0):
        lf_hi, lf_lo = _split(jnp.concatenate([hlf[h0], hlf[h0 + 1]], axis=1))
        bc = _dot(bdtril_ref[...], lf_hi) + _dot(bdtril_ref[...], lf_lo)
        hb[h0] = bc[:, 0:HEAD_DIM]
        hb[h0 + 1] = bc[:, HEAD_DIM:2 * HEAD_DIM]

    for c in range(1, HEADS):
        head_epilogue(c - 1)
        project_head(c)
        if c % 2 == 0:
            cumulate_forget(c - 2)
    head_epilogue(HEADS - 1)
    cumulate_forget(HEADS - 2)

    pab = _dot(xbuf[...], win_ref[:, COL_AB:COL_AB + 128])
    g_col = -jnp.exp(alogr_ref[...]) * _softplus(pab + dtbr_ref[...])
    lane = lax.broadcasted_iota(jnp.int32, pab.shape, 1)
    gbuf[...] = jnp.where(lane < HEADS, _sigmoid(pab), g_col)
    pabt = _dot_nt(wabt_ref[...], xbuf[...])
    g_row = -jnp.exp(alogc_ref[...]) * _softplus(pabt + dtbc_ref[...])
    gr_hi, gr_lo = _split(g_row)
    gc_row = _dot(gr_hi, bdtriu_ref[...]) + _dot(gr_lo, bdtriu_ref[...])
    for c in range(n_chunks):
        gcrow[c] = gc_row[:, c * CHUNK:(c + 1) * CHUNK]

    ri = lax.broadcasted_iota(jnp.int32, (CHUNK, CHUNK), 0)
    ci = lax.broadcasted_iota(jnp.int32, (CHUNK, CHUNK), 1)
    causal = ri >= ci
    strict = ri > ci
    trow = lax.broadcasted_iota(jnp.int32, (CHUNK, HEAD_DIM), 0)

    n_lev = len(_LEVELS)
    heads = range(HEADS)
    hsl = [slice(h * HEAD_DIM, (h + 1) * HEAD_DIM) for h in heads]

    def level_mask(l):
        return masks_ref[l * CHUNK:(l + 1) * CHUNK, :]

    units = [(u, h) for u in range(PREP_UNROLL) for h in heads]
    n_units = range(len(units))

    def prepare_chunks(j, carry):
        cs = [j * PREP_UNROLL + u for u in range(PREP_UNROLL)]
        rows = [pl.ds(pl.multiple_of(c * CHUNK, CHUNK), CHUNK) for c in cs]
        wexp = wexp_ref[...]
        gb, eg, gcr = [], [], []
        for u in range(PREP_UNROLL):
            gb.append(gbuf[rows[u], :])
            g_hi, g_lo = _split(gb[u])
            eg.append(_dot(wexp[0:2 * CHUNK], g_hi) + _dot(wexp[0:2 * CHUNK], g_lo))
            gcr.append(gcrow[cs[u]])

        def ld(ref, i):
            u, h = units[i]
            return ref[h, rows[u], :]

        dkk = [ld(dk, i) for i in n_units]
        dkb = [_bf(k) for k in dkk]
        beta = [gb[u][:, h:h + 1] for u, h in units]
        gcol = [eg[u][0:CHUNK, HEADS + h:HEADS + h + 1] for u, h in units]
        decay = [jnp.where(causal, jnp.exp(jnp.minimum(gcol[i] - gcr[u][HEADS + h:HEADS + h + 1, :], 0.0)), 0.0)
                 for i, (u, h) in enumerate(units)]
        kbeta = [dkk[i] * beta[i] for i in n_units]
        lmat = [jnp.where(strict, _dot_nt(_bf(kbeta[i]), dkb[i]) * decay[i], 0.0) for i in n_units]
        for i, (u, h) in enumerate(units):
            qkb[cs[u] * HEADS + h] = _bf(_dot_nt(_bf(ld(dq, i)), dkb[i]) * decay[i])

        hqv = [ld(hq, i) for i in n_units]
        hkv = [ld(hk, i) for i in n_units]
        hqb = [_bf(q) for q in hqv]
        hkb = [_bf(k) for k in hkv]
        sc = [level_mask(n_lev) * _dot_nt(hqb[i], hkb[i]) for i in n_units]
        nmat = [-(level_mask(n_lev - 1) * lmat[i]) for i in n_units]

        hbv = [ld(hb, i) for i in n_units]

        def level_factor(i, l):
            m = _LEVELS[l]
            if m >= SUBLANES // 2:
                ref = jnp.concatenate(
                    [jnp.broadcast_to(hbv[i][r:r + 1, :], (SUBLANES, HEAD_DIM))
                     for r in [(g * SUBLANES // (2 * m)) * (2 * m) + m - 1 for g in range(CHUNK // SUBLANES)]],
                    axis=0)
                return jnp.exp(-jnp.abs(hbv[i] - ref))
            f = ld(hf, i)
            if m == 1:
                return jnp.where(trow % 2 == 1, f, 1.0)
            f_prev = pltpu.roll(f, 1, 0)
            f_next = pltpu.roll(f, CHUNK - 1, 0)
            return jnp.where(trow % 4 == 0, f_next,
                             jnp.where(trow % 4 == 1, 1.0, jnp.where(trow % 4 == 2, f, f_prev * f)))

        def score_level(l):
            for i in n_units:
                zb = _bf(level_factor(i, l))
                sc[i] = sc[i] + level_mask(l) * _dot_nt(hqb[i] * zb, hkb[i] * zb)

        score_level(n_lev - 1)
        for l in range(n_lev - 2, -1, -1):
            cm = [level_mask(l) * lmat[i] for i in n_units]
            nb = [_bf(nmat[i]) for i in n_units]
            xm = [cm[i] + _dot(nb[i], _bf(cm[i])) for i in n_units]
            score_level(l)
            for i in n_units:
                nmat[i] = nmat[i] - (xm[i] + _dot(_bf(xm[i]), nb[i]))

        hvb = [_bf(ld(hv, i)) for i in n_units]
        for i, (u, h) in enumerate(units):
            oa[rows[u], hsl[h]] = _dot(_bf(sc[i]), hvb[i])
        for i, (u, h) in enumerate(units):
            kd = hkv[i] * jnp.exp(hbv[i][CHUNK - 1:CHUNK, :] - hbv[i])
            kvinc[cs[u] * HEADS + h] = _dot_tn(hvb[i], _bf(kd))
        for i, (u, h) in enumerate(units):
            qin[rows[u], hsl[h]] = _bf(hqv[i] * jnp.exp(hbv[i]))
            hdec[cs[u], :, hsl[h]] = jnp.exp(hbv[i][CHUNK - 1:CHUNK, :])

        for i, (u, h) in enumerate(units):
            eg_col = jnp.exp(gcol[i])
            rhs = jnp.concatenate([ld(dv, i) * beta[i], kbeta[i] * eg_col], axis=1)
            uw = rhs + _dot(_bf(nmat[i]), _bf(rhs))
            du[rows[u], hsl[h]] = uw[:, 0:HEAD_DIM]
            wq[cs[u], 0:CHUNK, hsl[h]] = _bf(uw[:, HEAD_DIM:2 * HEAD_DIM])
            wq[cs[u], CHUNK:2 * CHUNK, hsl[h]] = _bf(ld(dq, i) * eg_col)
            kdec[rows[u], hsl[h]] = _bf(dkk[i] * jnp.exp(eg[u][CHUNK:2 * CHUNK, HEADS + h:HEADS + h + 1]))
        for u in range(PREP_UNROLL):
            gl[cs[u]] = jnp.exp(eg[u][CHUNK - 1:CHUNK, :])
        return carry

    lax.fori_loop(0, n_chunks // PREP_UNROLL, prepare_chunks, 0)

    gate_tiles = ([(COL_HG_G + t, sg, t, _silu) for t in range(0, KW, GATE_TILE)]
                  + [(COL_DN_Z + t, sz, t, _silu) for t in range(0, KW, GATE_TILE)]
                  + [(COL_GATE_A + t, sga, t, _sigmoid) for t in range(0, D_MODEL, GATE_TILE)]
                  + [(COL_GATE_B + t, sgb, t, _sigmoid) for t in range(0, D_MODEL, GATE_TILE)])
    n_slots = 2 * n_chunks

    def gate_slot(slot):
        for k in range(slot * len(gate_tiles) // n_slots, (slot + 1) * len(gate_tiles) // n_slots):
            src, dst, off, act = gate_tiles[k]
            dst[:, off:off + GATE_TILE] = act(_dot(xbuf[...], win_ref[:, src:src + GATE_TILE]))

    for c in range(n_chunks):
        rows = slice(c * CHUNK, (c + 1) * CHUNK)
        s = [sdn[h] for h in heads]
        wqc = wq[c]
        ws = [_dot(wqc[:, hsl[h]], _bf(s[h])) for h in heads]
        st = [shg[h] for h in heads]
        dec = hdec[c]
        for h in heads:
            oa[rows, hsl[h]] = oa[rows, hsl[h]] + _dot_nt(qin[rows, hsl[h]], _bf(st[h]))
            shg[h] = st[h] * dec[:, hsl[h]] + kvinc[c * HEADS + h]
        gate_slot(2 * c)
        vnb = [_bf(du[rows, hsl[h]] - ws[h][0:CHUNK]) for h in heads]
        glc = gl[c]
        for h in heads:
            ob[rows, hsl[h]] = ws[h][CHUNK:2 * CHUNK] + _dot(qkb[c * HEADS + h], vnb[h])
        for h in heads:
            sdn[h] = s[h] * glc[:, HEADS + h:HEADS + h + 1] + _dot_tn(kdec[rows, hsl[h]], vnb[h])
        gate_slot(2 * c + 1)

    def gated_norm(o_ref, w_row, gate):
        parts = []
        for h in range(HEADS):
            oh = o_ref[:, h * HEAD_DIM:(h + 1) * HEAD_DIM]
            parts.append(oh * lax.rsqrt(jnp.mean(oh * oh, axis=-1, keepdims=True) + RMS_EPS))
        return jnp.concatenate(parts, axis=1) * w_row * gate

    na = gated_norm(oa, hgnw_ref[...], sg[...])
    nb_ = gated_norm(ob, dnnw_ref[...], sz[...])
    merged = sga[...] * _dot(_bf(na), wa_ref[...]) + sgb[...] * _dot(_bf(nb_), wb_ref[...])
    mix = _dot(_bf(merged), wo_ref[...])
    out_ref[...] = _layer_norm(ALPHA * x_ref[...] + mix, g1_ref[...], b1_ref[...])


def _mlp_kernel(h_ref, wup_ref, wdown_ref, g2_ref, b2_ref, out_ref):
    h = h_ref[...]
    hb = _bf(h)
    acc = jnp.zeros(h.shape, jnp.float32)
    for j in range(D_FF // FF_BLOCK):
        up = _dot(hb, wup_ref[:, j * FF_BLOCK:(j + 1) * FF_BLOCK])
        act = jnp.square(jnp.maximum(up, 0.0))
        acc = acc + _dot(_bf(act), wdown_ref[j * FF_BLOCK:(j + 1) * FF_BLOCK, :])
    out_ref[...] = _layer_norm(ALPHA * h + acc, g2_ref[...], b2_ref[...])


def _resident(shape):
    nd = len(shape)
    return pl.BlockSpec(shape, lambda *_: (0,) * nd, pipeline_mode=pl.Buffered(1))


def _mixer_call(x, lbl, win, wabt, convw, hgnw, alogr, dtbr, alogc, dtbc, dnnw,
                wa, wb, wo, g1, b1):
    bsz, seq, _ = x.shape
    tb = min(TOKEN_BLOCK, seq)
    assert seq % tb == 0 and tb % (CHUNK * PREP_UNROLL) == 0
    n_chunks = tb // CHUNK
    wexp = jnp.asarray(_WEXP, jnp.bfloat16)
    masks = jnp.asarray(_MASKS, jnp.float32)
    tt = np.arange(tb)
    bdtriu = jnp.asarray(((tt[:, None] <= tt[None, :]) & ((tt[:, None] // CHUNK) == (tt[None, :] // CHUNK))),
                         jnp.bfloat16)
    consts = (lbl, win, wabt, convw, hgnw, alogr, dtbr, alogc, dtbc, dnnw, wa, wb, wo,
              g1, b1, wexp, masks, bdtriu, bdtriu.T)
    f32 = jnp.float32
    blk = functools.partial(pltpu.VMEM, (tb, KW))
    bf16 = jnp.bfloat16
    per_head = functools.partial(pltpu.VMEM, (HEADS, tb, HEAD_DIM))
    scratch = [pltpu.VMEM((tb, D_MODEL), bf16)]
    scratch += [per_head(f32) for _ in range(9)]
    scratch += [blk(f32) for _ in range(3)]
    scratch += [pltpu.VMEM((tb, 128), f32),
                pltpu.VMEM((n_chunks, 8, CHUNK), f32),
                pltpu.VMEM((HEADS, tb + CONV_PAD, 3 * HEAD_DIM), f32),
                pltpu.VMEM((HEADS, tb, 3 * HEAD_DIM), f32),
                blk(bf16), blk(bf16),
                pltpu.VMEM((n_chunks, 2 * CHUNK, KW), bf16),
                pltpu.VMEM((n_chunks * HEADS, CHUNK, CHUNK), bf16),
                pltpu.VMEM((n_chunks * HEADS, HEAD_DIM, HEAD_DIM), f32),
                pltpu.VMEM((n_chunks, 1, KW), f32),
                pltpu.VMEM((n_chunks, 1, 128), f32),
                blk(f32), blk(f32),
                pltpu.VMEM((tb, D_MODEL), f32),
                pltpu.VMEM((tb, D_MODEL), f32),
                pltpu.VMEM((HEADS, HEAD_DIM, HEAD_DIM), f32),
                pltpu.VMEM((HEADS, HEAD_DIM, HEAD_DIM), f32)]
    return pl.pallas_call(
        _mixer_kernel,
        grid=(bsz, seq // tb),
        in_specs=[pl.BlockSpec((None, tb, D_MODEL), lambda b, i: (b, i, 0))]
                 + [_resident(c.shape) for c in consts],
        out_specs=pl.BlockSpec((None, tb, D_MODEL), lambda b, i: (b, i, 0)),
        out_shape=jax.ShapeDtypeStruct(x.shape, f32),
        scratch_shapes=scratch,
        compiler_params=pltpu.CompilerParams(dimension_semantics=("arbitrary", "arbitrary"),
                                             vmem_limit_bytes=VMEM_LIMIT_BYTES),
        name="token_mixer",
    )(x, *consts)


def _mlp_call(h, wup, wdown, g2, b2):
    m = h.shape[0]
    tm = min(MLP_BLOCK, m)
    assert m % tm == 0
    consts = (wup, wdown, g2, b2)
    return pl.pallas_call(
        _mlp_kernel,
        grid=(m // tm,),
        in_specs=[pl.BlockSpec((tm, D_MODEL), lambda i: (i, 0))] + [_resident(c.shape) for c in consts],
        out_specs=pl.BlockSpec((tm, D_MODEL), lambda i: (i, 0)),
        out_shape=jax.ShapeDtypeStruct(h.shape, jnp.float32),
        compiler_params=pltpu.CompilerParams(dimension_semantics=("arbitrary",),
                                             vmem_limit_bytes=VMEM_LIMIT_BYTES),
        name="relu2_mlp",
    )(h, *consts)


def kernel(x, hg_lb_logits, w_in, conv_w, hg_norm_w, dn_A_log, dn_dt_bias, dn_norm_w, w_branch_a, w_branch_b, w_o, ln1_g, ln1_b, w_up, w_down, ln2_g, ln2_b):
    bsz, seq, _ = x.shape
    f32 = jnp.float32
    bf16 = jnp.bfloat16
    o = _OFFS
    l = 0
    w = w_in[l]
    wab_f = w[:, o[6]:o[8]]
    win = jnp.concatenate([w[:, :o[6]], w[:, o[8]:], jnp.pad(wab_f, ((0, 0), (0, 128 - 2 * HEADS)))],
                          axis=1).astype(bf16)
    wabt = wab_f.T.astype(bf16)
    zeros4 = jnp.zeros((HEADS,), f32)
    alog8 = jnp.concatenate([zeros4, dn_A_log[l].astype(f32)])
    dtb8 = jnp.concatenate([zeros4, dn_dt_bias[l].astype(f32)])
    alogr = jnp.pad(alog8, (0, 128 - 2 * HEADS)).reshape(1, 128)
    dtbr = jnp.pad(dtb8, (0, 128 - 2 * HEADS)).reshape(1, 128)
    h1 = _mixer_call(
        x, hg_lb_logits.astype(f32), win, wabt, conv_w[l].astype(f32),
        hg_norm_w[l].reshape(1, KW), alogr, dtbr, alog8.reshape(8, 1), dtb8.reshape(8, 1),
        jnp.tile(dn_norm_w[l], HEADS).reshape(1, KW),
        w_branch_a[l].astype(bf16), w_branch_b[l].astype(bf16), w_o[l].astype(bf16),
        ln1_g[l].reshape(1, D_MODEL), ln1_b[l].reshape(1, D_MODEL))
    out = _mlp_call(h1.reshape(bsz * seq, D_MODEL), w_up[l].astype(bf16), w_down[l].astype(bf16),
                    ln2_g[l].reshape(1, D_MODEL), ln2_b[l].reshape(1, D_MODEL))
    return out.reshape(bsz, seq, D_MODEL)
```

```python
import functools

import numpy as np
import jax
import jax.numpy as jnp
from jax import lax
from jax.experimental import pallas as pl
from jax.experimental.pallas import tpu as pltpu

D_MODEL = 1024
HEADS = 4
HEAD_DIM = 128
KW = HEADS * HEAD_DIM
CHUNK = 64
CONV_W = 4
D_FF = 4 * D_MODEL
DEPTH = 1
ALPHA = (2 * DEPTH) ** 0.25
LN_EPS = 1e-5
RMS_EPS = 1e-6
L2_EPS = 1e-6
QK_SCALE = HEAD_DIM ** -0.5
NEG_LOG2E = -1.4426950408889634

TOKEN_BLOCK = 256
PREP_UNROLL = 4
GATE_TILE = 256
MLP_BLOCK = 1024
FF_BLOCK = 1024
SUBLANES = 8
CONV_PAD = 8
VMEM_LIMIT_BYTES = 56 * 1024 * 1024

_SIZES = (KW, KW, KW, KW, 3 * KW, KW, HEADS, HEADS, D_MODEL, D_MODEL)
_OFFS = np.concatenate([[0], np.cumsum(_SIZES)]).tolist()

COL_HG_QFI = _OFFS[0]
COL_HG_G = _OFFS[3]
COL_DN_QKV = _OFFS[4]
COL_DN_Z = _OFFS[5]

_LEVELS = (32, 16, 8, 4, 2, 1)


def _build_constants():
    t = np.arange(CHUNK)[:, None]
    r = np.arange(CHUNK)[None, :]
    rows = [r <= t, r > t]
    masks = []
    for m in _LEVELS:
        last_key = (t // (2 * m)) * (2 * m) + m - 1
        is_query = (t % (2 * m)) >= m
        rows.append((is_query & (r > last_key) & (r <= t)) | (~is_query & (r > t) & (r <= last_key)))
        masks.append(((t // (2 * m)) == (r // (2 * m))) & is_query & ((r % (2 * m)) < m))
    masks.append(t == r)
    wexp = np.concatenate(rows, 0).astype(np.float32)
    masks = np.concatenate(masks, 0).astype(np.float32)
    return wexp, masks


_WEXP, _MASKS = _build_constants()


def _dot(a, b):
    return jnp.dot(a, b, preferred_element_type=jnp.float32)


def _dot_nt(a, b):
    return lax.dot_general(a, b, (((1,), (1,)), ((), ())), preferred_element_type=jnp.float32)


def _dot_tn(a, b):
    return lax.dot_general(a, b, (((0,), (0,)), ((), ())), preferred_element_type=jnp.float32)


def _bf(x):
    return x.astype(jnp.bfloat16)


def _split(x):
    hi = _bf(x)
    return hi, _bf(x - hi.astype(jnp.float32))


def _sigmoid(x):
    return 1.0 / (1.0 + jnp.exp2(x * NEG_LOG2E))


def _silu(x):
    return x * _sigmoid(x)


def _softplus(x):
    return jnp.maximum(x, 0.0) + jnp.log1p(jnp.exp(-jnp.abs(x)))


def _layer_norm(x, g, b):
    mu = jnp.mean(x, axis=-1, keepdims=True)
    xc = x - mu
    var = jnp.mean(xc * xc, axis=-1, keepdims=True)
    return xc * lax.rsqrt(var + LN_EPS) * g + b


def _mixer_kernel(x_ref, lbl_ref, win_ref, wgate_ref, wab_ref, convw_ref, hgnw_ref,
                  alogr_ref, dtbr_ref, alogc_ref, dtbc_ref, dnnw_ref, wa_ref, wb_ref, wo_ref,
                  g1_ref, b1_ref, wexp_ref, masks_ref, bdtriu_ref, bdtril_ref, out_ref,
                  xbuf, hq, hk, hv, hlf, hf, hb, dq, dk, dv, oa, ob, du, gbuf, gcrow, convbuf, rawhg, qin, kdec, wq, qkb,
                  kvinc, hdec, gl, sg, sz, sga, sgb, shg, sdn):
    tb = x_ref.shape[0]
    n_chunks = tb // CHUNK

    @pl.when(pl.program_id(1) == 0)
    def _reset_carries():
        shg[...] = jnp.zeros_like(shg)
        sdn[...] = jnp.zeros_like(sdn)
        convbuf[:, 0:CONV_PAD, :] = jnp.zeros((HEADS, CONV_PAD, 3 * HEAD_DIM), jnp.float32)

    xbuf[...] = _bf(x_ref[...])

    def project_head(c):
        cols = ([COL_DN_QKV + k * KW + c * HEAD_DIM for k in range(3)]
                + [COL_HG_QFI + k * KW + c * HEAD_DIM for k in range(3)])
        w_head = jnp.concatenate([win_ref[:, o:o + HEAD_DIM] for o in cols], axis=1)
        raw = _dot(xbuf[...], w_head)
        convbuf[c, CONV_PAD:CONV_PAD + tb, :] = raw[:, 0:3 * HEAD_DIM]
        rawhg[c] = raw[:, 3 * HEAD_DIM:6 * HEAD_DIM]

    def head_epilogue(c):
        cw = jnp.concatenate([convw_ref[:, k * KW + c * HEAD_DIM:k * KW + (c + 1) * HEAD_DIM]
                              for k in range(3)], axis=1)
        base = CONV_PAD - (CONV_W - 1)
        acc = convbuf[c, base:base + tb, :] * cw[0:1, :]
        for j in range(1, CONV_W):
            acc = acc + convbuf[c, base + j:base + j + tb, :] * cw[j:j + 1, :]
        convbuf[c, 0:CONV_PAD, :] = convbuf[c, tb:tb + CONV_PAD, :]
        qkv = _silu(acc)
        qh = qkv[:, 0:HEAD_DIM]
        kh = qkv[:, HEAD_DIM:2 * HEAD_DIM]
        dq[c] = qh * (lax.rsqrt(jnp.sum(qh * qh, axis=-1, keepdims=True) + L2_EPS) * QK_SCALE)
        dk[c] = kh * lax.rsqrt(jnp.sum(kh * kh, axis=-1, keepdims=True) + L2_EPS)
        dv[c] = qkv[:, 2 * HEAD_DIM:3 * HEAD_DIM]
        lg = lbl_ref[:, c * HEAD_DIM:(c + 1) * HEAD_DIM]
        e = jnp.exp(lg - jnp.max(lg, axis=0, keepdims=True))
        lb = e[0:1, :] / jnp.sum(e, axis=0, keepdims=True)
        f = lb + (1.0 - lb) * _sigmoid(rawhg[c, :, HEAD_DIM:2 * HEAD_DIM])
        hlf[c] = jnp.log(f)
        hf[c] = f
        hk[c] = 1.0 - f
        hq[c] = _silu(rawhg[c, :, 0:HEAD_DIM]) * QK_SCALE
        hv[c] = rawhg[c, :, 2 * HEAD_DIM:3 * HEAD_DIM]

    project_head(0)

    def cumulate_forget(h0):
        lf_hi, lf_lo = _split(jnp.concatenate([hlf[h0], hlf[h0 + 1]], axis=1))
        bc = _dot(bdtril_ref[...], lf_hi) + _dot(bdtril_ref[...], lf_lo)
        hb[h0] = bc[:, 0:HEAD_DIM]
        hb[h0 + 1] = bc[:, HEAD_DIM:2 * HEAD_DIM]

    for c in range(1, HEADS):
        head_epilogue(c - 1)
        project_head(c)
        if c % 2 == 0:
            cumulate_forget(c - 2)
    head_epilogue(HEADS - 1)
    cumulate_forget(HEADS - 2)

    pab = _dot(xbuf[...], wab_ref[...])
    g_col = -jnp.exp(alogr_ref[...]) * _softplus(pab + dtbr_ref[...])
    lane = lax.broadcasted_iota(jnp.int32, pab.shape, 1)
    gbuf[...] = jnp.where(lane < HEADS, _sigmoid(pab), g_col)
    pabt = jnp.transpose(pab)[0:2 * HEADS, :]
    g_row = -jnp.exp(alogc_ref[...]) * _softplus(pabt + dtbc_ref[...])
    gr_hi, gr_lo = _split(g_row)
    gc_row = _dot(gr_hi, bdtriu_ref[...]) + _dot(gr_lo, bdtriu_ref[...])
    for c in range(n_chunks):
        gcrow[c] = gc_row[:, c * CHUNK:(c + 1) * CHUNK]

    ri = lax.broadcasted_iota(jnp.int32, (CHUNK, CHUNK), 0)
    ci = lax.broadcasted_iota(jnp.int32, (CHUNK, CHUNK), 1)
    causal = ri >= ci
    strict = ri > ci
    trow = lax.broadcasted_iota(jnp.int32, (CHUNK, HEAD_DIM), 0)

    n_lev = len(_LEVELS)
    heads = range(HEADS)
    hsl = [slice(h * HEAD_DIM, (h + 1) * HEAD_DIM) for h in heads]

    def level_mask(l):
        return masks_ref[l * CHUNK:(l + 1) * CHUNK, :]

    units = [(u, h) for u in range(PREP_UNROLL) for h in heads]
    n_units = range(len(units))

    def prepare_chunks(j, carry):
        cs = [j * PREP_UNROLL + u for u in range(PREP_UNROLL)]
        rows = [pl.ds(pl.multiple_of(c * CHUNK, CHUNK), CHUNK) for c in cs]
        wexp = wexp_ref[...]
        gb, eg, gcr = [], [], []
        for u in range(PREP_UNROLL):
            gb.append(gbuf[rows[u], :])
            g_hi, g_lo = _split(gb[u])
            eg.append(_dot(wexp[0:2 * CHUNK], g_hi) + _dot(wexp[0:2 * CHUNK], g_lo))
            gcr.append(gcrow[cs[u]])

        def ld(ref, i):
            u, h = units[i]
            return ref[h, rows[u], :]

        dkk = [ld(dk, i) for i in n_units]
        dkb = [_bf(k) for k in dkk]
        beta = [gb[u][:, h:h + 1] for u, h in units]
        gcol = [eg[u][0:CHUNK, HEADS + h:HEADS + h + 1] for u, h in units]
        decay = [jnp.where(causal, jnp.exp(jnp.minimum(gcol[i] - gcr[u][HEADS + h:HEADS + h + 1, :], 0.0)), 0.0)
                 for i, (u, h) in enumerate(units)]
        kbeta = [dkk[i] * beta[i] for i in n_units]
        lmat = [jnp.where(strict, _dot_nt(_bf(kbeta[i]), dkb[i]) * decay[i], 0.0) for i in n_units]
        for i, (u, h) in enumerate(units):
            qkb[cs[u] * HEADS + h] = _bf(_dot_nt(_bf(ld(dq, i)), dkb[i]) * decay[i])

        hqv = [ld(hq, i) for i in n_units]
        hkv = [ld(hk, i) for i in n_units]
        hqb = [_bf(q) for q in hqv]
        hkb = [_bf(k) for k in hkv]
        sc = [level_mask(n_lev) * _dot_nt(hqb[i], hkb[i]) for i in n_units]
        nmat = [-(level_mask(n_lev - 1) * lmat[i]) for i in n_units]

        hbv = [ld(hb, i) for i in n_units]

        def level_factor(i, l):
            m = _LEVELS[l]
            if m >= SUBLANES // 2:
                ref = jnp.concatenate(
                    [jnp.broadcast_to(hbv[i][r:r + 1, :], (SUBLANES, HEAD_DIM))
                     for r in [(g * SUBLANES // (2 * m)) * (2 * m) + m - 1 for g in range(CHUNK // SUBLANES)]],
                    axis=0)
                return jnp.exp(-jnp.abs(hbv[i] - ref))
            f = ld(hf, i)
            if m == 1:
                return jnp.where(trow % 2 == 1, f, 1.0)
            f_prev = pltpu.roll(f, 1, 0)
            f_next = pltpu.roll(f, CHUNK - 1, 0)
            return jnp.where(trow % 4 == 0, f_next,
                             jnp.where(trow % 4 == 1, 1.0, jnp.where(trow % 4 == 2, f, f_prev * f)))

        def score_level(l):
            for i in n_units:
                zb = _bf(level_factor(i, l))
                sc[i] = sc[i] + level_mask(l) * _dot_nt(hqb[i] * zb, hkb[i] * zb)

        score_level(n_lev - 1)
        for l in range(n_lev - 2, -1, -1):
            cm = [level_mask(l) * lmat[i] for i in n_units]
            nb = [_bf(nmat[i]) for i in n_units]
            xm = [cm[i] + _dot(nb[i], _bf(cm[i])) for i in n_units]
            score_level(l)
            for i in n_units:
                nmat[i] = nmat[i] - (xm[i] + _dot(_bf(xm[i]), nb[i]))

        hvb = [_bf(ld(hv, i)) for i in n_units]
        for i, (u, h) in enumerate(units):
            oa[rows[u], hsl[h]] = _dot(_bf(sc[i]), hvb[i])
        for i, (u, h) in enumerate(units):
            kd = hkv[i] * jnp.exp(hbv[i][CHUNK - 1:CHUNK, :] - hbv[i])
            kvinc[cs[u] * HEADS + h] = _dot_tn(hvb[i], _bf(kd))
        for i, (u, h) in enumerate(units):
            qin[rows[u], hsl[h]] = _bf(hqv[i] * jnp.exp(hbv[i]))
            hdec[cs[u], :, hsl[h]] = jnp.exp(hbv[i][CHUNK - 1:CHUNK, :])

        for i, (u, h) in enumerate(units):
            eg_col = jnp.exp(gcol[i])
            rhs = jnp.concatenate([ld(dv, i) * beta[i], kbeta[i] * eg_col], axis=1)
            uw = rhs + _dot(_bf(nmat[i]), _bf(rhs))
            du[rows[u], hsl[h]] = uw[:, 0:HEAD_DIM]
            wq[cs[u], 0:CHUNK, hsl[h]] = _bf(uw[:, HEAD_DIM:2 * HEAD_DIM])
            wq[cs[u], CHUNK:2 * CHUNK, hsl[h]] = _bf(ld(dq, i) * eg_col)
            kdec[rows[u], hsl[h]] = _bf(dkk[i] * jnp.exp(eg[u][CHUNK:2 * CHUNK, HEADS + h:HEADS + h + 1]))
        for u in range(PREP_UNROLL):
            gl[cs[u]] = jnp.exp(eg[u][CHUNK - 1:CHUNK, :])
        return carry

    lax.fori_loop(0, n_chunks // PREP_UNROLL, prepare_chunks, 0)

    gate_tiles = ([(win_ref, COL_HG_G + t, sg, t, _silu) for t in range(0, KW, GATE_TILE)]
                  + [(win_ref, COL_DN_Z + t, sz, t, _silu) for t in range(0, KW, GATE_TILE)]
                  + [(wgate_ref, t, sga, t, _sigmoid) for t in range(0, D_MODEL, GATE_TILE)]
                  + [(wgate_ref, D_MODEL + t, sgb, t, _sigmoid) for t in range(0, D_MODEL, GATE_TILE)])
    n_slots = 2 * n_chunks

    def gate_slot(slot):
        for k in range(slot * len(gate_tiles) // n_slots, (slot + 1) * len(gate_tiles) // n_slots):
            w_ref, src, dst, off, act = gate_tiles[k]
            dst[:, off:off + GATE_TILE] = act(_dot(xbuf[...], w_ref[:, src:src + GATE_TILE]))

    for c in range(n_chunks):
        rows = slice(c * CHUNK, (c + 1) * CHUNK)
        s = [sdn[h] for h in heads]
        wqc = wq[c]
        ws = [_dot(wqc[:, hsl[h]], _bf(s[h])) for h in heads]
        st = [shg[h] for h in heads]
        dec = hdec[c]
        for h in heads:
            oa[rows, hsl[h]] = oa[rows, hsl[h]] + _dot_nt(qin[rows, hsl[h]], _bf(st[h]))
            shg[h] = st[h] * dec[:, hsl[h]] + kvinc[c * HEADS + h]
        gate_slot(2 * c)
        vnb = [_bf(du[rows, hsl[h]] - ws[h][0:CHUNK]) for h in heads]
        glc = gl[c]
        for h in heads:
            ob[rows, hsl[h]] = ws[h][CHUNK:2 * CHUNK] + _dot(qkb[c * HEADS + h], vnb[h])
        for h in heads:
            sdn[h] = s[h] * glc[:, HEADS + h:HEADS + h + 1] + _dot_tn(kdec[rows, hsl[h]], vnb[h])
        gate_slot(2 * c + 1)

    def gated_norm(o_ref, w_row, gate):
        parts = []
        for h in range(HEADS):
            oh = o_ref[:, h * HEAD_DIM:(h + 1) * HEAD_DIM]
            parts.append(oh * lax.rsqrt(jnp.mean(oh * oh, axis=-1, keepdims=True) + RMS_EPS))
        return jnp.concatenate(parts, axis=1) * w_row * gate

    na = gated_norm(oa, hgnw_ref[...], sg[...])
    nb_ = gated_norm(ob, dnnw_ref[...], sz[...])
    merged = sga[...] * _dot(_bf(na), wa_ref[...]) + sgb[...] * _dot(_bf(nb_), wb_ref[...])
    mix = _dot(_bf(merged), wo_ref[...])
    out_ref[...] = _layer_norm(ALPHA * x_ref[...] + mix, g1_ref[...], b1_ref[...])


def _mlp_kernel(h_ref, wup_ref, wdown_ref, g2_ref, b2_ref, out_ref):
    h = h_ref[...]
    hb = _bf(h)
    acc = jnp.zeros(h.shape, jnp.float32)
    for j in range(D_FF // FF_BLOCK):
        up = _dot(hb, wup_ref[:, j * FF_BLOCK:(j + 1) * FF_BLOCK])
        act = jnp.square(jnp.maximum(up, 0.0))
        acc = acc + _dot(_bf(act), wdown_ref[j * FF_BLOCK:(j + 1) * FF_BLOCK, :])
    out_ref[...] = _layer_norm(ALPHA * h + acc, g2_ref[...], b2_ref[...])


def _resident(shape):
    nd = len(shape)
    return pl.BlockSpec(shape, lambda *_: (0,) * nd, pipeline_mode=pl.Buffered(1))


def _mixer_call(x, lbl, win, wgate, wab, convw, hgnw, alogr, dtbr, alogc, dtbc, dnnw,
                wa, wb, wo, g1, b1):
    bsz, seq, _ = x.shape
    tb = min(TOKEN_BLOCK, seq)
    assert seq % tb == 0 and tb % (CHUNK * PREP_UNROLL) == 0
    n_chunks = tb // CHUNK
    wexp = jnp.asarray(_WEXP, jnp.bfloat16)
    masks = jnp.asarray(_MASKS, jnp.float32)
    tt = np.arange(tb)
    bdtriu = jnp.asarray(((tt[:, None] <= tt[None, :]) & ((tt[:, None] // CHUNK) == (tt[None, :] // CHUNK))),
                         jnp.bfloat16)
    consts = (lbl, win, wgate, wab, convw, hgnw, alogr, dtbr, alogc, dtbc, dnnw, wa, wb, wo,
              g1, b1, wexp, masks, bdtriu, bdtriu.T)
    f32 = jnp.float32
    blk = functools.partial(pltpu.VMEM, (tb, KW))
    bf16 = jnp.bfloat16
    per_head = functools.partial(pltpu.VMEM, (HEADS, tb, HEAD_DIM))
    scratch = [pltpu.VMEM((tb, D_MODEL), bf16)]
    scratch += [per_head(f32) for _ in range(9)]
    scratch += [blk(f32) for _ in range(3)]
    scratch += [pltpu.VMEM((tb, 128), f32),
                pltpu.VMEM((n_chunks, 8, CHUNK), f32),
                pltpu.VMEM((HEADS, tb + CONV_PAD, 3 * HEAD_DIM), f32),
                pltpu.VMEM((HEADS, tb, 3 * HEAD_DIM), f32),
                blk(bf16), blk(bf16),
                pltpu.VMEM((n_chunks, 2 * CHUNK, KW), bf16),
                pltpu.VMEM((n_chunks * HEADS, CHUNK, CHUNK), bf16),
                pltpu.VMEM((n_chunks * HEADS, HEAD_DIM, HEAD_DIM), f32),
                pltpu.VMEM((n_chunks, 1, KW), f32),
                pltpu.VMEM((n_chunks, 1, 128), f32),
                blk(f32), blk(f32),
                pltpu.VMEM((tb, D_MODEL), f32),
                pltpu.VMEM((tb, D_MODEL), f32),
                pltpu.VMEM((HEADS, HEAD_DIM, HEAD_DIM), f32),
                pltpu.VMEM((HEADS, HEAD_DIM, HEAD_DIM), f32)]
    return pl.pallas_call(
        _mixer_kernel,
        grid=(bsz, seq // tb),
        in_specs=[pl.BlockSpec((None, tb, D_MODEL), lambda b, i: (b, i, 0))]
                 + [_resident(c.shape) for c in consts],
        out_specs=pl.BlockSpec((None, tb, D_MODEL), lambda b, i: (b, i, 0)),
        out_shape=jax.ShapeDtypeStruct(x.shape, f32),
        scratch_shapes=scratch,
        compiler_params=pltpu.CompilerParams(dimension_semantics=("arbitrary", "arbitrary"),
                                             vmem_limit_bytes=VMEM_LIMIT_BYTES),
        name="token_mixer",
    )(x, *consts)


def _mlp_call(h, wup, wdown, g2, b2):
    m = h.shape[0]
    tm = min(MLP_BLOCK, m)
    assert m % tm == 0
    consts = (wup, wdown, g2, b2)
    return pl.pallas_call(
        _mlp_kernel,
        grid=(m // tm,),
        in_specs=[pl.BlockSpec((tm, D_MODEL), lambda i: (i, 0))] + [_resident(c.shape) for c in consts],
        out_specs=pl.BlockSpec((tm, D_MODEL), lambda i: (i, 0)),
        out_shape=jax.ShapeDtypeStruct(h.shape, jnp.float32),
        compiler_params=pltpu.CompilerParams(dimension_semantics=("arbitrary",),
                                             vmem_limit_bytes=VMEM_LIMIT_BYTES),
        name="relu2_mlp",
    )(h, *consts)


def kernel(x, hg_lb_logits, w_in, conv_w, hg_norm_w, dn_A_log, dn_dt_bias, dn_norm_w, w_branch_a, w_branch_b, w_o, ln1_g, ln1_b, w_up, w_down, ln2_g, ln2_b):
    bsz, seq, _ = x.shape
    f32 = jnp.float32
    bf16 = jnp.bfloat16
    o = _OFFS
    l = 0
    win = w_in[l].astype(bf16)
    wgate = win[:, o[8]:o[10]]
    wab = jnp.pad(win[:, o[6]:o[8]], ((0, 0), (0, 128 - 2 * HEADS)))
    zeros4 = jnp.zeros((HEADS,), f32)
    alog8 = jnp.concatenate([zeros4, dn_A_log[l].astype(f32)])
    dtb8 = jnp.concatenate([zeros4, dn_dt_bias[l].astype(f32)])
    alogr = jnp.pad(alog8, (0, 128 - 2 * HEADS)).reshape(1, 128)
    dtbr = jnp.pad(dtb8, (0, 128 - 2 * HEADS)).reshape(1, 128)
    h1 = _mixer_call(
        x, hg_lb_logits.astype(f32), win, wgate, wab, conv_w[l].astype(f32),
        hg_norm_w[l].reshape(1, KW), alogr, dtbr, alog8.reshape(8, 1), dtb8.reshape(8, 1),
        jnp.tile(dn_norm_w[l], HEADS).reshape(1, KW),
        w_branch_a[l].astype(bf16), w_branch_b[l].astype(bf16), w_o[l].astype(bf16),
        ln1_g[l].reshape(1, D_MODEL), ln1_b[l].reshape(1, D_MODEL))
    out = _mlp_call(h1.reshape(bsz * seq, D_MODEL), w_up[l].astype(bf16), w_down[l].astype(bf16),
                    ln2_g[l].reshape(1, D_MODEL), ln2_b[l].reshape(1, D_MODEL))
    return out.reshape(bsz, seq, D_MODEL)
```

```python
import functools

import numpy as np
import jax
import jax.numpy as jnp
from jax import lax
from jax.experimental import pallas as pl
from jax.experimental.pallas import tpu as pltpu

D_MODEL = 1024
HEADS = 4
HEAD_DIM = 128
KW = HEADS * HEAD_DIM
CHUNK = 64
CONV_W = 4
D_FF = 4 * D_MODEL
DEPTH = 1
ALPHA = (2 * DEPTH) ** 0.25
LN_EPS = 1e-5
RMS_EPS = 1e-6
L2_EPS = 1e-6
QK_SCALE = HEAD_DIM ** -0.5
NEG_LOG2E = -1.4426950408889634

TOKEN_BLOCK = 256
PREP_UNROLL = 4
GATE_TILE = 256
GATE_TILES_AFTER_PROJECTIONS = 4
MLP_BLOCK = 1024
FF_BLOCK = 1024
SUBLANES = 8
CONV_PAD = 8
VMEM_LIMIT_BYTES = 56 * 1024 * 1024

_SIZES = (KW, KW, KW, KW, 3 * KW, KW, HEADS, HEADS, D_MODEL, D_MODEL)
_OFFS = np.concatenate([[0], np.cumsum(_SIZES)]).tolist()

COL_HG_QFI = _OFFS[0]
COL_HG_G = _OFFS[3]
COL_DN_QKV = _OFFS[4]
COL_DN_Z = _OFFS[5]

_LEVELS = (32, 16, 8, 4, 2, 1)


def _build_constants():
    t = np.arange(CHUNK)[:, None]
    r = np.arange(CHUNK)[None, :]
    rows = [r <= t, r > t]
    masks = []
    for m in _LEVELS:
        last_key = (t // (2 * m)) * (2 * m) + m - 1
        is_query = (t % (2 * m)) >= m
        rows.append((is_query & (r > last_key) & (r <= t)) | (~is_query & (r > t) & (r <= last_key)))
        masks.append(((t // (2 * m)) == (r // (2 * m))) & is_query & ((r % (2 * m)) < m))
    masks.append(t == r)
    wexp = np.concatenate(rows, 0).astype(np.float32)
    masks = np.concatenate(masks, 0).astype(np.float32)
    return wexp, masks


_WEXP, _MASKS = _build_constants()


def _dot(a, b):
    return jnp.dot(a, b, preferred_element_type=jnp.float32)


def _dot_nt(a, b):
    return lax.dot_general(a, b, (((1,), (1,)), ((), ())), preferred_element_type=jnp.float32)


def _dot_tn(a, b):
    return lax.dot_general(a, b, (((0,), (0,)), ((), ())), preferred_element_type=jnp.float32)


def _bf(x):
    return x.astype(jnp.bfloat16)


def _split(x):
    hi = _bf(x)
    return hi, _bf(x - hi.astype(jnp.float32))


def _sigmoid(x):
    return 1.0 / (1.0 + jnp.exp2(x * NEG_LOG2E))


def _silu(x):
    return x * _sigmoid(x)


def _softplus(x):
    return jnp.maximum(x, 0.0) + jnp.log1p(jnp.exp(-jnp.abs(x)))


def _layer_norm(x, g, b):
    mu = jnp.mean(x, axis=-1, keepdims=True)
    xc = x - mu
    var = jnp.mean(xc * xc, axis=-1, keepdims=True)
    return xc * lax.rsqrt(var + LN_EPS) * g + b


def _mixer_kernel(x_ref, xnext_ref, lbl_ref, win_ref, wgate_ref, wab_ref, convw_ref, hgnw_ref,
                  alogr_ref, dtbr_ref, alogc_ref, dtbc_ref, dnnw_ref, wa_ref, wb_ref, wo_ref,
                  g1_ref, b1_ref, wexp_ref, masks_ref, bdtriu_ref, bdtril_ref, out_ref,
                  xbuf, xnbuf, hq, hk, hv, hlf, hf, hb, dq, dk, dv, oa, ob, du, gbuf, gcrow, convbuf, rawhg, qin, kdec, wq, qkb,
                  kvinc, hdec, gl, sg, sz, sga, sgb, shg, sdn):
    tb = x_ref.shape[0]
    n_chunks = tb // CHUNK

    @pl.when(pl.program_id(1) == 0)
    def _reset_carries():
        shg[...] = jnp.zeros_like(shg)
        sdn[...] = jnp.zeros_like(sdn)
        convbuf[:, 0:CONV_PAD, :] = jnp.zeros((HEADS, CONV_PAD, 3 * HEAD_DIM), jnp.float32)


    def project_head(c, xsrc=xbuf):
        cols = ([COL_DN_QKV + k * KW + c * HEAD_DIM for k in range(3)]
                + [COL_HG_QFI + k * KW + c * HEAD_DIM for k in range(3)])
        w_head = jnp.concatenate([win_ref[:, o:o + HEAD_DIM] for o in cols], axis=1)
        raw = _dot(xsrc[...], w_head)
        convbuf[c, CONV_PAD:CONV_PAD + tb, :] = raw[:, 0:3 * HEAD_DIM]
        rawhg[c] = raw[:, 3 * HEAD_DIM:6 * HEAD_DIM]

    def head_epilogue(c):
        cw = jnp.concatenate([convw_ref[:, k * KW + c * HEAD_DIM:k * KW + (c + 1) * HEAD_DIM]
                              for k in range(3)], axis=1)
        base = CONV_PAD - (CONV_W - 1)
        acc = convbuf[c, base:base + tb, :] * cw[0:1, :]
        for j in range(1, CONV_W):
            acc = acc + convbuf[c, base + j:base + j + tb, :] * cw[j:j + 1, :]
        convbuf[c, 0:CONV_PAD, :] = convbuf[c, tb:tb + CONV_PAD, :]
        qkv = _silu(acc)
        qh = qkv[:, 0:HEAD_DIM]
        kh = qkv[:, HEAD_DIM:2 * HEAD_DIM]
        dq[c] = qh * (lax.rsqrt(jnp.sum(qh * qh, axis=-1, keepdims=True) + L2_EPS) * QK_SCALE)
        dk[c] = kh * lax.rsqrt(jnp.sum(kh * kh, axis=-1, keepdims=True) + L2_EPS)
        dv[c] = qkv[:, 2 * HEAD_DIM:3 * HEAD_DIM]
        lg = lbl_ref[:, c * HEAD_DIM:(c + 1) * HEAD_DIM]
        e = jnp.exp(lg - jnp.max(lg, axis=0, keepdims=True))
        lb = e[0:1, :] / jnp.sum(e, axis=0, keepdims=True)
        f = lb + (1.0 - lb) * _sigmoid(rawhg[c, :, HEAD_DIM:2 * HEAD_DIM])
        hlf[c] = jnp.log(f)
        hf[c] = f
        hk[c] = 1.0 - f
        hq[c] = _silu(rawhg[c, :, 0:HEAD_DIM]) * QK_SCALE
        hv[c] = rawhg[c, :, 2 * HEAD_DIM:3 * HEAD_DIM]

    gate_tiles = ([(win_ref, COL_HG_G + t, sg, t, _silu) for t in range(0, KW, GATE_TILE)]
                  + [(win_ref, COL_DN_Z + t, sz, t, _silu) for t in range(0, KW, GATE_TILE)]
                  + [(wgate_ref, t, sga, t, _sigmoid) for t in range(0, D_MODEL, GATE_TILE)]
                  + [(wgate_ref, D_MODEL + t, sgb, t, _sigmoid) for t in range(0, D_MODEL, GATE_TILE)])
    gate_tiles_todo = list(range(len(gate_tiles)))

    def issue_gate_tiles(count):
        for _ in range(min(count, len(gate_tiles_todo))):
            w_ref, src, dst, off, act = gate_tiles[gate_tiles_todo.pop(0)]
            dst[:, off:off + GATE_TILE] = act(_dot(xbuf[...], w_ref[:, src:src + GATE_TILE]))

    @pl.when((pl.program_id(0) == 0) & (pl.program_id(1) == 0))
    def _first_projection():
        xnbuf[...] = _bf(x_ref[...])
        project_head(0, xnbuf)

    xbuf[...] = xnbuf[...]

    def cumulate_forget(h0):
        lf_hi, lf_lo = _split(jnp.concatenate([hlf[h0], hlf[h0 + 1]], axis=1))
        bc = _dot(bdtril_ref[...], lf_hi) + _dot(bdtril_ref[...], lf_lo)
        hb[h0] = bc[:, 0:HEAD_DIM]
        hb[h0 + 1] = bc[:, HEAD_DIM:2 * HEAD_DIM]

    for c in range(1, HEADS):
        head_epilogue(c - 1)
        project_head(c)
        if c % 2 == 0:
            cumulate_forget(c - 2)
    head_epilogue(HEADS - 1)
    issue_gate_tiles(GATE_TILES_AFTER_PROJECTIONS)
    cumulate_forget(HEADS - 2)

    pab = _dot(xbuf[...], wab_ref[...])
    g_col = -jnp.exp(alogr_ref[...]) * _softplus(pab + dtbr_ref[...])
    lane = lax.broadcasted_iota(jnp.int32, pab.shape, 1)
    gbuf[...] = jnp.where(lane < HEADS, _sigmoid(pab), g_col)
    pabt = jnp.transpose(pab)[0:2 * HEADS, :]
    g_row = -jnp.exp(alogc_ref[...]) * _softplus(pabt + dtbc_ref[...])
    gr_hi, gr_lo = _split(g_row)
    gc_row = _dot(gr_hi, bdtriu_ref[...]) + _dot(gr_lo, bdtriu_ref[...])
    for c in range(n_chunks):
        gcrow[c] = gc_row[:, c * CHUNK:(c + 1) * CHUNK]

    ri = lax.broadcasted_iota(jnp.int32, (CHUNK, CHUNK), 0)
    ci = lax.broadcasted_iota(jnp.int32, (CHUNK, CHUNK), 1)
    causal = ri >= ci
    strict = ri > ci
    trow = lax.broadcasted_iota(jnp.int32, (CHUNK, HEAD_DIM), 0)

    n_lev = len(_LEVELS)
    heads = range(HEADS)
    hsl = [slice(h * HEAD_DIM, (h + 1) * HEAD_DIM) for h in heads]

    def level_mask(l):
        return masks_ref[l * CHUNK:(l + 1) * CHUNK, :]

    units = [(u, h) for u in range(PREP_UNROLL) for h in heads]
    n_units = range(len(units))

    def prepare_chunks(j, carry):
        cs = [j * PREP_UNROLL + u for u in range(PREP_UNROLL)]
        rows = [pl.ds(pl.multiple_of(c * CHUNK, CHUNK), CHUNK) for c in cs]
        wexp = wexp_ref[...]
        gb, eg, gcr = [], [], []
        for u in range(PREP_UNROLL):
            gb.append(gbuf[rows[u], :])
            g_hi, g_lo = _split(gb[u])
            eg.append(_dot(wexp[0:2 * CHUNK], g_hi) + _dot(wexp[0:2 * CHUNK], g_lo))
            gcr.append(gcrow[cs[u]])

        def ld(ref, i):
            u, h = units[i]
            return ref[h, rows[u], :]

        dkk = [ld(dk, i) for i in n_units]
        dkb = [_bf(k) for k in dkk]
        beta = [gb[u][:, h:h + 1] for u, h in units]
        gcol = [eg[u][0:CHUNK, HEADS + h:HEADS + h + 1] for u, h in units]
        decay = [jnp.where(causal, jnp.exp(jnp.minimum(gcol[i] - gcr[u][HEADS + h:HEADS + h + 1, :], 0.0)), 0.0)
                 for i, (u, h) in enumerate(units)]
        kbeta = [dkk[i] * beta[i] for i in n_units]
        lmat = [jnp.where(strict, _dot_nt(_bf(kbeta[i]), dkb[i]) * decay[i], 0.0) for i in n_units]
        for i, (u, h) in enumerate(units):
            qkb[cs[u] * HEADS + h] = _bf(_dot_nt(_bf(ld(dq, i)), dkb[i]) * decay[i])

        hqv = [ld(hq, i) for i in n_units]
        hkv = [ld(hk, i) for i in n_units]
        hqb = [_bf(q) for q in hqv]
        hkb = [_bf(k) for k in hkv]
        sc = [level_mask(n_lev) * _dot_nt(hqb[i], hkb[i]) for i in n_units]
        nmat = [-(level_mask(n_lev - 1) * lmat[i]) for i in n_units]

        hbv = [ld(hb, i) for i in n_units]

        def level_factor(i, l):
            m = _LEVELS[l]
            if m >= SUBLANES // 2:
                ref = jnp.concatenate(
                    [jnp.broadcast_to(hbv[i][r:r + 1, :], (SUBLANES, HEAD_DIM))
                     for r in [(g * SUBLANES // (2 * m)) * (2 * m) + m - 1 for g in range(CHUNK // SUBLANES)]],
                    axis=0)
                return jnp.exp(-jnp.abs(hbv[i] - ref))
            f = ld(hf, i)
            if m == 1:
                return jnp.where(trow % 2 == 1, f, 1.0)
            f_prev = pltpu.roll(f, 1, 0)
            f_next = pltpu.roll(f, CHUNK - 1, 0)
            return jnp.where(trow % 4 == 0, f_next,
                             jnp.where(trow % 4 == 1, 1.0, jnp.where(trow % 4 == 2, f, f_prev * f)))

        def score_level(l):
            for i in n_units:
                zb = _bf(level_factor(i, l))
                sc[i] = sc[i] + level_mask(l) * _dot_nt(hqb[i] * zb, hkb[i] * zb)

        score_level(n_lev - 1)
        for l in range(n_lev - 2, -1, -1):
            cm = [level_mask(l) * lmat[i] for i in n_units]
            nb = [_bf(nmat[i]) for i in n_units]
            xm = [cm[i] + _dot(nb[i], _bf(cm[i])) for i in n_units]
            score_level(l)
            for i in n_units:
                nmat[i] = nmat[i] - (xm[i] + _dot(_bf(xm[i]), nb[i]))

        hvb = [_bf(ld(hv, i)) for i in n_units]
        for i, (u, h) in enumerate(units):
            oa[rows[u], hsl[h]] = _dot(_bf(sc[i]), hvb[i])
        for i, (u, h) in enumerate(units):
            kd = hkv[i] * jnp.exp(hbv[i][CHUNK - 1:CHUNK, :] - hbv[i])
            kvinc[cs[u] * HEADS + h] = _dot_tn(hvb[i], _bf(kd))
        for i, (u, h) in enumerate(units):
            qin[rows[u], hsl[h]] = _bf(hqv[i] * jnp.exp(hbv[i]))
            hdec[cs[u], :, hsl[h]] = jnp.exp(hbv[i][CHUNK - 1:CHUNK, :])

        for i, (u, h) in enumerate(units):
            eg_col = jnp.exp(gcol[i])
            rhs = jnp.concatenate([ld(dv, i) * beta[i], kbeta[i] * eg_col], axis=1)
            uw = rhs + _dot(_bf(nmat[i]), _bf(rhs))
            du[rows[u], hsl[h]] = uw[:, 0:HEAD_DIM]
            wq[cs[u], 0:CHUNK, hsl[h]] = _bf(uw[:, HEAD_DIM:2 * HEAD_DIM])
            wq[cs[u], CHUNK:2 * CHUNK, hsl[h]] = _bf(ld(dq, i) * eg_col)
            kdec[rows[u], hsl[h]] = _bf(dkk[i] * jnp.exp(eg[u][CHUNK:2 * CHUNK, HEADS + h:HEADS + h + 1]))
        for u in range(PREP_UNROLL):
            gl[cs[u]] = jnp.exp(eg[u][CHUNK - 1:CHUNK, :])
        return carry

    prepare_chunks(0, 0)

    for c in range(n_chunks):
        rows = slice(c * CHUNK, (c + 1) * CHUNK)
        s = [sdn[h] for h in heads]
        wqc = wq[c]
        ws = [_dot(wqc[:, hsl[h]], _bf(s[h])) for h in heads]
        st = [shg[h] for h in heads]
        dec = hdec[c]
        for h in heads:
            oa[rows, hsl[h]] = oa[rows, hsl[h]] + _dot_nt(qin[rows, hsl[h]], _bf(st[h]))
            shg[h] = st[h] * dec[:, hsl[h]] + kvinc[c * HEADS + h]
        issue_gate_tiles(-(-len(gate_tiles_todo) // (2 * (n_chunks - c))))
        vnb = [_bf(du[rows, hsl[h]] - ws[h][0:CHUNK]) for h in heads]
        glc = gl[c]
        for h in heads:
            ob[rows, hsl[h]] = ws[h][CHUNK:2 * CHUNK] + _dot(qkb[c * HEADS + h], vnb[h])
        for h in heads:
            sdn[h] = s[h] * glc[:, HEADS + h:HEADS + h + 1] + _dot_tn(kdec[rows, hsl[h]], vnb[h])
        issue_gate_tiles(-(-len(gate_tiles_todo) // (2 * (n_chunks - c) - 1)))

    def gated_norm(o_ref, w_row, gate):
        parts = []
        for h in range(HEADS):
            oh = o_ref[:, h * HEAD_DIM:(h + 1) * HEAD_DIM]
            parts.append(oh * lax.rsqrt(jnp.mean(oh * oh, axis=-1, keepdims=True) + RMS_EPS))
        return jnp.concatenate(parts, axis=1) * w_row * gate

    na = gated_norm(oa, hgnw_ref[...], sg[...])
    nb_ = gated_norm(ob, dnnw_ref[...], sz[...])
    merged = sga[...] * _dot(_bf(na), wa_ref[...]) + sgb[...] * _dot(_bf(nb_), wb_ref[...])
    mix = _dot(_bf(merged), wo_ref[...])
    out_ref[...] = _layer_norm(ALPHA * x_ref[...] + mix, g1_ref[...], b1_ref[...])

    xnbuf[...] = _bf(xnext_ref[...])
    project_head(0, xnbuf)


def _mlp_kernel(h_ref, wup_ref, wdown_ref, g2_ref, b2_ref, out_ref):
    h = h_ref[...]
    hb = _bf(h)
    acc = jnp.zeros(h.shape, jnp.float32)
    for j in range(D_FF // FF_BLOCK):
        up = _dot(hb, wup_ref[:, j * FF_BLOCK:(j + 1) * FF_BLOCK])
        act = jnp.square(jnp.maximum(up, 0.0))
        acc = acc + _dot(_bf(act), wdown_ref[j * FF_BLOCK:(j + 1) * FF_BLOCK, :])
    out_ref[...] = _layer_norm(ALPHA * h + acc, g2_ref[...], b2_ref[...])


def _resident(shape):
    nd = len(shape)
    return pl.BlockSpec(shape, lambda *_: (0,) * nd, pipeline_mode=pl.Buffered(1))


def _mixer_call(x, lbl, win, wgate, wab, convw, hgnw, alogr, dtbr, alogc, dtbc, dnnw,
                wa, wb, wo, g1, b1):
    bsz, seq, _ = x.shape
    tb = min(TOKEN_BLOCK, seq)
    assert seq % tb == 0 and tb == CHUNK * PREP_UNROLL
    n_chunks = tb // CHUNK
    wexp = jnp.asarray(_WEXP, jnp.bfloat16)
    masks = jnp.asarray(_MASKS, jnp.float32)
    tt = np.arange(tb)
    bdtriu = jnp.asarray(((tt[:, None] <= tt[None, :]) & ((tt[:, None] // CHUNK) == (tt[None, :] // CHUNK))),
                         jnp.bfloat16)
    consts = (lbl, win, wgate, wab, convw, hgnw, alogr, dtbr, alogc, dtbc, dnnw, wa, wb, wo,
              g1, b1, wexp, masks, bdtriu, bdtriu.T)
    f32 = jnp.float32
    blk = functools.partial(pltpu.VMEM, (tb, KW))
    bf16 = jnp.bfloat16
    per_head = functools.partial(pltpu.VMEM, (HEADS, tb, HEAD_DIM))
    scratch = [pltpu.VMEM((tb, D_MODEL), bf16), pltpu.VMEM((tb, D_MODEL), bf16)]
    scratch += [per_head(f32) for _ in range(9)]
    scratch += [blk(f32) for _ in range(3)]
    scratch += [pltpu.VMEM((tb, 128), f32),
                pltpu.VMEM((n_chunks, 8, CHUNK), f32),
                pltpu.VMEM((HEADS, tb + CONV_PAD, 3 * HEAD_DIM), f32),
                pltpu.VMEM((HEADS, tb, 3 * HEAD_DIM), f32),
                blk(bf16), blk(bf16),
                pltpu.VMEM((n_chunks, 2 * CHUNK, KW), bf16),
                pltpu.VMEM((n_chunks * HEADS, CHUNK, CHUNK), bf16),
                pltpu.VMEM((n_chunks * HEADS, HEAD_DIM, HEAD_DIM), f32),
                pltpu.VMEM((n_chunks, 1, KW), f32),
                pltpu.VMEM((n_chunks, 1, 128), f32),
                blk(f32), blk(f32),
                pltpu.VMEM((tb, D_MODEL), f32),
                pltpu.VMEM((tb, D_MODEL), f32),
                pltpu.VMEM((HEADS, HEAD_DIM, HEAD_DIM), f32),
                pltpu.VMEM((HEADS, HEAD_DIM, HEAD_DIM), f32)]
    n_blocks = seq // tb

    def next_block(b, i):
        flat = jnp.minimum(b * n_blocks + i + 1, bsz * n_blocks - 1)
        return flat // n_blocks, flat % n_blocks, 0

    return pl.pallas_call(
        _mixer_kernel,
        grid=(bsz, seq // tb),
        in_specs=[pl.BlockSpec((None, tb, D_MODEL), lambda b, i: (b, i, 0)),
                  pl.BlockSpec((None, tb, D_MODEL), next_block)]
                 + [_resident(c.shape) for c in consts],
        out_specs=pl.BlockSpec((None, tb, D_MODEL), lambda b, i: (b, i, 0)),
        out_shape=jax.ShapeDtypeStruct(x.shape, f32),
        scratch_shapes=scratch,
        compiler_params=pltpu.CompilerParams(dimension_semantics=("arbitrary", "arbitrary"),
                                             vmem_limit_bytes=VMEM_LIMIT_BYTES),
        name="token_mixer",
    )(x, x, *consts)


def _mlp_call(h, wup, wdown, g2, b2):
    m = h.shape[0]
    tm = min(MLP_BLOCK, m)
    assert m % tm == 0
    consts = (wup, wdown, g2, b2)
    return pl.pallas_call(
        _mlp_kernel,
        grid=(m // tm,),
        in_specs=[pl.BlockSpec((tm, D_MODEL), lambda i: (i, 0))] + [_resident(c.shape) for c in consts],
        out_specs=pl.BlockSpec((tm, D_MODEL), lambda i: (i, 0)),
        out_shape=jax.ShapeDtypeStruct(h.shape, jnp.float32),
        compiler_params=pltpu.CompilerParams(dimension_semantics=("arbitrary",),
                                             vmem_limit_bytes=VMEM_LIMIT_BYTES),
        name="relu2_mlp",
    )(h, *consts)


def kernel(x, hg_lb_logits, w_in, conv_w, hg_norm_w, dn_A_log, dn_dt_bias, dn_norm_w, w_branch_a, w_branch_b, w_o, ln1_g, ln1_b, w_up, w_down, ln2_g, ln2_b):
    bsz, seq, _ = x.shape
    f32 = jnp.float32
    bf16 = jnp.bfloat16
    o = _OFFS
    l = 0
    win = w_in[l].astype(bf16)
    wgate = win[:, o[8]:o[10]]
    wab = jnp.pad(win[:, o[6]:o[8]], ((0, 0), (0, 128 - 2 * HEADS)))
    zeros4 = jnp.zeros((HEADS,), f32)
    alog8 = jnp.concatenate([zeros4, dn_A_log[l].astype(f32)])
    dtb8 = jnp.concatenate([zeros4, dn_dt_bias[l].astype(f32)])
    alogr = jnp.pad(alog8, (0, 128 - 2 * HEADS)).reshape(1, 128)
    dtbr = jnp.pad(dtb8, (0, 128 - 2 * HEADS)).reshape(1, 128)
    h1 = _mixer_call(
        x, hg_lb_logits.astype(f32), win, wgate, wab, conv_w[l].astype(f32),
        hg_norm_w[l].reshape(1, KW), alogr, dtbr, alog8.reshape(8, 1), dtb8.reshape(8, 1),
        jnp.tile(dn_norm_w[l], HEADS).reshape(1, KW),
        w_branch_a[l].astype(bf16), w_branch_b[l].astype(bf16), w_o[l].astype(bf16),
        ln1_g[l].reshape(1, D_MODEL), ln1_b[l].reshape(1, D_MODEL))
    out = _mlp_call(h1.reshape(bsz * seq, D_MODEL), w_up[l].astype(bf16), w_down[l].astype(bf16),
                    ln2_g[l].reshape(1, D_MODEL), ln2_b[l].reshape(1, D_MODEL))
    return out.reshape(bsz, seq, D_MODEL)
```

```python
import functools

import numpy as np
import jax
import jax.numpy as jnp
from jax import lax
from jax.experimental import pallas as pl
from jax.experimental.pallas import tpu as pltpu

D_MODEL = 1024
HEADS = 4
HEAD_DIM = 128
KW = HEADS * HEAD_DIM
CHUNK = 64
CONV_W = 4
D_FF = 4 * D_MODEL
DEPTH = 1
ALPHA = (2 * DEPTH) ** 0.25
LN_EPS = 1e-5
RMS_EPS = 1e-6
L2_EPS = 1e-6
QK_SCALE = HEAD_DIM ** -0.5
NEG_LOG2E = -1.4426950408889634

TOKEN_BLOCK = 256
PREP_UNROLL = 4
GATE_TILE = 256
GATE_TILES_AFTER_PROJECTIONS = 4
MLP_BLOCK = 1024
MLP_TAIL_SPLIT = 4
FF_BLOCK = 1024
SUBLANES = 8
BF16_ROWS = 16
CONV_PAD = 8
VMEM_LIMIT_BYTES = 56 * 1024 * 1024

_SIZES = (KW, KW, KW, KW, 3 * KW, KW, HEADS, HEADS, D_MODEL, D_MODEL)
_OFFS = np.concatenate([[0], np.cumsum(_SIZES)]).tolist()

COL_HG_QFI = _OFFS[0]
COL_HG_G = _OFFS[3]
COL_DN_QKV = _OFFS[4]
COL_DN_Z = _OFFS[5]

_LEVELS = (32, 16, 8, 4, 2, 1)


def _build_constants():
    t = np.arange(CHUNK)[:, None]
    r = np.arange(CHUNK)[None, :]
    rows = [r <= t, r > t]
    masks = []
    for m in _LEVELS:
        last_key = (t // (2 * m)) * (2 * m) + m - 1
        is_query = (t % (2 * m)) >= m
        rows.append((is_query & (r > last_key) & (r <= t)) | (~is_query & (r > t) & (r <= last_key)))
        masks.append(((t // (2 * m)) == (r // (2 * m))) & is_query & ((r % (2 * m)) < m))
    masks.append(t == r)
    wexp = np.concatenate(rows, 0).astype(np.float32)
    masks = np.concatenate(masks, 0).astype(np.float32)
    return wexp, masks


_WEXP, _MASKS = _build_constants()


def _dot(a, b):
    return jnp.dot(a, b, preferred_element_type=jnp.float32)


def _dot_nt(a, b):
    return lax.dot_general(a, b, (((1,), (1,)), ((), ())), preferred_element_type=jnp.float32)


def _dot_tn(a, b):
    return lax.dot_general(a, b, (((0,), (0,)), ((), ())), preferred_element_type=jnp.float32)


def _bf(x):
    return x.astype(jnp.bfloat16)


def _split(x):
    hi = _bf(x)
    return hi, _bf(x - hi.astype(jnp.float32))


def _sigmoid(x):
    return 1.0 / (1.0 + jnp.exp2(x * NEG_LOG2E))


def _silu(x):
    return x * _sigmoid(x)


def _softplus(x):
    return jnp.maximum(x, 0.0) + jnp.log1p(jnp.exp(-jnp.abs(x)))


def _layer_norm(x, g, b):
    mu = jnp.mean(x, axis=-1, keepdims=True)
    xc = x - mu
    var = jnp.mean(xc * xc, axis=-1, keepdims=True)
    return xc * lax.rsqrt(var + LN_EPS) * g + b


def _mixer_kernel(x_ref, xnext_ref, wup32_ref, wdown32_ref, lbl_ref, win_ref, wgate_ref, wab_ref, convw_ref, hgnw_ref,
                  alogr_ref, dtbr_ref, alogc_ref, dtbc_ref, dnnw_ref, wa_ref, wb_ref, wo_ref,
                  g1_ref, b1_ref, wexp_ref, masks_ref, bdtriu_ref, bdtril_ref, out_ref, wupb_ref, wdownb_ref,
                  xbuf, xnbuf, hq, hk, hv, hlf, hf, hb, dq, dk, dv, oa, ob, du, gbuf, gcrow, convbuf, rawhg, qin, kdec, wq, qkb,
                  kvinc, hdec, gl, sg, sz, sga, sgb, shg, sdn):
    tb = x_ref.shape[0]
    n_chunks = tb // CHUNK

    wupb_ref[...] = _bf(wup32_ref[...])
    wdownb_ref[...] = _bf(wdown32_ref[...])

    @pl.when(pl.program_id(1) == 0)
    def _reset_carries():
        shg[...] = jnp.zeros_like(shg)
        sdn[...] = jnp.zeros_like(sdn)
        convbuf[:, 0:CONV_PAD, :] = jnp.zeros((HEADS, CONV_PAD, 3 * HEAD_DIM), jnp.float32)


    def project_head(c, xsrc=xbuf):
        cols = ([COL_DN_QKV + k * KW + c * HEAD_DIM for k in range(3)]
                + [COL_HG_QFI + k * KW + c * HEAD_DIM for k in range(3)])
        w_head = jnp.concatenate([win_ref[:, o:o + HEAD_DIM] for o in cols], axis=1)
        raw = _dot(xsrc[...], w_head)
        convbuf[c, CONV_PAD:CONV_PAD + tb, :] = raw[:, 0:3 * HEAD_DIM]
        rawhg[c] = raw[:, 3 * HEAD_DIM:6 * HEAD_DIM]

    def head_epilogue(c):
        cw = jnp.concatenate([convw_ref[:, k * KW + c * HEAD_DIM:k * KW + (c + 1) * HEAD_DIM]
                              for k in range(3)], axis=1)
        base = CONV_PAD - (CONV_W - 1)
        acc = convbuf[c, base:base + tb, :] * cw[0:1, :]
        for j in range(1, CONV_W):
            acc = acc + convbuf[c, base + j:base + j + tb, :] * cw[j:j + 1, :]
        convbuf[c, 0:CONV_PAD, :] = convbuf[c, tb:tb + CONV_PAD, :]
        qkv = _silu(acc)
        qh = qkv[:, 0:HEAD_DIM]
        kh = qkv[:, HEAD_DIM:2 * HEAD_DIM]
        dq[c] = qh * (lax.rsqrt(jnp.sum(qh * qh, axis=-1, keepdims=True) + L2_EPS) * QK_SCALE)
        dk[c] = kh * lax.rsqrt(jnp.sum(kh * kh, axis=-1, keepdims=True) + L2_EPS)
        dv[c] = qkv[:, 2 * HEAD_DIM:3 * HEAD_DIM]
        lg = lbl_ref[:, c * HEAD_DIM:(c + 1) * HEAD_DIM]
        e = jnp.exp(lg - jnp.max(lg, axis=0, keepdims=True))
        lb = e[0:1, :] / jnp.sum(e, axis=0, keepdims=True)
        f = lb + (1.0 - lb) * _sigmoid(rawhg[c, :, HEAD_DIM:2 * HEAD_DIM])
        hlf[c] = jnp.log(f)
        hf[c] = f
        hk[c] = 1.0 - f
        hq[c] = _silu(rawhg[c, :, 0:HEAD_DIM]) * QK_SCALE
        hv[c] = rawhg[c, :, 2 * HEAD_DIM:3 * HEAD_DIM]

    gate_tiles = ([(win_ref, COL_HG_G + t, sg, t, _silu) for t in range(0, KW, GATE_TILE)]
                  + [(win_ref, COL_DN_Z + t, sz, t, _silu) for t in range(0, KW, GATE_TILE)]
                  + [(wgate_ref, t, sga, t, _sigmoid) for t in range(0, D_MODEL, GATE_TILE)]
                  + [(wgate_ref, D_MODEL + t, sgb, t, _sigmoid) for t in range(0, D_MODEL, GATE_TILE)])
    gate_tiles_todo = list(range(len(gate_tiles)))

    def issue_gate_tiles(count):
        for _ in range(min(count, len(gate_tiles_todo))):
            w_ref, src, dst, off, act = gate_tiles[gate_tiles_todo.pop(0)]
            dst[:, off:off + GATE_TILE] = act(_dot(xbuf[...], w_ref[:, src:src + GATE_TILE]))

    @pl.when((pl.program_id(0) == 0) & (pl.program_id(1) == 0))
    def _first_projection():
        xnbuf[...] = _bf(x_ref[...])
        project_head(0, xnbuf)

    xbuf[...] = xnbuf[...]

    def cumulate_forget(h0):
        lf_hi, lf_lo = _split(jnp.concatenate([hlf[h0], hlf[h0 + 1]], axis=1))
        bc = _dot(bdtril_ref[...], lf_hi) + _dot(bdtril_ref[...], lf_lo)
        hb[h0] = bc[:, 0:HEAD_DIM]
        hb[h0 + 1] = bc[:, HEAD_DIM:2 * HEAD_DIM]

    for c in range(1, HEADS):
        head_epilogue(c - 1)
        project_head(c)
        if c % 2 == 0:
            cumulate_forget(c - 2)
    head_epilogue(HEADS - 1)
    issue_gate_tiles(GATE_TILES_AFTER_PROJECTIONS)
    cumulate_forget(HEADS - 2)

    pab = _dot(xbuf[...], wab_ref[...])
    g_col = -jnp.exp(alogr_ref[...]) * _softplus(pab + dtbr_ref[...])
    lane = lax.broadcasted_iota(jnp.int32, pab.shape, 1)
    gbuf[...] = jnp.where(lane < HEADS, _sigmoid(pab), g_col)
    pabt = jnp.transpose(pab)[0:2 * HEADS, :]
    g_row = -jnp.exp(alogc_ref[...]) * _softplus(pabt + dtbc_ref[...])
    gr_hi, gr_lo = _split(g_row)
    gc_row = _dot(gr_hi, bdtriu_ref[...]) + _dot(gr_lo, bdtriu_ref[...])
    for c in range(n_chunks):
        gcrow[c] = gc_row[:, c * CHUNK:(c + 1) * CHUNK]

    ri = lax.broadcasted_iota(jnp.int32, (CHUNK, CHUNK), 0)
    ci = lax.broadcasted_iota(jnp.int32, (CHUNK, CHUNK), 1)
    causal = ri >= ci
    strict = ri > ci
    trow = lax.broadcasted_iota(jnp.int32, (CHUNK, HEAD_DIM), 0)

    n_lev = len(_LEVELS)
    heads = range(HEADS)
    hsl = [slice(h * HEAD_DIM, (h + 1) * HEAD_DIM) for h in heads]

    def level_mask(l):
        return masks_ref[l * CHUNK:(l + 1) * CHUNK, :]

    units = [(u, h) for u in range(PREP_UNROLL) for h in heads]
    n_units = range(len(units))

    def prepare_chunks(j, carry):
        cs = [j * PREP_UNROLL + u for u in range(PREP_UNROLL)]
        rows = [pl.ds(pl.multiple_of(c * CHUNK, CHUNK), CHUNK) for c in cs]
        wexp = wexp_ref[...]
        gb, eg, gcr = [], [], []
        for u in range(PREP_UNROLL):
            gb.append(gbuf[rows[u], :])
            g_hi, g_lo = _split(gb[u])
            eg.append(_dot(wexp[0:2 * CHUNK], g_hi) + _dot(wexp[0:2 * CHUNK], g_lo))
            gcr.append(gcrow[cs[u]])

        def ld(ref, i):
            u, h = units[i]
            return ref[h, rows[u], :]

        dkk = [ld(dk, i) for i in n_units]
        dkb = [_bf(k) for k in dkk]
        beta = [gb[u][:, h:h + 1] for u, h in units]
        gcol = [eg[u][0:CHUNK, HEADS + h:HEADS + h + 1] for u, h in units]
        decay = [jnp.where(causal, jnp.exp(jnp.minimum(gcol[i] - gcr[u][HEADS + h:HEADS + h + 1, :], 0.0)), 0.0)
                 for i, (u, h) in enumerate(units)]
        kbeta = [dkk[i] * beta[i] for i in n_units]
        lmat = [jnp.where(strict, _dot_nt(_bf(kbeta[i]), dkb[i]) * decay[i], 0.0) for i in n_units]
        for i, (u, h) in enumerate(units):
            qkb[cs[u] * HEADS + h] = _bf(_dot_nt(_bf(ld(dq, i)), dkb[i]) * decay[i])

        hqv = [ld(hq, i) for i in n_units]
        hkv = [ld(hk, i) for i in n_units]
        hqb = [_bf(q) for q in hqv]
        hkb = [_bf(k) for k in hkv]
        sc = [level_mask(n_lev) * _dot_nt(hqb[i], hkb[i]) for i in n_units]
        nmat = [-(level_mask(n_lev - 1) * lmat[i]) for i in n_units]

        hbv = [ld(hb, i) for i in n_units]

        def level_factor(i, l):
            m = _LEVELS[l]
            if m >= SUBLANES // 2:
                ref = jnp.concatenate(
                    [jnp.broadcast_to(hbv[i][r:r + 1, :], (SUBLANES, HEAD_DIM))
                     for r in [(g * SUBLANES // (2 * m)) * (2 * m) + m - 1 for g in range(CHUNK // SUBLANES)]],
                    axis=0)
                return jnp.exp(-jnp.abs(hbv[i] - ref))
            f = ld(hf, i)
            if m == 1:
                return jnp.where(trow % 2 == 1, f, 1.0)
            f_prev = pltpu.roll(f, 1, 0)
            f_next = pltpu.roll(f, CHUNK - 1, 0)
            return jnp.where(trow % 4 == 0, f_next,
                             jnp.where(trow % 4 == 1, 1.0, jnp.where(trow % 4 == 2, f, f_prev * f)))

        def score_level(l):
            for i in n_units:
                zb = _bf(level_factor(i, l))
                sc[i] = sc[i] + level_mask(l) * _dot_nt(hqb[i] * zb, hkb[i] * zb)

        score_level(n_lev - 1)
        for l in range(n_lev - 2, -1, -1):
            cm = [level_mask(l) * lmat[i] for i in n_units]
            nb = [_bf(nmat[i]) for i in n_units]
            xm = [cm[i] + _dot(nb[i], _bf(cm[i])) for i in n_units]
            score_level(l)
            for i in n_units:
                nmat[i] = nmat[i] - (xm[i] + _dot(_bf(xm[i]), nb[i]))

        hvb = [_bf(ld(hv, i)) for i in n_units]
        for i, (u, h) in enumerate(units):
            oa[rows[u], hsl[h]] = _dot(_bf(sc[i]), hvb[i])
        for i, (u, h) in enumerate(units):
            kd = hkv[i] * jnp.exp(hbv[i][CHUNK - 1:CHUNK, :] - hbv[i])
            kvinc[cs[u] * HEADS + h] = _dot_tn(hvb[i], _bf(kd))
        for i, (u, h) in enumerate(units):
            qin[rows[u], hsl[h]] = _bf(hqv[i] * jnp.exp(hbv[i]))
            hdec[cs[u], :, hsl[h]] = jnp.exp(hbv[i][CHUNK - 1:CHUNK, :])

        for i, (u, h) in enumerate(units):
            eg_col = jnp.exp(gcol[i])
            rhs = jnp.concatenate([ld(dv, i) * beta[i], kbeta[i] * eg_col], axis=1)
            uw = rhs + _dot(_bf(nmat[i]), _bf(rhs))
            du[rows[u], hsl[h]] = uw[:, 0:HEAD_DIM]
            wq[cs[u], 0:CHUNK, hsl[h]] = _bf(uw[:, HEAD_DIM:2 * HEAD_DIM])
            wq[cs[u], CHUNK:2 * CHUNK, hsl[h]] = _bf(ld(dq, i) * eg_col)
            kdec[rows[u], hsl[h]] = _bf(dkk[i] * jnp.exp(eg[u][CHUNK:2 * CHUNK, HEADS + h:HEADS + h + 1]))
        for u in range(PREP_UNROLL):
            gl[cs[u]] = jnp.exp(eg[u][CHUNK - 1:CHUNK, :])
        return carry

    prepare_chunks(0, 0)

    for c in range(n_chunks):
        rows = slice(c * CHUNK, (c + 1) * CHUNK)
        s = [sdn[h] for h in heads]
        wqc = wq[c]
        ws = [_dot(wqc[:, hsl[h]], _bf(s[h])) for h in heads]
        st = [shg[h] for h in heads]
        dec = hdec[c]
        for h in heads:
            oa[rows, hsl[h]] = oa[rows, hsl[h]] + _dot_nt(qin[rows, hsl[h]], _bf(st[h]))
            shg[h] = st[h] * dec[:, hsl[h]] + kvinc[c * HEADS + h]
        issue_gate_tiles(-(-len(gate_tiles_todo) // (2 * (n_chunks - c))))
        vnb = [_bf(du[rows, hsl[h]] - ws[h][0:CHUNK]) for h in heads]
        glc = gl[c]
        for h in heads:
            ob[rows, hsl[h]] = ws[h][CHUNK:2 * CHUNK] + _dot(qkb[c * HEADS + h], vnb[h])
        for h in heads:
            sdn[h] = s[h] * glc[:, HEADS + h:HEADS + h + 1] + _dot_tn(kdec[rows, hsl[h]], vnb[h])
        issue_gate_tiles(-(-len(gate_tiles_todo) // (2 * (n_chunks - c) - 1)))

    def gated_norm(o_ref, w_row, gate):
        parts = []
        for h in range(HEADS):
            oh = o_ref[:, h * HEAD_DIM:(h + 1) * HEAD_DIM]
            parts.append(oh * lax.rsqrt(jnp.mean(oh * oh, axis=-1, keepdims=True) + RMS_EPS))
        return jnp.concatenate(parts, axis=1) * w_row * gate

    na = gated_norm(oa, hgnw_ref[...], sg[...])
    nb_ = gated_norm(ob, dnnw_ref[...], sz[...])
    merged = sga[...] * _dot(_bf(na), wa_ref[...]) + sgb[...] * _dot(_bf(nb_), wb_ref[...])
    mix = _dot(_bf(merged), wo_ref[...])
    out_ref[...] = _layer_norm(ALPHA * x_ref[...] + mix, g1_ref[...], b1_ref[...])

    xnbuf[...] = _bf(xnext_ref[...])
    project_head(0, xnbuf)


def _mlp_kernel(h_ref, wup_ref, wdown_ref, g2_ref, b2_ref, out_ref):
    tm = h_ref.shape[0]
    hb = _bf(h_ref[...])
    n_ff = D_FF // FF_BLOCK
    acc = None
    for j in range(n_ff - 1):
        up = _dot(hb, wup_ref[:, j * FF_BLOCK:(j + 1) * FF_BLOCK])
        act = jnp.square(jnp.maximum(up, 0.0))
        part = _dot(_bf(act), wdown_ref[j * FF_BLOCK:(j + 1) * FF_BLOCK, :])
        acc = part if acc is None else acc + part
    up = _dot(hb, wup_ref[:, (n_ff - 1) * FF_BLOCK:n_ff * FF_BLOCK])
    actb = _bf(jnp.square(jnp.maximum(up, 0.0)))
    for r in range(MLP_TAIL_SPLIT):
        rows = slice(r * tm // MLP_TAIL_SPLIT, (r + 1) * tm // MLP_TAIL_SPLIT)
        part = _dot(actb[rows], wdown_ref[(n_ff - 1) * FF_BLOCK:n_ff * FF_BLOCK, :])
        out_ref[rows, :] = _layer_norm(ALPHA * h_ref[rows, :] + acc[rows] + part, g2_ref[...], b2_ref[...])


def _resident(shape):
    nd = len(shape)
    return pl.BlockSpec(shape, lambda *_: (0,) * nd, pipeline_mode=pl.Buffered(1))


def _mixer_call(x, wup, wdown, lbl, win, wgate, wab, convw, hgnw, alogr, dtbr, alogc, dtbc, dnnw,
                wa, wb, wo, g1, b1):
    bsz, seq, _ = x.shape
    tb = min(TOKEN_BLOCK, seq)
    assert seq % tb == 0 and tb == CHUNK * PREP_UNROLL
    n_chunks = tb // CHUNK
    wexp = jnp.asarray(_WEXP, jnp.bfloat16)
    masks = jnp.asarray(_MASKS, jnp.float32)
    tt = np.arange(tb)
    bdtriu = jnp.asarray(((tt[:, None] <= tt[None, :]) & ((tt[:, None] // CHUNK) == (tt[None, :] // CHUNK))),
                         jnp.bfloat16)
    consts = (lbl, win, wgate, wab, convw, hgnw, alogr, dtbr, alogc, dtbc, dnnw, wa, wb, wo,
              g1, b1, wexp, masks, bdtriu, bdtriu.T)
    f32 = jnp.float32
    blk = functools.partial(pltpu.VMEM, (tb, KW))
    bf16 = jnp.bfloat16
    per_head = functools.partial(pltpu.VMEM, (HEADS, tb, HEAD_DIM))
    scratch = [pltpu.VMEM((tb, D_MODEL), bf16), pltpu.VMEM((tb, D_MODEL), bf16)]
    scratch += [per_head(f32) for _ in range(9)]
    scratch += [blk(f32) for _ in range(3)]
    scratch += [pltpu.VMEM((tb, 128), f32),
                pltpu.VMEM((n_chunks, 8, CHUNK), f32),
                pltpu.VMEM((HEADS, tb + CONV_PAD, 3 * HEAD_DIM), f32),
                pltpu.VMEM((HEADS, tb, 3 * HEAD_DIM), f32),
                blk(bf16), blk(bf16),
                pltpu.VMEM((n_chunks, 2 * CHUNK, KW), bf16),
                pltpu.VMEM((n_chunks * HEADS, CHUNK, CHUNK), bf16),
                pltpu.VMEM((n_chunks * HEADS, HEAD_DIM, HEAD_DIM), f32),
                pltpu.VMEM((n_chunks, 1, KW), f32),
                pltpu.VMEM((n_chunks, 1, 128), f32),
                blk(f32), blk(f32),
                pltpu.VMEM((tb, D_MODEL), f32),
                pltpu.VMEM((tb, D_MODEL), f32),
                pltpu.VMEM((HEADS, HEAD_DIM, HEAD_DIM), f32),
                pltpu.VMEM((HEADS, HEAD_DIM, HEAD_DIM), f32)]
    n_blocks = seq // tb

    def next_block(b, i):
        flat = jnp.minimum(b * n_blocks + i + 1, bsz * n_blocks - 1)
        return flat // n_blocks, flat % n_blocks, 0

    n_steps = bsz * n_blocks
    up_rows, down_rows = wup.shape[0] // n_steps, wdown.shape[0] // n_steps
    assert up_rows * n_steps == wup.shape[0] and down_rows * n_steps == wdown.shape[0]
    assert up_rows % BF16_ROWS == 0 and down_rows % BF16_ROWS == 0

    def step_rows(b, i):
        return b * n_blocks + i, 0

    return pl.pallas_call(
        _mixer_kernel,
        grid=(bsz, n_blocks),
        in_specs=[pl.BlockSpec((None, tb, D_MODEL), lambda b, i: (b, i, 0)),
                  pl.BlockSpec((None, tb, D_MODEL), next_block),
                  pl.BlockSpec((up_rows, wup.shape[1]), step_rows),
                  pl.BlockSpec((down_rows, wdown.shape[1]), step_rows)]
                 + [_resident(c.shape) for c in consts],
        out_specs=[pl.BlockSpec((None, tb, D_MODEL), lambda b, i: (b, i, 0)),
                   pl.BlockSpec((up_rows, wup.shape[1]), step_rows),
                   pl.BlockSpec((down_rows, wdown.shape[1]), step_rows)],
        out_shape=[jax.ShapeDtypeStruct(x.shape, f32),
                   jax.ShapeDtypeStruct(wup.shape, bf16),
                   jax.ShapeDtypeStruct(wdown.shape, bf16)],
        scratch_shapes=scratch,
        compiler_params=pltpu.CompilerParams(dimension_semantics=("arbitrary", "arbitrary"),
                                             vmem_limit_bytes=VMEM_LIMIT_BYTES),
        name="token_mixer",
    )(x, x, wup, wdown, *consts)


def _mlp_call(h, wup, wdown, g2, b2):
    m = h.shape[0]
    tm = min(MLP_BLOCK, m)
    assert m % tm == 0
    consts = (wup, wdown, g2, b2)
    return pl.pallas_call(
        _mlp_kernel,
        grid=(m // tm,),
        in_specs=[pl.BlockSpec((tm, D_MODEL), lambda i: (i, 0))] + [_resident(c.shape) for c in consts],
        out_specs=pl.BlockSpec((tm, D_MODEL), lambda i: (i, 0)),
        out_shape=jax.ShapeDtypeStruct(h.shape, jnp.float32),
        compiler_params=pltpu.CompilerParams(dimension_semantics=("arbitrary",),
                                             vmem_limit_bytes=VMEM_LIMIT_BYTES),
        name="relu2_mlp",
    )(h, *consts)


def kernel(x, hg_lb_logits, w_in, conv_w, hg_norm_w, dn_A_log, dn_dt_bias, dn_norm_w, w_branch_a, w_branch_b, w_o, ln1_g, ln1_b, w_up, w_down, ln2_g, ln2_b):
    bsz, seq, _ = x.shape
    f32 = jnp.float32
    bf16 = jnp.bfloat16
    o = _OFFS
    l = 0
    win = w_in[l].astype(bf16)
    wgate = win[:, o[8]:o[10]]
    wab = jnp.pad(win[:, o[6]:o[8]], ((0, 0), (0, 128 - 2 * HEADS)))
    zeros4 = jnp.zeros((HEADS,), f32)
    alog8 = jnp.concatenate([zeros4, dn_A_log[l].astype(f32)])
    dtb8 = jnp.concatenate([zeros4, dn_dt_bias[l].astype(f32)])
    alogr = jnp.pad(alog8, (0, 128 - 2 * HEADS)).reshape(1, 128)
    dtbr = jnp.pad(dtb8, (0, 128 - 2 * HEADS)).reshape(1, 128)
    h1, wup_bf, wdown_bf = _mixer_call(
        x, w_up[l], w_down[l], hg_lb_logits.astype(f32), win, wgate, wab, conv_w[l].astype(f32),
        hg_norm_w[l].reshape(1, KW), alogr, dtbr, alog8.reshape(8, 1), dtb8.reshape(8, 1),
        jnp.tile(dn_norm_w[l], HEADS).reshape(1, KW),
        w_branch_a[l].astype(bf16), w_branch_b[l].astype(bf16), w_o[l].astype(bf16),
        ln1_g[l].reshape(1, D_MODEL), ln1_b[l].reshape(1, D_MODEL))
    out = _mlp_call(h1.reshape(bsz * seq, D_MODEL), wup_bf, wdown_bf,
                    ln2_g[l].reshape(1, D_MODEL), ln2_b[l].reshape(1, D_MODEL))
    return out.reshape(bsz, seq, D_MODEL)
```

```python
import functools

import numpy as np
import jax
import jax.numpy as jnp
from jax import lax
from jax.experimental import pallas as pl
from jax.experimental.pallas import tpu as pltpu

D_MODEL = 1024
HEADS = 4
HEAD_DIM = 128
KW = HEADS * HEAD_DIM
CHUNK = 64
CONV_W = 4
D_FF = 4 * D_MODEL
DEPTH = 1
ALPHA = (2 * DEPTH) ** 0.25
LN_EPS = 1e-5
RMS_EPS = 1e-6
L2_EPS = 1e-6
QK_SCALE = HEAD_DIM ** -0.5
NEG_LOG2E = -1.4426950408889634

TOKEN_BLOCK = 256
PREP_UNROLL = 4
GATE_TILE = 256
GATE_TILES_AFTER_PROJECTIONS = 4
MLP_BLOCK = 1024
MLP_TAIL_SPLIT = 4
FF_BLOCK = 1024
SUBLANES = 8
BF16_ROWS = 16
CAST_ROWS = 128
CONV_PAD = 8
VMEM_LIMIT_BYTES = 56 * 1024 * 1024

_SIZES = (KW, KW, KW, KW, 3 * KW, KW, HEADS, HEADS, D_MODEL, D_MODEL)
_OFFS = np.concatenate([[0], np.cumsum(_SIZES)]).tolist()

COL_HG_QFI = _OFFS[0]
COL_HG_G = _OFFS[3]
COL_DN_QKV = _OFFS[4]
COL_DN_Z = _OFFS[5]

_LEVELS = (32, 16, 8, 4, 2, 1)


def _build_constants():
    t = np.arange(CHUNK)[:, None]
    r = np.arange(CHUNK)[None, :]
    rows = [r <= t, r > t]
    masks = []
    for m in _LEVELS:
        last_key = (t // (2 * m)) * (2 * m) + m - 1
        is_query = (t % (2 * m)) >= m
        rows.append((is_query & (r > last_key) & (r <= t)) | (~is_query & (r > t) & (r <= last_key)))
        masks.append(((t // (2 * m)) == (r // (2 * m))) & is_query & ((r % (2 * m)) < m))
    masks.append(t == r)
    wexp = np.concatenate(rows, 0).astype(np.float32)
    masks = np.concatenate(masks, 0).astype(np.float32)
    return wexp, masks


_WEXP, _MASKS = _build_constants()


def _dot(a, b):
    return jnp.dot(a, b, preferred_element_type=jnp.float32)


def _dot_nt(a, b):
    return lax.dot_general(a, b, (((1,), (1,)), ((), ())), preferred_element_type=jnp.float32)


def _dot_tn(a, b):
    return lax.dot_general(a, b, (((0,), (0,)), ((), ())), preferred_element_type=jnp.float32)


def _bf(x):
    return x.astype(jnp.bfloat16)


def _split(x):
    hi = _bf(x)
    return hi, _bf(x - hi.astype(jnp.float32))


def _sigmoid(x):
    return 1.0 / (1.0 + jnp.exp2(x * NEG_LOG2E))


def _silu(x):
    return x * _sigmoid(x)


def _softplus(x):
    return jnp.maximum(x, 0.0) + jnp.log1p(jnp.exp(-jnp.abs(x)))


def _layer_norm(x, g, b):
    mu = jnp.mean(x, axis=-1, keepdims=True)
    xc = x - mu
    var = jnp.mean(xc * xc, axis=-1, keepdims=True)
    return xc * lax.rsqrt(var + LN_EPS) * g + b


def _mixer_kernel(x_ref, xnext_ref, wup32_ref, wdown32_ref, win32_hbm, lbl_ref, convw_ref, hgnw_ref,
                  alogr_ref, dtbr_ref, alogc_ref, dtbc_ref, dnnw_ref, wa_ref, wb_ref, wo_ref,
                  g1_ref, b1_ref, wexp_ref, masks_ref, bdtriu_ref, bdtril_ref, out_ref, wupb_ref, wdownb_ref,
                  win_ref, wgate_ref, wab_ref, wstage, wsem, xbuf, xnbuf, hq, hk, hv, hlf, hf, hb, dq, dk, dv, oa, ob, du, gbuf, gcrow, convbuf, rawhg, qin, kdec, wq, qkb,
                  kvinc, hdec, gl, sg, sz, sga, sgb, shg, sdn):
    tb = x_ref.shape[0]
    n_chunks = tb // CHUNK

    wupb_ref[...] = _bf(wup32_ref[...])
    wdownb_ref[...] = _bf(wdown32_ref[...])

    @pl.when((pl.program_id(0) == 0) & (pl.program_id(1) == 0))
    def _load_projection_weight():
        n_slabs = D_MODEL // CAST_ROWS

        def slab_copy(r):
            return pltpu.make_async_copy(win32_hbm.at[pl.ds(r * CAST_ROWS, CAST_ROWS), :],
                                         wstage.at[r % 2], wsem.at[r % 2])

        slab_copy(0).start()
        for r in range(n_slabs):
            if r + 1 < n_slabs:
                slab_copy(r + 1).start()
            slab_copy(r).wait()
            rows = slice(r * CAST_ROWS, (r + 1) * CAST_ROWS)
            v = wstage[r % 2]
            win_ref[rows, :] = _bf(v[:, 0:_OFFS[6]])
            wgate_ref[rows, :] = _bf(v[:, _OFFS[8]:_OFFS[10]])
            wab_ref[rows, :] = _bf(jnp.concatenate(
                [v[:, _OFFS[6]:_OFFS[8]], jnp.zeros((CAST_ROWS, 128 - 2 * HEADS), jnp.float32)], axis=1))

    @pl.when(pl.program_id(1) == 0)
    def _reset_carries():
        shg[...] = jnp.zeros_like(shg)
        sdn[...] = jnp.zeros_like(sdn)
        convbuf[:, 0:CONV_PAD, :] = jnp.zeros((HEADS, CONV_PAD, 3 * HEAD_DIM), jnp.float32)


    def project_head(c, xsrc=xbuf):
        cols = ([COL_DN_QKV + k * KW + c * HEAD_DIM for k in range(3)]
                + [COL_HG_QFI + k * KW + c * HEAD_DIM for k in range(3)])
        w_head = jnp.concatenate([win_ref[:, o:o + HEAD_DIM] for o in cols], axis=1)
        raw = _dot(xsrc[...], w_head)
        convbuf[c, CONV_PAD:CONV_PAD + tb, :] = raw[:, 0:3 * HEAD_DIM]
        rawhg[c] = raw[:, 3 * HEAD_DIM:6 * HEAD_DIM]

    def head_epilogue(c):
        cw = jnp.concatenate([convw_ref[:, k * KW + c * HEAD_DIM:k * KW + (c + 1) * HEAD_DIM]
                              for k in range(3)], axis=1)
        base = CONV_PAD - (CONV_W - 1)
        acc = convbuf[c, base:base + tb, :] * cw[0:1, :]
        for j in range(1, CONV_W):
            acc = acc + convbuf[c, base + j:base + j + tb, :] * cw[j:j + 1, :]
        convbuf[c, 0:CONV_PAD, :] = convbuf[c, tb:tb + CONV_PAD, :]
        qkv = _silu(acc)
        qh = qkv[:, 0:HEAD_DIM]
        kh = qkv[:, HEAD_DIM:2 * HEAD_DIM]
        dq[c] = qh * (lax.rsqrt(jnp.sum(qh * qh, axis=-1, keepdims=True) + L2_EPS) * QK_SCALE)
        dk[c] = kh * lax.rsqrt(jnp.sum(kh * kh, axis=-1, keepdims=True) + L2_EPS)
        dv[c] = qkv[:, 2 * HEAD_DIM:3 * HEAD_DIM]
        lg = lbl_ref[:, c * HEAD_DIM:(c + 1) * HEAD_DIM]
        e = jnp.exp(lg - jnp.max(lg, axis=0, keepdims=True))
        lb = e[0:1, :] / jnp.sum(e, axis=0, keepdims=True)
        f = lb + (1.0 - lb) * _sigmoid(rawhg[c, :, HEAD_DIM:2 * HEAD_DIM])
        hlf[c] = jnp.log(f)
        hf[c] = f
        hk[c] = 1.0 - f
        hq[c] = _silu(rawhg[c, :, 0:HEAD_DIM]) * QK_SCALE
        hv[c] = rawhg[c, :, 2 * HEAD_DIM:3 * HEAD_DIM]

    gate_tiles = ([(win_ref, COL_HG_G + t, sg, t, _silu) for t in range(0, KW, GATE_TILE)]
                  + [(win_ref, COL_DN_Z + t, sz, t, _silu) for t in range(0, KW, GATE_TILE)]
                  + [(wgate_ref, t, sga, t, _sigmoid) for t in range(0, D_MODEL, GATE_TILE)]
                  + [(wgate_ref, D_MODEL + t, sgb, t, _sigmoid) for t in range(0, D_MODEL, GATE_TILE)])
    gate_tiles_todo = list(range(len(gate_tiles)))

    def issue_gate_tiles(count):
        for _ in range(min(count, len(gate_tiles_todo))):
            w_ref, src, dst, off, act = gate_tiles[gate_tiles_todo.pop(0)]
            dst[:, off:off + GATE_TILE] = act(_dot(xbuf[...], w_ref[:, src:src + GATE_TILE]))

    @pl.when((pl.program_id(0) == 0) & (pl.program_id(1) == 0))
    def _first_projection():
        xnbuf[...] = _bf(x_ref[...])
        project_head(0, xnbuf)

    xbuf[...] = xnbuf[...]

    def cumulate_forget(h0):
        lf_hi, lf_lo = _split(jnp.concatenate([hlf[h0], hlf[h0 + 1]], axis=1))
        bc = _dot(bdtril_ref[...], lf_hi) + _dot(bdtril_ref[...], lf_lo)
        hb[h0] = bc[:, 0:HEAD_DIM]
        hb[h0 + 1] = bc[:, HEAD_DIM:2 * HEAD_DIM]

    for c in range(1, HEADS):
        head_epilogue(c - 1)
        project_head(c)
        if c % 2 == 0:
            cumulate_forget(c - 2)
    head_epilogue(HEADS - 1)
    issue_gate_tiles(GATE_TILES_AFTER_PROJECTIONS)
    cumulate_forget(HEADS - 2)

    pab = _dot(xbuf[...], wab_ref[...])
    g_col = -jnp.exp(alogr_ref[...]) * _softplus(pab + dtbr_ref[...])
    lane = lax.broadcasted_iota(jnp.int32, pab.shape, 1)
    gbuf[...] = jnp.where(lane < HEADS, _sigmoid(pab), g_col)
    pabt = jnp.transpose(pab)[0:2 * HEADS, :]
    g_row = -jnp.exp(alogc_ref[...]) * _softplus(pabt + dtbc_ref[...])
    gr_hi, gr_lo = _split(g_row)
    gc_row = _dot(gr_hi, bdtriu_ref[...]) + _dot(gr_lo, bdtriu_ref[...])
    for c in range(n_chunks):
        gcrow[c] = gc_row[:, c * CHUNK:(c + 1) * CHUNK]

    ri = lax.broadcasted_iota(jnp.int32, (CHUNK, CHUNK), 0)
    ci = lax.broadcasted_iota(jnp.int32, (CHUNK, CHUNK), 1)
    causal = ri >= ci
    strict = ri > ci
    trow = lax.broadcasted_iota(jnp.int32, (CHUNK, HEAD_DIM), 0)

    n_lev = len(_LEVELS)
    heads = range(HEADS)
    hsl = [slice(h * HEAD_DIM, (h + 1) * HEAD_DIM) for h in heads]

    def level_mask(l):
        return masks_ref[l * CHUNK:(l + 1) * CHUNK, :]

    units = [(u, h) for u in range(PREP_UNROLL) for h in heads]
    n_units = range(len(units))

    def prepare_chunks(j):
        cs = [j * PREP_UNROLL + u for u in range(PREP_UNROLL)]
        rows = [slice(c * CHUNK, (c + 1) * CHUNK) for c in cs]
        wexp = wexp_ref[...]
        gb, eg, gcr = [], [], []
        for u in range(PREP_UNROLL):
            gb.append(gbuf[rows[u], :])
            g_hi, g_lo = _split(gb[u])
            eg.append(_dot(wexp[0:2 * CHUNK], g_hi) + _dot(wexp[0:2 * CHUNK], g_lo))
            gcr.append(gcrow[cs[u]])

        def ld(ref, i):
            u, h = units[i]
            return ref[h, rows[u], :]

        dkk = [ld(dk, i) for i in n_units]
        dkb = [_bf(k) for k in dkk]
        beta = [gb[u][:, h:h + 1] for u, h in units]
        gcol = [eg[u][0:CHUNK, HEADS + h:HEADS + h + 1] for u, h in units]
        decay = [jnp.where(causal, jnp.exp(jnp.minimum(gcol[i] - gcr[u][HEADS + h:HEADS + h + 1, :], 0.0)), 0.0)
                 for i, (u, h) in enumerate(units)]
        kbeta = [dkk[i] * beta[i] for i in n_units]
        lmat = [jnp.where(strict, _dot_nt(_bf(kbeta[i]), dkb[i]) * decay[i], 0.0) for i in n_units]
        for i, (u, h) in enumerate(units):
            qkb[cs[u] * HEADS + h] = _bf(_dot_nt(_bf(ld(dq, i)), dkb[i]) * decay[i])

        hqv = [ld(hq, i) for i in n_units]
        hkv = [ld(hk, i) for i in n_units]
        hqb = [_bf(q) for q in hqv]
        hkb = [_bf(k) for k in hkv]
        sc = [level_mask(n_lev) * _dot_nt(hqb[i], hkb[i]) for i in n_units]
        nmat = [-(level_mask(n_lev - 1) * lmat[i]) for i in n_units]

        hbv = [ld(hb, i) for i in n_units]

        def level_factor(i, l):
            m = _LEVELS[l]
            if m >= SUBLANES // 2:
                ref = jnp.concatenate(
                    [jnp.broadcast_to(hbv[i][r:r + 1, :], (SUBLANES, HEAD_DIM))
                     for r in [(g * SUBLANES // (2 * m)) * (2 * m) + m - 1 for g in range(CHUNK // SUBLANES)]],
                    axis=0)
                return jnp.exp(-jnp.abs(hbv[i] - ref))
            f = ld(hf, i)
            if m == 1:
                return jnp.where(trow % 2 == 1, f, 1.0)
            f_prev = pltpu.roll(f, 1, 0)
            f_next = pltpu.roll(f, CHUNK - 1, 0)
            return jnp.where(trow % 4 == 0, f_next,
                             jnp.where(trow % 4 == 1, 1.0, jnp.where(trow % 4 == 2, f, f_prev * f)))

        def score_level(l):
            for i in n_units:
                zb = _bf(level_factor(i, l))
                sc[i] = sc[i] + level_mask(l) * _dot_nt(hqb[i] * zb, hkb[i] * zb)

        score_level(n_lev - 1)
        for l in range(n_lev - 2, -1, -1):
            cm = [level_mask(l) * lmat[i] for i in n_units]
            nb = [_bf(nmat[i]) for i in n_units]
            xm = [cm[i] + _dot(nb[i], _bf(cm[i])) for i in n_units]
            score_level(l)
            for i in n_units:
                nmat[i] = nmat[i] - (xm[i] + _dot(_bf(xm[i]), nb[i]))

        hvb = [_bf(ld(hv, i)) for i in n_units]
        for i, (u, h) in enumerate(units):
            oa[rows[u], hsl[h]] = _dot(_bf(sc[i]), hvb[i])
        for i, (u, h) in enumerate(units):
            kd = hkv[i] * jnp.exp(hbv[i][CHUNK - 1:CHUNK, :] - hbv[i])
            kvinc[cs[u] * HEADS + h] = _dot_tn(hvb[i], _bf(kd))
        for i, (u, h) in enumerate(units):
            qin[rows[u], hsl[h]] = _bf(hqv[i] * jnp.exp(hbv[i]))
            hdec[cs[u], :, hsl[h]] = jnp.exp(hbv[i][CHUNK - 1:CHUNK, :])

        for i, (u, h) in enumerate(units):
            eg_col = jnp.exp(gcol[i])
            rhs = jnp.concatenate([ld(dv, i) * beta[i], kbeta[i] * eg_col], axis=1)
            uw = rhs + _dot(_bf(nmat[i]), _bf(rhs))
            du[rows[u], hsl[h]] = uw[:, 0:HEAD_DIM]
            wq[cs[u], 0:CHUNK, hsl[h]] = _bf(uw[:, HEAD_DIM:2 * HEAD_DIM])
            wq[cs[u], CHUNK:2 * CHUNK, hsl[h]] = _bf(ld(dq, i) * eg_col)
            kdec[rows[u], hsl[h]] = _bf(dkk[i] * jnp.exp(eg[u][CHUNK:2 * CHUNK, HEADS + h:HEADS + h + 1]))
        for u in range(PREP_UNROLL):
            gl[cs[u]] = jnp.exp(eg[u][CHUNK - 1:CHUNK, :])

    for j in range(n_chunks // PREP_UNROLL):
        prepare_chunks(j)

    for c in range(n_chunks):
        rows = slice(c * CHUNK, (c + 1) * CHUNK)
        s = [sdn[h] for h in heads]
        wqc = wq[c]
        ws = [_dot(wqc[:, hsl[h]], _bf(s[h])) for h in heads]
        st = [shg[h] for h in heads]
        dec = hdec[c]
        for h in heads:
            oa[rows, hsl[h]] = oa[rows, hsl[h]] + _dot_nt(qin[rows, hsl[h]], _bf(st[h]))
            shg[h] = st[h] * dec[:, hsl[h]] + kvinc[c * HEADS + h]
        issue_gate_tiles(-(-len(gate_tiles_todo) // (2 * (n_chunks - c))))
        vnb = [_bf(du[rows, hsl[h]] - ws[h][0:CHUNK]) for h in heads]
        glc = gl[c]
        for h in heads:
            ob[rows, hsl[h]] = ws[h][CHUNK:2 * CHUNK] + _dot(qkb[c * HEADS + h], vnb[h])
        for h in heads:
            sdn[h] = s[h] * glc[:, HEADS + h:HEADS + h + 1] + _dot_tn(kdec[rows, hsl[h]], vnb[h])
        issue_gate_tiles(-(-len(gate_tiles_todo) // (2 * (n_chunks - c) - 1)))

    def gated_norm(o_ref, w_row, gate):
        parts = []
        for h in range(HEADS):
            oh = o_ref[:, h * HEAD_DIM:(h + 1) * HEAD_DIM]
            parts.append(oh * lax.rsqrt(jnp.mean(oh * oh, axis=-1, keepdims=True) + RMS_EPS))
        return jnp.concatenate(parts, axis=1) * w_row * gate

    na = gated_norm(oa, hgnw_ref[...], sg[...])
    nb_ = gated_norm(ob, dnnw_ref[...], sz[...])
    merged = sga[...] * _dot(_bf(na), wa_ref[...]) + sgb[...] * _dot(_bf(nb_), wb_ref[...])
    mix = _dot(_bf(merged), wo_ref[...])
    out_ref[...] = _layer_norm(ALPHA * x_ref[...] + mix, g1_ref[...], b1_ref[...])

    xnbuf[...] = _bf(xnext_ref[...])
    project_head(0, xnbuf)


def _mlp_kernel(h_ref, wup_ref, wdown_ref, g2_ref, b2_ref, out_ref):
    tm = h_ref.shape[0]
    hb = _bf(h_ref[...])
    n_ff = D_FF // FF_BLOCK
    acc = None
    for j in range(n_ff - 1):
        up = _dot(hb, wup_ref[:, j * FF_BLOCK:(j + 1) * FF_BLOCK])
        act = jnp.square(jnp.maximum(up, 0.0))
        part = _dot(_bf(act), wdown_ref[j * FF_BLOCK:(j + 1) * FF_BLOCK, :])
        acc = part if acc is None else acc + part
    up = _dot(hb, wup_ref[:, (n_ff - 1) * FF_BLOCK:n_ff * FF_BLOCK])
    actb = _bf(jnp.square(jnp.maximum(up, 0.0)))
    for r in range(MLP_TAIL_SPLIT):
        rows = slice(r * tm // MLP_TAIL_SPLIT, (r + 1) * tm // MLP_TAIL_SPLIT)
        part = _dot(actb[rows], wdown_ref[(n_ff - 1) * FF_BLOCK:n_ff * FF_BLOCK, :])
        out_ref[rows, :] = _layer_norm(ALPHA * h_ref[rows, :] + acc[rows] + part, g2_ref[...], b2_ref[...])


def _resident(shape):
    nd = len(shape)
    return pl.BlockSpec(shape, lambda *_: (0,) * nd, pipeline_mode=pl.Buffered(1))


def _mixer_call(x, wup, wdown, win32, lbl, convw, hgnw, alogr, dtbr, alogc, dtbc, dnnw,
                wa, wb, wo, g1, b1):
    bsz, seq, _ = x.shape
    tb = min(TOKEN_BLOCK, seq)
    assert seq % tb == 0 and tb % (CHUNK * PREP_UNROLL) == 0
    n_chunks = tb // CHUNK
    wexp = jnp.asarray(_WEXP, jnp.bfloat16)
    masks = jnp.asarray(_MASKS, jnp.float32)
    tt = np.arange(tb)
    bdtriu = jnp.asarray(((tt[:, None] <= tt[None, :]) & ((tt[:, None] // CHUNK) == (tt[None, :] // CHUNK))),
                         jnp.bfloat16)
    consts = (lbl, convw, hgnw, alogr, dtbr, alogc, dtbc, dnnw, wa, wb, wo,
              g1, b1, wexp, masks, bdtriu, bdtriu.T)
    f32 = jnp.float32
    blk = functools.partial(pltpu.VMEM, (tb, KW))
    bf16 = jnp.bfloat16
    per_head = functools.partial(pltpu.VMEM, (HEADS, tb, HEAD_DIM))
    scratch = [pltpu.VMEM((D_MODEL, _OFFS[6]), bf16),
               pltpu.VMEM((D_MODEL, 2 * D_MODEL), bf16),
               pltpu.VMEM((D_MODEL, 128), bf16),
               pltpu.VMEM((2, CAST_ROWS, win32.shape[1]), f32),
               pltpu.SemaphoreType.DMA((2,)),
               pltpu.VMEM((tb, D_MODEL), bf16), pltpu.VMEM((tb, D_MODEL), bf16)]
    scratch += [per_head(f32) for _ in range(9)]
    scratch += [blk(f32) for _ in range(3)]
    scratch += [pltpu.VMEM((tb, 128), f32),
                pltpu.VMEM((n_chunks, 8, CHUNK), f32),
                pltpu.VMEM((HEADS, tb + CONV_PAD, 3 * HEAD_DIM), f32),
                pltpu.VMEM((HEADS, tb, 3 * HEAD_DIM), f32),
                blk(bf16), blk(bf16),
                pltpu.VMEM((n_chunks, 2 * CHUNK, KW), bf16),
                pltpu.VMEM((n_chunks * HEADS, CHUNK, CHUNK), bf16),
                pltpu.VMEM((n_chunks * HEADS, HEAD_DIM, HEAD_DIM), f32),
                pltpu.VMEM((n_chunks, 1, KW), f32),
                pltpu.VMEM((n_chunks, 1, 128), f32),
                blk(f32), blk(f32),
                pltpu.VMEM((tb, D_MODEL), f32),
                pltpu.VMEM((tb, D_MODEL), f32),
                pltpu.VMEM((HEADS, HEAD_DIM, HEAD_DIM), f32),
                pltpu.VMEM((HEADS, HEAD_DIM, HEAD_DIM), f32)]
    n_blocks = seq // tb

    def next_block(b, i):
        flat = jnp.minimum(b * n_blocks + i + 1, bsz * n_blocks - 1)
        return flat // n_blocks, flat % n_blocks, 0

    n_steps = bsz * n_blocks
    up_rows, down_rows = wup.shape[0] // n_steps, wdown.shape[0] // n_steps
    assert up_rows * n_steps == wup.shape[0] and down_rows * n_steps == wdown.shape[0]
    assert up_rows % BF16_ROWS == 0 and down_rows % BF16_ROWS == 0

    def step_rows(b, i):
        return b * n_blocks + i, 0

    return pl.pallas_call(
        _mixer_kernel,
        grid=(bsz, n_blocks),
        in_specs=[pl.BlockSpec((None, tb, D_MODEL), lambda b, i: (b, i, 0)),
                  pl.BlockSpec((None, tb, D_MODEL), next_block),
                  pl.BlockSpec((up_rows, wup.shape[1]), step_rows),
                  pl.BlockSpec((down_rows, wdown.shape[1]), step_rows),
                  pl.BlockSpec(memory_space=pl.ANY)]
                 + [_resident(c.shape) for c in consts],
        out_specs=[pl.BlockSpec((None, tb, D_MODEL), lambda b, i: (b, i, 0)),
                   pl.BlockSpec((up_rows, wup.shape[1]), step_rows),
                   pl.BlockSpec((down_rows, wdown.shape[1]), step_rows)],
        out_shape=[jax.ShapeDtypeStruct(x.shape, f32),
                   jax.ShapeDtypeStruct(wup.shape, bf16),
                   jax.ShapeDtypeStruct(wdown.shape, bf16)],
        scratch_shapes=scratch,
        compiler_params=pltpu.CompilerParams(dimension_semantics=("arbitrary", "arbitrary"),
                                             vmem_limit_bytes=VMEM_LIMIT_BYTES),
        name="token_mixer",
    )(x, x, wup, wdown, win32, *consts)


def _mlp_call(h, wup, wdown, g2, b2):
    m = h.shape[0]
    tm = min(MLP_BLOCK, m)
    assert m % tm == 0
    consts = (wup, wdown, g2, b2)
    return pl.pallas_call(
        _mlp_kernel,
        grid=(m // tm,),
        in_specs=[pl.BlockSpec((tm, D_MODEL), lambda i: (i, 0))] + [_resident(c.shape) for c in consts],
        out_specs=pl.BlockSpec((tm, D_MODEL), lambda i: (i, 0)),
        out_shape=jax.ShapeDtypeStruct(h.shape, jnp.float32),
        compiler_params=pltpu.CompilerParams(dimension_semantics=("arbitrary",),
                                             vmem_limit_bytes=VMEM_LIMIT_BYTES),
        name="relu2_mlp",
    )(h, *consts)


def kernel(x, hg_lb_logits, w_in, conv_w, hg_norm_w, dn_A_log, dn_dt_bias, dn_norm_w, w_branch_a, w_branch_b, w_o, ln1_g, ln1_b, w_up, w_down, ln2_g, ln2_b):
    bsz, seq, _ = x.shape
    f32 = jnp.float32
    bf16 = jnp.bfloat16
    o = _OFFS
    l = 0
    zeros4 = jnp.zeros((HEADS,), f32)
    alog8 = jnp.concatenate([zeros4, dn_A_log[l].astype(f32)])
    dtb8 = jnp.concatenate([zeros4, dn_dt_bias[l].astype(f32)])
    alogr = jnp.pad(alog8, (0, 128 - 2 * HEADS)).reshape(1, 128)
    dtbr = jnp.pad(dtb8, (0, 128 - 2 * HEADS)).reshape(1, 128)
    h1, wup_bf, wdown_bf = _mixer_call(
        x, w_up[l], w_down[l], w_in[l], hg_lb_logits.astype(f32), conv_w[l].astype(f32),
        hg_norm_w[l].reshape(1, KW), alogr, dtbr, alog8.reshape(8, 1), dtb8.reshape(8, 1),
        jnp.tile(dn_norm_w[l], HEADS).reshape(1, KW),
        w_branch_a[l].astype(bf16), w_branch_b[l].astype(bf16), w_o[l].astype(bf16),
        ln1_g[l].reshape(1, D_MODEL), ln1_b[l].reshape(1, D_MODEL))
    out = _mlp_call(h1.reshape(bsz * seq, D_MODEL), wup_bf, wdown_bf,
                    ln2_g[l].reshape(1, D_MODEL), ln2_b[l].reshape(1, D_MODEL))
    return out.reshape(bsz, seq, D_MODEL)
```

```python
import functools

import numpy as np
import jax
import jax.numpy as jnp
from jax import lax
from jax.experimental import pallas as pl
from jax.experimental.pallas import tpu as pltpu

D_MODEL = 1024
HEADS = 4
HEAD_DIM = 128
KW = HEADS * HEAD_DIM
CHUNK = 64
CONV_W = 4
D_FF = 4 * D_MODEL
DEPTH = 1
LAYER = 0
ALPHA = (2 * DEPTH) ** 0.25
LN_EPS = 1e-5
RMS_EPS = 1e-6
L2_EPS = 1e-6
QK_SCALE = HEAD_DIM ** -0.5
NEG_LOG2E = -1.4426950408889634

TOKEN_BLOCK = 256
PREP_UNROLL = 4
GATE_TILE = 256
GATE_TILES_AFTER_PROJECTIONS = 4
MLP_BLOCK = 1024
MLP_TAIL_SPLIT = 4
FF_BLOCK = 1024
SUBLANES = 8
BF16_ROWS = 16
CAST_ROWS = 128
CONV_PAD = 8
VMEM_LIMIT_BYTES = 56 * 1024 * 1024

_SIZES = (KW, KW, KW, KW, 3 * KW, KW, HEADS, HEADS, D_MODEL, D_MODEL)
_OFFS = np.concatenate([[0], np.cumsum(_SIZES)]).tolist()

COL_HG_QFI = _OFFS[0]
COL_HG_G = _OFFS[3]
COL_DN_QKV = _OFFS[4]
COL_DN_Z = _OFFS[5]

_LEVELS = (32, 16, 8, 4, 2, 1)


def _build_constants():
    t = np.arange(CHUNK)[:, None]
    r = np.arange(CHUNK)[None, :]
    rows = [r <= t, r > t]
    masks = []
    for m in _LEVELS:
        last_key = (t // (2 * m)) * (2 * m) + m - 1
        is_query = (t % (2 * m)) >= m
        rows.append((is_query & (r > last_key) & (r <= t)) | (~is_query & (r > t) & (r <= last_key)))
        masks.append(((t // (2 * m)) == (r // (2 * m))) & is_query & ((r % (2 * m)) < m))
    masks.append(t == r)
    wexp = np.concatenate(rows, 0).astype(np.float32)
    masks = np.concatenate(masks, 0).astype(np.float32)
    return wexp, masks


_WEXP, _MASKS = _build_constants()


def _dot(a, b):
    return jnp.dot(a, b, preferred_element_type=jnp.float32)


def _dot_nt(a, b):
    return lax.dot_general(a, b, (((1,), (1,)), ((), ())), preferred_element_type=jnp.float32)


def _dot_tn(a, b):
    return lax.dot_general(a, b, (((0,), (0,)), ((), ())), preferred_element_type=jnp.float32)


def _bf(x):
    return x.astype(jnp.bfloat16)


def _split(x):
    hi = _bf(x)
    return hi, _bf(x - hi.astype(jnp.float32))


def _sigmoid(x):
    return 1.0 / (1.0 + jnp.exp2(x * NEG_LOG2E))


def _silu(x):
    return x * _sigmoid(x)


def _softplus(x):
    return jnp.maximum(x, 0.0) + jnp.log1p(jnp.exp(-jnp.abs(x)))


def _layer_norm(x, g, b):
    mu = jnp.mean(x, axis=-1, keepdims=True)
    xc = x - mu
    var = jnp.mean(xc * xc, axis=-1, keepdims=True)
    return xc * lax.rsqrt(var + LN_EPS) * g + b


def _mixer_kernel(x_ref, xnext_ref, wup32_ref, wdown32_ref, win32_hbm, lbl_ref, convw_ref, hgnw_ref,
                  alogr_ref, dtbr_ref, alogc_ref, dtbc_ref, dnnw_ref, wa_ref, wb_ref, wo_ref,
                  g1_ref, b1_ref, wexp_ref, masks_ref, bdtriu_ref, bdtril_ref, out_ref, wupb_ref, wdownb_ref,
                  win_ref, wgate_ref, wab_ref, wstage, wsem, xbuf, xnbuf, hq, hk, hv, hlf, hf, hb, dq, dk, dv, oa, ob, du, gbuf, gcrow, convbuf, rawhg, qin, kdec, wq, qkb,
                  kvinc, hdec, gl, sg, sz, sga, sgb, shg, sdn):
    tb = x_ref.shape[0]
    n_chunks = tb // CHUNK

    wupb_ref[...] = _bf(wup32_ref[...])
    wdownb_ref[...] = _bf(wdown32_ref[...])

    @pl.when((pl.program_id(0) == 0) & (pl.program_id(1) == 0))
    def _load_projection_weight():
        n_slabs = D_MODEL // CAST_ROWS

        def slab_copy(r):
            return pltpu.make_async_copy(win32_hbm.at[LAYER, pl.ds(r * CAST_ROWS, CAST_ROWS), :],
                                         wstage.at[r % 2], wsem.at[r % 2])

        slab_copy(0).start()
        for r in range(n_slabs):
            if r + 1 < n_slabs:
                slab_copy(r + 1).start()
            slab_copy(r).wait()
            rows = slice(r * CAST_ROWS, (r + 1) * CAST_ROWS)
            v = wstage[r % 2]
            win_ref[rows, :] = _bf(v[:, 0:_OFFS[6]])
            wgate_ref[rows, :] = _bf(v[:, _OFFS[8]:_OFFS[10]])
            wab_ref[rows, :] = _bf(jnp.concatenate(
                [v[:, _OFFS[6]:_OFFS[8]], jnp.zeros((CAST_ROWS, 128 - 2 * HEADS), jnp.float32)], axis=1))

    @pl.when(pl.program_id(1) == 0)
    def _reset_carries():
        shg[...] = jnp.zeros_like(shg)
        sdn[...] = jnp.zeros_like(sdn)
        convbuf[:, 0:CONV_PAD, :] = jnp.zeros((HEADS, CONV_PAD, 3 * HEAD_DIM), jnp.float32)


    def project_head(c, xsrc=xbuf):
        cols = ([COL_DN_QKV + k * KW + c * HEAD_DIM for k in range(3)]
                + [COL_HG_QFI + k * KW + c * HEAD_DIM for k in range(3)])
        w_head = jnp.concatenate([win_ref[:, o:o + HEAD_DIM] for o in cols], axis=1)
        raw = _dot(xsrc[...], w_head)
        convbuf[c, CONV_PAD:CONV_PAD + tb, :] = raw[:, 0:3 * HEAD_DIM]
        rawhg[c] = raw[:, 3 * HEAD_DIM:6 * HEAD_DIM]

    def head_epilogue(c):
        cw = jnp.concatenate([convw_ref[:, k * KW + c * HEAD_DIM:k * KW + (c + 1) * HEAD_DIM]
                              for k in range(3)], axis=1)
        base = CONV_PAD - (CONV_W - 1)
        acc = convbuf[c, base:base + tb, :] * cw[0:1, :]
        for j in range(1, CONV_W):
            acc = acc + convbuf[c, base + j:base + j + tb, :] * cw[j:j + 1, :]
        convbuf[c, 0:CONV_PAD, :] = convbuf[c, tb:tb + CONV_PAD, :]
        qkv = _silu(acc)
        qh = qkv[:, 0:HEAD_DIM]
        kh = qkv[:, HEAD_DIM:2 * HEAD_DIM]
        dq[c] = qh * (lax.rsqrt(jnp.sum(qh * qh, axis=-1, keepdims=True) + L2_EPS) * QK_SCALE)
        dk[c] = kh * lax.rsqrt(jnp.sum(kh * kh, axis=-1, keepdims=True) + L2_EPS)
        dv[c] = qkv[:, 2 * HEAD_DIM:3 * HEAD_DIM]
        lg = lbl_ref[:, c * HEAD_DIM:(c + 1) * HEAD_DIM]
        e = jnp.exp(lg - jnp.max(lg, axis=0, keepdims=True))
        lb = e[0:1, :] / jnp.sum(e, axis=0, keepdims=True)
        f = lb + (1.0 - lb) * _sigmoid(rawhg[c, :, HEAD_DIM:2 * HEAD_DIM])
        hlf[c] = jnp.log(f)
        hf[c] = f
        hk[c] = 1.0 - f
        hq[c] = _silu(rawhg[c, :, 0:HEAD_DIM]) * QK_SCALE
        hv[c] = rawhg[c, :, 2 * HEAD_DIM:3 * HEAD_DIM]

    gate_tiles = ([(win_ref, COL_HG_G + t, sg, t, _silu) for t in range(0, KW, GATE_TILE)]
                  + [(win_ref, COL_DN_Z + t, sz, t, _silu) for t in range(0, KW, GATE_TILE)]
                  + [(wgate_ref, t, sga, t, _sigmoid) for t in range(0, D_MODEL, GATE_TILE)]
                  + [(wgate_ref, D_MODEL + t, sgb, t, _sigmoid) for t in range(0, D_MODEL, GATE_TILE)])
    gate_tiles_todo = list(range(len(gate_tiles)))

    def issue_gate_tiles(count):
        for _ in range(min(count, len(gate_tiles_todo))):
            w_ref, src, dst, off, act = gate_tiles[gate_tiles_todo.pop(0)]
            dst[:, off:off + GATE_TILE] = act(_dot(xbuf[...], w_ref[:, src:src + GATE_TILE]))

    @pl.when((pl.program_id(0) == 0) & (pl.program_id(1) == 0))
    def _first_projection():
        xnbuf[...] = _bf(x_ref[...])
        project_head(0, xnbuf)

    xbuf[...] = xnbuf[...]

    def cumulate_forget(h0):
        lf_hi, lf_lo = _split(jnp.concatenate([hlf[h0], hlf[h0 + 1]], axis=1))
        bc = _dot(bdtril_ref[...], lf_hi) + _dot(bdtril_ref[...], lf_lo)
        hb[h0] = bc[:, 0:HEAD_DIM]
        hb[h0 + 1] = bc[:, HEAD_DIM:2 * HEAD_DIM]

    for c in range(1, HEADS):
        head_epilogue(c - 1)
        project_head(c)
        if c % 2 == 0:
            cumulate_forget(c - 2)
    head_epilogue(HEADS - 1)
    issue_gate_tiles(GATE_TILES_AFTER_PROJECTIONS)
    cumulate_forget(HEADS - 2)

    pab = _dot(xbuf[...], wab_ref[...])
    g_col = -jnp.exp(alogr_ref[...]) * _softplus(pab + dtbr_ref[...])
    lane = lax.broadcasted_iota(jnp.int32, pab.shape, 1)
    gbuf[...] = jnp.where(lane < HEADS, _sigmoid(pab), g_col)
    pabt = jnp.transpose(pab)[0:2 * HEADS, :]
    g_row = -jnp.exp(alogc_ref[...]) * _softplus(pabt + dtbc_ref[...])
    gr_hi, gr_lo = _split(g_row)
    gc_row = _dot(gr_hi, bdtriu_ref[...]) + _dot(gr_lo, bdtriu_ref[...])
    for c in range(n_chunks):
        gcrow[c] = gc_row[:, c * CHUNK:(c + 1) * CHUNK]

    ri = lax.broadcasted_iota(jnp.int32, (CHUNK, CHUNK), 0)
    ci = lax.broadcasted_iota(jnp.int32, (CHUNK, CHUNK), 1)
    causal = ri >= ci
    strict = ri > ci
    trow = lax.broadcasted_iota(jnp.int32, (CHUNK, HEAD_DIM), 0)

    n_lev = len(_LEVELS)
    heads = range(HEADS)
    hsl = [slice(h * HEAD_DIM, (h + 1) * HEAD_DIM) for h in heads]

    def level_mask(l):
        return masks_ref[l * CHUNK:(l + 1) * CHUNK, :]

    units = [(u, h) for u in range(PREP_UNROLL) for h in heads]
    n_units = range(len(units))

    def prepare_chunks(j):
        cs = [j * PREP_UNROLL + u for u in range(PREP_UNROLL)]
        rows = [slice(c * CHUNK, (c + 1) * CHUNK) for c in cs]
        wexp = wexp_ref[...]
        gb, eg, gcr = [], [], []
        for u in range(PREP_UNROLL):
            gb.append(gbuf[rows[u], :])
            g_hi, g_lo = _split(gb[u])
            eg.append(_dot(wexp[0:2 * CHUNK], g_hi) + _dot(wexp[0:2 * CHUNK], g_lo))
            gcr.append(gcrow[cs[u]])

        def ld(ref, i):
            u, h = units[i]
            return ref[h, rows[u], :]

        dkk = [ld(dk, i) for i in n_units]
        dkb = [_bf(k) for k in dkk]
        beta = [gb[u][:, h:h + 1] for u, h in units]
        gcol = [eg[u][0:CHUNK, HEADS + h:HEADS + h + 1] for u, h in units]
        decay = [jnp.where(causal, jnp.exp(jnp.minimum(gcol[i] - gcr[u][HEADS + h:HEADS + h + 1, :], 0.0)), 0.0)
                 for i, (u, h) in enumerate(units)]
        kbeta = [dkk[i] * beta[i] for i in n_units]
        lmat = [jnp.where(strict, _dot_nt(_bf(kbeta[i]), dkb[i]) * decay[i], 0.0) for i in n_units]
        for i, (u, h) in enumerate(units):
            qkb[cs[u] * HEADS + h] = _bf(_dot_nt(_bf(ld(dq, i)), dkb[i]) * decay[i])

        hqv = [ld(hq, i) for i in n_units]
        hkv = [ld(hk, i) for i in n_units]
        hqb = [_bf(q) for q in hqv]
        hkb = [_bf(k) for k in hkv]
        sc = [level_mask(n_lev) * _dot_nt(hqb[i], hkb[i]) for i in n_units]
        nmat = [-(level_mask(n_lev - 1) * lmat[i]) for i in n_units]

        hbv = [ld(hb, i) for i in n_units]

        def level_factor(i, l):
            m = _LEVELS[l]
            if m >= SUBLANES // 2:
                ref = jnp.concatenate(
                    [jnp.broadcast_to(hbv[i][r:r + 1, :], (SUBLANES, HEAD_DIM))
                     for r in [(g * SUBLANES // (2 * m)) * (2 * m) + m - 1 for g in range(CHUNK // SUBLANES)]],
                    axis=0)
                return jnp.exp(-jnp.abs(hbv[i] - ref))
            f = ld(hf, i)
            if m == 1:
                return jnp.where(trow % 2 == 1, f, 1.0)
            f_prev = pltpu.roll(f, 1, 0)
            f_next = pltpu.roll(f, CHUNK - 1, 0)
            return jnp.where(trow % 4 == 0, f_next,
                             jnp.where(trow % 4 == 1, 1.0, jnp.where(trow % 4 == 2, f, f_prev * f)))

        def score_level(l):
            for i in n_units:
                zb = _bf(level_factor(i, l))
                sc[i] = sc[i] + level_mask(l) * _dot_nt(hqb[i] * zb, hkb[i] * zb)

        score_level(n_lev - 1)
        for l in range(n_lev - 2, -1, -1):
            cm = [level_mask(l) * lmat[i] for i in n_units]
            nb = [_bf(nmat[i]) for i in n_units]
            xm = [cm[i] + _dot(nb[i], _bf(cm[i])) for i in n_units]
            score_level(l)
            for i in n_units:
                nmat[i] = nmat[i] - (xm[i] + _dot(_bf(xm[i]), nb[i]))

        hvb = [_bf(ld(hv, i)) for i in n_units]
        for i, (u, h) in enumerate(units):
            oa[rows[u], hsl[h]] = _dot(_bf(sc[i]), hvb[i])
        for i, (u, h) in enumerate(units):
            kd = hkv[i] * jnp.exp(hbv[i][CHUNK - 1:CHUNK, :] - hbv[i])
            kvinc[cs[u] * HEADS + h] = _dot_tn(hvb[i], _bf(kd))
        for i, (u, h) in enumerate(units):
            qin[rows[u], hsl[h]] = _bf(hqv[i] * jnp.exp(hbv[i]))
            hdec[cs[u], :, hsl[h]] = jnp.exp(hbv[i][CHUNK - 1:CHUNK, :])

        for i, (u, h) in enumerate(units):
            eg_col = jnp.exp(gcol[i])
            rhs = jnp.concatenate([ld(dv, i) * beta[i], kbeta[i] * eg_col], axis=1)
            uw = rhs + _dot(_bf(nmat[i]), _bf(rhs))
            du[rows[u], hsl[h]] = uw[:, 0:HEAD_DIM]
            wq[cs[u], 0:CHUNK, hsl[h]] = _bf(uw[:, HEAD_DIM:2 * HEAD_DIM])
            wq[cs[u], CHUNK:2 * CHUNK, hsl[h]] = _bf(ld(dq, i) * eg_col)
            kdec[rows[u], hsl[h]] = _bf(dkk[i] * jnp.exp(eg[u][CHUNK:2 * CHUNK, HEADS + h:HEADS + h + 1]))
        for u in range(PREP_UNROLL):
            gl[cs[u]] = jnp.exp(eg[u][CHUNK - 1:CHUNK, :])

    for j in range(n_chunks // PREP_UNROLL):
        prepare_chunks(j)

    for c in range(n_chunks):
        rows = slice(c * CHUNK, (c + 1) * CHUNK)
        s = [sdn[h] for h in heads]
        wqc = wq[c]
        ws = [_dot(wqc[:, hsl[h]], _bf(s[h])) for h in heads]
        st = [shg[h] for h in heads]
        dec = hdec[c]
        for h in heads:
            oa[rows, hsl[h]] = oa[rows, hsl[h]] + _dot_nt(qin[rows, hsl[h]], _bf(st[h]))
            shg[h] = st[h] * dec[:, hsl[h]] + kvinc[c * HEADS + h]
        issue_gate_tiles(-(-len(gate_tiles_todo) // (2 * (n_chunks - c))))
        vnb = [_bf(du[rows, hsl[h]] - ws[h][0:CHUNK]) for h in heads]
        glc = gl[c]
        for h in heads:
            ob[rows, hsl[h]] = ws[h][CHUNK:2 * CHUNK] + _dot(qkb[c * HEADS + h], vnb[h])
        for h in heads:
            sdn[h] = s[h] * glc[:, HEADS + h:HEADS + h + 1] + _dot_tn(kdec[rows, hsl[h]], vnb[h])
        issue_gate_tiles(-(-len(gate_tiles_todo) // (2 * (n_chunks - c) - 1)))

    def gated_norm(o_ref, w_row, gate):
        parts = []
        for h in range(HEADS):
            oh = o_ref[:, h * HEAD_DIM:(h + 1) * HEAD_DIM]
            parts.append(oh * lax.rsqrt(jnp.mean(oh * oh, axis=-1, keepdims=True) + RMS_EPS))
        return jnp.concatenate(parts, axis=1) * w_row * gate

    na = gated_norm(oa, hgnw_ref[...], sg[...])
    nb_ = gated_norm(ob, dnnw_ref[...], sz[...])
    merged = sga[...] * _dot(_bf(na), wa_ref[...]) + sgb[...] * _dot(_bf(nb_), wb_ref[...])
    mix = _dot(_bf(merged), wo_ref[...])
    out_ref[...] = _layer_norm(ALPHA * x_ref[...] + mix, g1_ref[...], b1_ref[...])

    xnbuf[...] = _bf(xnext_ref[...])
    project_head(0, xnbuf)


def _mlp_kernel(h_ref, wup_ref, wdown_ref, g2_ref, b2_ref, out_ref):
    tm = h_ref.shape[0]
    hb = _bf(h_ref[...])
    n_ff = D_FF // FF_BLOCK
    acc = None
    for j in range(n_ff - 1):
        up = _dot(hb, wup_ref[:, j * FF_BLOCK:(j + 1) * FF_BLOCK])
        act = jnp.square(jnp.maximum(up, 0.0))
        part = _dot(_bf(act), wdown_ref[j * FF_BLOCK:(j + 1) * FF_BLOCK, :])
        acc = part if acc is None else acc + part
    up = _dot(hb, wup_ref[:, (n_ff - 1) * FF_BLOCK:n_ff * FF_BLOCK])
    actb = _bf(jnp.square(jnp.maximum(up, 0.0)))
    for r in range(MLP_TAIL_SPLIT):
        rows = slice(r * tm // MLP_TAIL_SPLIT, (r + 1) * tm // MLP_TAIL_SPLIT)
        part = _dot(actb[rows], wdown_ref[(n_ff - 1) * FF_BLOCK:n_ff * FF_BLOCK, :])
        out_ref[rows, :] = _layer_norm(ALPHA * h_ref[rows, :] + acc[rows] + part, g2_ref[...], b2_ref[...])


def _resident(shape):
    nd = len(shape)
    return pl.BlockSpec(shape, lambda *_: (0,) * nd, pipeline_mode=pl.Buffered(1))


def _mixer_call(x, wup, wdown, win32, lbl, convw, hgnw, alogr, dtbr, alogc, dtbc, dnnw,
                wa, wb, wo, g1, b1):
    bsz, seq, _ = x.shape
    tb = min(TOKEN_BLOCK, seq)
    assert seq % tb == 0 and tb % (CHUNK * PREP_UNROLL) == 0
    n_chunks = tb // CHUNK
    wexp = jnp.asarray(_WEXP, jnp.bfloat16)
    masks = jnp.asarray(_MASKS, jnp.float32)
    tt = np.arange(tb)
    bdtriu = jnp.asarray(((tt[:, None] <= tt[None, :]) & ((tt[:, None] // CHUNK) == (tt[None, :] // CHUNK))),
                         jnp.bfloat16)
    consts = (lbl, convw, hgnw, alogr, dtbr, alogc, dtbc, dnnw, wa, wb, wo,
              g1, b1, wexp, masks, bdtriu, bdtriu.T)
    f32 = jnp.float32
    blk = functools.partial(pltpu.VMEM, (tb, KW))
    bf16 = jnp.bfloat16
    per_head = functools.partial(pltpu.VMEM, (HEADS, tb, HEAD_DIM))
    scratch = [pltpu.VMEM((D_MODEL, _OFFS[6]), bf16),
               pltpu.VMEM((D_MODEL, 2 * D_MODEL), bf16),
               pltpu.VMEM((D_MODEL, 128), bf16),
               pltpu.VMEM((2, CAST_ROWS, win32.shape[-1]), f32),
               pltpu.SemaphoreType.DMA((2,)),
               pltpu.VMEM((tb, D_MODEL), bf16), pltpu.VMEM((tb, D_MODEL), bf16)]
    scratch += [per_head(f32) for _ in range(9)]
    scratch += [blk(f32) for _ in range(3)]
    scratch += [pltpu.VMEM((tb, 128), f32),
                pltpu.VMEM((n_chunks, 8, CHUNK), f32),
                pltpu.VMEM((HEADS, tb + CONV_PAD, 3 * HEAD_DIM), f32),
                pltpu.VMEM((HEADS, tb, 3 * HEAD_DIM), f32),
                blk(bf16), blk(bf16),
                pltpu.VMEM((n_chunks, 2 * CHUNK, KW), bf16),
                pltpu.VMEM((n_chunks * HEADS, CHUNK, CHUNK), bf16),
                pltpu.VMEM((n_chunks * HEADS, HEAD_DIM, HEAD_DIM), f32),
                pltpu.VMEM((n_chunks, 1, KW), f32),
                pltpu.VMEM((n_chunks, 1, 128), f32),
                blk(f32), blk(f32),
                pltpu.VMEM((tb, D_MODEL), f32),
                pltpu.VMEM((tb, D_MODEL), f32),
                pltpu.VMEM((HEADS, HEAD_DIM, HEAD_DIM), f32),
                pltpu.VMEM((HEADS, HEAD_DIM, HEAD_DIM), f32)]
    n_blocks = seq // tb

    def next_block(b, i):
        flat = jnp.minimum(b * n_blocks + i + 1, bsz * n_blocks - 1)
        return flat // n_blocks, flat % n_blocks, 0

    n_steps = bsz * n_blocks
    up_rows, down_rows = wup.shape[0] // n_steps, wdown.shape[0] // n_steps
    assert up_rows * n_steps == wup.shape[0] and down_rows * n_steps == wdown.shape[0]
    assert up_rows % BF16_ROWS == 0 and down_rows % BF16_ROWS == 0

    def step_rows(b, i):
        return b * n_blocks + i, 0

    return pl.pallas_call(
        _mixer_kernel,
        grid=(bsz, n_blocks),
        in_specs=[pl.BlockSpec((None, tb, D_MODEL), lambda b, i: (b, i, 0)),
                  pl.BlockSpec((None, tb, D_MODEL), next_block),
                  pl.BlockSpec((up_rows, wup.shape[1]), step_rows),
                  pl.BlockSpec((down_rows, wdown.shape[1]), step_rows),
                  pl.BlockSpec(memory_space=pl.ANY)]
                 + [_resident(c.shape) for c in consts],
        out_specs=[pl.BlockSpec((None, tb, D_MODEL), lambda b, i: (b, i, 0)),
                   pl.BlockSpec((up_rows, wup.shape[1]), step_rows),
                   pl.BlockSpec((down_rows, wdown.shape[1]), step_rows)],
        out_shape=[jax.ShapeDtypeStruct(x.shape, f32),
                   jax.ShapeDtypeStruct(wup.shape, bf16),
                   jax.ShapeDtypeStruct(wdown.shape, bf16)],
        scratch_shapes=scratch,
        compiler_params=pltpu.CompilerParams(dimension_semantics=("arbitrary", "arbitrary"),
                                             vmem_limit_bytes=VMEM_LIMIT_BYTES),
        name="token_mixer",
    )(x, x, wup, wdown, win32, *consts)


def _mlp_call(h, wup, wdown, g2, b2):
    m = h.shape[0]
    tm = min(MLP_BLOCK, m)
    assert m % tm == 0
    consts = (wup, wdown, g2, b2)
    return pl.pallas_call(
        _mlp_kernel,
        grid=(m // tm,),
        in_specs=[pl.BlockSpec((tm, D_MODEL), lambda i: (i, 0))] + [_resident(c.shape) for c in consts],
        out_specs=pl.BlockSpec((tm, D_MODEL), lambda i: (i, 0)),
        out_shape=jax.ShapeDtypeStruct(h.shape, jnp.float32),
        compiler_params=pltpu.CompilerParams(dimension_semantics=("arbitrary",),
                                             vmem_limit_bytes=VMEM_LIMIT_BYTES),
        name="relu2_mlp",
    )(h, *consts)


def kernel(x, hg_lb_logits, w_in, conv_w, hg_norm_w, dn_A_log, dn_dt_bias, dn_norm_w, w_branch_a, w_branch_b, w_o, ln1_g, ln1_b, w_up, w_down, ln2_g, ln2_b):
    bsz, seq, _ = x.shape
    f32 = jnp.float32
    bf16 = jnp.bfloat16
    l = LAYER
    zeros4 = jnp.zeros((HEADS,), f32)
    alog8 = jnp.concatenate([zeros4, dn_A_log[l].astype(f32)])
    dtb8 = jnp.concatenate([zeros4, dn_dt_bias[l].astype(f32)])
    alogr = jnp.pad(alog8, (0, 128 - 2 * HEADS)).reshape(1, 128)
    dtbr = jnp.pad(dtb8, (0, 128 - 2 * HEADS)).reshape(1, 128)
    h1, wup_bf, wdown_bf = _mixer_call(
        x, w_up[l], w_down[l], w_in, hg_lb_logits.astype(f32), conv_w[l].astype(f32),
        hg_norm_w[l].reshape(1, KW), alogr, dtbr, alog8.reshape(8, 1), dtb8.reshape(8, 1),
        jnp.tile(dn_norm_w[l], HEADS).reshape(1, KW),
        w_branch_a[l].astype(bf16), w_branch_b[l].astype(bf16), w_o[l].astype(bf16),
        ln1_g[l].reshape(1, D_MODEL), ln1_b[l].reshape(1, D_MODEL))
    out = _mlp_call(h1.reshape(bsz * seq, D_MODEL), wup_bf, wdown_bf,
                    ln2_g[l].reshape(1, D_MODEL), ln2_b[l].reshape(1, D_MODEL))
    return out.reshape(bsz, seq, D_MODEL)
```

```python
import functools

import numpy as np
import jax
import jax.numpy as jnp
from jax import lax
from jax.experimental import pallas as pl
from jax.experimental.pallas import tpu as pltpu

D_MODEL = 1024
HEADS = 4
HEAD_DIM = 128
KW = HEADS * HEAD_DIM
CHUNK = 64
CONV_W = 4
D_FF = 4 * D_MODEL
DEPTH = 1
LAYER = 0
ALPHA = (2 * DEPTH) ** 0.25
LN_EPS = 1e-5
RMS_EPS = 1e-6
L2_EPS = 1e-6
QK_SCALE = HEAD_DIM ** -0.5
NEG_LOG2E = -1.4426950408889634

TOKEN_BLOCK = 256
PREP_UNROLL = 4
GATE_TILE = 256
GATE_TILES_AFTER_PROJECTIONS = 4
MLP_BLOCK = 1024
MLP_TAIL_SPLIT = 4
FF_BLOCK = 1024
SUBLANES = 8
BF16_ROWS = 16
CONV_PAD = 8
VMEM_LIMIT_BYTES = 56 * 1024 * 1024

_SIZES = (KW, KW, KW, KW, 3 * KW, KW, HEADS, HEADS, D_MODEL, D_MODEL)
_OFFS = np.concatenate([[0], np.cumsum(_SIZES)]).tolist()

COL_HG_QFI = _OFFS[0]
COL_HG_G = _OFFS[3]
COL_DN_QKV = _OFFS[4]
COL_DN_Z = _OFFS[5]

_LEVELS = (32, 16, 8, 4, 2, 1)


def _build_constants():
    t = np.arange(CHUNK)[:, None]
    r = np.arange(CHUNK)[None, :]
    wsum = np.concatenate([r <= t, r > t], 0).astype(np.float32)
    masks = [((t // (2 * m)) == (r // (2 * m))) & ((t % (2 * m)) >= m) & ((r % (2 * m)) < m) for m in _LEVELS]
    masks.append(t == r)
    masks = np.concatenate(masks, 0).astype(np.float32)
    return wsum, masks


_WSUM, _MASKS = _build_constants()


def _dot(a, b):
    return jnp.dot(a, b, preferred_element_type=jnp.float32)


def _dot_nt(a, b):
    return lax.dot_general(a, b, (((1,), (1,)), ((), ())), preferred_element_type=jnp.float32)


def _dot_tn(a, b):
    return lax.dot_general(a, b, (((0,), (0,)), ((), ())), preferred_element_type=jnp.float32)


def _bf(x):
    return x.astype(jnp.bfloat16)


def _split(x):
    hi = _bf(x)
    return hi, _bf(x - hi.astype(jnp.float32))


def _sigmoid(x):
    return 1.0 / (1.0 + jnp.exp2(x * NEG_LOG2E))


def _silu(x):
    return x * _sigmoid(x)


def _softplus(x):
    return jnp.maximum(x, 0.0) + jnp.log1p(jnp.exp(-jnp.abs(x)))


def _layer_norm(x, g, b):
    mu = jnp.mean(x, axis=-1, keepdims=True)
    xc = x - mu
    var = jnp.mean(xc * xc, axis=-1, keepdims=True)
    return xc * lax.rsqrt(var + LN_EPS) * g + b


def _mixer_kernel(x_ref, xnext_ref, wup32_ref, wdown32_ref, lbl_ref, win_ref, wgate_ref, wab_ref, convw_ref, hgnw_ref,
                  alogr_ref, dtbr_ref, alogc_ref, dtbc_ref, dnnw_ref, wa_ref, wb_ref, wo_ref,
                  g1_ref, b1_ref, wsum_ref, masks_ref, bdtriu_ref, bdtril_ref, out_ref, wupb_ref, wdownb_ref,
                  xbuf, xnbuf, hq, hk, hv, hlf, hf, hb, dq, dk, dv, oa, ob, du, gbuf, gcrow, convbuf, rawhg, qin, kdec, wq, qkb,
                  kvinc, hdec, gl, sg, sz, sga, sgb, shg, sdn):
    tb = x_ref.shape[0]
    n_chunks = tb // CHUNK

    wupb_ref[...] = _bf(wup32_ref[...])
    wdownb_ref[...] = _bf(wdown32_ref[...])

    @pl.when(pl.program_id(1) == 0)
    def _reset_carries():
        shg[...] = jnp.zeros_like(shg)
        sdn[...] = jnp.zeros_like(sdn)
        convbuf[:, 0:CONV_PAD, :] = jnp.zeros((HEADS, CONV_PAD, 3 * HEAD_DIM), jnp.float32)

    def project_head(c, xsrc=xbuf):
        cols = ([COL_DN_QKV + k * KW + c * HEAD_DIM for k in range(3)]
                + [COL_HG_QFI + k * KW + c * HEAD_DIM for k in range(3)])
        w_head = jnp.concatenate([win_ref[:, o:o + HEAD_DIM] for o in cols], axis=1)
        raw = _dot(xsrc[...], w_head)
        convbuf[c, CONV_PAD:CONV_PAD + tb, :] = raw[:, 0:3 * HEAD_DIM]
        rawhg[c] = raw[:, 3 * HEAD_DIM:6 * HEAD_DIM]

    def head_epilogue(c):
        cw = jnp.concatenate([convw_ref[:, k * KW + c * HEAD_DIM:k * KW + (c + 1) * HEAD_DIM]
                              for k in range(3)], axis=1)
        base = CONV_PAD - (CONV_W - 1)
        acc = convbuf[c, base:base + tb, :] * cw[0:1, :]
        for j in range(1, CONV_W):
            acc = acc + convbuf[c, base + j:base + j + tb, :] * cw[j:j + 1, :]
        convbuf[c, 0:CONV_PAD, :] = convbuf[c, tb:tb + CONV_PAD, :]
        qkv = _silu(acc)
        qh = qkv[:, 0:HEAD_DIM]
        kh = qkv[:, HEAD_DIM:2 * HEAD_DIM]
        dq[c] = qh * (lax.rsqrt(jnp.sum(qh * qh, axis=-1, keepdims=True) + L2_EPS) * QK_SCALE)
        dk[c] = kh * lax.rsqrt(jnp.sum(kh * kh, axis=-1, keepdims=True) + L2_EPS)
        dv[c] = qkv[:, 2 * HEAD_DIM:3 * HEAD_DIM]
        lg = lbl_ref[:, c * HEAD_DIM:(c + 1) * HEAD_DIM]
        e = jnp.exp(lg - jnp.max(lg, axis=0, keepdims=True))
        lb = e[0:1, :] / jnp.sum(e, axis=0, keepdims=True)
        f = lb + (1.0 - lb) * _sigmoid(rawhg[c, :, HEAD_DIM:2 * HEAD_DIM])
        hlf[c] = jnp.log(f)
        hf[c] = f
        hk[c] = 1.0 - f
        hq[c] = _silu(rawhg[c, :, 0:HEAD_DIM]) * QK_SCALE
        hv[c] = rawhg[c, :, 2 * HEAD_DIM:3 * HEAD_DIM]

    gate_tiles = ([(win_ref, COL_HG_G + t, sg, t, _silu) for t in range(0, KW, GATE_TILE)]
                  + [(win_ref, COL_DN_Z + t, sz, t, _silu) for t in range(0, KW, GATE_TILE)]
                  + [(wgate_ref, t, sga, t, _sigmoid) for t in range(0, D_MODEL, GATE_TILE)]
                  + [(wgate_ref, D_MODEL + t, sgb, t, _sigmoid) for t in range(0, D_MODEL, GATE_TILE)])
    gate_tiles_todo = list(range(len(gate_tiles)))

    def issue_gate_tiles(count):
        for _ in range(min(count, len(gate_tiles_todo))):
            w_ref, src, dst, off, act = gate_tiles[gate_tiles_todo.pop(0)]
            dst[:, off:off + GATE_TILE] = act(_dot(xbuf[...], w_ref[:, src:src + GATE_TILE]))

    @pl.when((pl.program_id(0) == 0) & (pl.program_id(1) == 0))
    def _first_projection():
        xnbuf[...] = _bf(x_ref[...])
        project_head(0, xnbuf)

    xbuf[...] = xnbuf[...]

    def cumulate_forget(h0):
        lf_hi, lf_lo = _split(jnp.concatenate([hlf[h0], hlf[h0 + 1]], axis=1))
        bc = _dot(bdtril_ref[...], lf_hi) + _dot(bdtril_ref[...], lf_lo)
        hb[h0] = bc[:, 0:HEAD_DIM]
        hb[h0 + 1] = bc[:, HEAD_DIM:2 * HEAD_DIM]

    for c in range(1, HEADS):
        head_epilogue(c - 1)
        project_head(c)
        if c % 2 == 0:
            cumulate_forget(c - 2)
    head_epilogue(HEADS - 1)
    issue_gate_tiles(GATE_TILES_AFTER_PROJECTIONS)
    cumulate_forget(HEADS - 2)

    pab = _dot(xbuf[...], wab_ref[...])
    g_col = -jnp.exp(alogr_ref[...]) * _softplus(pab + dtbr_ref[...])
    lane = lax.broadcasted_iota(jnp.int32, pab.shape, 1)
    gbuf[...] = jnp.where(lane < HEADS, _sigmoid(pab), g_col)
    pabt = jnp.transpose(pab)[0:2 * HEADS, :]
    g_row = -jnp.exp(alogc_ref[...]) * _softplus(pabt + dtbc_ref[...])
    gr_hi, gr_lo = _split(g_row)
    gc_row = _dot(gr_hi, bdtriu_ref[...]) + _dot(gr_lo, bdtriu_ref[...])
    for c in range(n_chunks):
        gcrow[c] = gc_row[:, c * CHUNK:(c + 1) * CHUNK]

    ri = lax.broadcasted_iota(jnp.int32, (CHUNK, CHUNK), 0)
    ci = lax.broadcasted_iota(jnp.int32, (CHUNK, CHUNK), 1)
    causal = jnp.where(ri >= ci, 1.0, 0.0)
    strict = jnp.where(ri > ci, 1.0, 0.0)
    trow = lax.broadcasted_iota(jnp.int32, (CHUNK, HEAD_DIM), 0)

    n_lev = len(_LEVELS)
    heads = range(HEADS)
    hsl = [slice(h * HEAD_DIM, (h + 1) * HEAD_DIM) for h in heads]

    def level_mask(l):
        return masks_ref[l * CHUNK:(l + 1) * CHUNK, :]

    units = [(u, h) for u in range(PREP_UNROLL) for h in heads]
    n_units = range(len(units))

    def prepare_chunks(j):
        cs = [j * PREP_UNROLL + u for u in range(PREP_UNROLL)]
        rows = [slice(c * CHUNK, (c + 1) * CHUNK) for c in cs]
        wsum = wsum_ref[...]
        gb, eg, gcr = [], [], []
        for u in range(PREP_UNROLL):
            gb.append(gbuf[rows[u], :])
            g_hi, g_lo = _split(gb[u])
            eg.append(_dot(wsum, g_hi) + _dot(wsum, g_lo))
            gcr.append(gcrow[cs[u]])

        def ld(ref, i):
            u, h = units[i]
            return ref[h, rows[u], :]

        dkk = [ld(dk, i) for i in n_units]
        dkb = [_bf(k) for k in dkk]
        beta = [gb[u][:, h:h + 1] for u, h in units]
        gcol = [eg[u][0:CHUNK, HEADS + h:HEADS + h + 1] for u, h in units]
        decay = [causal * jnp.exp(jnp.minimum(gcol[i] - gcr[u][HEADS + h:HEADS + h + 1, :], 0.0))
                 for i, (u, h) in enumerate(units)]
        kbeta = [dkk[i] * beta[i] for i in n_units]
        lmat = [_dot_nt(_bf(kbeta[i]), dkb[i]) * (strict * decay[i]) for i in n_units]
        for i, (u, h) in enumerate(units):
            qkb[cs[u] * HEADS + h] = _bf(_dot_nt(_bf(ld(dq, i)), dkb[i]) * decay[i])

        hqv = [ld(hq, i) for i in n_units]
        hkv = [ld(hk, i) for i in n_units]
        hqb = [_bf(q) for q in hqv]
        hkb = [_bf(k) for k in hkv]
        sc = [level_mask(n_lev) * _dot_nt(hqb[i], hkb[i]) for i in n_units]
        nmat = [-(level_mask(n_lev - 1) * lmat[i]) for i in n_units]

        hbv = [ld(hb, i) for i in n_units]

        def level_factor(i, l):
            m = _LEVELS[l]
            if m >= SUBLANES // 2:
                ref = jnp.concatenate(
                    [jnp.broadcast_to(hbv[i][r:r + 1, :], (SUBLANES, HEAD_DIM))
                     for r in [(g * SUBLANES // (2 * m)) * (2 * m) + m - 1 for g in range(CHUNK // SUBLANES)]],
                    axis=0)
                return jnp.exp(-jnp.abs(hbv[i] - ref))
            f = ld(hf, i)
            if m == 1:
                return jnp.where(trow % 2 == 1, f, 1.0)
            f_prev = pltpu.roll(f, 1, 0)
            f_next = pltpu.roll(f, CHUNK - 1, 0)
            return jnp.where(trow % 4 == 0, f_next,
                             jnp.where(trow % 4 == 1, 1.0, jnp.where(trow % 4 == 2, f, f_prev * f)))

        def score_level(l):
            for i in n_units:
                zb = _bf(level_factor(i, l))
                sc[i] = sc[i] + level_mask(l) * _dot_nt(hqb[i] * zb, hkb[i] * zb)

        score_level(n_lev - 1)
        for l in range(n_lev - 2, -1, -1):
            cm = [level_mask(l) * lmat[i] for i in n_units]
            nb = [_bf(nmat[i]) for i in n_units]
            xm = [cm[i] + _dot(nb[i], _bf(cm[i])) for i in n_units]
            score_level(l)
            for i in n_units:
                nmat[i] = nmat[i] - (xm[i] + _dot(_bf(xm[i]), nb[i]))

        hvb = [_bf(ld(hv, i)) for i in n_units]
        for i, (u, h) in enumerate(units):
            oa[rows[u], hsl[h]] = _dot(_bf(sc[i]), hvb[i])
        for i, (u, h) in enumerate(units):
            kd = hkv[i] * jnp.exp(hbv[i][CHUNK - 1:CHUNK, :] - hbv[i])
            kvinc[cs[u] * HEADS + h] = _dot_tn(hvb[i], _bf(kd))
        for i, (u, h) in enumerate(units):
            qin[rows[u], hsl[h]] = _bf(hqv[i] * jnp.exp(hbv[i]))
            hdec[cs[u], :, hsl[h]] = jnp.exp(hbv[i][CHUNK - 1:CHUNK, :])

        for i, (u, h) in enumerate(units):
            eg_col = jnp.exp(gcol[i])
            rhs = jnp.concatenate([ld(dv, i) * beta[i], kbeta[i] * eg_col], axis=1)
            uw = rhs + _dot(_bf(nmat[i]), _bf(rhs))
            du[rows[u], hsl[h]] = uw[:, 0:HEAD_DIM]
            wq[cs[u], 0:CHUNK, hsl[h]] = _bf(uw[:, HEAD_DIM:2 * HEAD_DIM])
            wq[cs[u], CHUNK:2 * CHUNK, hsl[h]] = _bf(ld(dq, i) * eg_col)
            kdec[rows[u], hsl[h]] = _bf(dkk[i] * jnp.exp(eg[u][CHUNK:2 * CHUNK, HEADS + h:HEADS + h + 1]))
        for u in range(PREP_UNROLL):
            gl[cs[u]] = jnp.exp(eg[u][CHUNK - 1:CHUNK, :])

    for j in range(n_chunks // PREP_UNROLL):
        prepare_chunks(j)

    for c in range(n_chunks):
        rows = slice(c * CHUNK, (c + 1) * CHUNK)
        s = [sdn[h] for h in heads]
        wqc = wq[c]
        ws = [_dot(wqc[:, hsl[h]], _bf(s[h])) for h in heads]
        st = [shg[h] for h in heads]
        dec = hdec[c]
        for h in heads:
            oa[rows, hsl[h]] = oa[rows, hsl[h]] + _dot_nt(qin[rows, hsl[h]], _bf(st[h]))
            shg[h] = st[h] * dec[:, hsl[h]] + kvinc[c * HEADS + h]
        issue_gate_tiles(-(-len(gate_tiles_todo) // (2 * (n_chunks - c))))
        vnb = [_bf(du[rows, hsl[h]] - ws[h][0:CHUNK]) for h in heads]
        glc = gl[c]
        for h in heads:
            ob[rows, hsl[h]] = ws[h][CHUNK:2 * CHUNK] + _dot(qkb[c * HEADS + h], vnb[h])
        for h in heads:
            sdn[h] = s[h] * glc[:, HEADS + h:HEADS + h + 1] + _dot_tn(kdec[rows, hsl[h]], vnb[h])
        issue_gate_tiles(-(-len(gate_tiles_todo) // (2 * (n_chunks - c) - 1)))

    def gated_norm(o_ref, w_row, gate):
        parts = []
        for h in range(HEADS):
            oh = o_ref[:, h * HEAD_DIM:(h + 1) * HEAD_DIM]
            parts.append(oh * lax.rsqrt(jnp.mean(oh * oh, axis=-1, keepdims=True) + RMS_EPS))
        return jnp.concatenate(parts, axis=1) * w_row * gate

    na = gated_norm(oa, hgnw_ref[...], sg[...])
    nb_ = gated_norm(ob, dnnw_ref[...], sz[...])
    merged = sga[...] * _dot(_bf(na), wa_ref[...]) + sgb[...] * _dot(_bf(nb_), wb_ref[...])
    mix = _dot(_bf(merged), wo_ref[...])
    out_ref[...] = _layer_norm(ALPHA * x_ref[...] + mix, g1_ref[...], b1_ref[...])

    xnbuf[...] = _bf(xnext_ref[...])
    project_head(0, xnbuf)


def _mlp_kernel(h_ref, wup_ref, wdown_ref, g2_ref, b2_ref, out_ref):
    tm = h_ref.shape[0]
    hb = _bf(h_ref[...])
    n_ff = D_FF // FF_BLOCK
    assert n_ff >= 2 and tm % MLP_TAIL_SPLIT == 0
    acc = None
    for j in range(n_ff - 1):
        up = _dot(hb, wup_ref[:, j * FF_BLOCK:(j + 1) * FF_BLOCK])
        act = jnp.square(jnp.maximum(up, 0.0))
        part = _dot(_bf(act), wdown_ref[j * FF_BLOCK:(j + 1) * FF_BLOCK, :])
        acc = part if acc is None else acc + part
    up = _dot(hb, wup_ref[:, (n_ff - 1) * FF_BLOCK:n_ff * FF_BLOCK])
    actb = _bf(jnp.square(jnp.maximum(up, 0.0)))
    for r in range(MLP_TAIL_SPLIT):
        rows = slice(r * tm // MLP_TAIL_SPLIT, (r + 1) * tm // MLP_TAIL_SPLIT)
        part = _dot(actb[rows], wdown_ref[(n_ff - 1) * FF_BLOCK:n_ff * FF_BLOCK, :])
        out_ref[rows, :] = _layer_norm(ALPHA * h_ref[rows, :] + acc[rows] + part, g2_ref[...], b2_ref[...])


def _resident(shape):
    nd = len(shape)
    return pl.BlockSpec(shape, lambda *_: (0,) * nd, pipeline_mode=pl.Buffered(1))


def _mixer_call(x, wup, wdown, lbl, win, wgate, wab, convw, hgnw, alogr, dtbr, alogc, dtbc, dnnw,
                wa, wb, wo, g1, b1):
    bsz, seq, _ = x.shape
    tb = min(TOKEN_BLOCK, seq)
    assert seq % tb == 0 and tb % (CHUNK * PREP_UNROLL) == 0
    n_chunks = tb // CHUNK
    wsum = jnp.asarray(_WSUM, jnp.bfloat16)
    masks = jnp.asarray(_MASKS, jnp.float32)
    tt = np.arange(tb)
    bdtriu = jnp.asarray(((tt[:, None] <= tt[None, :]) & ((tt[:, None] // CHUNK) == (tt[None, :] // CHUNK))),
                         jnp.bfloat16)
    consts = (lbl, win, wgate, wab, convw, hgnw, alogr, dtbr, alogc, dtbc, dnnw, wa, wb, wo,
              g1, b1, wsum, masks, bdtriu, bdtriu.T)
    f32 = jnp.float32
    blk = functools.partial(pltpu.VMEM, (tb, KW))
    bf16 = jnp.bfloat16
    per_head = functools.partial(pltpu.VMEM, (HEADS, tb, HEAD_DIM))
    scratch = [pltpu.VMEM((tb, D_MODEL), bf16), pltpu.VMEM((tb, D_MODEL), bf16)]
    scratch += [per_head(f32) for _ in range(9)]
    scratch += [blk(f32) for _ in range(3)]
    scratch += [pltpu.VMEM((tb, 128), f32),
                pltpu.VMEM((n_chunks, 8, CHUNK), f32),
                pltpu.VMEM((HEADS, tb + CONV_PAD, 3 * HEAD_DIM), f32),
                pltpu.VMEM((HEADS, tb, 3 * HEAD_DIM), f32),
                blk(bf16), blk(bf16),
                pltpu.VMEM((n_chunks, 2 * CHUNK, KW), bf16),
                pltpu.VMEM((n_chunks * HEADS, CHUNK, CHUNK), bf16),
                pltpu.VMEM((n_chunks * HEADS, HEAD_DIM, HEAD_DIM), f32),
                pltpu.VMEM((n_chunks, 1, KW), f32),
                pltpu.VMEM((n_chunks, 1, 128), f32),
                blk(f32), blk(f32),
                pltpu.VMEM((tb, D_MODEL), f32),
                pltpu.VMEM((tb, D_MODEL), f32),
                pltpu.VMEM((HEADS, HEAD_DIM, HEAD_DIM), f32),
                pltpu.VMEM((HEADS, HEAD_DIM, HEAD_DIM), f32)]
    n_blocks = seq // tb

    def next_block(b, i):
        flat = jnp.minimum(b * n_blocks + i + 1, bsz * n_blocks - 1)
        return flat // n_blocks, flat % n_blocks, 0

    n_steps = bsz * n_blocks
    up_rows, down_rows = wup.shape[0] // n_steps, wdown.shape[0] // n_steps
    assert up_rows * n_steps == wup.shape[0] and down_rows * n_steps == wdown.shape[0]
    assert up_rows % BF16_ROWS == 0 and down_rows % BF16_ROWS == 0

    def step_rows(b, i):
        return b * n_blocks + i, 0

    return pl.pallas_call(
        _mixer_kernel,
        grid=(bsz, n_blocks),
        in_specs=[pl.BlockSpec((None, tb, D_MODEL), lambda b, i: (b, i, 0)),
                  pl.BlockSpec((None, tb, D_MODEL), next_block),
                  pl.BlockSpec((up_rows, wup.shape[1]), step_rows),
                  pl.BlockSpec((down_rows, wdown.shape[1]), step_rows)]
                 + [_resident(c.shape) for c in consts],
        out_specs=[pl.BlockSpec((None, tb, D_MODEL), lambda b, i: (b, i, 0)),
                   pl.BlockSpec((up_rows, wup.shape[1]), step_rows),
                   pl.BlockSpec((down_rows, wdown.shape[1]), step_rows)],
        out_shape=[jax.ShapeDtypeStruct(x.shape, f32),
                   jax.ShapeDtypeStruct(wup.shape, bf16),
                   jax.ShapeDtypeStruct(wdown.shape, bf16)],
        scratch_shapes=scratch,
        compiler_params=pltpu.CompilerParams(dimension_semantics=("arbitrary", "arbitrary"),
                                             vmem_limit_bytes=VMEM_LIMIT_BYTES),
        name="token_mixer",
    )(x, x, wup, wdown, *consts)


def _mlp_call(h, wup, wdown, g2, b2):
    m = h.shape[0]
    tm = min(MLP_BLOCK, m)
    assert m % tm == 0
    consts = (wup, wdown, g2, b2)
    return pl.pallas_call(
        _mlp_kernel,
        grid=(m // tm,),
        in_specs=[pl.BlockSpec((tm, D_MODEL), lambda i: (i, 0))] + [_resident(c.shape) for c in consts],
        out_specs=pl.BlockSpec((tm, D_MODEL), lambda i: (i, 0)),
        out_shape=jax.ShapeDtypeStruct(h.shape, jnp.float32),
        compiler_params=pltpu.CompilerParams(dimension_semantics=("arbitrary",),
                                             vmem_limit_bytes=VMEM_LIMIT_BYTES),
        name="relu2_mlp",
    )(h, *consts)


def kernel(x, hg_lb_logits, w_in, conv_w, hg_norm_w, dn_A_log, dn_dt_bias, dn_norm_w, w_branch_a, w_branch_b, w_o, ln1_g, ln1_b, w_up, w_down, ln2_g, ln2_b):
    bsz, seq, _ = x.shape
    f32 = jnp.float32
    bf16 = jnp.bfloat16
    o = _OFFS
    l = LAYER
    win = w_in[l].astype(bf16)
    wgate = win[:, o[8]:o[10]]
    wab = jnp.pad(win[:, o[6]:o[8]], ((0, 0), (0, 128 - 2 * HEADS)))
    zeros4 = jnp.zeros((HEADS,), f32)
    alog8 = jnp.concatenate([zeros4, dn_A_log[l].astype(f32)])
    dtb8 = jnp.concatenate([zeros4, dn_dt_bias[l].astype(f32)])
    alogr = jnp.pad(alog8, (0, 128 - 2 * HEADS)).reshape(1, 128)
    dtbr = jnp.pad(dtb8, (0, 128 - 2 * HEADS)).reshape(1, 128)
    h1, wup_bf, wdown_bf = _mixer_call(
        x, w_up[l], w_down[l], hg_lb_logits.astype(f32), win, wgate, wab, conv_w[l].astype(f32),
        hg_norm_w[l].reshape(1, KW), alogr, dtbr, alog8.reshape(8, 1), dtb8.reshape(8, 1),
        jnp.tile(dn_norm_w[l], HEADS).reshape(1, KW),
        w_branch_a[l].astype(bf16), w_branch_b[l].astype(bf16), w_o[l].astype(bf16),
        ln1_g[l].reshape(1, D_MODEL), ln1_b[l].reshape(1, D_MODEL))
    out = _mlp_call(h1.reshape(bsz * seq, D_MODEL), wup_bf, wdown_bf,
                    ln2_g[l].reshape(1, D_MODEL), ln2_b[l].reshape(1, D_MODEL))
    return out.reshape(bsz, seq, D_MODEL)
```

```python
import functools

import numpy as np
import jax
import jax.numpy as jnp
from jax import lax
from jax.experimental import pallas as pl
from jax.experimental.pallas import tpu as pltpu

D_MODEL = 1024
HEADS = 4
HEAD_DIM = 128
KW = HEADS * HEAD_DIM
CHUNK = 64
CONV_W = 4
D_FF = 4 * D_MODEL
DEPTH = 1
LAYER = 0
ALPHA = (2 * DEPTH) ** 0.25
LN_EPS = 1e-5
RMS_EPS = 1e-6
L2_EPS = 1e-6
QK_SCALE = HEAD_DIM ** -0.5
NEG_LOG2E = -1.4426950408889634

TOKEN_BLOCK = 256
PREP_UNROLL = 4
GATE_TILE = 256
GATE_TILES_AFTER_PROJECTIONS = 4
MLP_BLOCK = 1024
MLP_TAIL_SPLIT = 4
FF_BLOCK = 1024
SUBLANES = 8
BF16_ROWS = 16
CONV_PAD = 8
VMEM_LIMIT_BYTES = 56 * 1024 * 1024

_SIZES = (KW, KW, KW, KW, 3 * KW, KW, HEADS, HEADS, D_MODEL, D_MODEL)
_OFFS = np.concatenate([[0], np.cumsum(_SIZES)]).tolist()

COL_HG_QFI = _OFFS[0]
COL_HG_G = _OFFS[3]
COL_DN_QKV = _OFFS[4]
COL_DN_Z = _OFFS[5]

_LEVELS = (32, 16, 8, 4, 2, 1)


def _build_constants():
    t = np.arange(CHUNK)[:, None]
    r = np.arange(CHUNK)[None, :]
    wsum = np.concatenate([r <= t, r > t], 0).astype(np.float32)
    masks = [((t // (2 * m)) == (r // (2 * m))) & ((t % (2 * m)) >= m) & ((r % (2 * m)) < m) for m in _LEVELS]
    masks.append(t == r)
    masks = np.concatenate(masks, 0).astype(np.float32)
    return wsum, masks


_WSUM, _MASKS = _build_constants()


def _dot(a, b):
    return jnp.dot(a, b, preferred_element_type=jnp.float32)


def _dot_nt(a, b):
    return lax.dot_general(a, b, (((1,), (1,)), ((), ())), preferred_element_type=jnp.float32)


def _dot_tn(a, b):
    return lax.dot_general(a, b, (((0,), (0,)), ((), ())), preferred_element_type=jnp.float32)


def _bf(x):
    return x.astype(jnp.bfloat16)


def _split(x):
    hi = _bf(x)
    return hi, _bf(x - hi.astype(jnp.float32))


def _sigmoid(x):
    return 1.0 / (1.0 + jnp.exp2(x * NEG_LOG2E))


def _silu(x):
    return x * _sigmoid(x)


def _softplus(x):
    return jnp.maximum(x, 0.0) + jnp.log1p(jnp.exp(-jnp.abs(x)))


def _layer_norm(x, g, b):
    mu = jnp.mean(x, axis=-1, keepdims=True)
    xc = x - mu
    var = jnp.mean(xc * xc, axis=-1, keepdims=True)
    return xc * lax.rsqrt(var + LN_EPS) * g + b


def _mixer_kernel(x_ref, xnext_ref, wup32_ref, wdown32_ref, lbl_ref, win_ref, wgate_ref, wab_ref, convw_ref, hgnw_ref,
                  alogr_ref, dtbr_ref, alogc_ref, dtbc_ref, dnnw_ref, wa_ref, wb_ref, wo_ref,
                  g1_ref, b1_ref, wsum_ref, masks_ref, bdtriu_ref, bdtril_ref, out_ref, wupb_ref, wdownb_ref,
                  xbuf, xnbuf, hq, hk, hv, hlf, hf, hb, dq, dk, dv, oa, ob, du, gbuf, gcrow, convbuf, rawhg, qin, kdec, wq, qkb,
                  kvinc, hdec, gl, sg, sz, sga, sgb, shg, sdn):
    tb = x_ref.shape[0]
    n_chunks = tb // CHUNK

    wupb_ref[...] = _bf(wup32_ref[...])
    wdownb_ref[...] = _bf(wdown32_ref[...])

    @pl.when(pl.program_id(1) == 0)
    def _reset_carries():
        shg[...] = jnp.zeros_like(shg)
        sdn[...] = jnp.zeros_like(sdn)
        convbuf[:, 0:CONV_PAD, :] = jnp.zeros((HEADS, CONV_PAD, 3 * HEAD_DIM), jnp.float32)

    def project_head(c, xsrc=xbuf):
        cols = ([COL_DN_QKV + k * KW + c * HEAD_DIM for k in range(3)]
                + [COL_HG_QFI + k * KW + c * HEAD_DIM for k in range(3)])
        w_head = jnp.concatenate([win_ref[:, o:o + HEAD_DIM] for o in cols], axis=1)
        raw = _dot(xsrc[...], w_head)
        convbuf[c, CONV_PAD:CONV_PAD + tb, :] = raw[:, 0:3 * HEAD_DIM]
        rawhg[c] = raw[:, 3 * HEAD_DIM:6 * HEAD_DIM]

    def head_epilogue(c):
        cw = jnp.concatenate([convw_ref[:, k * KW + c * HEAD_DIM:k * KW + (c + 1) * HEAD_DIM]
                              for k in range(3)], axis=1)
        base = CONV_PAD - (CONV_W - 1)
        acc = convbuf[c, base:base + tb, :] * cw[0:1, :]
        for j in range(1, CONV_W):
            acc = acc + convbuf[c, base + j:base + j + tb, :] * cw[j:j + 1, :]
        convbuf[c, 0:CONV_PAD, :] = convbuf[c, tb:tb + CONV_PAD, :]
        qkv = _silu(acc)
        qh = qkv[:, 0:HEAD_DIM]
        kh = qkv[:, HEAD_DIM:2 * HEAD_DIM]
        dq[c] = qh * (lax.rsqrt(jnp.sum(qh * qh, axis=-1, keepdims=True) + L2_EPS) * QK_SCALE)
        dk[c] = kh * lax.rsqrt(jnp.sum(kh * kh, axis=-1, keepdims=True) + L2_EPS)
        dv[c] = qkv[:, 2 * HEAD_DIM:3 * HEAD_DIM]
        lg = lbl_ref[:, c * HEAD_DIM:(c + 1) * HEAD_DIM]
        e = jnp.exp(lg - jnp.max(lg, axis=0, keepdims=True))
        lb = e[0:1, :] / jnp.sum(e, axis=0, keepdims=True)
        f = lb + (1.0 - lb) * _sigmoid(rawhg[c, :, HEAD_DIM:2 * HEAD_DIM])
        hlf[c] = jnp.log(f)
        hf[c] = f
        hk[c] = 1.0 - f
        hq[c] = _silu(rawhg[c, :, 0:HEAD_DIM]) * QK_SCALE
        hv[c] = rawhg[c, :, 2 * HEAD_DIM:3 * HEAD_DIM]

    gate_tiles = ([(win_ref, COL_HG_G + t, sg, t, _silu) for t in range(0, KW, GATE_TILE)]
                  + [(win_ref, COL_DN_Z + t, sz, t, _silu) for t in range(0, KW, GATE_TILE)]
                  + [(wgate_ref, t, sga, t, _sigmoid) for t in range(0, D_MODEL, GATE_TILE)]
                  + [(wgate_ref, D_MODEL + t, sgb, t, _sigmoid) for t in range(0, D_MODEL, GATE_TILE)])
    gate_tiles_todo = list(range(len(gate_tiles)))

    def issue_gate_tiles(count):
        for _ in range(min(count, len(gate_tiles_todo))):
            w_ref, src, dst, off, act = gate_tiles[gate_tiles_todo.pop(0)]
            dst[:, off:off + GATE_TILE] = act(_dot(xbuf[...], w_ref[:, src:src + GATE_TILE]))

    @pl.when((pl.program_id(0) == 0) & (pl.program_id(1) == 0))
    def _first_projection():
        xnbuf[...] = _bf(x_ref[...])
        project_head(0, xnbuf)

    xbuf[...] = xnbuf[...]

    def cumulate_forget(h0):
        lf_hi, lf_lo = _split(jnp.concatenate([hlf[h0], hlf[h0 + 1]], axis=1))
        bc = _dot(bdtril_ref[...], lf_hi) + _dot(bdtril_ref[...], lf_lo)
        hb[h0] = bc[:, 0:HEAD_DIM]
        hb[h0 + 1] = bc[:, HEAD_DIM:2 * HEAD_DIM]

    for c in range(1, HEADS):
        head_epilogue(c - 1)
        project_head(c)
        if c % 2 == 0:
            cumulate_forget(c - 2)
    head_epilogue(HEADS - 1)
    issue_gate_tiles(GATE_TILES_AFTER_PROJECTIONS)
    cumulate_forget(HEADS - 2)

    pab = _dot(xbuf[...], wab_ref[...])
    g_col = -jnp.exp(alogr_ref[...]) * _softplus(pab + dtbr_ref[...])
    lane = lax.broadcasted_iota(jnp.int32, pab.shape, 1)
    gbuf[...] = jnp.where(lane < HEADS, _sigmoid(pab), g_col)
    pabt = jnp.transpose(pab)[0:2 * HEADS, :]
    g_row = -jnp.exp(alogc_ref[...]) * _softplus(pabt + dtbc_ref[...])
    gr_hi, gr_lo = _split(g_row)
    gc_row = _dot(gr_hi, bdtriu_ref[...]) + _dot(gr_lo, bdtriu_ref[...])
    for c in range(n_chunks):
        gcrow[c] = gc_row[:, c * CHUNK:(c + 1) * CHUNK]

    ri = lax.broadcasted_iota(jnp.int32, (CHUNK, CHUNK), 0)
    ci = lax.broadcasted_iota(jnp.int32, (CHUNK, CHUNK), 1)
    causal = jnp.where(ri >= ci, 1.0, 0.0)
    strict = jnp.where(ri > ci, 1.0, 0.0)
    trow = lax.broadcasted_iota(jnp.int32, (CHUNK, HEAD_DIM), 0)

    n_lev = len(_LEVELS)
    heads = range(HEADS)
    hsl = [slice(h * HEAD_DIM, (h + 1) * HEAD_DIM) for h in heads]

    def level_mask(l):
        return masks_ref[l * CHUNK:(l + 1) * CHUNK, :]

    units = [(u, h) for u in range(PREP_UNROLL) for h in heads]
    n_units = range(len(units))

    def prepare_chunks(j):
        cs = [j * PREP_UNROLL + u for u in range(PREP_UNROLL)]
        rows = [slice(c * CHUNK, (c + 1) * CHUNK) for c in cs]
        wsum = wsum_ref[...]
        gb, eg, gcr = [], [], []
        for u in range(PREP_UNROLL):
            gb.append(gbuf[rows[u], :])
            g_hi, g_lo = _split(gb[u])
            eg.append(_dot(wsum, g_hi) + _dot(wsum, g_lo))
            gcr.append(gcrow[cs[u]])

        def ld(ref, i):
            u, h = units[i]
            return ref[h, rows[u], :]

        dkk = [ld(dk, i) for i in n_units]
        dkb = [_bf(k) for k in dkk]
        beta = [gb[u][:, h:h + 1] for u, h in units]
        gcol = [eg[u][0:CHUNK, HEADS + h:HEADS + h + 1] for u, h in units]
        decay = [causal * jnp.exp(jnp.minimum(gcol[i] - gcr[u][HEADS + h:HEADS + h + 1, :], 0.0))
                 for i, (u, h) in enumerate(units)]
        kbeta = [dkk[i] * beta[i] for i in n_units]
        lmat = [_dot_nt(_bf(kbeta[i]), dkb[i]) * (strict * decay[i]) for i in n_units]
        for i, (u, h) in enumerate(units):
            qkb[cs[u] * HEADS + h] = _bf(_dot_nt(_bf(ld(dq, i)), dkb[i]) * decay[i])

        hqv = [ld(hq, i) for i in n_units]
        hkv = [ld(hk, i) for i in n_units]
        hqb = [_bf(q) for q in hqv]
        hkb = [_bf(k) for k in hkv]
        sc = [level_mask(n_lev) * _dot_nt(hqb[i], hkb[i]) for i in n_units]
        nmat = [-(level_mask(n_lev - 1) * lmat[i]) for i in n_units]

        hbv = [ld(hb, i) for i in n_units]

        def level_factor(i, l):
            m = _LEVELS[l]
            if m >= SUBLANES // 2:
                ref = jnp.concatenate(
                    [jnp.broadcast_to(hbv[i][r:r + 1, :], (SUBLANES, HEAD_DIM))
                     for r in [(g * SUBLANES // (2 * m)) * (2 * m) + m - 1 for g in range(CHUNK // SUBLANES)]],
                    axis=0)
                return jnp.exp2(jnp.abs(hbv[i] - ref) * NEG_LOG2E)
            f = ld(hf, i)
            if m == 1:
                return jnp.where(trow % 2 == 1, f, 1.0)
            f8 = f.reshape(CHUNK // SUBLANES, SUBLANES, HEAD_DIM)
            f_prev = pltpu.roll(f8, 1, 1).reshape(CHUNK, HEAD_DIM)
            f_next = pltpu.roll(f8, SUBLANES - 1, 1).reshape(CHUNK, HEAD_DIM)
            return jnp.where(trow % 4 == 0, f_next,
                             jnp.where(trow % 4 == 1, 1.0, jnp.where(trow % 4 == 2, f, f_prev * f)))

        def score_level(l):
            for i in n_units:
                zb = _bf(level_factor(i, l))
                sc[i] = sc[i] + level_mask(l) * _dot_nt(hqb[i] * zb, hkb[i] * zb)

        score_level(n_lev - 1)
        lmb = [_bf(lmat[i]) for i in n_units]
        for l in range(n_lev - 2, -1, -1):
            nb = [_bf(nmat[i]) for i in n_units]
            xm = [level_mask(l) * (lmat[i] + _dot(nb[i], lmb[i])) for i in n_units]
            score_level(l)
            for i in n_units:
                nmat[i] = nmat[i] - (xm[i] + _dot(_bf(xm[i]), nb[i]))

        hvb = [_bf(ld(hv, i)) for i in n_units]
        for i, (u, h) in enumerate(units):
            oa[rows[u], hsl[h]] = _dot(_bf(sc[i]), hvb[i])
        for i, (u, h) in enumerate(units):
            kd = hkv[i] * jnp.exp(hbv[i][CHUNK - 1:CHUNK, :] - hbv[i])
            kvinc[cs[u] * HEADS + h] = _dot_tn(hvb[i], _bf(kd))
        for i, (u, h) in enumerate(units):
            qin[rows[u], hsl[h]] = _bf(hqv[i] * jnp.exp(hbv[i]))
            hdec[cs[u], :, hsl[h]] = jnp.exp(hbv[i][CHUNK - 1:CHUNK, :])

        for i, (u, h) in enumerate(units):
            eg_col = jnp.exp(gcol[i])
            rhs = jnp.concatenate([ld(dv, i) * beta[i], kbeta[i] * eg_col], axis=1)
            uw = rhs + _dot(_bf(nmat[i]), _bf(rhs))
            du[rows[u], hsl[h]] = uw[:, 0:HEAD_DIM]
            wq[cs[u], 0:CHUNK, hsl[h]] = _bf(uw[:, HEAD_DIM:2 * HEAD_DIM])
            wq[cs[u], CHUNK:2 * CHUNK, hsl[h]] = _bf(ld(dq, i) * eg_col)
            kdec[rows[u], hsl[h]] = _bf(dkk[i] * jnp.exp(eg[u][CHUNK:2 * CHUNK, HEADS + h:HEADS + h + 1]))
        for u in range(PREP_UNROLL):
            gl[cs[u]] = jnp.exp(eg[u][CHUNK - 1:CHUNK, :])

    for j in range(n_chunks // PREP_UNROLL):
        prepare_chunks(j)

    for c in range(n_chunks):
        rows = slice(c * CHUNK, (c + 1) * CHUNK)
        s = [sdn[h] for h in heads]
        wqc = wq[c]
        ws = [_dot(wqc[:, hsl[h]], _bf(s[h])) for h in heads]
        st = [shg[h] for h in heads]
        dec = hdec[c]
        for h in heads:
            oa[rows, hsl[h]] = oa[rows, hsl[h]] + _dot_nt(qin[rows, hsl[h]], _bf(st[h]))
            shg[h] = st[h] * dec[:, hsl[h]] + kvinc[c * HEADS + h]
        issue_gate_tiles(-(-len(gate_tiles_todo) // (2 * (n_chunks - c))))
        vnb = [_bf(du[rows, hsl[h]] - ws[h][0:CHUNK]) for h in heads]
        glc = gl[c]
        for h in heads:
            ob[rows, hsl[h]] = ws[h][CHUNK:2 * CHUNK] + _dot(qkb[c * HEADS + h], vnb[h])
        for h in heads:
            sdn[h] = s[h] * glc[:, HEADS + h:HEADS + h + 1] + _dot_tn(kdec[rows, hsl[h]], vnb[h])
        issue_gate_tiles(-(-len(gate_tiles_todo) // (2 * (n_chunks - c) - 1)))

    def gated_norm(o_ref, w_row, gate):
        parts = []
        for h in range(HEADS):
            oh = o_ref[:, h * HEAD_DIM:(h + 1) * HEAD_DIM]
            parts.append(oh * lax.rsqrt(jnp.mean(oh * oh, axis=-1, keepdims=True) + RMS_EPS))
        return jnp.concatenate(parts, axis=1) * w_row * gate

    na = gated_norm(oa, hgnw_ref[...], sg[...])
    nb_ = gated_norm(ob, dnnw_ref[...], sz[...])
    merged = sga[...] * _dot(_bf(na), wa_ref[...]) + sgb[...] * _dot(_bf(nb_), wb_ref[...])
    mix = _dot(_bf(merged), wo_ref[...])
    out_ref[...] = _layer_norm(ALPHA * x_ref[...] + mix, g1_ref[...], b1_ref[...])

    xnbuf[...] = _bf(xnext_ref[...])
    project_head(0, xnbuf)


def _mlp_kernel(h_ref, wup_ref, wdown_ref, g2_ref, b2_ref, out_ref):
    tm = h_ref.shape[0]
    hb = _bf(h_ref[...])
    n_ff = D_FF // FF_BLOCK
    assert n_ff >= 2 and tm % MLP_TAIL_SPLIT == 0
    acc = None
    for j in range(n_ff - 1):
        up = _dot(hb, wup_ref[:, j * FF_BLOCK:(j + 1) * FF_BLOCK])
        act = jnp.square(jnp.maximum(up, 0.0))
        part = _dot(_bf(act), wdown_ref[j * FF_BLOCK:(j + 1) * FF_BLOCK, :])
        acc = part if acc is None else acc + part
    up = _dot(hb, wup_ref[:, (n_ff - 1) * FF_BLOCK:n_ff * FF_BLOCK])
    actb = _bf(jnp.square(jnp.maximum(up, 0.0)))
    for r in range(MLP_TAIL_SPLIT):
        rows = slice(r * tm // MLP_TAIL_SPLIT, (r + 1) * tm // MLP_TAIL_SPLIT)
        part = _dot(actb[rows], wdown_ref[(n_ff - 1) * FF_BLOCK:n_ff * FF_BLOCK, :])
        out_ref[rows, :] = _layer_norm(ALPHA * h_ref[rows, :] + acc[rows] + part, g2_ref[...], b2_ref[...])


def _resident(shape):
    nd = len(shape)
    return pl.BlockSpec(shape, lambda *_: (0,) * nd, pipeline_mode=pl.Buffered(1))


def _mixer_call(x, wup, wdown, lbl, win, wgate, wab, convw, hgnw, alogr, dtbr, alogc, dtbc, dnnw,
                wa, wb, wo, g1, b1):
    bsz, seq, _ = x.shape
    tb = min(TOKEN_BLOCK, seq)
    assert seq % tb == 0 and tb % (CHUNK * PREP_UNROLL) == 0
    n_chunks = tb // CHUNK
    wsum = jnp.asarray(_WSUM, jnp.bfloat16)
    masks = jnp.asarray(_MASKS, jnp.float32)
    tt = np.arange(tb)
    bdtriu = jnp.asarray(((tt[:, None] <= tt[None, :]) & ((tt[:, None] // CHUNK) == (tt[None, :] // CHUNK))),
                         jnp.bfloat16)
    consts = (lbl, win, wgate, wab, convw, hgnw, alogr, dtbr, alogc, dtbc, dnnw, wa, wb, wo,
              g1, b1, wsum, masks, bdtriu, bdtriu.T)
    f32 = jnp.float32
    blk = functools.partial(pltpu.VMEM, (tb, KW))
    bf16 = jnp.bfloat16
    per_head = functools.partial(pltpu.VMEM, (HEADS, tb, HEAD_DIM))
    scratch = [pltpu.VMEM((tb, D_MODEL), bf16), pltpu.VMEM((tb, D_MODEL), bf16)]
    scratch += [per_head(f32) for _ in range(9)]
    scratch += [blk(f32) for _ in range(3)]
    scratch += [pltpu.VMEM((tb, 128), f32),
                pltpu.VMEM((n_chunks, 8, CHUNK), f32),
                pltpu.VMEM((HEADS, tb + CONV_PAD, 3 * HEAD_DIM), f32),
                pltpu.VMEM((HEADS, tb, 3 * HEAD_DIM), f32),
                blk(bf16), blk(bf16),
                pltpu.VMEM((n_chunks, 2 * CHUNK, KW), bf16),
                pltpu.VMEM((n_chunks * HEADS, CHUNK, CHUNK), bf16),
                pltpu.VMEM((n_chunks * HEADS, HEAD_DIM, HEAD_DIM), f32),
                pltpu.VMEM((n_chunks, 1, KW), f32),
                pltpu.VMEM((n_chunks, 1, 128), f32),
                blk(f32), blk(f32),
                pltpu.VMEM((tb, D_MODEL), f32),
                pltpu.VMEM((tb, D_MODEL), f32),
                pltpu.VMEM((HEADS, HEAD_DIM, HEAD_DIM), f32),
                pltpu.VMEM((HEADS, HEAD_DIM, HEAD_DIM), f32)]
    n_blocks = seq // tb

    def next_block(b, i):
        flat = jnp.minimum(b * n_blocks + i + 1, bsz * n_blocks - 1)
        return flat // n_blocks, flat % n_blocks, 0

    n_steps = bsz * n_blocks
    up_rows, down_rows = wup.shape[0] // n_steps, wdown.shape[0] // n_steps
    assert up_rows * n_steps == wup.shape[0] and down_rows * n_steps == wdown.shape[0]
    assert up_rows % BF16_ROWS == 0 and down_rows % BF16_ROWS == 0

    def step_rows(b, i):
        return b * n_blocks + i, 0

    return pl.pallas_call(
        _mixer_kernel,
        grid=(bsz, n_blocks),
        in_specs=[pl.BlockSpec((None, tb, D_MODEL), lambda b, i: (b, i, 0)),
                  pl.BlockSpec((None, tb, D_MODEL), next_block),
                  pl.BlockSpec((up_rows, wup.shape[1]), step_rows),
                  pl.BlockSpec((down_rows, wdown.shape[1]), step_rows)]
                 + [_resident(c.shape) for c in consts],
        out_specs=[pl.BlockSpec((None, tb, D_MODEL), lambda b, i: (b, i, 0)),
                   pl.BlockSpec((up_rows, wup.shape[1]), step_rows),
                   pl.BlockSpec((down_rows, wdown.shape[1]), step_rows)],
        out_shape=[jax.ShapeDtypeStruct(x.shape, f32),
                   jax.ShapeDtypeStruct(wup.shape, bf16),
                   jax.ShapeDtypeStruct(wdown.shape, bf16)],
        scratch_shapes=scratch,
        compiler_params=pltpu.CompilerParams(dimension_semantics=("arbitrary", "arbitrary"),
                                             vmem_limit_bytes=VMEM_LIMIT_BYTES),
        name="token_mixer",
    )(x, x, wup, wdown, *consts)


def _mlp_call(h, wup, wdown, g2, b2):
    m = h.shape[0]
    tm = min(MLP_BLOCK, m)
    assert m % tm == 0
    consts = (wup, wdown, g2, b2)
    return pl.pallas_call(
        _mlp_kernel,
        grid=(m // tm,),
        in_specs=[pl.BlockSpec((tm, D_MODEL), lambda i: (i, 0))] + [_resident(c.shape) for c in consts],
        out_specs=pl.BlockSpec((tm, D_MODEL), lambda i: (i, 0)),
        out_shape=jax.ShapeDtypeStruct(h.shape, jnp.float32),
        compiler_params=pltpu.CompilerParams(dimension_semantics=("arbitrary",),
                                             vmem_limit_bytes=VMEM_LIMIT_BYTES),
        name="relu2_mlp",
    )(h, *consts)


def kernel(x, hg_lb_logits, w_in, conv_w, hg_norm_w, dn_A_log, dn_dt_bias, dn_norm_w, w_branch_a, w_branch_b, w_o, ln1_g, ln1_b, w_up, w_down, ln2_g, ln2_b):
    bsz, seq, _ = x.shape
    f32 = jnp.float32
    bf16 = jnp.bfloat16
    o = _OFFS
    l = LAYER
    win = w_in[l].astype(bf16)
    wgate = win[:, o[8]:o[10]]
    wab = jnp.pad(win[:, o[6]:o[8]], ((0, 0), (0, 128 - 2 * HEADS)))
    zeros4 = jnp.zeros((HEADS,), f32)
    alog8 = jnp.concatenate([zeros4, dn_A_log[l].astype(f32)])
    dtb8 = jnp.concatenate([zeros4, dn_dt_bias[l].astype(f32)])
    alogr = jnp.pad(alog8, (0, 128 - 2 * HEADS)).reshape(1, 128)
    dtbr = jnp.pad(dtb8, (0, 128 - 2 * HEADS)).reshape(1, 128)
    h1, wup_bf, wdown_bf = _mixer_call(
        x, w_up[l], w_down[l], hg_lb_logits.astype(f32), win, wgate, wab, conv_w[l].astype(f32),
        hg_norm_w[l].reshape(1, KW), alogr, dtbr, alog8.reshape(8, 1), dtb8.reshape(8, 1),
        jnp.tile(dn_norm_w[l], HEADS).reshape(1, KW),
        w_branch_a[l].astype(bf16), w_branch_b[l].astype(bf16), w_o[l].astype(bf16),
        ln1_g[l].reshape(1, D_MODEL), ln1_b[l].reshape(1, D_MODEL))
    out = _mlp_call(h1.reshape(bsz * seq, D_MODEL), wup_bf, wdown_bf,
                    ln2_g[l].reshape(1, D_MODEL), ln2_b[l].reshape(1, D_MODEL))
    return out.reshape(bsz, seq, D_MODEL)
```

```python
import functools

import numpy as np
import jax
import jax.numpy as jnp
from jax import lax
from jax.experimental import pallas as pl
from jax.experimental.pallas import tpu as pltpu

D_MODEL = 1024
HEADS = 4
HEAD_DIM = 128
KW = HEADS * HEAD_DIM
CHUNK = 64
CONV_W = 4
D_FF = 4 * D_MODEL
DEPTH = 1
LAYER = 0
ALPHA = (2 * DEPTH) ** 0.25
LN_EPS = 1e-5
RMS_EPS = 1e-6
L2_EPS = 1e-6
QK_SCALE = HEAD_DIM ** -0.5
NEG_LOG2E = -1.4426950408889634

TOKEN_BLOCK = 256
PREP_UNROLL = 4
GATE_TILE = 256
GATE_TILES_AFTER_PROJECTIONS = 4
MLP_BLOCK = 1024
MLP_TAIL_SPLIT = 4
FF_BLOCK = 1024
SUBLANES = 8
BF16_ROWS = 16
CONV_PAD = 8
VMEM_LIMIT_BYTES = 56 * 1024 * 1024

_SIZES = (KW, KW, KW, KW, 3 * KW, KW, HEADS, HEADS, D_MODEL, D_MODEL)
_OFFS = np.concatenate([[0], np.cumsum(_SIZES)]).tolist()

COL_HG_QFI = _OFFS[0]
COL_HG_G = _OFFS[3]
COL_DN_QKV = _OFFS[4]
COL_DN_Z = _OFFS[5]

_LEVELS = (32, 16, 8, 4, 2, 1)


def _build_constants():
    t = np.arange(CHUNK)[:, None]
    r = np.arange(CHUNK)[None, :]
    wsum = np.concatenate([r <= t, r > t], 0).astype(np.float32)
    masks = [((t // (2 * m)) == (r // (2 * m))) & ((t % (2 * m)) >= m) & ((r % (2 * m)) < m) for m in _LEVELS]
    masks.append(t == r)
    masks = np.concatenate(masks, 0).astype(np.float32)
    return wsum, masks


_WSUM, _MASKS = _build_constants()


def _dot(a, b):
    return jnp.dot(a, b, preferred_element_type=jnp.float32)


def _dot_nt(a, b):
    return lax.dot_general(a, b, (((1,), (1,)), ((), ())), preferred_element_type=jnp.float32)


def _dot_tn(a, b):
    return lax.dot_general(a, b, (((0,), (0,)), ((), ())), preferred_element_type=jnp.float32)


def _bf(x):
    return x.astype(jnp.bfloat16)


def _split(x):
    hi = _bf(x)
    return hi, _bf(x - hi.astype(jnp.float32))


def _sigmoid(x):
    return 1.0 / (1.0 + jnp.exp2(x * NEG_LOG2E))


def _silu(x):
    return x * _sigmoid(x)


def _softplus(x):
    return jnp.maximum(x, 0.0) + jnp.log1p(jnp.exp(-jnp.abs(x)))


def _layer_norm(x, g, b):
    mu = jnp.mean(x, axis=-1, keepdims=True)
    xc = x - mu
    var = jnp.mean(xc * xc, axis=-1, keepdims=True)
    return xc * lax.rsqrt(var + LN_EPS) * g + b


def _mixer_kernel(x_ref, xnext_ref, wup32_ref, wdown32_ref, lbl_ref, win_ref, wgate_ref, wab_ref, convw_ref, hgnw_ref,
                  alogr_ref, dtbr_ref, alogc_ref, dtbc_ref, dnnw_ref, wa_ref, wb_ref, wo_ref,
                  g1_ref, b1_ref, wsum_ref, masks_ref, bdtriu_ref, bdtril_ref, out_ref, wupb_ref, wdownb_ref,
                  xbuf, xnbuf, hq, hk, hv, hlf, hf, hb, dq, dk, dv, oa, ob, du, gbuf, gcrow, convbuf, rawhg, qin, kdec, wq, qkb,
                  kvinc, hdec, gl, sg, sz, sga, sgb, shg, sdn):
    tb = x_ref.shape[0]
    n_chunks = tb // CHUNK

    wupb_ref[...] = _bf(wup32_ref[...])
    wdownb_ref[...] = _bf(wdown32_ref[...])

    @pl.when(pl.program_id(1) == 0)
    def _reset_carries():
        shg[...] = jnp.zeros_like(shg)
        sdn[...] = jnp.zeros_like(sdn)
        convbuf[:, 0:CONV_PAD, :] = jnp.zeros((HEADS, CONV_PAD, 3 * HEAD_DIM), jnp.float32)

    def project_head(c, xsrc=xbuf):
        cols = ([COL_DN_QKV + k * KW + c * HEAD_DIM for k in range(3)]
                + [COL_HG_QFI + k * KW + c * HEAD_DIM for k in range(3)])
        w_head = jnp.concatenate([win_ref[:, o:o + HEAD_DIM] for o in cols], axis=1)
        raw = _dot(xsrc[...], w_head)
        convbuf[c, CONV_PAD:CONV_PAD + tb, :] = raw[:, 0:3 * HEAD_DIM]
        rawhg[c] = raw[:, 3 * HEAD_DIM:6 * HEAD_DIM]

    def head_epilogue(c):
        cw = jnp.concatenate([convw_ref[:, k * KW + c * HEAD_DIM:k * KW + (c + 1) * HEAD_DIM]
                              for k in range(3)], axis=1)
        base = CONV_PAD - (CONV_W - 1)
        acc = convbuf[c, base:base + tb, :] * cw[0:1, :]
        for j in range(1, CONV_W):
            acc = acc + convbuf[c, base + j:base + j + tb, :] * cw[j:j + 1, :]
        convbuf[c, 0:CONV_PAD, :] = convbuf[c, tb:tb + CONV_PAD, :]
        qkv = _silu(acc)
        qh = qkv[:, 0:HEAD_DIM]
        kh = qkv[:, HEAD_DIM:2 * HEAD_DIM]
        dq[c] = qh * (lax.rsqrt(jnp.sum(qh * qh, axis=-1, keepdims=True) + L2_EPS) * QK_SCALE)
        dk[c] = kh * lax.rsqrt(jnp.sum(kh * kh, axis=-1, keepdims=True) + L2_EPS)
        dv[c] = qkv[:, 2 * HEAD_DIM:3 * HEAD_DIM]
        lg = lbl_ref[:, c * HEAD_DIM:(c + 1) * HEAD_DIM]
        e = jnp.exp(lg - jnp.max(lg, axis=0, keepdims=True))
        lb = e[0:1, :] / jnp.sum(e, axis=0, keepdims=True)
        f = lb + (1.0 - lb) * _sigmoid(rawhg[c, :, HEAD_DIM:2 * HEAD_DIM])
        hlf[c] = jnp.log(f)
        hf[c] = f
        hk[c] = 1.0 - f
        hq[c] = _silu(rawhg[c, :, 0:HEAD_DIM]) * QK_SCALE
        hv[c] = rawhg[c, :, 2 * HEAD_DIM:3 * HEAD_DIM]

    gate_tiles = ([(win_ref, COL_HG_G + t, sg, t, _silu) for t in range(0, KW, GATE_TILE)]
                  + [(win_ref, COL_DN_Z + t, sz, t, _silu) for t in range(0, KW, GATE_TILE)]
                  + [(wgate_ref, t, sga, t, _sigmoid) for t in range(0, D_MODEL, GATE_TILE)]
                  + [(wgate_ref, D_MODEL + t, sgb, t, _sigmoid) for t in range(0, D_MODEL, GATE_TILE)])
    gate_tiles_todo = list(range(len(gate_tiles)))

    def issue_gate_tiles(count):
        for _ in range(min(count, len(gate_tiles_todo))):
            w_ref, src, dst, off, act = gate_tiles[gate_tiles_todo.pop(0)]
            dst[:, off:off + GATE_TILE] = act(_dot(xbuf[...], w_ref[:, src:src + GATE_TILE]))

    @pl.when((pl.program_id(0) == 0) & (pl.program_id(1) == 0))
    def _first_projection():
        xnbuf[...] = _bf(x_ref[...])
        project_head(0, xnbuf)

    xbuf[...] = xnbuf[...]

    def cumulate_forget(h0):
        lf_hi, lf_lo = _split(jnp.concatenate([hlf[h0], hlf[h0 + 1]], axis=1))
        bc = _dot(bdtril_ref[...], lf_hi) + _dot(bdtril_ref[...], lf_lo)
        hb[h0] = bc[:, 0:HEAD_DIM]
        hb[h0 + 1] = bc[:, HEAD_DIM:2 * HEAD_DIM]

    for c in range(1, HEADS):
        head_epilogue(c - 1)
        project_head(c)
        if c % 2 == 0:
            cumulate_forget(c - 2)
    head_epilogue(HEADS - 1)
    issue_gate_tiles(GATE_TILES_AFTER_PROJECTIONS)
    cumulate_forget(HEADS - 2)

    pab = _dot(xbuf[...], wab_ref[...])
    g_col = -jnp.exp(alogr_ref[...]) * _softplus(pab + dtbr_ref[...])
    lane = lax.broadcasted_iota(jnp.int32, pab.shape, 1)
    gbuf[...] = jnp.where(lane < HEADS, _sigmoid(pab), g_col)
    pabt = jnp.transpose(pab)[0:2 * HEADS, :]
    g_row = -jnp.exp(alogc_ref[...]) * _softplus(pabt + dtbc_ref[...])
    gr_hi, gr_lo = _split(g_row)
    gc_row = _dot(gr_hi, bdtriu_ref[...]) + _dot(gr_lo, bdtriu_ref[...])
    for c in range(n_chunks):
        gcrow[c] = gc_row[:, c * CHUNK:(c + 1) * CHUNK]

    ri = lax.broadcasted_iota(jnp.int32, (CHUNK, CHUNK), 0)
    ci = lax.broadcasted_iota(jnp.int32, (CHUNK, CHUNK), 1)
    causal = jnp.where(ri >= ci, 1.0, 0.0)
    strict = jnp.where(ri > ci, 1.0, 0.0)
    trow = lax.broadcasted_iota(jnp.int32, (CHUNK, HEAD_DIM), 0)

    n_lev = len(_LEVELS)
    heads = range(HEADS)
    hsl = [slice(h * HEAD_DIM, (h + 1) * HEAD_DIM) for h in heads]

    def level_mask(l):
        return masks_ref[l * CHUNK:(l + 1) * CHUNK, 0:CHUNK]

    def level_mask2(l):
        return masks_ref[l * CHUNK:(l + 1) * CHUNK, :]

    units = [(u, h) for u in range(PREP_UNROLL) for h in heads]
    n_units = range(len(units))

    def prepare_chunks(j):
        cs = [j * PREP_UNROLL + u for u in range(PREP_UNROLL)]
        rows = [slice(c * CHUNK, (c + 1) * CHUNK) for c in cs]
        wsum = wsum_ref[...]
        gb, eg, gcr = [], [], []
        for u in range(PREP_UNROLL):
            gb.append(gbuf[rows[u], :])
            g_hi, g_lo = _split(gb[u])
            eg.append(_dot(wsum, g_hi) + _dot(wsum, g_lo))
            gcr.append(gcrow[cs[u]])

        def ld(ref, i):
            u, h = units[i]
            return ref[h, rows[u], :]

        dkk = [ld(dk, i) for i in n_units]
        dkb = [_bf(k) for k in dkk]
        beta = [gb[u][:, h:h + 1] for u, h in units]
        gcol = [eg[u][0:CHUNK, HEADS + h:HEADS + h + 1] for u, h in units]
        decay = [causal * jnp.exp(jnp.minimum(gcol[i] - gcr[u][HEADS + h:HEADS + h + 1, :], 0.0))
                 for i, (u, h) in enumerate(units)]
        kbeta = [dkk[i] * beta[i] for i in n_units]
        lmat = [_dot_nt(_bf(kbeta[i]), dkb[i]) * (strict * decay[i]) for i in n_units]
        for i, (u, h) in enumerate(units):
            qkb[cs[u] * HEADS + h] = _bf(_dot_nt(_bf(ld(dq, i)), dkb[i]) * decay[i])

        hqv = [ld(hq, i) for i in n_units]
        hkv = [ld(hk, i) for i in n_units]
        hqb = [_bf(q) for q in hqv]
        hkb = [_bf(k) for k in hkv]
        pairs = [(i, i + 1) for i in range(0, len(units), 2)]
        zero_k = jnp.zeros((CHUNK, HEAD_DIM), jnp.bfloat16)

        def pair_scores(qa, qb, ka, kb):
            lhs = jnp.concatenate([qa, qb], axis=1)
            rhs = jnp.concatenate([jnp.concatenate([ka, zero_k], axis=1),
                                   jnp.concatenate([zero_k, kb], axis=1)], axis=0)
            return _dot_nt(lhs, rhs)

        sc = [level_mask2(n_lev) * pair_scores(hqb[a], hqb[b], hkb[a], hkb[b]) for a, b in pairs]
        nmat = [-(level_mask(n_lev - 1) * lmat[i]) for i in n_units]

        hbv = [ld(hb, i) for i in n_units]

        def level_factor(i, l):
            m = _LEVELS[l]
            if m >= SUBLANES // 2:
                ref = jnp.concatenate(
                    [jnp.broadcast_to(hbv[i][r:r + 1, :], (SUBLANES, HEAD_DIM))
                     for r in [(g * SUBLANES // (2 * m)) * (2 * m) + m - 1 for g in range(CHUNK // SUBLANES)]],
                    axis=0)
                return jnp.exp2(jnp.abs(hbv[i] - ref) * NEG_LOG2E)
            f = ld(hf, i)
            if m == 1:
                return jnp.where(trow % 2 == 1, f, 1.0)
            f8 = f.reshape(CHUNK // SUBLANES, SUBLANES, HEAD_DIM)
            f_prev = pltpu.roll(f8, 1, 1).reshape(CHUNK, HEAD_DIM)
            f_next = pltpu.roll(f8, SUBLANES - 1, 1).reshape(CHUNK, HEAD_DIM)
            return jnp.where(trow % 4 == 0, f_next,
                             jnp.where(trow % 4 == 1, 1.0, jnp.where(trow % 4 == 2, f, f_prev * f)))

        def score_level(l):
            for p, (a, b) in enumerate(pairs):
                za = _bf(level_factor(a, l))
                zb = _bf(level_factor(b, l))
                sc[p] = sc[p] + level_mask2(l) * pair_scores(hqb[a] * za, hqb[b] * zb,
                                                             hkb[a] * za, hkb[b] * zb)

        score_level(n_lev - 1)
        lmb = [_bf(lmat[i]) for i in n_units]
        for l in range(n_lev - 2, -1, -1):
            nb = [_bf(nmat[i]) for i in n_units]
            xm = [level_mask(l) * (lmat[i] + _dot(nb[i], lmb[i])) for i in n_units]
            score_level(l)
            for i in n_units:
                nmat[i] = nmat[i] - (xm[i] + _dot(_bf(xm[i]), nb[i]))

        hvb = [_bf(ld(hv, i)) for i in n_units]
        for p, (a, b) in enumerate(pairs):
            scb = _bf(sc[p])
            (ua, ha), (ub, hb_) = units[a], units[b]
            oa[rows[ua], hsl[ha]] = _dot(scb, jnp.concatenate([hvb[a], zero_k], axis=0))
            oa[rows[ub], hsl[hb_]] = _dot(scb, jnp.concatenate([zero_k, hvb[b]], axis=0))
        for i, (u, h) in enumerate(units):
            kd = hkv[i] * jnp.exp(hbv[i][CHUNK - 1:CHUNK, :] - hbv[i])
            kvinc[cs[u] * HEADS + h] = _dot_tn(hvb[i], _bf(kd))
        for i, (u, h) in enumerate(units):
            qin[rows[u], hsl[h]] = _bf(hqv[i] * jnp.exp(hbv[i]))
            hdec[cs[u], :, hsl[h]] = jnp.exp(hbv[i][CHUNK - 1:CHUNK, :])

        for i, (u, h) in enumerate(units):
            eg_col = jnp.exp(gcol[i])
            rhs = jnp.concatenate([ld(dv, i) * beta[i], kbeta[i] * eg_col], axis=1)
            uw = rhs + _dot(_bf(nmat[i]), _bf(rhs))
            du[rows[u], hsl[h]] = uw[:, 0:HEAD_DIM]
            wq[cs[u], 0:CHUNK, hsl[h]] = _bf(uw[:, HEAD_DIM:2 * HEAD_DIM])
            wq[cs[u], CHUNK:2 * CHUNK, hsl[h]] = _bf(ld(dq, i) * eg_col)
            kdec[rows[u], hsl[h]] = _bf(dkk[i] * jnp.exp(eg[u][CHUNK:2 * CHUNK, HEADS + h:HEADS + h + 1]))
        for u in range(PREP_UNROLL):
            gl[cs[u]] = jnp.exp(eg[u][CHUNK - 1:CHUNK, :])

    for j in range(n_chunks // PREP_UNROLL):
        prepare_chunks(j)

    for c in range(n_chunks):
        rows = slice(c * CHUNK, (c + 1) * CHUNK)
        s = [sdn[h] for h in heads]
        wqc = wq[c]
        ws = [_dot(wqc[:, hsl[h]], _bf(s[h])) for h in heads]
        st = [shg[h] for h in heads]
        dec = hdec[c]
        for h in heads:
            oa[rows, hsl[h]] = oa[rows, hsl[h]] + _dot_nt(qin[rows, hsl[h]], _bf(st[h]))
            shg[h] = st[h] * dec[:, hsl[h]] + kvinc[c * HEADS + h]
        issue_gate_tiles(-(-len(gate_tiles_todo) // (2 * (n_chunks - c))))
        vnb = [_bf(du[rows, hsl[h]] - ws[h][0:CHUNK]) for h in heads]
        glc = gl[c]
        for h in heads:
            ob[rows, hsl[h]] = ws[h][CHUNK:2 * CHUNK] + _dot(qkb[c * HEADS + h], vnb[h])
        for h in heads:
            sdn[h] = s[h] * glc[:, HEADS + h:HEADS + h + 1] + _dot_tn(kdec[rows, hsl[h]], vnb[h])
        issue_gate_tiles(-(-len(gate_tiles_todo) // (2 * (n_chunks - c) - 1)))

    def gated_norm(o_ref, w_row, gate):
        parts = []
        for h in range(HEADS):
            oh = o_ref[:, h * HEAD_DIM:(h + 1) * HEAD_DIM]
            parts.append(oh * lax.rsqrt(jnp.mean(oh * oh, axis=-1, keepdims=True) + RMS_EPS))
        return jnp.concatenate(parts, axis=1) * w_row * gate

    na = gated_norm(oa, hgnw_ref[...], sg[...])
    nb_ = gated_norm(ob, dnnw_ref[...], sz[...])
    merged = sga[...] * _dot(_bf(na), wa_ref[...]) + sgb[...] * _dot(_bf(nb_), wb_ref[...])
    mix = _dot(_bf(merged), wo_ref[...])
    out_ref[...] = _layer_norm(ALPHA * x_ref[...] + mix, g1_ref[...], b1_ref[...])

    xnbuf[...] = _bf(xnext_ref[...])
    project_head(0, xnbuf)


def _mlp_kernel(h_ref, wup_ref, wdown_ref, g2_ref, b2_ref, out_ref):
    tm = h_ref.shape[0]
    hb = _bf(h_ref[...])
    n_ff = D_FF // FF_BLOCK
    assert n_ff >= 2 and tm % MLP_TAIL_SPLIT == 0
    acc = None
    for j in range(n_ff - 1):
        up = _dot(hb, wup_ref[:, j * FF_BLOCK:(j + 1) * FF_BLOCK])
        act = jnp.square(jnp.maximum(up, 0.0))
        part = _dot(_bf(act), wdown_ref[j * FF_BLOCK:(j + 1) * FF_BLOCK, :])
        acc = part if acc is None else acc + part
    up = _dot(hb, wup_ref[:, (n_ff - 1) * FF_BLOCK:n_ff * FF_BLOCK])
    actb = _bf(jnp.square(jnp.maximum(up, 0.0)))
    for r in range(MLP_TAIL_SPLIT):
        rows = slice(r * tm // MLP_TAIL_SPLIT, (r + 1) * tm // MLP_TAIL_SPLIT)
        part = _dot(actb[rows], wdown_ref[(n_ff - 1) * FF_BLOCK:n_ff * FF_BLOCK, :])
        out_ref[rows, :] = _layer_norm(ALPHA * h_ref[rows, :] + acc[rows] + part, g2_ref[...], b2_ref[...])


def _resident(shape):
    nd = len(shape)
    return pl.BlockSpec(shape, lambda *_: (0,) * nd, pipeline_mode=pl.Buffered(1))


def _mixer_call(x, wup, wdown, lbl, win, wgate, wab, convw, hgnw, alogr, dtbr, alogc, dtbc, dnnw,
                wa, wb, wo, g1, b1):
    bsz, seq, _ = x.shape
    tb = min(TOKEN_BLOCK, seq)
    assert seq % tb == 0 and tb % (CHUNK * PREP_UNROLL) == 0
    n_chunks = tb // CHUNK
    wsum = jnp.asarray(_WSUM, jnp.bfloat16)
    masks = jnp.asarray(np.concatenate([_MASKS, _MASKS], axis=1), jnp.float32)
    tt = np.arange(tb)
    bdtriu = jnp.asarray(((tt[:, None] <= tt[None, :]) & ((tt[:, None] // CHUNK) == (tt[None, :] // CHUNK))),
                         jnp.bfloat16)
    consts = (lbl, win, wgate, wab, convw, hgnw, alogr, dtbr, alogc, dtbc, dnnw, wa, wb, wo,
              g1, b1, wsum, masks, bdtriu, bdtriu.T)
    f32 = jnp.float32
    blk = functools.partial(pltpu.VMEM, (tb, KW))
    bf16 = jnp.bfloat16
    per_head = functools.partial(pltpu.VMEM, (HEADS, tb, HEAD_DIM))
    scratch = [pltpu.VMEM((tb, D_MODEL), bf16), pltpu.VMEM((tb, D_MODEL), bf16)]
    scratch += [per_head(f32) for _ in range(9)]
    scratch += [blk(f32) for _ in range(3)]
    scratch += [pltpu.VMEM((tb, 128), f32),
                pltpu.VMEM((n_chunks, 8, CHUNK), f32),
                pltpu.VMEM((HEADS, tb + CONV_PAD, 3 * HEAD_DIM), f32),
                pltpu.VMEM((HEADS, tb, 3 * HEAD_DIM), f32),
                blk(bf16), blk(bf16),
                pltpu.VMEM((n_chunks, 2 * CHUNK, KW), bf16),
                pltpu.VMEM((n_chunks * HEADS, CHUNK, CHUNK), bf16),
                pltpu.VMEM((n_chunks * HEADS, HEAD_DIM, HEAD_DIM), f32),
                pltpu.VMEM((n_chunks, 1, KW), f32),
                pltpu.VMEM((n_chunks, 1, 128), f32),
                blk(f32), blk(f32),
                pltpu.VMEM((tb, D_MODEL), f32),
                pltpu.VMEM((tb, D_MODEL), f32),
                pltpu.VMEM((HEADS, HEAD_DIM, HEAD_DIM), f32),
                pltpu.VMEM((HEADS, HEAD_DIM, HEAD_DIM), f32)]
    n_blocks = seq // tb

    def next_block(b, i):
        flat = jnp.minimum(b * n_blocks + i + 1, bsz * n_blocks - 1)
        return flat // n_blocks, flat % n_blocks, 0

    n_steps = bsz * n_blocks
    up_rows, down_rows = wup.shape[0] // n_steps, wdown.shape[0] // n_steps
    assert up_rows * n_steps == wup.shape[0] and down_rows * n_steps == wdown.shape[0]
    assert up_rows % BF16_ROWS == 0 and down_rows % BF16_ROWS == 0

    def step_rows(b, i):
        return b * n_blocks + i, 0

    return pl.pallas_call(
        _mixer_kernel,
        grid=(bsz, n_blocks),
        in_specs=[pl.BlockSpec((None, tb, D_MODEL), lambda b, i: (b, i, 0)),
                  pl.BlockSpec((None, tb, D_MODEL), next_block),
                  pl.BlockSpec((up_rows, wup.shape[1]), step_rows),
                  pl.BlockSpec((down_rows, wdown.shape[1]), step_rows)]
                 + [_resident(c.shape) for c in consts],
        out_specs=[pl.BlockSpec((None, tb, D_MODEL), lambda b, i: (b, i, 0)),
                   pl.BlockSpec((up_rows, wup.shape[1]), step_rows),
                   pl.BlockSpec((down_rows, wdown.shape[1]), step_rows)],
        out_shape=[jax.ShapeDtypeStruct(x.shape, f32),
                   jax.ShapeDtypeStruct(wup.shape, bf16),
                   jax.ShapeDtypeStruct(wdown.shape, bf16)],
        scratch_shapes=scratch,
        compiler_params=pltpu.CompilerParams(dimension_semantics=("arbitrary", "arbitrary"),
                                             vmem_limit_bytes=VMEM_LIMIT_BYTES),
        name="token_mixer",
    )(x, x, wup, wdown, *consts)


def _mlp_call(h, wup, wdown, g2, b2):
    m = h.shape[0]
    tm = min(MLP_BLOCK, m)
    assert m % tm == 0
    consts = (wup, wdown, g2, b2)
    return pl.pallas_call(
        _mlp_kernel,
        grid=(m // tm,),
        in_specs=[pl.BlockSpec((tm, D_MODEL), lambda i: (i, 0))] + [_resident(c.shape) for c in consts],
        out_specs=pl.BlockSpec((tm, D_MODEL), lambda i: (i, 0)),
        out_shape=jax.ShapeDtypeStruct(h.shape, jnp.float32),
        compiler_params=pltpu.CompilerParams(dimension_semantics=("arbitrary",),
                                             vmem_limit_bytes=VMEM_LIMIT_BYTES),
        name="relu2_mlp",
    )(h, *consts)


def kernel(x, hg_lb_logits, w_in, conv_w, hg_norm_w, dn_A_log, dn_dt_bias, dn_norm_w, w_branch_a, w_branch_b, w_o, ln1_g, ln1_b, w_up, w_down, ln2_g, ln2_b):
    bsz, seq, _ = x.shape
    f32 = jnp.float32
    bf16 = jnp.bfloat16
    o = _OFFS
    l = LAYER
    win = w_in[l].astype(bf16)
    wgate = win[:, o[8]:o[10]]
    wab = jnp.pad(win[:, o[6]:o[8]], ((0, 0), (0, 128 - 2 * HEADS)))
    zeros4 = jnp.zeros((HEADS,), f32)
    alog8 = jnp.concatenate([zeros4, dn_A_log[l].astype(f32)])
    dtb8 = jnp.concatenate([zeros4, dn_dt_bias[l].astype(f32)])
    alogr = jnp.pad(alog8, (0, 128 - 2 * HEADS)).reshape(1, 128)
    dtbr = jnp.pad(dtb8, (0, 128 - 2 * HEADS)).reshape(1, 128)
    h1, wup_bf, wdown_bf = _mixer_call(
        x, w_up[l], w_down[l], hg_lb_logits.astype(f32), win, wgate, wab, conv_w[l].astype(f32),
        hg_norm_w[l].reshape(1, KW), alogr, dtbr, alog8.reshape(8, 1), dtb8.reshape(8, 1),
        jnp.tile(dn_norm_w[l], HEADS).reshape(1, KW),
        w_branch_a[l].astype(bf16), w_branch_b[l].astype(bf16), w_o[l].astype(bf16),
        ln1_g[l].reshape(1, D_MODEL), ln1_b[l].reshape(1, D_MODEL))
    out = _mlp_call(h1.reshape(bsz * seq, D_MODEL), wup_bf, wdown_bf,
                    ln2_g[l].reshape(1, D_MODEL), ln2_b[l].reshape(1, D_MODEL))
    return out.reshape(bsz, seq, D_MODEL)
```

```python
import functools

import numpy as np
import jax
import jax.numpy as jnp
from jax import lax
from jax.experimental import pallas as pl
from jax.experimental.pallas import tpu as pltpu

D_MODEL = 1024
HEADS = 4
HEAD_DIM = 128
KW = HEADS * HEAD_DIM
CHUNK = 64
CONV_W = 4
D_FF = 4 * D_MODEL
DEPTH = 1
LAYER = 0
ALPHA = (2 * DEPTH) ** 0.25
LN_EPS = 1e-5
RMS_EPS = 1e-6
L2_EPS = 1e-6
QK_SCALE = HEAD_DIM ** -0.5
NEG_LOG2E = -1.4426950408889634

TOKEN_BLOCK = 256
PREP_UNROLL = 4
GATE_TILE = 256
GATE_TILES_AFTER_PROJECTIONS = 4
MLP_BLOCK = 1024
MLP_TAIL_SPLIT = 4
FF_BLOCK = 1024
LANES = 128
SUBLANES = 8
BF16_ROWS = 16
CONV_PAD = 8
VMEM_LIMIT_BYTES = 56 * 1024 * 1024

_SIZES = (KW, KW, KW, KW, 3 * KW, KW, HEADS, HEADS, D_MODEL, D_MODEL)
_OFFS = np.concatenate([[0], np.cumsum(_SIZES)]).tolist()

COL_HG_QFI = _OFFS[0]
COL_HG_G = _OFFS[3]
COL_DN_QKV = _OFFS[4]
COL_DN_Z = _OFFS[5]

_LEVELS = (32, 16, 8, 4, 2, 1)


def _build_constants():
    t = np.arange(CHUNK)[:, None]
    r = np.arange(CHUNK)[None, :]
    wsum = np.concatenate([r <= t, r > t], 0).astype(np.float32)
    masks = [((t // (2 * m)) == (r // (2 * m))) & ((t % (2 * m)) >= m) & ((r % (2 * m)) < m) for m in _LEVELS]
    masks.append(t == r)
    masks = np.concatenate(masks, 0).astype(np.float32)
    return wsum, masks


_WSUM, _MASKS = _build_constants()


def _dot(a, b):
    return jnp.dot(a, b, preferred_element_type=jnp.float32)


def _dot_nt(a, b):
    return lax.dot_general(a, b, (((1,), (1,)), ((), ())), preferred_element_type=jnp.float32)


def _dot_tn(a, b):
    return lax.dot_general(a, b, (((0,), (0,)), ((), ())), preferred_element_type=jnp.float32)


def _bf(x):
    return x.astype(jnp.bfloat16)


def _split(x):
    hi = _bf(x)
    return hi, _bf(x - hi.astype(jnp.float32))


def _sigmoid(x):
    return 1.0 / (1.0 + jnp.exp2(x * NEG_LOG2E))


def _silu(x):
    return x * _sigmoid(x)


def _softplus(x):
    return jnp.maximum(x, 0.0) + jnp.log1p(jnp.exp(-jnp.abs(x)))


def _layer_norm(x, g, b):
    mu = jnp.mean(x, axis=-1, keepdims=True)
    xc = x - mu
    var = jnp.mean(xc * xc, axis=-1, keepdims=True)
    return xc * lax.rsqrt(var + LN_EPS) * g + b


def _mixer_kernel(x_ref, xnext_ref, wup32_ref, wdown32_ref, lbl_ref, win_ref, wgate_ref, wab_ref, convw_ref, hgnw_ref,
                  alogr_ref, dtbr_ref, alogc_ref, dtbc_ref, dnnw_ref, wa_ref, wb_ref, wo_ref,
                  g1_ref, b1_ref, wsum_ref, masks_ref, bdtriu_ref, bdtril_ref, out_ref, wupb_ref, wdownb_ref,
                  xbuf, xnbuf, hq, hk, hv, hlf, hf, hb, dq, dk, dv, oa, ob, du, gbuf, gcrow, convbuf, rawhg, qin, kdec, wq, qkb,
                  kvinc, hdec, gl, sg, sz, sga, sgb, shg, sdn):
    tb = x_ref.shape[0]
    n_chunks = tb // CHUNK

    wupb_ref[...] = _bf(wup32_ref[...])
    wdownb_ref[...] = _bf(wdown32_ref[...])

    @pl.when(pl.program_id(1) == 0)
    def _reset_carries():
        shg[...] = jnp.zeros_like(shg)
        sdn[...] = jnp.zeros_like(sdn)
        convbuf[:, 0:CONV_PAD, :] = jnp.zeros((HEADS, CONV_PAD, 3 * HEAD_DIM), jnp.float32)

    def project_head(c, xsrc=xbuf):
        cols = ([COL_DN_QKV + k * KW + c * HEAD_DIM for k in range(3)]
                + [COL_HG_QFI + k * KW + c * HEAD_DIM for k in range(3)])
        w_head = jnp.concatenate([win_ref[:, o:o + HEAD_DIM] for o in cols], axis=1)
        raw = _dot(xsrc[...], w_head)
        convbuf[c, CONV_PAD:CONV_PAD + tb, :] = raw[:, 0:3 * HEAD_DIM]
        rawhg[c] = raw[:, 3 * HEAD_DIM:6 * HEAD_DIM]

    def head_epilogue(c):
        cw = jnp.concatenate([convw_ref[:, k * KW + c * HEAD_DIM:k * KW + (c + 1) * HEAD_DIM]
                              for k in range(3)], axis=1)
        base = CONV_PAD - (CONV_W - 1)
        acc = convbuf[c, base:base + tb, :] * cw[0:1, :]
        for j in range(1, CONV_W):
            acc = acc + convbuf[c, base + j:base + j + tb, :] * cw[j:j + 1, :]
        convbuf[c, 0:CONV_PAD, :] = convbuf[c, tb:tb + CONV_PAD, :]
        qkv = _silu(acc)
        qh = qkv[:, 0:HEAD_DIM]
        kh = qkv[:, HEAD_DIM:2 * HEAD_DIM]
        dq[c] = qh * (lax.rsqrt(jnp.sum(qh * qh, axis=-1, keepdims=True) + L2_EPS) * QK_SCALE)
        dk[c] = kh * lax.rsqrt(jnp.sum(kh * kh, axis=-1, keepdims=True) + L2_EPS)
        dv[c] = qkv[:, 2 * HEAD_DIM:3 * HEAD_DIM]
        lg = lbl_ref[:, c * HEAD_DIM:(c + 1) * HEAD_DIM]
        e = jnp.exp(lg - jnp.max(lg, axis=0, keepdims=True))
        lb = e[0:1, :] / jnp.sum(e, axis=0, keepdims=True)
        f = lb + (1.0 - lb) * _sigmoid(rawhg[c, :, HEAD_DIM:2 * HEAD_DIM])
        hlf[c] = jnp.log(f)
        hf[c] = f
        hk[c] = 1.0 - f
        hq[c] = _silu(rawhg[c, :, 0:HEAD_DIM]) * QK_SCALE
        hv[c] = rawhg[c, :, 2 * HEAD_DIM:3 * HEAD_DIM]

    gate_tiles = ([(win_ref, COL_HG_G + t, sg, t, _silu) for t in range(0, KW, GATE_TILE)]
                  + [(win_ref, COL_DN_Z + t, sz, t, _silu) for t in range(0, KW, GATE_TILE)]
                  + [(wgate_ref, t, sga, t, _sigmoid) for t in range(0, D_MODEL, GATE_TILE)]
                  + [(wgate_ref, D_MODEL + t, sgb, t, _sigmoid) for t in range(0, D_MODEL, GATE_TILE)])
    gate_tiles_todo = list(range(len(gate_tiles)))

    def issue_gate_tiles(count):
        for _ in range(min(count, len(gate_tiles_todo))):
            w_ref, src, dst, off, act = gate_tiles[gate_tiles_todo.pop(0)]
            dst[:, off:off + GATE_TILE] = act(_dot(xbuf[...], w_ref[:, src:src + GATE_TILE]))

    @pl.when((pl.program_id(0) == 0) & (pl.program_id(1) == 0))
    def _first_projection():
        xnbuf[...] = _bf(x_ref[...])
        project_head(0, xnbuf)

    xbuf[...] = xnbuf[...]

    def cumulate_forget(h0):
        lf_hi, lf_lo = _split(jnp.concatenate([hlf[h0], hlf[h0 + 1]], axis=1))
        bc = _dot(bdtril_ref[...], lf_hi) + _dot(bdtril_ref[...], lf_lo)
        hb[h0] = bc[:, 0:HEAD_DIM]
        hb[h0 + 1] = bc[:, HEAD_DIM:2 * HEAD_DIM]

    for c in range(1, HEADS):
        head_epilogue(c - 1)
        project_head(c)
        if c % 2 == 0:
            cumulate_forget(c - 2)
    head_epilogue(HEADS - 1)
    issue_gate_tiles(GATE_TILES_AFTER_PROJECTIONS)
    cumulate_forget(HEADS - 2)

    pab = _dot(xbuf[...], wab_ref[...])
    g_col = -jnp.exp(alogr_ref[...]) * _softplus(pab + dtbr_ref[...])
    lane = lax.broadcasted_iota(jnp.int32, pab.shape, 1)
    gbuf[...] = jnp.where(lane < HEADS, _sigmoid(pab), g_col)
    pabt = jnp.transpose(pab)[0:2 * HEADS, :]
    g_row = -jnp.exp(alogc_ref[...]) * _softplus(pabt + dtbc_ref[...])
    gr_hi, gr_lo = _split(g_row)
    gc_row = _dot(gr_hi, bdtriu_ref[...]) + _dot(gr_lo, bdtriu_ref[...])
    for c in range(n_chunks):
        gcrow[c] = gc_row[:, c * CHUNK:(c + 1) * CHUNK]

    ri = lax.broadcasted_iota(jnp.int32, (CHUNK, CHUNK), 0)
    ci = lax.broadcasted_iota(jnp.int32, (CHUNK, CHUNK), 1)
    causal = jnp.where(ri >= ci, 1.0, 0.0)
    strict = jnp.where(ri > ci, 1.0, 0.0)
    trow = lax.broadcasted_iota(jnp.int32, (CHUNK, HEAD_DIM), 0)

    n_lev = len(_LEVELS)
    heads = range(HEADS)
    hsl = [slice(h * HEAD_DIM, (h + 1) * HEAD_DIM) for h in heads]

    def level_mask(l):
        return masks_ref[l * CHUNK:(l + 1) * CHUNK, :]

    units = [(u, h) for u in range(PREP_UNROLL) for h in heads]
    n_units = range(len(units))

    def prepare_chunks(j):
        cs = [j * PREP_UNROLL + u for u in range(PREP_UNROLL)]
        rows = [slice(c * CHUNK, (c + 1) * CHUNK) for c in cs]
        wsum = wsum_ref[...]
        gb, eg, gcr = [], [], []
        for u in range(PREP_UNROLL):
            gb.append(gbuf[rows[u], :])
            g_hi, g_lo = _split(gb[u])
            eg.append(_dot(wsum, g_hi) + _dot(wsum, g_lo))
            gcr.append(gcrow[cs[u]])

        def ld(ref, i):
            u, h = units[i]
            return ref[h, rows[u], :]

        dkk = [ld(dk, i) for i in n_units]
        dkb = [_bf(k) for k in dkk]
        beta = [gb[u][:, h:h + 1] for u, h in units]
        gcol = [eg[u][0:CHUNK, HEADS + h:HEADS + h + 1] for u, h in units]
        decay = [causal * jnp.exp(jnp.minimum(gcol[i] - gcr[u][HEADS + h:HEADS + h + 1, :], 0.0))
                 for i, (u, h) in enumerate(units)]
        kbeta = [dkk[i] * beta[i] for i in n_units]
        lmat = [_dot_nt(_bf(kbeta[i]), dkb[i]) * (strict * decay[i]) for i in n_units]
        for i, (u, h) in enumerate(units):
            qkb[cs[u] * HEADS + h] = _bf(_dot_nt(_bf(ld(dq, i)), dkb[i]) * decay[i])

        hqv = [ld(hq, i) for i in n_units]
        hkv = [ld(hk, i) for i in n_units]
        hqb = [_bf(q) for q in hqv]
        hkb = [_bf(k) for k in hkv]
        sc = [level_mask(n_lev) * _dot_nt(hqb[i], hkb[i]) for i in n_units]
        nmat = [-(level_mask(n_lev - 1) * lmat[i]) for i in n_units]

        hbv = [ld(hb, i) for i in n_units]

        def level_factor(i, l):
            m = _LEVELS[l]
            if m >= SUBLANES // 2:
                ref = jnp.concatenate(
                    [jnp.broadcast_to(hbv[i][r:r + 1, :], (SUBLANES, HEAD_DIM))
                     for r in [(g * SUBLANES // (2 * m)) * (2 * m) + m - 1 for g in range(CHUNK // SUBLANES)]],
                    axis=0)
                return jnp.exp2(jnp.abs(hbv[i] - ref) * NEG_LOG2E)
            f = ld(hf, i)
            if m == 1:
                return jnp.where(trow % 2 == 1, f, 1.0)
            f8 = f.reshape(CHUNK // SUBLANES, SUBLANES, HEAD_DIM)
            f_prev = pltpu.roll(f8, 1, 1).reshape(CHUNK, HEAD_DIM)
            f_next = pltpu.roll(f8, SUBLANES - 1, 1).reshape(CHUNK, HEAD_DIM)
            return jnp.where(trow % 4 == 0, f_next,
                             jnp.where(trow % 4 == 1, 1.0, jnp.where(trow % 4 == 2, f, f_prev * f)))

        def score_level(l):
            for i in n_units:
                zb = _bf(level_factor(i, l))
                sc[i] = sc[i] + level_mask(l) * _dot_nt(hqb[i] * zb, hkb[i] * zb)

        score_level(n_lev - 1)
        lmb = [_bf(lmat[i]) for i in n_units]
        for l in range(n_lev - 2, -1, -1):
            nb = [_bf(nmat[i]) for i in n_units]
            xm = [level_mask(l) * (lmat[i] + _dot(nb[i], lmb[i])) for i in n_units]
            score_level(l)
            for i in n_units:
                nmat[i] = nmat[i] - (xm[i] + _dot(_bf(xm[i]), nb[i]))

        hvb = [_bf(ld(hv, i)) for i in n_units]
        for i, (u, h) in enumerate(units):
            oa[rows[u], hsl[h]] = _dot(_bf(sc[i]), hvb[i])
        for i, (u, h) in enumerate(units):
            kd = hkv[i] * jnp.exp(hbv[i][CHUNK - 1:CHUNK, :] - hbv[i])
            kvinc[cs[u] * HEADS + h] = _dot_tn(hvb[i], _bf(kd))
        for i, (u, h) in enumerate(units):
            qin[rows[u], hsl[h]] = _bf(hqv[i] * jnp.exp(hbv[i]))
            hdec[cs[u], :, hsl[h]] = jnp.exp(hbv[i][CHUNK - 1:CHUNK, :])

        for i, (u, h) in enumerate(units):
            eg_col = jnp.exp(gcol[i])
            rhs = jnp.concatenate([ld(dv, i) * beta[i], kbeta[i] * eg_col], axis=1)
            uw = rhs + _dot(_bf(nmat[i]), _bf(rhs))
            du[rows[u], hsl[h]] = uw[:, 0:HEAD_DIM]
            wq[cs[u], 0:CHUNK, hsl[h]] = _bf(uw[:, HEAD_DIM:2 * HEAD_DIM])
            wq[cs[u], CHUNK:2 * CHUNK, hsl[h]] = _bf(ld(dq, i) * eg_col)
            kdec[rows[u], hsl[h]] = _bf(dkk[i] * jnp.exp(eg[u][CHUNK:2 * CHUNK, HEADS + h:HEADS + h + 1]))
        for u in range(PREP_UNROLL):
            gl[cs[u]] = jnp.exp(eg[u][CHUNK - 1:CHUNK, :])

    for j in range(n_chunks // PREP_UNROLL):
        prepare_chunks(j)

    for c in range(n_chunks):
        rows = slice(c * CHUNK, (c + 1) * CHUNK)
        s = [sdn[h] for h in heads]
        wqc = wq[c]
        ws = [_dot(wqc[:, hsl[h]], _bf(s[h])) for h in heads]
        st = [shg[h] for h in heads]
        dec = hdec[c]
        for h in heads:
            oa[rows, hsl[h]] = oa[rows, hsl[h]] + _dot_nt(qin[rows, hsl[h]], _bf(st[h]))
            shg[h] = st[h] * dec[:, hsl[h]] + kvinc[c * HEADS + h]
        issue_gate_tiles(-(-len(gate_tiles_todo) // (2 * (n_chunks - c))))
        vnb = [_bf(du[rows, hsl[h]] - ws[h][0:CHUNK]) for h in heads]
        glc = gl[c]
        for h in heads:
            ob[rows, hsl[h]] = ws[h][CHUNK:2 * CHUNK] + _dot(qkb[c * HEADS + h], vnb[h])
        for h in heads:
            sdn[h] = s[h] * glc[:, HEADS + h:HEADS + h + 1] + _dot_tn(kdec[rows, hsl[h]], vnb[h])
        issue_gate_tiles(-(-len(gate_tiles_todo) // (2 * (n_chunks - c) - 1)))

    def gated_norm(o_ref, w_row, gate):
        parts = []
        for h in range(HEADS):
            oh = o_ref[:, h * HEAD_DIM:(h + 1) * HEAD_DIM]
            parts.append(oh * lax.rsqrt(jnp.mean(oh * oh, axis=-1, keepdims=True) + RMS_EPS))
        return jnp.concatenate(parts, axis=1) * w_row * gate

    na = gated_norm(oa, hgnw_ref[...], sg[...])
    nb_ = gated_norm(ob, dnnw_ref[...], sz[...])
    merged = sga[...] * _dot(_bf(na), wa_ref[...]) + sgb[...] * _dot(_bf(nb_), wb_ref[...])
    mix = _dot(_bf(merged), wo_ref[...])
    out_ref[...] = _layer_norm(ALPHA * x_ref[...] + mix, g1_ref[...], b1_ref[...])

    xnbuf[...] = _bf(xnext_ref[...])
    project_head(0, xnbuf)


def _mlp_kernel(h_ref, wup_ref, wdown_ref, g2_ref, b2_ref, out_ref):
    tm = h_ref.shape[0]
    hb = _bf(h_ref[...])
    n_ff = D_FF // FF_BLOCK
    assert n_ff >= 2 and tm % MLP_TAIL_SPLIT == 0
    acc = None
    for j in range(n_ff - 1):
        up = _dot(hb, wup_ref[:, j * FF_BLOCK:(j + 1) * FF_BLOCK])
        act = jnp.square(jnp.maximum(up, 0.0))
        part = _dot(_bf(act), wdown_ref[j * FF_BLOCK:(j + 1) * FF_BLOCK, :])
        acc = part if acc is None else acc + part
    up = _dot(hb, wup_ref[:, (n_ff - 1) * FF_BLOCK:n_ff * FF_BLOCK])
    actb = _bf(jnp.square(jnp.maximum(up, 0.0)))
    for r in range(MLP_TAIL_SPLIT):
        rows = slice(r * tm // MLP_TAIL_SPLIT, (r + 1) * tm // MLP_TAIL_SPLIT)
        part = _dot(actb[rows], wdown_ref[(n_ff - 1) * FF_BLOCK:n_ff * FF_BLOCK, :])
        out_ref[rows, :] = _layer_norm(ALPHA * h_ref[rows, :] + acc[rows] + part, g2_ref[...], b2_ref[...])


def _resident(shape):
    nd = len(shape)
    return pl.BlockSpec(shape, lambda *_: (0,) * nd, pipeline_mode=pl.Buffered(1))


def _mixer_call(x, wup, wdown, lbl, win, wgate, wab, convw, hgnw, alogr, dtbr, alogc, dtbc, dnnw,
                wa, wb, wo, g1, b1):
    bsz, seq, _ = x.shape
    tb = min(TOKEN_BLOCK, seq)
    assert seq % tb == 0 and tb % (CHUNK * PREP_UNROLL) == 0
    n_chunks = tb // CHUNK
    wsum = jnp.asarray(_WSUM, jnp.bfloat16)
    masks = jnp.asarray(_MASKS, jnp.float32)
    tt = np.arange(tb)
    bdtriu = jnp.asarray(((tt[:, None] <= tt[None, :]) & ((tt[:, None] // CHUNK) == (tt[None, :] // CHUNK))),
                         jnp.bfloat16)
    consts = (lbl, win, wgate, wab, convw, hgnw, alogr, dtbr, alogc, dtbc, dnnw, wa, wb, wo,
              g1, b1, wsum, masks, bdtriu, bdtriu.T)
    f32 = jnp.float32
    blk = functools.partial(pltpu.VMEM, (tb, KW))
    bf16 = jnp.bfloat16
    per_head = functools.partial(pltpu.VMEM, (HEADS, tb, HEAD_DIM))
    scratch = [pltpu.VMEM((tb, D_MODEL), bf16), pltpu.VMEM((tb, D_MODEL), bf16)]
    scratch += [per_head(f32) for _ in range(9)]
    scratch += [blk(f32) for _ in range(3)]
    scratch += [pltpu.VMEM((tb, LANES), f32),
                pltpu.VMEM((n_chunks, 8, CHUNK), f32),
                pltpu.VMEM((HEADS, tb + CONV_PAD, 3 * HEAD_DIM), f32),
                pltpu.VMEM((HEADS, tb, 3 * HEAD_DIM), f32),
                blk(bf16), blk(bf16),
                pltpu.VMEM((n_chunks, 2 * CHUNK, KW), bf16),
                pltpu.VMEM((n_chunks * HEADS, CHUNK, CHUNK), bf16),
                pltpu.VMEM((n_chunks * HEADS, HEAD_DIM, HEAD_DIM), f32),
                pltpu.VMEM((n_chunks, 1, KW), f32),
                pltpu.VMEM((n_chunks, 1, LANES), f32),
                blk(f32), blk(f32),
                pltpu.VMEM((tb, D_MODEL), f32),
                pltpu.VMEM((tb, D_MODEL), f32),
                pltpu.VMEM((HEADS, HEAD_DIM, HEAD_DIM), f32),
                pltpu.VMEM((HEADS, HEAD_DIM, HEAD_DIM), f32)]
    n_blocks = seq // tb

    def next_block(b, i):
        flat = jnp.minimum(b * n_blocks + i + 1, bsz * n_blocks - 1)
        return flat // n_blocks, flat % n_blocks, 0

    n_steps = bsz * n_blocks
    up_rows, down_rows = wup.shape[0] // n_steps, wdown.shape[0] // n_steps
    assert up_rows * n_steps == wup.shape[0] and down_rows * n_steps == wdown.shape[0]
    assert up_rows % BF16_ROWS == 0 and down_rows % BF16_ROWS == 0

    def step_rows(b, i):
        return b * n_blocks + i, 0

    return pl.pallas_call(
        _mixer_kernel,
        grid=(bsz, n_blocks),
        in_specs=[pl.BlockSpec((None, tb, D_MODEL), lambda b, i: (b, i, 0)),
                  pl.BlockSpec((None, tb, D_MODEL), next_block),
                  pl.BlockSpec((up_rows, wup.shape[1]), step_rows),
                  pl.BlockSpec((down_rows, wdown.shape[1]), step_rows)]
                 + [_resident(c.shape) for c in consts],
        out_specs=[pl.BlockSpec((None, tb, D_MODEL), lambda b, i: (b, i, 0)),
                   pl.BlockSpec((up_rows, wup.shape[1]), step_rows),
                   pl.BlockSpec((down_rows, wdown.shape[1]), step_rows)],
        out_shape=[jax.ShapeDtypeStruct(x.shape, f32),
                   jax.ShapeDtypeStruct(wup.shape, bf16),
                   jax.ShapeDtypeStruct(wdown.shape, bf16)],
        scratch_shapes=scratch,
        compiler_params=pltpu.CompilerParams(dimension_semantics=("arbitrary", "arbitrary"),
                                             vmem_limit_bytes=VMEM_LIMIT_BYTES),
        name="token_mixer",
    )(x, x, wup, wdown, *consts)


def _mlp_call(h, wup, wdown, g2, b2):
    m = h.shape[0]
    tm = min(MLP_BLOCK, m)
    assert m % tm == 0
    consts = (wup, wdown, g2, b2)
    return pl.pallas_call(
        _mlp_kernel,
        grid=(m // tm,),
        in_specs=[pl.BlockSpec((tm, D_MODEL), lambda i: (i, 0))] + [_resident(c.shape) for c in consts],
        out_specs=pl.BlockSpec((tm, D_MODEL), lambda i: (i, 0)),
        out_shape=jax.ShapeDtypeStruct(h.shape, jnp.float32),
        compiler_params=pltpu.CompilerParams(dimension_semantics=("arbitrary",),
                                             vmem_limit_bytes=VMEM_LIMIT_BYTES),
        name="relu2_mlp",
    )(h, *consts)


def kernel(x, hg_lb_logits, w_in, conv_w, hg_norm_w, dn_A_log, dn_dt_bias, dn_norm_w, w_branch_a, w_branch_b, w_o, ln1_g, ln1_b, w_up, w_down, ln2_g, ln2_b):
    bsz, seq, _ = x.shape
    f32 = jnp.float32
    bf16 = jnp.bfloat16
    o = _OFFS
    l = LAYER
    win = w_in[l].astype(bf16)
    wgate = win[:, o[8]:o[10]]
    wab = jnp.pad(win[:, o[6]:o[8]], ((0, 0), (0, LANES - 2 * HEADS)))
    zeros4 = jnp.zeros((HEADS,), f32)
    alog8 = jnp.concatenate([zeros4, dn_A_log[l].astype(f32)])
    dtb8 = jnp.concatenate([zeros4, dn_dt_bias[l].astype(f32)])
    alogr = jnp.pad(alog8, (0, LANES - 2 * HEADS)).reshape(1, LANES)
    dtbr = jnp.pad(dtb8, (0, LANES - 2 * HEADS)).reshape(1, LANES)
    h1, wup_bf, wdown_bf = _mixer_call(
        x, w_up[l], w_down[l], hg_lb_logits.astype(f32), win, wgate, wab, conv_w[l].astype(f32),
        hg_norm_w[l].reshape(1, KW), alogr, dtbr, alog8.reshape(8, 1), dtb8.reshape(8, 1),
        jnp.tile(dn_norm_w[l], HEADS).reshape(1, KW),
        w_branch_a[l].astype(bf16), w_branch_b[l].astype(bf16), w_o[l].astype(bf16),
        ln1_g[l].reshape(1, D_MODEL), ln1_b[l].reshape(1, D_MODEL))
    out = _mlp_call(h1.reshape(bsz * seq, D_MODEL), wup_bf, wdown_bf,
                    ln2_g[l].reshape(1, D_MODEL), ln2_b[l].reshape(1, D_MODEL))
    return out.reshape(bsz, seq, D_MODEL)
```

```python
import functools

import numpy as np
import jax
import jax.numpy as jnp
from jax import lax
from jax.experimental import pallas as pl
from jax.experimental.pallas import tpu as pltpu

D_MODEL = 1024
HEADS = 4
HEAD_DIM = 128
KW = HEADS * HEAD_DIM
CHUNK = 64
CONV_W = 4
D_FF = 4 * D_MODEL
DEPTH = 1
LAYER = 0
ALPHA = (2 * DEPTH) ** 0.25
LN_EPS = 1e-5
RMS_EPS = 1e-6
L2_EPS = 1e-6
QK_SCALE = HEAD_DIM ** -0.5
NEG_LOG2E = -1.4426950408889634

TOKEN_BLOCK = 256
PREP_UNROLL = 4
GATE_TILE = 256
GATE_TILES_AFTER_PROJECTIONS = 4
MLP_BLOCK = 1024
MLP_TAIL_SPLIT = 4
FF_BLOCK = 1024
LANES = 128
SUBLANES = 8
BF16_ROWS = 16
CONV_PAD = 8
VMEM_LIMIT_BYTES = 56 * 1024 * 1024

_SIZES = (KW, KW, KW, KW, 3 * KW, KW, HEADS, HEADS, D_MODEL, D_MODEL)
_OFFS = np.concatenate([[0], np.cumsum(_SIZES)]).tolist()

COL_HG_QFI = _OFFS[0]
COL_HG_G = _OFFS[3]
COL_DN_QKV = _OFFS[4]
COL_DN_Z = _OFFS[5]

_LEVELS = (32, 16, 8, 4, 2, 1)


def _build_constants():
    t = np.arange(CHUNK)[:, None]
    r = np.arange(CHUNK)[None, :]
    wsum = np.concatenate([r <= t, r > t], 0).astype(np.float32)
    masks = [((t // (2 * m)) == (r // (2 * m))) & ((t % (2 * m)) >= m) & ((r % (2 * m)) < m) for m in _LEVELS]
    masks.append(t == r)
    masks = np.concatenate(masks, 0).astype(np.float32)
    return wsum, masks


_WSUM, _MASKS = _build_constants()


def _dot(a, b):
    return jnp.dot(a, b, preferred_element_type=jnp.float32)


def _dot_nt(a, b):
    return lax.dot_general(a, b, (((1,), (1,)), ((), ())), preferred_element_type=jnp.float32)


def _dot_tn(a, b):
    return lax.dot_general(a, b, (((0,), (0,)), ((), ())), preferred_element_type=jnp.float32)


def _bf(x):
    return x.astype(jnp.bfloat16)


def _split(x):
    hi = _bf(x)
    return hi, _bf(x - hi.astype(jnp.float32))


def _sigmoid(x):
    return 1.0 / (1.0 + jnp.exp2(x * NEG_LOG2E))


def _silu(x):
    return x * _sigmoid(x)


def _softplus(x):
    return jnp.maximum(x, 0.0) + jnp.log1p(jnp.exp(-jnp.abs(x)))


def _layer_norm(x, g, b):
    mu = jnp.mean(x, axis=-1, keepdims=True)
    xc = x - mu
    var = jnp.mean(xc * xc, axis=-1, keepdims=True)
    return xc * lax.rsqrt(var + LN_EPS) * g + b


def _mixer_kernel(x_ref, xnext_ref, wup32_ref, wdown32_ref, lbl_ref, win_ref, wgate_ref, wab_ref, convw_ref, hgnw_ref,
                  alogr_ref, dtbr_ref, alogc_ref, dtbc_ref, dnnw_ref, wa_ref, wb_ref, wo_ref,
                  g1_ref, b1_ref, wsum_ref, masks_ref, bdtriu_ref, bdtril_ref, out_ref, wupb_ref, wdownb_ref,
                  xbuf, xnbuf, hq, hk, hv, hlf, hf, hb, dq, dk, dv, oa, ob, du, gbuf, gcrow, convbuf, rawhg, qin, kdec, wq, qkb,
                  kvinc, hdec, gl, sg, sz, sga, sgb, shg, sdn):
    tb = x_ref.shape[0]
    n_chunks = tb // CHUNK

    wupb_ref[...] = _bf(wup32_ref[...])
    wdownb_ref[...] = _bf(wdown32_ref[...])

    @pl.when(pl.program_id(1) == 0)
    def _reset_carries():
        shg[...] = jnp.zeros_like(shg)
        sdn[...] = jnp.zeros_like(sdn)
        convbuf[:, 0:CONV_PAD, :] = jnp.zeros((HEADS, CONV_PAD, 3 * HEAD_DIM), jnp.float32)

    def project_head(c, xsrc=xbuf):
        cols = ([COL_DN_QKV + k * KW + c * HEAD_DIM for k in range(3)]
                + [COL_HG_QFI + k * KW + c * HEAD_DIM for k in range(3)])
        w_head = jnp.concatenate([win_ref[:, o:o + HEAD_DIM] for o in cols], axis=1)
        raw = _dot(xsrc[...], w_head)
        convbuf[c, CONV_PAD:CONV_PAD + tb, :] = raw[:, 0:3 * HEAD_DIM]
        rawhg[c] = raw[:, 3 * HEAD_DIM:6 * HEAD_DIM]

    def head_epilogue(c):
        cw = jnp.concatenate([convw_ref[:, k * KW + c * HEAD_DIM:k * KW + (c + 1) * HEAD_DIM]
                              for k in range(3)], axis=1)
        base = CONV_PAD - (CONV_W - 1)
        acc = convbuf[c, base:base + tb, :] * cw[0:1, :]
        for j in range(1, CONV_W):
            acc = acc + convbuf[c, base + j:base + j + tb, :] * cw[j:j + 1, :]
        convbuf[c, 0:CONV_PAD, :] = convbuf[c, tb:tb + CONV_PAD, :]
        qkv = _silu(acc)
        qh = qkv[:, 0:HEAD_DIM]
        kh = qkv[:, HEAD_DIM:2 * HEAD_DIM]
        dq[c] = qh * (lax.rsqrt(jnp.sum(qh * qh, axis=-1, keepdims=True) + L2_EPS) * QK_SCALE)
        dk[c] = kh * lax.rsqrt(jnp.sum(kh * kh, axis=-1, keepdims=True) + L2_EPS)
        dv[c] = qkv[:, 2 * HEAD_DIM:3 * HEAD_DIM]
        lg = lbl_ref[:, c * HEAD_DIM:(c + 1) * HEAD_DIM]
        e = jnp.exp(lg - jnp.max(lg, axis=0, keepdims=True))
        lb = e[0:1, :] / jnp.sum(e, axis=0, keepdims=True)
        f = lb + (1.0 - lb) * _sigmoid(rawhg[c, :, HEAD_DIM:2 * HEAD_DIM])
        hlf[c] = jnp.log(f)
        hf[c] = f
        hk[c] = 1.0 - f
        hq[c] = _silu(rawhg[c, :, 0:HEAD_DIM]) * QK_SCALE
        hv[c] = rawhg[c, :, 2 * HEAD_DIM:3 * HEAD_DIM]

    gate_tiles = ([(win_ref, COL_HG_G + t, sg, t, _silu) for t in range(0, KW, GATE_TILE)]
                  + [(win_ref, COL_DN_Z + t, sz, t, _silu) for t in range(0, KW, GATE_TILE)]
                  + [(wgate_ref, t, sga, t, _sigmoid) for t in range(0, D_MODEL, GATE_TILE)]
                  + [(wgate_ref, D_MODEL + t, sgb, t, _sigmoid) for t in range(0, D_MODEL, GATE_TILE)])
    gate_tiles_todo = list(range(len(gate_tiles)))

    def issue_gate_tiles(count):
        for _ in range(min(count, len(gate_tiles_todo))):
            w_ref, src, dst, off, act = gate_tiles[gate_tiles_todo.pop(0)]
            dst[:, off:off + GATE_TILE] = act(_dot(xbuf[...], w_ref[:, src:src + GATE_TILE]))

    @pl.when((pl.program_id(0) == 0) & (pl.program_id(1) == 0))
    def _first_projection():
        xnbuf[...] = _bf(x_ref[...])
        project_head(0, xnbuf)

    xbuf[...] = xnbuf[...]

    def cumulate_forget(h0):
        lf_hi, lf_lo = _split(jnp.concatenate([hlf[h0], hlf[h0 + 1]], axis=1))
        bc = _dot(bdtril_ref[...], lf_hi) + _dot(bdtril_ref[...], lf_lo)
        hb[h0] = bc[:, 0:HEAD_DIM]
        hb[h0 + 1] = bc[:, HEAD_DIM:2 * HEAD_DIM]

    for c in range(1, HEADS):
        head_epilogue(c - 1)
        project_head(c)
        if c % 2 == 0:
            cumulate_forget(c - 2)
    head_epilogue(HEADS - 1)
    issue_gate_tiles(GATE_TILES_AFTER_PROJECTIONS)
    cumulate_forget(HEADS - 2)

    pab = _dot(xbuf[...], wab_ref[...])
    g_col = -jnp.exp(alogr_ref[...]) * _softplus(pab + dtbr_ref[...])
    lane = lax.broadcasted_iota(jnp.int32, pab.shape, 1)
    gbuf[...] = jnp.where(lane < HEADS, _sigmoid(pab), g_col)
    pabt = jnp.transpose(pab)[0:2 * HEADS, :]
    g_row = -jnp.exp(alogc_ref[...]) * _softplus(pabt + dtbc_ref[...])
    gr_hi, gr_lo = _split(g_row)
    gc_row = _dot(gr_hi, bdtriu_ref[...]) + _dot(gr_lo, bdtriu_ref[...])
    for c in range(n_chunks):
        gcrow[c] = gc_row[:, c * CHUNK:(c + 1) * CHUNK]

    ri = lax.broadcasted_iota(jnp.int32, (CHUNK, CHUNK), 0)
    ci = lax.broadcasted_iota(jnp.int32, (CHUNK, CHUNK), 1)
    causal = jnp.where(ri >= ci, 1.0, 0.0)
    strict = jnp.where(ri > ci, 1.0, 0.0)
    trow = lax.broadcasted_iota(jnp.int32, (CHUNK, HEAD_DIM), 0)

    n_lev = len(_LEVELS)
    heads = range(HEADS)
    hsl = [slice(h * HEAD_DIM, (h + 1) * HEAD_DIM) for h in heads]

    def level_mask(l):
        return masks_ref[l * CHUNK:(l + 1) * CHUNK, 0:CHUNK]

    def level_mask2(l):
        return masks_ref[l * CHUNK:(l + 1) * CHUNK, :]

    units = [(u, h) for u in range(PREP_UNROLL) for h in heads]
    n_units = range(len(units))

    def prepare_chunks(j):
        cs = [j * PREP_UNROLL + u for u in range(PREP_UNROLL)]
        rows = [slice(c * CHUNK, (c + 1) * CHUNK) for c in cs]
        wsum = wsum_ref[...]
        gb, eg, gcr = [], [], []
        for u in range(PREP_UNROLL):
            gb.append(gbuf[rows[u], :])
            g_hi, g_lo = _split(gb[u])
            eg.append(_dot(wsum, g_hi) + _dot(wsum, g_lo))
            gcr.append(gcrow[cs[u]])

        def ld(ref, i):
            u, h = units[i]
            return ref[h, rows[u], :]

        dkk = [ld(dk, i) for i in n_units]
        dkb = [_bf(k) for k in dkk]
        beta = [gb[u][:, h:h + 1] for u, h in units]
        gcol = [eg[u][0:CHUNK, HEADS + h:HEADS + h + 1] for u, h in units]
        decay = [causal * jnp.exp(jnp.minimum(gcol[i] - gcr[u][HEADS + h:HEADS + h + 1, :], 0.0))
                 for i, (u, h) in enumerate(units)]
        kbeta = [dkk[i] * beta[i] for i in n_units]
        lmat = [_dot_nt(_bf(kbeta[i]), dkb[i]) * (strict * decay[i]) for i in n_units]
        for i, (u, h) in enumerate(units):
            qkb[cs[u] * HEADS + h] = _bf(_dot_nt(_bf(ld(dq, i)), dkb[i]) * decay[i])

        hqv = [ld(hq, i) for i in n_units]
        hkv = [ld(hk, i) for i in n_units]
        hqb = [_bf(q) for q in hqv]
        hkb = [_bf(k) for k in hkv]
        sc = [level_mask(n_lev) * _dot_nt(hqb[i], hkb[i]) for i in n_units]
        pairs = [(i, i + 1) for i in range(0, len(units), 2)]
        lane2 = lax.broadcasted_iota(jnp.int32, (CHUNK, 2 * CHUNK), 1)
        first_f = jnp.where(lane2 < CHUNK, 1.0, 0.0)
        second_f = 1.0 - first_f

        def block_diag(m2, lo, hi):
            return jnp.concatenate([_bf(m2 * lo), _bf(m2 * hi)], axis=0)

        lmat2 = [jnp.concatenate([lmat[a], lmat[b]], axis=1) for a, b in pairs]
        lbd = [block_diag(m2, first_f, second_f) for m2 in lmat2]
        nmat2 = [-(level_mask2(n_lev - 1) * m2) for m2 in lmat2]
        first_b, second_b = _bf(first_f), _bf(second_f)

        hbv = [ld(hb, i) for i in n_units]

        def level_factor(i, l):
            m = _LEVELS[l]
            if m >= SUBLANES // 2:
                ref = jnp.concatenate(
                    [jnp.broadcast_to(hbv[i][r:r + 1, :], (SUBLANES, HEAD_DIM))
                     for r in [(g * SUBLANES // (2 * m)) * (2 * m) + m - 1 for g in range(CHUNK // SUBLANES)]],
                    axis=0)
                return jnp.exp2(jnp.abs(hbv[i] - ref) * NEG_LOG2E)
            f = ld(hf, i)
            if m == 1:
                return jnp.where(trow % 2 == 1, f, 1.0)
            f8 = f.reshape(CHUNK // SUBLANES, SUBLANES, HEAD_DIM)
            f_prev = pltpu.roll(f8, 1, 1).reshape(CHUNK, HEAD_DIM)
            f_next = pltpu.roll(f8, SUBLANES - 1, 1).reshape(CHUNK, HEAD_DIM)
            return jnp.where(trow % 4 == 0, f_next,
                             jnp.where(trow % 4 == 1, 1.0, jnp.where(trow % 4 == 2, f, f_prev * f)))

        def score_level(l):
            for i in n_units:
                zb = _bf(level_factor(i, l))
                sc[i] = sc[i] + level_mask(l) * _dot_nt(hqb[i] * zb, hkb[i] * zb)

        score_level(n_lev - 1)
        n_pairs = range(len(pairs))
        for l in range(n_lev - 2, -1, -1):
            nb = [_bf(nmat2[p]) for p in n_pairs]
            xm = [level_mask2(l) * (lmat2[p] + _dot(nb[p], lbd[p])) for p in n_pairs]
            score_level(l)
            for p in n_pairs:
                nbd = jnp.concatenate([nb[p] * first_b, nb[p] * second_b], axis=0)
                nmat2[p] = nmat2[p] - (xm[p] + _dot(_bf(xm[p]), nbd))

        hvb = [_bf(ld(hv, i)) for i in n_units]
        for i, (u, h) in enumerate(units):
            oa[rows[u], hsl[h]] = _dot(_bf(sc[i]), hvb[i])
        for i, (u, h) in enumerate(units):
            kd = hkv[i] * jnp.exp(hbv[i][CHUNK - 1:CHUNK, :] - hbv[i])
            kvinc[cs[u] * HEADS + h] = _dot_tn(hvb[i], _bf(kd))
        for i, (u, h) in enumerate(units):
            qin[rows[u], hsl[h]] = _bf(hqv[i] * jnp.exp(hbv[i]))
            hdec[cs[u], :, hsl[h]] = jnp.exp(hbv[i][CHUNK - 1:CHUNK, :])

        for i, (u, h) in enumerate(units):
            eg_col = jnp.exp(gcol[i])
            rhs = jnp.concatenate([ld(dv, i) * beta[i], kbeta[i] * eg_col], axis=1)
            zero_rhs = jnp.zeros(rhs.shape, jnp.bfloat16)
            rhs_pad = jnp.concatenate([_bf(rhs), zero_rhs] if i % 2 == 0 else [zero_rhs, _bf(rhs)], axis=0)
            uw = rhs + _dot(_bf(nmat2[i // 2]), rhs_pad)
            du[rows[u], hsl[h]] = uw[:, 0:HEAD_DIM]
            wq[cs[u], 0:CHUNK, hsl[h]] = _bf(uw[:, HEAD_DIM:2 * HEAD_DIM])
            wq[cs[u], CHUNK:2 * CHUNK, hsl[h]] = _bf(ld(dq, i) * eg_col)
            kdec[rows[u], hsl[h]] = _bf(dkk[i] * jnp.exp(eg[u][CHUNK:2 * CHUNK, HEADS + h:HEADS + h + 1]))
        for u in range(PREP_UNROLL):
            gl[cs[u]] = jnp.exp(eg[u][CHUNK - 1:CHUNK, :])

    for j in range(n_chunks // PREP_UNROLL):
        prepare_chunks(j)

    for c in range(n_chunks):
        rows = slice(c * CHUNK, (c + 1) * CHUNK)
        s = [sdn[h] for h in heads]
        wqc = wq[c]
        ws = [_dot(wqc[:, hsl[h]], _bf(s[h])) for h in heads]
        st = [shg[h] for h in heads]
        dec = hdec[c]
        for h in heads:
            oa[rows, hsl[h]] = oa[rows, hsl[h]] + _dot_nt(qin[rows, hsl[h]], _bf(st[h]))
            shg[h] = st[h] * dec[:, hsl[h]] + kvinc[c * HEADS + h]
        issue_gate_tiles(-(-len(gate_tiles_todo) // (2 * (n_chunks - c))))
        vnb = [_bf(du[rows, hsl[h]] - ws[h][0:CHUNK]) for h in heads]
        glc = gl[c]
        for h in heads:
            ob[rows, hsl[h]] = ws[h][CHUNK:2 * CHUNK] + _dot(qkb[c * HEADS + h], vnb[h])
        for h in heads:
            sdn[h] = s[h] * glc[:, HEADS + h:HEADS + h + 1] + _dot_tn(kdec[rows, hsl[h]], vnb[h])
        issue_gate_tiles(-(-len(gate_tiles_todo) // (2 * (n_chunks - c) - 1)))

    def gated_norm(o_ref, w_row, gate):
        parts = []
        for h in range(HEADS):
            oh = o_ref[:, h * HEAD_DIM:(h + 1) * HEAD_DIM]
            parts.append(oh * lax.rsqrt(jnp.mean(oh * oh, axis=-1, keepdims=True) + RMS_EPS))
        return jnp.concatenate(parts, axis=1) * w_row * gate

    na = gated_norm(oa, hgnw_ref[...], sg[...])
    nb_ = gated_norm(ob, dnnw_ref[...], sz[...])
    merged = sga[...] * _dot(_bf(na), wa_ref[...]) + sgb[...] * _dot(_bf(nb_), wb_ref[...])
    mix = _dot(_bf(merged), wo_ref[...])
    out_ref[...] = _layer_norm(ALPHA * x_ref[...] + mix, g1_ref[...], b1_ref[...])

    xnbuf[...] = _bf(xnext_ref[...])
    project_head(0, xnbuf)


def _mlp_kernel(h_ref, wup_ref, wdown_ref, g2_ref, b2_ref, out_ref):
    tm = h_ref.shape[0]
    hb = _bf(h_ref[...])
    n_ff = D_FF // FF_BLOCK
    assert n_ff >= 2 and tm % MLP_TAIL_SPLIT == 0
    acc = None
    for j in range(n_ff - 1):
        up = _dot(hb, wup_ref[:, j * FF_BLOCK:(j + 1) * FF_BLOCK])
        act = jnp.square(jnp.maximum(up, 0.0))
        part = _dot(_bf(act), wdown_ref[j * FF_BLOCK:(j + 1) * FF_BLOCK, :])
        acc = part if acc is None else acc + part
    up = _dot(hb, wup_ref[:, (n_ff - 1) * FF_BLOCK:n_ff * FF_BLOCK])
    actb = _bf(jnp.square(jnp.maximum(up, 0.0)))
    for r in range(MLP_TAIL_SPLIT):
        rows = slice(r * tm // MLP_TAIL_SPLIT, (r + 1) * tm // MLP_TAIL_SPLIT)
        part = _dot(actb[rows], wdown_ref[(n_ff - 1) * FF_BLOCK:n_ff * FF_BLOCK, :])
        out_ref[rows, :] = _layer_norm(ALPHA * h_ref[rows, :] + acc[rows] + part, g2_ref[...], b2_ref[...])


def _resident(shape):
    nd = len(shape)
    return pl.BlockSpec(shape, lambda *_: (0,) * nd, pipeline_mode=pl.Buffered(1))


def _mixer_call(x, wup, wdown, lbl, win, wgate, wab, convw, hgnw, alogr, dtbr, alogc, dtbc, dnnw,
                wa, wb, wo, g1, b1):
    bsz, seq, _ = x.shape
    tb = min(TOKEN_BLOCK, seq)
    assert seq % tb == 0 and tb % (CHUNK * PREP_UNROLL) == 0
    n_chunks = tb // CHUNK
    wsum = jnp.asarray(_WSUM, jnp.bfloat16)
    masks = jnp.asarray(np.concatenate([_MASKS, _MASKS], axis=1), jnp.float32)
    tt = np.arange(tb)
    bdtriu = jnp.asarray(((tt[:, None] <= tt[None, :]) & ((tt[:, None] // CHUNK) == (tt[None, :] // CHUNK))),
                         jnp.bfloat16)
    consts = (lbl, win, wgate, wab, convw, hgnw, alogr, dtbr, alogc, dtbc, dnnw, wa, wb, wo,
              g1, b1, wsum, masks, bdtriu, bdtriu.T)
    f32 = jnp.float32
    blk = functools.partial(pltpu.VMEM, (tb, KW))
    bf16 = jnp.bfloat16
    per_head = functools.partial(pltpu.VMEM, (HEADS, tb, HEAD_DIM))
    scratch = [pltpu.VMEM((tb, D_MODEL), bf16), pltpu.VMEM((tb, D_MODEL), bf16)]
    scratch += [per_head(f32) for _ in range(9)]
    scratch += [blk(f32) for _ in range(3)]
    scratch += [pltpu.VMEM((tb, LANES), f32),
                pltpu.VMEM((n_chunks, 8, CHUNK), f32),
                pltpu.VMEM((HEADS, tb + CONV_PAD, 3 * HEAD_DIM), f32),
                pltpu.VMEM((HEADS, tb, 3 * HEAD_DIM), f32),
                blk(bf16), blk(bf16),
                pltpu.VMEM((n_chunks, 2 * CHUNK, KW), bf16),
                pltpu.VMEM((n_chunks * HEADS, CHUNK, CHUNK), bf16),
                pltpu.VMEM((n_chunks * HEADS, HEAD_DIM, HEAD_DIM), f32),
                pltpu.VMEM((n_chunks, 1, KW), f32),
                pltpu.VMEM((n_chunks, 1, LANES), f32),
                blk(f32), blk(f32),
                pltpu.VMEM((tb, D_MODEL), f32),
                pltpu.VMEM((tb, D_MODEL), f32),
                pltpu.VMEM((HEADS, HEAD_DIM, HEAD_DIM), f32),
                pltpu.VMEM((HEADS, HEAD_DIM, HEAD_DIM), f32)]
    n_blocks = seq // tb

    def next_block(b, i):
        flat = jnp.minimum(b * n_blocks + i + 1, bsz * n_blocks - 1)
        return flat // n_blocks, flat % n_blocks, 0

    n_steps = bsz * n_blocks
    up_rows, down_rows = wup.shape[0] // n_steps, wdown.shape[0] // n_steps
    assert up_rows * n_steps == wup.shape[0] and down_rows * n_steps == wdown.shape[0]
    assert up_rows % BF16_ROWS == 0 and down_rows % BF16_ROWS == 0

    def step_rows(b, i):
        return b * n_blocks + i, 0

    return pl.pallas_call(
        _mixer_kernel,
        grid=(bsz, n_blocks),
        in_specs=[pl.BlockSpec((None, tb, D_MODEL), lambda b, i: (b, i, 0)),
                  pl.BlockSpec((None, tb, D_MODEL), next_block),
                  pl.BlockSpec((up_rows, wup.shape[1]), step_rows),
                  pl.BlockSpec((down_rows, wdown.shape[1]), step_rows)]
                 + [_resident(c.shape) for c in consts],
        out_specs=[pl.BlockSpec((None, tb, D_MODEL), lambda b, i: (b, i, 0)),
                   pl.BlockSpec((up_rows, wup.shape[1]), step_rows),
                   pl.BlockSpec((down_rows, wdown.shape[1]), step_rows)],
        out_shape=[jax.ShapeDtypeStruct(x.shape, f32),
                   jax.ShapeDtypeStruct(wup.shape, bf16),
                   jax.ShapeDtypeStruct(wdown.shape, bf16)],
        scratch_shapes=scratch,
        compiler_params=pltpu.CompilerParams(dimension_semantics=("arbitrary", "arbitrary"),
                                             vmem_limit_bytes=VMEM_LIMIT_BYTES),
        name="token_mixer",
    )(x, x, wup, wdown, *consts)


def _mlp_call(h, wup, wdown, g2, b2):
    m = h.shape[0]
    tm = min(MLP_BLOCK, m)
    assert m % tm == 0
    consts = (wup, wdown, g2, b2)
    return pl.pallas_call(
        _mlp_kernel,
        grid=(m // tm,),
        in_specs=[pl.BlockSpec((tm, D_MODEL), lambda i: (i, 0))] + [_resident(c.shape) for c in consts],
        out_specs=pl.BlockSpec((tm, D_MODEL), lambda i: (i, 0)),
        out_shape=jax.ShapeDtypeStruct(h.shape, jnp.float32),
        compiler_params=pltpu.CompilerParams(dimension_semantics=("arbitrary",),
                                             vmem_limit_bytes=VMEM_LIMIT_BYTES),
        name="relu2_mlp",
    )(h, *consts)


def kernel(x, hg_lb_logits, w_in, conv_w, hg_norm_w, dn_A_log, dn_dt_bias, dn_norm_w, w_branch_a, w_branch_b, w_o, ln1_g, ln1_b, w_up, w_down, ln2_g, ln2_b):
    bsz, seq, _ = x.shape
    f32 = jnp.float32
    bf16 = jnp.bfloat16
    o = _OFFS
    l = LAYER
    win = w_in[l].astype(bf16)
    wgate = win[:, o[8]:o[10]]
    wab = jnp.pad(win[:, o[6]:o[8]], ((0, 0), (0, LANES - 2 * HEADS)))
    zeros4 = jnp.zeros((HEADS,), f32)
    alog8 = jnp.concatenate([zeros4, dn_A_log[l].astype(f32)])
    dtb8 = jnp.concatenate([zeros4, dn_dt_bias[l].astype(f32)])
    alogr = jnp.pad(alog8, (0, LANES - 2 * HEADS)).reshape(1, LANES)
    dtbr = jnp.pad(dtb8, (0, LANES - 2 * HEADS)).reshape(1, LANES)
    h1, wup_bf, wdown_bf = _mixer_call(
        x, w_up[l], w_down[l], hg_lb_logits.astype(f32), win, wgate, wab, conv_w[l].astype(f32),
        hg_norm_w[l].reshape(1, KW), alogr, dtbr, alog8.reshape(8, 1), dtb8.reshape(8, 1),
        jnp.tile(dn_norm_w[l], HEADS).reshape(1, KW),
        w_branch_a[l].astype(bf16), w_branch_b[l].astype(bf16), w_o[l].astype(bf16),
        ln1_g[l].reshape(1, D_MODEL), ln1_b[l].reshape(1, D_MODEL))
    out = _mlp_call(h1.reshape(bsz * seq, D_MODEL), wup_bf, wdown_bf,
                    ln2_g[l].reshape(1, D_MODEL), ln2_b[l].reshape(1, D_MODEL))
    return out.reshape(bsz, seq, D_MODEL)
```

```python
import functools

import numpy as np
import jax
import jax.numpy as jnp
from jax import lax
from jax.experimental import pallas as pl
from jax.experimental.pallas import tpu as pltpu

D_MODEL = 1024
HEADS = 4
HEAD_DIM = 128
KW = HEADS * HEAD_DIM
CHUNK = 64
CONV_W = 4
D_FF = 4 * D_MODEL
DEPTH = 1
LAYER = 0
ALPHA = (2 * DEPTH) ** 0.25
LN_EPS = 1e-5
RMS_EPS = 1e-6
L2_EPS = 1e-6
QK_SCALE = HEAD_DIM ** -0.5
NEG_LOG2E = -1.4426950408889634

TOKEN_BLOCK = 256
PREP_UNROLL = 4
GATE_TILE = 256
GATE_TILES_AFTER_PROJECTIONS = 4
MLP_BLOCK = 1024
MLP_TAIL_SPLIT = 4
FF_BLOCK = 1024
LANES = 128
SUBLANES = 8
BF16_ROWS = 16
CONV_PAD = 8
VMEM_LIMIT_BYTES = 56 * 1024 * 1024

_SIZES = (KW, KW, KW, KW, 3 * KW, KW, HEADS, HEADS, D_MODEL, D_MODEL)
_OFFS = np.concatenate([[0], np.cumsum(_SIZES)]).tolist()

COL_HG_QFI = _OFFS[0]
COL_HG_G = _OFFS[3]
COL_DN_QKV = _OFFS[4]
COL_DN_Z = _OFFS[5]

_LEVELS = (32, 16, 8, 4, 2, 1)


def _build_constants():
    t = np.arange(CHUNK)[:, None]
    r = np.arange(CHUNK)[None, :]
    wsum = np.concatenate([r <= t, r > t], 0).astype(np.float32)
    masks = [((t // (2 * m)) == (r // (2 * m))) & ((t % (2 * m)) >= m) & ((r % (2 * m)) < m) for m in _LEVELS]
    masks.append(t == r)
    masks = np.concatenate(masks, 0).astype(np.float32)
    return wsum, masks


_WSUM, _MASKS = _build_constants()


def _dot(a, b):
    return jnp.dot(a, b, preferred_element_type=jnp.float32)


def _dot_nt(a, b):
    return lax.dot_general(a, b, (((1,), (1,)), ((), ())), preferred_element_type=jnp.float32)


def _dot_tn(a, b):
    return lax.dot_general(a, b, (((0,), (0,)), ((), ())), preferred_element_type=jnp.float32)


def _bf(x):
    return x.astype(jnp.bfloat16)


def _split(x):
    hi = _bf(x)
    return hi, _bf(x - hi.astype(jnp.float32))


def _sigmoid(x):
    return 1.0 / (1.0 + jnp.exp2(x * NEG_LOG2E))


def _silu(x):
    return x * _sigmoid(x)


def _softplus(x):
    return jnp.maximum(x, 0.0) + jnp.log1p(jnp.exp(-jnp.abs(x)))


def _layer_norm(x, g, b):
    mu = jnp.mean(x, axis=-1, keepdims=True)
    xc = x - mu
    var = jnp.mean(xc * xc, axis=-1, keepdims=True)
    return xc * lax.rsqrt(var + LN_EPS) * g + b


def _mixer_kernel(x_ref, xnext_ref, wup32_ref, wdown32_ref, lbl_ref, win_ref, wgate_ref, wab_ref, convw_ref, hgnw_ref,
                  alogr_ref, dtbr_ref, alogc_ref, dtbc_ref, dnnw_ref, wa_ref, wb_ref, wo_ref,
                  g1_ref, b1_ref, wsum_ref, masks_ref, bdtriu_ref, bdtril_ref, out_ref, wupb_ref, wdownb_ref,
                  xbuf, xnbuf, hq, hk, hv, hlf, hf, hb, dq, dk, dv, oa, ob, du, gbuf, gcrow, convbuf, rawhg, qin, kdec, wq, qkb,
                  kvinc, hdec, gl, sg, sz, sga, sgb, shg, sdn):
    tb = x_ref.shape[0]
    n_chunks = tb // CHUNK

    wupb_ref[...] = _bf(wup32_ref[...])
    wdownb_ref[...] = _bf(wdown32_ref[...])

    @pl.when(pl.program_id(1) == 0)
    def _reset_carries():
        shg[...] = jnp.zeros_like(shg)
        sdn[...] = jnp.zeros_like(sdn)
        convbuf[:, 0:CONV_PAD, :] = jnp.zeros((HEADS, CONV_PAD, 3 * HEAD_DIM), jnp.float32)

    def project_head(c, xsrc=xbuf):
        cols = ([COL_DN_QKV + k * KW + c * HEAD_DIM for k in range(3)]
                + [COL_HG_QFI + k * KW + c * HEAD_DIM for k in range(3)])
        w_head = jnp.concatenate([win_ref[:, o:o + HEAD_DIM] for o in cols], axis=1)
        raw = _dot(xsrc[...], w_head)
        convbuf[c, CONV_PAD:CONV_PAD + tb, :] = raw[:, 0:3 * HEAD_DIM]
        rawhg[c] = raw[:, 3 * HEAD_DIM:6 * HEAD_DIM]

    def head_epilogue(c):
        cw = jnp.concatenate([convw_ref[:, k * KW + c * HEAD_DIM:k * KW + (c + 1) * HEAD_DIM]
                              for k in range(3)], axis=1)
        base = CONV_PAD - (CONV_W - 1)
        acc = convbuf[c, base:base + tb, :] * cw[0:1, :]
        for j in range(1, CONV_W):
            acc = acc + convbuf[c, base + j:base + j + tb, :] * cw[j:j + 1, :]
        convbuf[c, 0:CONV_PAD, :] = convbuf[c, tb:tb + CONV_PAD, :]
        qkv = _silu(acc)
        qh = qkv[:, 0:HEAD_DIM]
        kh = qkv[:, HEAD_DIM:2 * HEAD_DIM]
        dq[c] = qh * (lax.rsqrt(jnp.sum(qh * qh, axis=-1, keepdims=True) + L2_EPS) * QK_SCALE)
        dk[c] = kh * lax.rsqrt(jnp.sum(kh * kh, axis=-1, keepdims=True) + L2_EPS)
        dv[c] = qkv[:, 2 * HEAD_DIM:3 * HEAD_DIM]
        lg = lbl_ref[:, c * HEAD_DIM:(c + 1) * HEAD_DIM]
        e = jnp.exp(lg - jnp.max(lg, axis=0, keepdims=True))
        lb = e[0:1, :] / jnp.sum(e, axis=0, keepdims=True)
        f = lb + (1.0 - lb) * _sigmoid(rawhg[c, :, HEAD_DIM:2 * HEAD_DIM])
        hlf[c] = jnp.log(f)
        hf[c] = f
        hk[c] = 1.0 - f
        hq[c] = _silu(rawhg[c, :, 0:HEAD_DIM]) * QK_SCALE
        hv[c] = rawhg[c, :, 2 * HEAD_DIM:3 * HEAD_DIM]

    gate_tiles = ([(win_ref, COL_HG_G + t, sg, t, _silu) for t in range(0, KW, GATE_TILE)]
                  + [(win_ref, COL_DN_Z + t, sz, t, _silu) for t in range(0, KW, GATE_TILE)]
                  + [(wgate_ref, t, sga, t, _sigmoid) for t in range(0, D_MODEL, GATE_TILE)]
                  + [(wgate_ref, D_MODEL + t, sgb, t, _sigmoid) for t in range(0, D_MODEL, GATE_TILE)])
    gate_tiles_todo = list(range(len(gate_tiles)))

    def issue_gate_tiles(count):
        for _ in range(min(count, len(gate_tiles_todo))):
            w_ref, src, dst, off, act = gate_tiles[gate_tiles_todo.pop(0)]
            dst[:, off:off + GATE_TILE] = act(_dot(xbuf[...], w_ref[:, src:src + GATE_TILE]))

    @pl.when((pl.program_id(0) == 0) & (pl.program_id(1) == 0))
    def _first_projection():
        xnbuf[...] = _bf(x_ref[...])
        project_head(0, xnbuf)

    xbuf[...] = xnbuf[...]

    def cumulate_forget(h0):
        lf_hi, lf_lo = _split(jnp.concatenate([hlf[h0], hlf[h0 + 1]], axis=1))
        bc = _dot(bdtril_ref[...], lf_hi) + _dot(bdtril_ref[...], lf_lo)
        hb[h0] = bc[:, 0:HEAD_DIM]
        hb[h0 + 1] = bc[:, HEAD_DIM:2 * HEAD_DIM]

    for c in range(1, HEADS):
        head_epilogue(c - 1)
        project_head(c)
        if c % 2 == 0:
            cumulate_forget(c - 2)
    head_epilogue(HEADS - 1)
    issue_gate_tiles(GATE_TILES_AFTER_PROJECTIONS)
    cumulate_forget(HEADS - 2)

    pab = _dot(xbuf[...], wab_ref[...])
    g_col = -jnp.exp(alogr_ref[...]) * _softplus(pab + dtbr_ref[...])
    lane = lax.broadcasted_iota(jnp.int32, pab.shape, 1)
    gbuf[...] = jnp.where(lane < HEADS, _sigmoid(pab), g_col)
    pabt = jnp.transpose(pab)[0:2 * HEADS, :]
    g_row = -jnp.exp(alogc_ref[...]) * _softplus(pabt + dtbc_ref[...])
    gr_hi, gr_lo = _split(g_row)
    gc_row = _dot(gr_hi, bdtriu_ref[...]) + _dot(gr_lo, bdtriu_ref[...])
    for c in range(n_chunks):
        gcrow[c] = gc_row[:, c * CHUNK:(c + 1) * CHUNK]

    ri = lax.broadcasted_iota(jnp.int32, (CHUNK, CHUNK), 0)
    ci = lax.broadcasted_iota(jnp.int32, (CHUNK, CHUNK), 1)
    causal = jnp.where(ri >= ci, 1.0, 0.0)
    strict = jnp.where(ri > ci, 1.0, 0.0)
    trow = lax.broadcasted_iota(jnp.int32, (CHUNK, HEAD_DIM), 0)

    n_lev = len(_LEVELS)
    heads = range(HEADS)
    hsl = [slice(h * HEAD_DIM, (h + 1) * HEAD_DIM) for h in heads]

    def level_mask(l):
        return masks_ref[l * CHUNK:(l + 1) * CHUNK, 0:CHUNK]

    def level_mask2(l):
        return masks_ref[l * CHUNK:(l + 1) * CHUNK, :]

    units = [(u, h) for u in range(PREP_UNROLL) for h in heads]
    n_units = range(len(units))

    def prepare_chunks(j):
        cs = [j * PREP_UNROLL + u for u in range(PREP_UNROLL)]
        rows = [slice(c * CHUNK, (c + 1) * CHUNK) for c in cs]
        wsum = wsum_ref[...]
        gb, eg, gcr = [], [], []
        for u in range(PREP_UNROLL):
            gb.append(gbuf[rows[u], :])
            g_hi, g_lo = _split(gb[u])
            eg.append(_dot(wsum, g_hi) + _dot(wsum, g_lo))
            gcr.append(gcrow[cs[u]])

        def ld(ref, i):
            u, h = units[i]
            return ref[h, rows[u], :]

        dkk = [ld(dk, i) for i in n_units]
        dkb = [_bf(k) for k in dkk]
        beta = [gb[u][:, h:h + 1] for u, h in units]
        gcol = [eg[u][0:CHUNK, HEADS + h:HEADS + h + 1] for u, h in units]
        decay = [causal * jnp.exp(jnp.minimum(gcol[i] - gcr[u][HEADS + h:HEADS + h + 1, :], 0.0))
                 for i, (u, h) in enumerate(units)]
        kbeta = [dkk[i] * beta[i] for i in n_units]
        lmat = [_dot_nt(_bf(kbeta[i]), dkb[i]) * (strict * decay[i]) for i in n_units]
        qk = [_dot_nt(_bf(ld(dq, i)), dkb[i]) * decay[i] for i in n_units]
        for i, (u, h) in enumerate(units):
            if h % 2 == 0:
                qkb[cs[u] * (HEADS // 2) + h // 2] = _bf(jnp.concatenate([qk[i], qk[i + 1]], axis=1))

        hqv = [ld(hq, i) for i in n_units]
        hkv = [ld(hk, i) for i in n_units]
        hqb = [_bf(q) for q in hqv]
        hkb = [_bf(k) for k in hkv]
        sc = [level_mask(n_lev) * _dot_nt(hqb[i], hkb[i]) for i in n_units]
        pairs = [(i, i + 1) for i in range(0, len(units), 2)]
        lane2 = lax.broadcasted_iota(jnp.int32, (CHUNK, 2 * CHUNK), 1)
        first_f = jnp.where(lane2 < CHUNK, 1.0, 0.0)
        second_f = 1.0 - first_f

        def block_diag(m2, lo, hi):
            return jnp.concatenate([_bf(m2 * lo), _bf(m2 * hi)], axis=0)

        lmat2 = [jnp.concatenate([lmat[a], lmat[b]], axis=1) for a, b in pairs]
        lbd = [block_diag(m2, first_f, second_f) for m2 in lmat2]
        nmat2 = [-(level_mask2(n_lev - 1) * m2) for m2 in lmat2]
        first_b, second_b = _bf(first_f), _bf(second_f)

        hbv = [ld(hb, i) for i in n_units]

        def level_factor(i, l):
            m = _LEVELS[l]
            if m >= SUBLANES // 2:
                ref = jnp.concatenate(
                    [jnp.broadcast_to(hbv[i][r:r + 1, :], (SUBLANES, HEAD_DIM))
                     for r in [(g * SUBLANES // (2 * m)) * (2 * m) + m - 1 for g in range(CHUNK // SUBLANES)]],
                    axis=0)
                return jnp.exp2(jnp.abs(hbv[i] - ref) * NEG_LOG2E)
            f = ld(hf, i)
            if m == 1:
                return jnp.where(trow % 2 == 1, f, 1.0)
            f8 = f.reshape(CHUNK // SUBLANES, SUBLANES, HEAD_DIM)
            f_prev = pltpu.roll(f8, 1, 1).reshape(CHUNK, HEAD_DIM)
            f_next = pltpu.roll(f8, SUBLANES - 1, 1).reshape(CHUNK, HEAD_DIM)
            return jnp.where(trow % 4 == 0, f_next,
                             jnp.where(trow % 4 == 1, 1.0, jnp.where(trow % 4 == 2, f, f_prev * f)))

        def score_level(l):
            for i in n_units:
                zb = _bf(level_factor(i, l))
                sc[i] = sc[i] + level_mask(l) * _dot_nt(hqb[i] * zb, hkb[i] * zb)

        score_level(n_lev - 1)
        n_pairs = range(len(pairs))
        for l in range(n_lev - 2, -1, -1):
            nb = [_bf(nmat2[p]) for p in n_pairs]
            xm = [level_mask2(l) * (lmat2[p] + _dot(nb[p], lbd[p])) for p in n_pairs]
            score_level(l)
            for p in n_pairs:
                nbd = jnp.concatenate([nb[p] * first_b, nb[p] * second_b], axis=0)
                nmat2[p] = nmat2[p] - (xm[p] + _dot(_bf(xm[p]), nbd))

        hvb = [_bf(ld(hv, i)) for i in n_units]
        for i, (u, h) in enumerate(units):
            oa[rows[u], hsl[h]] = _dot(_bf(sc[i]), hvb[i])
        for i, (u, h) in enumerate(units):
            kd = hkv[i] * jnp.exp(hbv[i][CHUNK - 1:CHUNK, :] - hbv[i])
            kvinc[cs[u] * HEADS + h] = _dot_tn(hvb[i], _bf(kd))
        for i, (u, h) in enumerate(units):
            qin[rows[u], hsl[h]] = _bf(hqv[i] * jnp.exp(hbv[i]))
            hdec[cs[u], :, hsl[h]] = jnp.exp(hbv[i][CHUNK - 1:CHUNK, :])

        for i, (u, h) in enumerate(units):
            eg_col = jnp.exp(gcol[i])
            rhs = jnp.concatenate([ld(dv, i) * beta[i], kbeta[i] * eg_col], axis=1)
            zero_rhs = jnp.zeros(rhs.shape, jnp.bfloat16)
            rhs_pad = jnp.concatenate([_bf(rhs), zero_rhs] if i % 2 == 0 else [zero_rhs, _bf(rhs)], axis=0)
            uw = rhs + _dot(_bf(nmat2[i // 2]), rhs_pad)
            du[rows[u], hsl[h]] = uw[:, 0:HEAD_DIM]
            wq[cs[u], 0:CHUNK, hsl[h]] = _bf(uw[:, HEAD_DIM:2 * HEAD_DIM])
            wq[cs[u], CHUNK:2 * CHUNK, hsl[h]] = _bf(ld(dq, i) * eg_col)
            kdec[rows[u], hsl[h]] = _bf(dkk[i] * jnp.exp(eg[u][CHUNK:2 * CHUNK, HEADS + h:HEADS + h + 1]))
        for u in range(PREP_UNROLL):
            gl[cs[u]] = jnp.exp(eg[u][CHUNK - 1:CHUNK, :])

    for j in range(n_chunks // PREP_UNROLL):
        prepare_chunks(j)

    for c in range(n_chunks):
        rows = slice(c * CHUNK, (c + 1) * CHUNK)
        s = [sdn[h] for h in heads]
        wqc = wq[c]
        zero_s = jnp.zeros((HEAD_DIM, HEAD_DIM), jnp.bfloat16)
        ws = []
        for h in range(0, HEADS, 2):
            s_bd = jnp.concatenate([jnp.concatenate([_bf(s[h]), zero_s], axis=1),
                                    jnp.concatenate([zero_s, _bf(s[h + 1])], axis=1)], axis=0)
            ws2 = _dot(wqc[:, h * HEAD_DIM:(h + 2) * HEAD_DIM], s_bd)
            ws += [ws2[:, 0:HEAD_DIM], ws2[:, HEAD_DIM:2 * HEAD_DIM]]
        st = [shg[h] for h in heads]
        dec = hdec[c]
        for h in heads:
            oa[rows, hsl[h]] = oa[rows, hsl[h]] + _dot_nt(qin[rows, hsl[h]], _bf(st[h]))
            shg[h] = st[h] * dec[:, hsl[h]] + kvinc[c * HEADS + h]
        issue_gate_tiles(-(-len(gate_tiles_todo) // (2 * (n_chunks - c))))
        vnb = [_bf(du[rows, hsl[h]] - ws[h][0:CHUNK]) for h in heads]
        glc = gl[c]
        zero_v = jnp.zeros((CHUNK, HEAD_DIM), jnp.bfloat16)
        for h in range(0, HEADS, 2):
            v_bd = jnp.concatenate([jnp.concatenate([vnb[h], zero_v], axis=1),
                                    jnp.concatenate([zero_v, vnb[h + 1]], axis=1)], axis=0)
            intra = _dot(qkb[c * (HEADS // 2) + h // 2], v_bd)
            ob[rows, hsl[h]] = ws[h][CHUNK:2 * CHUNK] + intra[:, 0:HEAD_DIM]
            ob[rows, hsl[h + 1]] = ws[h + 1][CHUNK:2 * CHUNK] + intra[:, HEAD_DIM:2 * HEAD_DIM]
        for h in heads:
            sdn[h] = s[h] * glc[:, HEADS + h:HEADS + h + 1] + _dot_tn(kdec[rows, hsl[h]], vnb[h])
        issue_gate_tiles(-(-len(gate_tiles_todo) // (2 * (n_chunks - c) - 1)))

    def gated_norm(o_ref, w_row, gate):
        parts = []
        for h in range(HEADS):
            oh = o_ref[:, h * HEAD_DIM:(h + 1) * HEAD_DIM]
            parts.append(oh * lax.rsqrt(jnp.mean(oh * oh, axis=-1, keepdims=True) + RMS_EPS))
        return jnp.concatenate(parts, axis=1) * w_row * gate

    na = gated_norm(oa, hgnw_ref[...], sg[...])
    nb_ = gated_norm(ob, dnnw_ref[...], sz[...])
    merged = sga[...] * _dot(_bf(na), wa_ref[...]) + sgb[...] * _dot(_bf(nb_), wb_ref[...])
    mix = _dot(_bf(merged), wo_ref[...])
    out_ref[...] = _layer_norm(ALPHA * x_ref[...] + mix, g1_ref[...], b1_ref[...])

    xnbuf[...] = _bf(xnext_ref[...])
    project_head(0, xnbuf)


def _mlp_kernel(h_ref, wup_ref, wdown_ref, g2_ref, b2_ref, out_ref):
    tm = h_ref.shape[0]
    hb = _bf(h_ref[...])
    n_ff = D_FF // FF_BLOCK
    assert n_ff >= 2 and tm % MLP_TAIL_SPLIT == 0
    acc = None
    for j in range(n_ff - 1):
        up = _dot(hb, wup_ref[:, j * FF_BLOCK:(j + 1) * FF_BLOCK])
        act = jnp.square(jnp.maximum(up, 0.0))
        part = _dot(_bf(act), wdown_ref[j * FF_BLOCK:(j + 1) * FF_BLOCK, :])
        acc = part if acc is None else acc + part
    up = _dot(hb, wup_ref[:, (n_ff - 1) * FF_BLOCK:n_ff * FF_BLOCK])
    actb = _bf(jnp.square(jnp.maximum(up, 0.0)))
    for r in range(MLP_TAIL_SPLIT):
        rows = slice(r * tm // MLP_TAIL_SPLIT, (r + 1) * tm // MLP_TAIL_SPLIT)
        part = _dot(actb[rows], wdown_ref[(n_ff - 1) * FF_BLOCK:n_ff * FF_BLOCK, :])
        out_ref[rows, :] = _layer_norm(ALPHA * h_ref[rows, :] + acc[rows] + part, g2_ref[...], b2_ref[...])


def _resident(shape):
    nd = len(shape)
    return pl.BlockSpec(shape, lambda *_: (0,) * nd, pipeline_mode=pl.Buffered(1))


def _mixer_call(x, wup, wdown, lbl, win, wgate, wab, convw, hgnw, alogr, dtbr, alogc, dtbc, dnnw,
                wa, wb, wo, g1, b1):
    bsz, seq, _ = x.shape
    tb = min(TOKEN_BLOCK, seq)
    assert seq % tb == 0 and tb % (CHUNK * PREP_UNROLL) == 0
    n_chunks = tb // CHUNK
    wsum = jnp.asarray(_WSUM, jnp.bfloat16)
    masks = jnp.asarray(np.concatenate([_MASKS, _MASKS], axis=1), jnp.float32)
    tt = np.arange(tb)
    bdtriu = jnp.asarray(((tt[:, None] <= tt[None, :]) & ((tt[:, None] // CHUNK) == (tt[None, :] // CHUNK))),
                         jnp.bfloat16)
    consts = (lbl, win, wgate, wab, convw, hgnw, alogr, dtbr, alogc, dtbc, dnnw, wa, wb, wo,
              g1, b1, wsum, masks, bdtriu, bdtriu.T)
    f32 = jnp.float32
    blk = functools.partial(pltpu.VMEM, (tb, KW))
    bf16 = jnp.bfloat16
    per_head = functools.partial(pltpu.VMEM, (HEADS, tb, HEAD_DIM))
    scratch = [pltpu.VMEM((tb, D_MODEL), bf16), pltpu.VMEM((tb, D_MODEL), bf16)]
    scratch += [per_head(f32) for _ in range(9)]
    scratch += [blk(f32) for _ in range(3)]
    scratch += [pltpu.VMEM((tb, LANES), f32),
                pltpu.VMEM((n_chunks, 8, CHUNK), f32),
                pltpu.VMEM((HEADS, tb + CONV_PAD, 3 * HEAD_DIM), f32),
                pltpu.VMEM((HEADS, tb, 3 * HEAD_DIM), f32),
                blk(bf16), blk(bf16),
                pltpu.VMEM((n_chunks, 2 * CHUNK, KW), bf16),
                pltpu.VMEM((n_chunks * HEADS // 2, CHUNK, 2 * CHUNK), bf16),
                pltpu.VMEM((n_chunks * HEADS, HEAD_DIM, HEAD_DIM), f32),
                pltpu.VMEM((n_chunks, 1, KW), f32),
                pltpu.VMEM((n_chunks, 1, LANES), f32),
                blk(f32), blk(f32),
                pltpu.VMEM((tb, D_MODEL), f32),
                pltpu.VMEM((tb, D_MODEL), f32),
                pltpu.VMEM((HEADS, HEAD_DIM, HEAD_DIM), f32),
                pltpu.VMEM((HEADS, HEAD_DIM, HEAD_DIM), f32)]
    n_blocks = seq // tb

    def next_block(b, i):
        flat = jnp.minimum(b * n_blocks + i + 1, bsz * n_blocks - 1)
        return flat // n_blocks, flat % n_blocks, 0

    n_steps = bsz * n_blocks
    up_rows, down_rows = wup.shape[0] // n_steps, wdown.shape[0] // n_steps
    assert up_rows * n_steps == wup.shape[0] and down_rows * n_steps == wdown.shape[0]
    assert up_rows % BF16_ROWS == 0 and down_rows % BF16_ROWS == 0

    def step_rows(b, i):
        return b * n_blocks + i, 0

    return pl.pallas_call(
        _mixer_kernel,
        grid=(bsz, n_blocks),
        in_specs=[pl.BlockSpec((None, tb, D_MODEL), lambda b, i: (b, i, 0)),
                  pl.BlockSpec((None, tb, D_MODEL), next_block),
                  pl.BlockSpec((up_rows, wup.shape[1]), step_rows),
                  pl.BlockSpec((down_rows, wdown.shape[1]), step_rows)]
                 + [_resident(c.shape) for c in consts],
        out_specs=[pl.BlockSpec((None, tb, D_MODEL), lambda b, i: (b, i, 0)),
                   pl.BlockSpec((up_rows, wup.shape[1]), step_rows),
                   pl.BlockSpec((down_rows, wdown.shape[1]), step_rows)],
        out_shape=[jax.ShapeDtypeStruct(x.shape, f32),
                   jax.ShapeDtypeStruct(wup.shape, bf16),
                   jax.ShapeDtypeStruct(wdown.shape, bf16)],
        scratch_shapes=scratch,
        compiler_params=pltpu.CompilerParams(dimension_semantics=("arbitrary", "arbitrary"),
                                             vmem_limit_bytes=VMEM_LIMIT_BYTES),
        name="token_mixer",
    )(x, x, wup, wdown, *consts)


def _mlp_call(h, wup, wdown, g2, b2):
    m = h.shape[0]
    tm = min(MLP_BLOCK, m)
    assert m % tm == 0
    consts = (wup, wdown, g2, b2)
    return pl.pallas_call(
        _mlp_kernel,
        grid=(m // tm,),
        in_specs=[pl.BlockSpec((tm, D_MODEL), lambda i: (i, 0))] + [_resident(c.shape) for c in consts],
        out_specs=pl.BlockSpec((tm, D_MODEL), lambda i: (i, 0)),
        out_shape=jax.ShapeDtypeStruct(h.shape, jnp.float32),
        compiler_params=pltpu.CompilerParams(dimension_semantics=("arbitrary",),
                                             vmem_limit_bytes=VMEM_LIMIT_BYTES),
        name="relu2_mlp",
    )(h, *consts)


def kernel(x, hg_lb_logits, w_in, conv_w, hg_norm_w, dn_A_log, dn_dt_bias, dn_norm_w, w_branch_a, w_branch_b, w_o, ln1_g, ln1_b, w_up, w_down, ln2_g, ln2_b):
    bsz, seq, _ = x.shape
    f32 = jnp.float32
    bf16 = jnp.bfloat16
    o = _OFFS
    l = LAYER
    win = w_in[l].astype(bf16)
    wgate = win[:, o[8]:o[10]]
    wab = jnp.pad(win[:, o[6]:o[8]], ((0, 0), (0, LANES - 2 * HEADS)))
    zeros4 = jnp.zeros((HEADS,), f32)
    alog8 = jnp.concatenate([zeros4, dn_A_log[l].astype(f32)])
    dtb8 = jnp.concatenate([zeros4, dn_dt_bias[l].astype(f32)])
    alogr = jnp.pad(alog8, (0, LANES - 2 * HEADS)).reshape(1, LANES)
    dtbr = jnp.pad(dtb8, (0, LANES - 2 * HEADS)).reshape(1, LANES)
    h1, wup_bf, wdown_bf = _mixer_call(
        x, w_up[l], w_down[l], hg_lb_logits.astype(f32), win, wgate, wab, conv_w[l].astype(f32),
        hg_norm_w[l].reshape(1, KW), alogr, dtbr, alog8.reshape(8, 1), dtb8.reshape(8, 1),
        jnp.tile(dn_norm_w[l], HEADS).reshape(1, KW),
        w_branch_a[l].astype(bf16), w_branch_b[l].astype(bf16), w_o[l].astype(bf16),
        ln1_g[l].reshape(1, D_MODEL), ln1_b[l].reshape(1, D_MODEL))
    out = _mlp_call(h1.reshape(bsz * seq, D_MODEL), wup_bf, wdown_bf,
                    ln2_g[l].reshape(1, D_MODEL), ln2_b[l].reshape(1, D_MODEL))
    return out.reshape(bsz, seq, D_MODEL)
```

```python
import functools

import numpy as np
import jax
import jax.numpy as jnp
from jax import lax
from jax.experimental import pallas as pl
from jax.experimental.pallas import tpu as pltpu

D_MODEL = 1024
HEADS = 4
HEAD_DIM = 128
KW = HEADS * HEAD_DIM
CHUNK = 64
CONV_W = 4
D_FF = 4 * D_MODEL
DEPTH = 1
LAYER = 0
ALPHA = (2 * DEPTH) ** 0.25
LN_EPS = 1e-5
RMS_EPS = 1e-6
L2_EPS = 1e-6
QK_SCALE = HEAD_DIM ** -0.5
NEG_LOG2E = -1.4426950408889634

TOKEN_BLOCK = 256
PREP_UNROLL = 4
GATE_TILE = 256
GATE_TILES_AFTER_PROJECTIONS = 4
MLP_BLOCK = 1024
MLP_TAIL_SPLIT = 4
FF_BLOCK = 1024
LANES = 128
SUBLANES = 8
BF16_ROWS = 16
CONV_PAD = 8
VMEM_LIMIT_BYTES = 56 * 1024 * 1024

_SIZES = (KW, KW, KW, KW, 3 * KW, KW, HEADS, HEADS, D_MODEL, D_MODEL)
_OFFS = np.concatenate([[0], np.cumsum(_SIZES)]).tolist()

COL_HG_QFI = _OFFS[0]
COL_HG_G = _OFFS[3]
COL_DN_QKV = _OFFS[4]
COL_DN_Z = _OFFS[5]

_LEVELS = (32, 16, 8, 4, 2, 1)


def _build_constants():
    t = np.arange(CHUNK)[:, None]
    r = np.arange(CHUNK)[None, :]
    wsum = np.concatenate([r <= t, r > t], 0).astype(np.float32)
    masks = [((t // (2 * m)) == (r // (2 * m))) & ((t % (2 * m)) >= m) & ((r % (2 * m)) < m) for m in _LEVELS]
    masks.append(t == r)
    masks = np.concatenate(masks, 0).astype(np.float32)
    return wsum, masks


_WSUM, _MASKS = _build_constants()


def _dot(a, b):
    return jnp.dot(a, b, preferred_element_type=jnp.float32)


def _dot_nt(a, b):
    return lax.dot_general(a, b, (((1,), (1,)), ((), ())), preferred_element_type=jnp.float32)


def _dot_tn(a, b):
    return lax.dot_general(a, b, (((0,), (0,)), ((), ())), preferred_element_type=jnp.float32)


def _bf(x):
    return x.astype(jnp.bfloat16)


def _split(x):
    hi = _bf(x)
    return hi, _bf(x - hi.astype(jnp.float32))


def _sigmoid(x):
    return 1.0 / (1.0 + jnp.exp2(x * NEG_LOG2E))


def _silu(x):
    return x * _sigmoid(x)


def _softplus(x):
    return jnp.maximum(x, 0.0) + jnp.log1p(jnp.exp(-jnp.abs(x)))


def _layer_norm(x, g, b):
    mu = jnp.mean(x, axis=-1, keepdims=True)
    xc = x - mu
    var = jnp.mean(xc * xc, axis=-1, keepdims=True)
    return xc * lax.rsqrt(var + LN_EPS) * g + b


def _mixer_kernel(x_ref, xnext_ref, wup32_ref, wdown32_ref, lbl_ref, win_ref, wgate_ref, wab_ref, convw_ref, hgnw_ref,
                  alogr_ref, dtbr_ref, alogc_ref, dtbc_ref, dnnw_ref, wa_ref, wb_ref, wo_ref,
                  g1_ref, b1_ref, wsum_ref, masks_ref, bdtriu_ref, bdtril_ref, out_ref, wupb_ref, wdownb_ref,
                  xbuf, xnbuf, hq, hk, hv, hlf, hf, hb, dq, dk, dv, oa, ob, du, gbuf, gcrow, convbuf, rawhg, qin, kdec, wq, qkb,
                  kvinc, hdec, gl, sg, sz, sga, sgb, shg, sdn):
    tb = x_ref.shape[0]
    n_chunks = tb // CHUNK

    wupb_ref[...] = _bf(wup32_ref[...])
    wdownb_ref[...] = _bf(wdown32_ref[...])

    @pl.when(pl.program_id(1) == 0)
    def _reset_carries():
        shg[...] = jnp.zeros_like(shg)
        sdn[...] = jnp.zeros_like(sdn)
        convbuf[:, 0:CONV_PAD, :] = jnp.zeros((HEADS, CONV_PAD, 3 * HEAD_DIM), jnp.float32)

    def project_head(c, xsrc=xbuf):
        cols = ([COL_DN_QKV + k * KW + c * HEAD_DIM for k in range(3)]
                + [COL_HG_QFI + k * KW + c * HEAD_DIM for k in range(3)])
        w_head = jnp.concatenate([win_ref[:, o:o + HEAD_DIM] for o in cols], axis=1)
        raw = _dot(xsrc[...], w_head)
        convbuf[c, CONV_PAD:CONV_PAD + tb, :] = raw[:, 0:3 * HEAD_DIM]
        rawhg[c] = raw[:, 3 * HEAD_DIM:6 * HEAD_DIM]

    def head_epilogue(c):
        cw = jnp.concatenate([convw_ref[:, k * KW + c * HEAD_DIM:k * KW + (c + 1) * HEAD_DIM]
                              for k in range(3)], axis=1)
        base = CONV_PAD - (CONV_W - 1)
        acc = convbuf[c, base:base + tb, :] * cw[0:1, :]
        for j in range(1, CONV_W):
            acc = acc + convbuf[c, base + j:base + j + tb, :] * cw[j:j + 1, :]
        convbuf[c, 0:CONV_PAD, :] = convbuf[c, tb:tb + CONV_PAD, :]
        qkv = _silu(acc)
        qh = qkv[:, 0:HEAD_DIM]
        kh = qkv[:, HEAD_DIM:2 * HEAD_DIM]
        dq[c] = qh * (lax.rsqrt(jnp.sum(qh * qh, axis=-1, keepdims=True) + L2_EPS) * QK_SCALE)
        dk[c] = kh * lax.rsqrt(jnp.sum(kh * kh, axis=-1, keepdims=True) + L2_EPS)
        dv[c] = qkv[:, 2 * HEAD_DIM:3 * HEAD_DIM]
        lg = lbl_ref[:, c * HEAD_DIM:(c + 1) * HEAD_DIM]
        e = jnp.exp(lg - jnp.max(lg, axis=0, keepdims=True))
        lb = e[0:1, :] / jnp.sum(e, axis=0, keepdims=True)
        f = lb + (1.0 - lb) * _sigmoid(rawhg[c, :, HEAD_DIM:2 * HEAD_DIM])
        hlf[c] = jnp.log(f)
        hf[c] = f
        hk[c] = 1.0 - f
        hq[c] = _silu(rawhg[c, :, 0:HEAD_DIM]) * QK_SCALE
        hv[c] = rawhg[c, :, 2 * HEAD_DIM:3 * HEAD_DIM]

    gate_tiles = ([(win_ref, COL_HG_G + t, sg, t, _silu) for t in range(0, KW, GATE_TILE)]
                  + [(win_ref, COL_DN_Z + t, sz, t, _silu) for t in range(0, KW, GATE_TILE)]
                  + [(wgate_ref, t, sga, t, _sigmoid) for t in range(0, D_MODEL, GATE_TILE)]
                  + [(wgate_ref, D_MODEL + t, sgb, t, _sigmoid) for t in range(0, D_MODEL, GATE_TILE)])
    gate_tiles_todo = list(range(len(gate_tiles)))

    def issue_gate_tiles(count):
        for _ in range(min(count, len(gate_tiles_todo))):
            w_ref, src, dst, off, act = gate_tiles[gate_tiles_todo.pop(0)]
            dst[:, off:off + GATE_TILE] = act(_dot(xbuf[...], w_ref[:, src:src + GATE_TILE]))

    @pl.when((pl.program_id(0) == 0) & (pl.program_id(1) == 0))
    def _first_projection():
        xnbuf[...] = _bf(x_ref[...])
        project_head(0, xnbuf)

    xbuf[...] = xnbuf[...]

    def cumulate_forget(h0):
        lf_hi, lf_lo = _split(jnp.concatenate([hlf[h0], hlf[h0 + 1]], axis=1))
        bc = _dot(bdtril_ref[...], lf_hi) + _dot(bdtril_ref[...], lf_lo)
        hb[h0] = bc[:, 0:HEAD_DIM]
        hb[h0 + 1] = bc[:, HEAD_DIM:2 * HEAD_DIM]

    for c in range(1, HEADS):
        head_epilogue(c - 1)
        project_head(c)
        if c % 2 == 0:
            cumulate_forget(c - 2)
    head_epilogue(HEADS - 1)
    issue_gate_tiles(GATE_TILES_AFTER_PROJECTIONS)
    cumulate_forget(HEADS - 2)

    pab = _dot(xbuf[...], wab_ref[...])
    g_col = -jnp.exp(alogr_ref[...]) * _softplus(pab + dtbr_ref[...])
    lane = lax.broadcasted_iota(jnp.int32, pab.shape, 1)
    gbuf[...] = jnp.where(lane < HEADS, _sigmoid(pab), g_col)
    pabt = jnp.transpose(pab)[0:2 * HEADS, :]
    g_row = -jnp.exp(alogc_ref[...]) * _softplus(pabt + dtbc_ref[...])
    gr_hi, gr_lo = _split(g_row)
    gc_row = _dot(gr_hi, bdtriu_ref[...]) + _dot(gr_lo, bdtriu_ref[...])
    for c in range(n_chunks):
        gcrow[c] = gc_row[:, c * CHUNK:(c + 1) * CHUNK]

    ri = lax.broadcasted_iota(jnp.int32, (CHUNK, CHUNK), 0)
    ci = lax.broadcasted_iota(jnp.int32, (CHUNK, CHUNK), 1)
    causal = jnp.where(ri >= ci, 1.0, 0.0)
    strict = jnp.where(ri > ci, 1.0, 0.0)
    trow = lax.broadcasted_iota(jnp.int32, (CHUNK, HEAD_DIM), 0)

    n_lev = len(_LEVELS)
    heads = range(HEADS)
    hsl = [slice(h * HEAD_DIM, (h + 1) * HEAD_DIM) for h in heads]

    def level_mask(l):
        return masks_ref[l * CHUNK:(l + 1) * CHUNK, 0:CHUNK]

    def level_mask2(l):
        return masks_ref[l * CHUNK:(l + 1) * CHUNK, :]

    units = [(u, h) for u in range(PREP_UNROLL) for h in heads]
    n_units = range(len(units))

    def prepare_chunks(j):
        cs = [j * PREP_UNROLL + u for u in range(PREP_UNROLL)]
        rows = [slice(c * CHUNK, (c + 1) * CHUNK) for c in cs]
        wsum = wsum_ref[...]
        gb, eg, gcr = [], [], []
        for u in range(PREP_UNROLL):
            gb.append(gbuf[rows[u], :])
            g_hi, g_lo = _split(gb[u])
            eg.append(_dot(wsum, g_hi) + _dot(wsum, g_lo))
            gcr.append(gcrow[cs[u]])

        def ld(ref, i):
            u, h = units[i]
            return ref[h, rows[u], :]

        dkk = [ld(dk, i) for i in n_units]
        dkb = [_bf(k) for k in dkk]
        beta = [gb[u][:, h:h + 1] for u, h in units]
        gcol = [eg[u][0:CHUNK, HEADS + h:HEADS + h + 1] for u, h in units]
        decay = [causal * jnp.exp(jnp.minimum(gcol[i] - gcr[u][HEADS + h:HEADS + h + 1, :], 0.0))
                 for i, (u, h) in enumerate(units)]
        kbeta = [dkk[i] * beta[i] for i in n_units]
        lmat = [_dot_nt(_bf(kbeta[i]), dkb[i]) * (strict * decay[i]) for i in n_units]
        qk = [_dot_nt(_bf(ld(dq, i)), dkb[i]) * decay[i] for i in n_units]
        for i, (u, h) in enumerate(units):
            if h % 2 == 0:
                qkb[cs[u] * (HEADS // 2) + h // 2] = _bf(jnp.concatenate([qk[i], qk[i + 1]], axis=1))

        hqv = [ld(hq, i) for i in n_units]
        hkv = [ld(hk, i) for i in n_units]
        hqb = [_bf(q) for q in hqv]
        hkb = [_bf(k) for k in hkv]
        sc = [level_mask(n_lev) * _dot_nt(hqb[i], hkb[i]) for i in n_units]
        pairs = [(i, i + 1) for i in range(0, len(units), 2)]
        lane2 = lax.broadcasted_iota(jnp.int32, (CHUNK, 2 * CHUNK), 1)
        first_f = jnp.where(lane2 < CHUNK, 1.0, 0.0)
        second_f = 1.0 - first_f

        def block_diag(m2, lo, hi):
            return jnp.concatenate([_bf(m2 * lo), _bf(m2 * hi)], axis=0)

        lmat2 = [jnp.concatenate([lmat[a], lmat[b]], axis=1) for a, b in pairs]
        lbd = [block_diag(m2, first_f, second_f) for m2 in lmat2]
        nmat2 = [-(level_mask2(n_lev - 1) * m2) for m2 in lmat2]
        first_b, second_b = _bf(first_f), _bf(second_f)

        hbv = [ld(hb, i) for i in n_units]

        def level_factor(i, l):
            m = _LEVELS[l]
            if m >= SUBLANES // 2:
                ref = jnp.concatenate(
                    [jnp.broadcast_to(hbv[i][r:r + 1, :], (SUBLANES, HEAD_DIM))
                     for r in [(g * SUBLANES // (2 * m)) * (2 * m) + m - 1 for g in range(CHUNK // SUBLANES)]],
                    axis=0)
                return jnp.exp2(jnp.abs(hbv[i] - ref) * NEG_LOG2E)
            f = ld(hf, i)
            if m == 1:
                return jnp.where(trow % 2 == 1, f, 1.0)
            f8 = f.reshape(CHUNK // SUBLANES, SUBLANES, HEAD_DIM)
            f_prev = pltpu.roll(f8, 1, 1).reshape(CHUNK, HEAD_DIM)
            f_next = pltpu.roll(f8, SUBLANES - 1, 1).reshape(CHUNK, HEAD_DIM)
            return jnp.where(trow % 4 == 0, f_next,
                             jnp.where(trow % 4 == 1, 1.0, jnp.where(trow % 4 == 2, f, f_prev * f)))

        def score_level(l):
            for i in n_units:
                zb = _bf(level_factor(i, l))
                sc[i] = sc[i] + level_mask(l) * _dot_nt(hqb[i] * zb, hkb[i] * zb)

        score_level(n_lev - 1)
        n_pairs = range(len(pairs))
        for l in range(n_lev - 2, -1, -1):
            nb = [_bf(nmat2[p]) for p in n_pairs]
            xm = [level_mask2(l) * (lmat2[p] + _dot(nb[p], lbd[p])) for p in n_pairs]
            score_level(l)
            for p in n_pairs:
                nbd = jnp.concatenate([nb[p] * first_b, nb[p] * second_b], axis=0)
                nmat2[p] = nmat2[p] - (xm[p] + _dot(_bf(xm[p]), nbd))

        hvb = [_bf(ld(hv, i)) for i in n_units]
        for i, (u, h) in enumerate(units):
            oa[rows[u], hsl[h]] = _dot(_bf(sc[i]), hvb[i])
        for i, (u, h) in enumerate(units):
            kd = hkv[i] * jnp.exp(hbv[i][CHUNK - 1:CHUNK, :] - hbv[i])
            kvinc[cs[u] * HEADS + h] = _dot_tn(_bf(kd), hvb[i])
        for i, (u, h) in enumerate(units):
            qin[rows[u], hsl[h]] = _bf(hqv[i] * jnp.exp(hbv[i]))
            dec_rows = jnp.broadcast_to(jnp.exp(hbv[i][CHUNK - 1:CHUNK, :]), (SUBLANES, HEAD_DIM))
            hdec[cs[u] * HEADS + h] = jnp.transpose(dec_rows)

        for i, (u, h) in enumerate(units):
            eg_col = jnp.exp(gcol[i])
            rhs = jnp.concatenate([ld(dv, i) * beta[i], kbeta[i] * eg_col], axis=1)
            zero_rhs = jnp.zeros(rhs.shape, jnp.bfloat16)
            rhs_pad = jnp.concatenate([_bf(rhs), zero_rhs] if i % 2 == 0 else [zero_rhs, _bf(rhs)], axis=0)
            uw = rhs + _dot(_bf(nmat2[i // 2]), rhs_pad)
            du[rows[u], hsl[h]] = uw[:, 0:HEAD_DIM]
            wq[cs[u], 0:CHUNK, hsl[h]] = _bf(uw[:, HEAD_DIM:2 * HEAD_DIM])
            wq[cs[u], CHUNK:2 * CHUNK, hsl[h]] = _bf(ld(dq, i) * eg_col)
            kdec[rows[u], hsl[h]] = _bf(dkk[i] * jnp.exp(eg[u][CHUNK:2 * CHUNK, HEADS + h:HEADS + h + 1]))
        for u in range(PREP_UNROLL):
            gl[cs[u]] = jnp.exp(eg[u][CHUNK - 1:CHUNK, :])

    for j in range(n_chunks // PREP_UNROLL):
        prepare_chunks(j)

    for c in range(n_chunks):
        rows = slice(c * CHUNK, (c + 1) * CHUNK)
        s = [sdn[h] for h in heads]
        wqc = wq[c]
        zero_s = jnp.zeros((HEAD_DIM, HEAD_DIM), jnp.bfloat16)
        ws = []
        for h in range(0, HEADS, 2):
            s_bd = jnp.concatenate([jnp.concatenate([_bf(s[h]), zero_s], axis=1),
                                    jnp.concatenate([zero_s, _bf(s[h + 1])], axis=1)], axis=0)
            ws2 = _dot(wqc[:, h * HEAD_DIM:(h + 2) * HEAD_DIM], s_bd)
            ws += [ws2[:, 0:HEAD_DIM], ws2[:, HEAD_DIM:2 * HEAD_DIM]]
        st = [shg[h] for h in heads]
        for h in range(0, HEADS, 2):
            st_bd = jnp.concatenate([jnp.concatenate([_bf(st[h]), zero_s], axis=1),
                                     jnp.concatenate([zero_s, _bf(st[h + 1])], axis=1)], axis=0)
            pair = slice(h * HEAD_DIM, (h + 2) * HEAD_DIM)
            oa[rows, pair] = oa[rows, pair] + _dot(qin[rows, pair], st_bd)
        for h in heads:
            shg[h] = st[h] * hdec[c * HEADS + h][:, 0:1] + kvinc[c * HEADS + h]
        issue_gate_tiles(-(-len(gate_tiles_todo) // (2 * (n_chunks - c))))
        vnb = [_bf(du[rows, hsl[h]] - ws[h][0:CHUNK]) for h in heads]
        glc = gl[c]
        zero_v = jnp.zeros((CHUNK, HEAD_DIM), jnp.bfloat16)
        for h in range(0, HEADS, 2):
            v_bd = jnp.concatenate([jnp.concatenate([vnb[h], zero_v], axis=1),
                                    jnp.concatenate([zero_v, vnb[h + 1]], axis=1)], axis=0)
            intra = _dot(qkb[c * (HEADS // 2) + h // 2], v_bd)
            ob[rows, hsl[h]] = ws[h][CHUNK:2 * CHUNK] + intra[:, 0:HEAD_DIM]
            ob[rows, hsl[h + 1]] = ws[h + 1][CHUNK:2 * CHUNK] + intra[:, HEAD_DIM:2 * HEAD_DIM]
        for h in heads:
            sdn[h] = s[h] * glc[:, HEADS + h:HEADS + h + 1] + _dot_tn(kdec[rows, hsl[h]], vnb[h])
        issue_gate_tiles(-(-len(gate_tiles_todo) // (2 * (n_chunks - c) - 1)))

    def gated_norm(o_ref, w_row, gate):
        parts = []
        for h in range(HEADS):
            oh = o_ref[:, h * HEAD_DIM:(h + 1) * HEAD_DIM]
            parts.append(oh * lax.rsqrt(jnp.mean(oh * oh, axis=-1, keepdims=True) + RMS_EPS))
        return jnp.concatenate(parts, axis=1) * w_row * gate

    na = gated_norm(oa, hgnw_ref[...], sg[...])
    nb_ = gated_norm(ob, dnnw_ref[...], sz[...])
    merged = sga[...] * _dot(_bf(na), wa_ref[...]) + sgb[...] * _dot(_bf(nb_), wb_ref[...])
    mix = _dot(_bf(merged), wo_ref[...])
    out_ref[...] = _layer_norm(ALPHA * x_ref[...] + mix, g1_ref[...], b1_ref[...])

    xnbuf[...] = _bf(xnext_ref[...])
    project_head(0, xnbuf)


def _mlp_kernel(h_ref, wup_ref, wdown_ref, g2_ref, b2_ref, out_ref):
    tm = h_ref.shape[0]
    hb = _bf(h_ref[...])
    n_ff = D_FF // FF_BLOCK
    assert n_ff >= 2 and tm % MLP_TAIL_SPLIT == 0
    acc = None
    for j in range(n_ff - 1):
        up = _dot(hb, wup_ref[:, j * FF_BLOCK:(j + 1) * FF_BLOCK])
        act = jnp.square(jnp.maximum(up, 0.0))
        part = _dot(_bf(act), wdown_ref[j * FF_BLOCK:(j + 1) * FF_BLOCK, :])
        acc = part if acc is None else acc + part
    up = _dot(hb, wup_ref[:, (n_ff - 1) * FF_BLOCK:n_ff * FF_BLOCK])
    actb = _bf(jnp.square(jnp.maximum(up, 0.0)))
    for r in range(MLP_TAIL_SPLIT):
        rows = slice(r * tm // MLP_TAIL_SPLIT, (r + 1) * tm // MLP_TAIL_SPLIT)
        part = _dot(actb[rows], wdown_ref[(n_ff - 1) * FF_BLOCK:n_ff * FF_BLOCK, :])
        out_ref[rows, :] = _layer_norm(ALPHA * h_ref[rows, :] + acc[rows] + part, g2_ref[...], b2_ref[...])


def _resident(shape):
    nd = len(shape)
    return pl.BlockSpec(shape, lambda *_: (0,) * nd, pipeline_mode=pl.Buffered(1))


def _mixer_call(x, wup, wdown, lbl, win, wgate, wab, convw, hgnw, alogr, dtbr, alogc, dtbc, dnnw,
                wa, wb, wo, g1, b1):
    bsz, seq, _ = x.shape
    tb = min(TOKEN_BLOCK, seq)
    assert seq % tb == 0 and tb % (CHUNK * PREP_UNROLL) == 0
    n_chunks = tb // CHUNK
    wsum = jnp.asarray(_WSUM, jnp.bfloat16)
    masks = jnp.asarray(np.concatenate([_MASKS, _MASKS], axis=1), jnp.float32)
    tt = np.arange(tb)
    bdtriu = jnp.asarray(((tt[:, None] <= tt[None, :]) & ((tt[:, None] // CHUNK) == (tt[None, :] // CHUNK))),
                         jnp.bfloat16)
    consts = (lbl, win, wgate, wab, convw, hgnw, alogr, dtbr, alogc, dtbc, dnnw, wa, wb, wo,
              g1, b1, wsum, masks, bdtriu, bdtriu.T)
    f32 = jnp.float32
    blk = functools.partial(pltpu.VMEM, (tb, KW))
    bf16 = jnp.bfloat16
    per_head = functools.partial(pltpu.VMEM, (HEADS, tb, HEAD_DIM))
    scratch = [pltpu.VMEM((tb, D_MODEL), bf16), pltpu.VMEM((tb, D_MODEL), bf16)]
    scratch += [per_head(f32) for _ in range(9)]
    scratch += [blk(f32) for _ in range(3)]
    scratch += [pltpu.VMEM((tb, LANES), f32),
                pltpu.VMEM((n_chunks, 8, CHUNK), f32),
                pltpu.VMEM((HEADS, tb + CONV_PAD, 3 * HEAD_DIM), f32),
                pltpu.VMEM((HEADS, tb, 3 * HEAD_DIM), f32),
                blk(bf16), blk(bf16),
                pltpu.VMEM((n_chunks, 2 * CHUNK, KW), bf16),
                pltpu.VMEM((n_chunks * HEADS // 2, CHUNK, 2 * CHUNK), bf16),
                pltpu.VMEM((n_chunks * HEADS, HEAD_DIM, HEAD_DIM), f32),
                pltpu.VMEM((n_chunks * HEADS, HEAD_DIM, SUBLANES), f32),
                pltpu.VMEM((n_chunks, 1, LANES), f32),
                blk(f32), blk(f32),
                pltpu.VMEM((tb, D_MODEL), f32),
                pltpu.VMEM((tb, D_MODEL), f32),
                pltpu.VMEM((HEADS, HEAD_DIM, HEAD_DIM), f32),
                pltpu.VMEM((HEADS, HEAD_DIM, HEAD_DIM), f32)]
    n_blocks = seq // tb

    def next_block(b, i):
        flat = jnp.minimum(b * n_blocks + i + 1, bsz * n_blocks - 1)
        return flat // n_blocks, flat % n_blocks, 0

    n_steps = bsz * n_blocks
    up_rows, down_rows = wup.shape[0] // n_steps, wdown.shape[0] // n_steps
    assert up_rows * n_steps == wup.shape[0] and down_rows * n_steps == wdown.shape[0]
    assert up_rows % BF16_ROWS == 0 and down_rows % BF16_ROWS == 0

    def step_rows(b, i):
        return b * n_blocks + i, 0

    return pl.pallas_call(
        _mixer_kernel,
        grid=(bsz, n_blocks),
        in_specs=[pl.BlockSpec((None, tb, D_MODEL), lambda b, i: (b, i, 0)),
                  pl.BlockSpec((None, tb, D_MODEL), next_block),
                  pl.BlockSpec((up_rows, wup.shape[1]), step_rows),
                  pl.BlockSpec((down_rows, wdown.shape[1]), step_rows)]
                 + [_resident(c.shape) for c in consts],
        out_specs=[pl.BlockSpec((None, tb, D_MODEL), lambda b, i: (b, i, 0)),
                   pl.BlockSpec((up_rows, wup.shape[1]), step_rows),
                   pl.BlockSpec((down_rows, wdown.shape[1]), step_rows)],
        out_shape=[jax.ShapeDtypeStruct(x.shape, f32),
                   jax.ShapeDtypeStruct(wup.shape, bf16),
                   jax.ShapeDtypeStruct(wdown.shape, bf16)],
        scratch_shapes=scratch,
        compiler_params=pltpu.CompilerParams(dimension_semantics=("arbitrary", "arbitrary"),
                                             vmem_limit_bytes=VMEM_LIMIT_BYTES),
        name="token_mixer",
    )(x, x, wup, wdown, *consts)


def _mlp_call(h, wup, wdown, g2, b2):
    m = h.shape[0]
    tm = min(MLP_BLOCK, m)
    assert m % tm == 0
    consts = (wup, wdown, g2, b2)
    return pl.pallas_call(
        _mlp_kernel,
        grid=(m // tm,),
        in_specs=[pl.BlockSpec((tm, D_MODEL), lambda i: (i, 0))] + [_resident(c.shape) for c in consts],
        out_specs=pl.BlockSpec((tm, D_MODEL), lambda i: (i, 0)),
        out_shape=jax.ShapeDtypeStruct(h.shape, jnp.float32),
        compiler_params=pltpu.CompilerParams(dimension_semantics=("arbitrary",),
                                             vmem_limit_bytes=VMEM_LIMIT_BYTES),
        name="relu2_mlp",
    )(h, *consts)


def kernel(x, hg_lb_logits, w_in, conv_w, hg_norm_w, dn_A_log, dn_dt_bias, dn_norm_w, w_branch_a, w_branch_b, w_o, ln1_g, ln1_b, w_up, w_down, ln2_g, ln2_b):
    bsz, seq, _ = x.shape
    f32 = jnp.float32
    bf16 = jnp.bfloat16
    o = _OFFS
    l = LAYER
    win = w_in[l].astype(bf16)
    wgate = win[:, o[8]:o[10]]
    wab = jnp.pad(win[:, o[6]:o[8]], ((0, 0), (0, LANES - 2 * HEADS)))
    zeros4 = jnp.zeros((HEADS,), f32)
    alog8 = jnp.concatenate([zeros4, dn_A_log[l].astype(f32)])
    dtb8 = jnp.concatenate([zeros4, dn_dt_bias[l].astype(f32)])
    alogr = jnp.pad(alog8, (0, LANES - 2 * HEADS)).reshape(1, LANES)
    dtbr = jnp.pad(dtb8, (0, LANES - 2 * HEADS)).reshape(1, LANES)
    h1, wup_bf, wdown_bf = _mixer_call(
        x, w_up[l], w_down[l], hg_lb_logits.astype(f32), win, wgate, wab, conv_w[l].astype(f32),
        hg_norm_w[l].reshape(1, KW), alogr, dtbr, alog8.reshape(8, 1), dtb8.reshape(8, 1),
        jnp.tile(dn_norm_w[l], HEADS).reshape(1, KW),
        w_branch_a[l].astype(bf16), w_branch_b[l].astype(bf16), w_o[l].astype(bf16),
        ln1_g[l].reshape(1, D_MODEL), ln1_b[l].reshape(1, D_MODEL))
    out = _mlp_call(h1.reshape(bsz * seq, D_MODEL), wup_bf, wdown_bf,
                    ln2_g[l].reshape(1, D_MODEL), ln2_b[l].reshape(1, D_MODEL))
    return out.reshape(bsz, seq, D_MODEL)
```

```python
import functools

import numpy as np
import jax
import jax.numpy as jnp
from jax import lax
from jax.experimental import pallas as pl
from jax.experimental.pallas import tpu as pltpu

D_MODEL = 1024
HEADS = 4
HEAD_DIM = 128
KW = HEADS * HEAD_DIM
CHUNK = 64
CONV_W = 4
D_FF = 4 * D_MODEL
DEPTH = 1
LAYER = 0
ALPHA = (2 * DEPTH) ** 0.25
LN_EPS = 1e-5
RMS_EPS = 1e-6
L2_EPS = 1e-6
QK_SCALE = HEAD_DIM ** -0.5
NEG_LOG2E = -1.4426950408889634

TOKEN_BLOCK = 256
PREP_UNROLL = 4
GATE_TILE = 256
GATE_TILES_AFTER_PROJECTIONS = 4
MLP_BLOCK = 1024
MLP_TAIL_SPLIT = 4
FF_BLOCK = 1024
LANES = 128
SUBLANES = 8
BF16_ROWS = 16
CONV_PAD = 8
VMEM_LIMIT_BYTES = 56 * 1024 * 1024

_SIZES = (KW, KW, KW, KW, 3 * KW, KW, HEADS, HEADS, D_MODEL, D_MODEL)
_OFFS = np.concatenate([[0], np.cumsum(_SIZES)]).tolist()

COL_HG_QFI = _OFFS[0]
COL_HG_G = _OFFS[3]
COL_DN_QKV = _OFFS[4]
COL_DN_Z = _OFFS[5]

_LEVELS = (32, 16, 8, 4, 2, 1)


def _build_constants():
    t = np.arange(CHUNK)[:, None]
    r = np.arange(CHUNK)[None, :]
    wsum = np.concatenate([r <= t, r > t], 0).astype(np.float32)
    masks = [((t // (2 * m)) == (r // (2 * m))) & ((t % (2 * m)) >= m) & ((r % (2 * m)) < m) for m in _LEVELS]
    masks.append(t == r)
    masks = np.concatenate(masks, 0).astype(np.float32)
    return wsum, masks


_WSUM, _MASKS = _build_constants()


def _dot(a, b):
    return jnp.dot(a, b, preferred_element_type=jnp.float32)


def _dot_nt(a, b):
    return lax.dot_general(a, b, (((1,), (1,)), ((), ())), preferred_element_type=jnp.float32)


def _dot_tn(a, b):
    return lax.dot_general(a, b, (((0,), (0,)), ((), ())), preferred_element_type=jnp.float32)


def _bf(x):
    return x.astype(jnp.bfloat16)


def _split(x):
    hi = _bf(x)
    return hi, _bf(x - hi.astype(jnp.float32))


def _sigmoid(x):
    return 1.0 / (1.0 + jnp.exp2(x * NEG_LOG2E))


def _silu(x):
    return x * _sigmoid(x)


def _softplus(x):
    return jnp.maximum(x, 0.0) + jnp.log1p(jnp.exp(-jnp.abs(x)))


def _layer_norm(x, g, b):
    mu = jnp.mean(x, axis=-1, keepdims=True)
    xc = x - mu
    var = jnp.mean(xc * xc, axis=-1, keepdims=True)
    return xc * lax.rsqrt(var + LN_EPS) * g + b


def _mixer_kernel(x_ref, xnext_ref, wup32_ref, wdown32_ref, lbl_ref, win_ref, wgate_ref, wab_ref, convw_ref, hgnw_ref,
                  alogr_ref, dtbr_ref, alogc_ref, dtbc_ref, dnnw_ref, wa_ref, wb_ref, wo_ref,
                  g1_ref, b1_ref, wsum_ref, masks_ref, bdtriu_ref, bdtril_ref, out_ref, wupb_ref, wdownb_ref,
                  xbuf, xnbuf, hq, hk, hv, hlf, hf, hb, dq, dk, dv, oa, ob, du, gbuf, gcrow, convbuf, rawhg, qin, kdec, wq, qkb,
                  kvinc, hdec, gl, sg, sz, sga, sgb, shg, sdn):
    tb = x_ref.shape[0]
    n_chunks = tb // CHUNK

    wupb_ref[...] = _bf(wup32_ref[...])
    wdownb_ref[...] = _bf(wdown32_ref[...])

    @pl.when(pl.program_id(1) == 0)
    def _reset_carries():
        shg[...] = jnp.zeros_like(shg)
        sdn[...] = jnp.zeros_like(sdn)
        convbuf[:, 0:CONV_PAD, :] = jnp.zeros((HEADS, CONV_PAD, 3 * HEAD_DIM), jnp.float32)

    def project_head(c, xsrc=xbuf):
        cols = ([COL_DN_QKV + k * KW + c * HEAD_DIM for k in range(3)]
                + [COL_HG_QFI + k * KW + c * HEAD_DIM for k in range(3)])
        w_head = jnp.concatenate([win_ref[:, o:o + HEAD_DIM] for o in cols], axis=1)
        raw = _dot(xsrc[...], w_head)
        convbuf[c, CONV_PAD:CONV_PAD + tb, :] = raw[:, 0:3 * HEAD_DIM]
        rawhg[c] = raw[:, 3 * HEAD_DIM:6 * HEAD_DIM]

    def head_epilogue(c):
        cw = jnp.concatenate([convw_ref[:, k * KW + c * HEAD_DIM:k * KW + (c + 1) * HEAD_DIM]
                              for k in range(3)], axis=1)
        base = CONV_PAD - (CONV_W - 1)
        acc = convbuf[c, base:base + tb, :] * cw[0:1, :]
        for j in range(1, CONV_W):
            acc = acc + convbuf[c, base + j:base + j + tb, :] * cw[j:j + 1, :]
        convbuf[c, 0:CONV_PAD, :] = convbuf[c, tb:tb + CONV_PAD, :]
        qkv = _silu(acc)
        qh = qkv[:, 0:HEAD_DIM]
        kh = qkv[:, HEAD_DIM:2 * HEAD_DIM]
        dq[c] = qh * (lax.rsqrt(jnp.sum(qh * qh, axis=-1, keepdims=True) + L2_EPS) * QK_SCALE)
        dk[c] = kh * lax.rsqrt(jnp.sum(kh * kh, axis=-1, keepdims=True) + L2_EPS)
        dv[c] = qkv[:, 2 * HEAD_DIM:3 * HEAD_DIM]
        lg = lbl_ref[:, c * HEAD_DIM:(c + 1) * HEAD_DIM]
        e = jnp.exp(lg - jnp.max(lg, axis=0, keepdims=True))
        lb = e[0:1, :] / jnp.sum(e, axis=0, keepdims=True)
        f = lb + (1.0 - lb) * _sigmoid(rawhg[c, :, HEAD_DIM:2 * HEAD_DIM])
        hlf[c] = jnp.log(f)
        hf[c] = f
        hk[c] = 1.0 - f
        hq[c] = _silu(rawhg[c, :, 0:HEAD_DIM]) * QK_SCALE
        hv[c] = rawhg[c, :, 2 * HEAD_DIM:3 * HEAD_DIM]

    gate_tiles = ([(win_ref, COL_HG_G + t, sg, t, _silu) for t in range(0, KW, GATE_TILE)]
                  + [(win_ref, COL_DN_Z + t, sz, t, _silu) for t in range(0, KW, GATE_TILE)]
                  + [(wgate_ref, t, sga, t, _sigmoid) for t in range(0, D_MODEL, GATE_TILE)]
                  + [(wgate_ref, D_MODEL + t, sgb, t, _sigmoid) for t in range(0, D_MODEL, GATE_TILE)])
    gate_tiles_todo = list(range(len(gate_tiles)))

    def issue_gate_tiles(count):
        for _ in range(min(count, len(gate_tiles_todo))):
            w_ref, src, dst, off, act = gate_tiles[gate_tiles_todo.pop(0)]
            dst[:, off:off + GATE_TILE] = act(_dot(xbuf[...], w_ref[:, src:src + GATE_TILE]))

    @pl.when((pl.program_id(0) == 0) & (pl.program_id(1) == 0))
    def _first_projection():
        xnbuf[...] = _bf(x_ref[...])
        project_head(0, xnbuf)

    xbuf[...] = xnbuf[...]

    def cumulate_forget(h0):
        lf_hi, lf_lo = _split(jnp.concatenate([hlf[h0], hlf[h0 + 1]], axis=1))
        bc = _dot(bdtril_ref[...], lf_hi) + _dot(bdtril_ref[...], lf_lo)
        hb[h0] = bc[:, 0:HEAD_DIM]
        hb[h0 + 1] = bc[:, HEAD_DIM:2 * HEAD_DIM]

    for c in range(1, HEADS):
        head_epilogue(c - 1)
        project_head(c)
        if c % 2 == 0:
            cumulate_forget(c - 2)
    head_epilogue(HEADS - 1)
    issue_gate_tiles(GATE_TILES_AFTER_PROJECTIONS)
    cumulate_forget(HEADS - 2)

    pab = _dot(xbuf[...], wab_ref[...])
    g_col = -jnp.exp(alogr_ref[...]) * _softplus(pab + dtbr_ref[...])
    lane = lax.broadcasted_iota(jnp.int32, pab.shape, 1)
    gbuf[...] = jnp.where(lane < HEADS, _sigmoid(pab), g_col)
    pabt = jnp.transpose(pab)[0:2 * HEADS, :]
    g_row = -jnp.exp(alogc_ref[...]) * _softplus(pabt + dtbc_ref[...])
    gr_hi, gr_lo = _split(g_row)
    gc_row = _dot(gr_hi, bdtriu_ref[...]) + _dot(gr_lo, bdtriu_ref[...])
    for c in range(n_chunks):
        gcrow[c] = gc_row[:, c * CHUNK:(c + 1) * CHUNK]

    ri = lax.broadcasted_iota(jnp.int32, (CHUNK, CHUNK), 0)
    ci = lax.broadcasted_iota(jnp.int32, (CHUNK, CHUNK), 1)
    causal = jnp.where(ri >= ci, 1.0, 0.0)
    strict = jnp.where(ri > ci, 1.0, 0.0)
    trow = lax.broadcasted_iota(jnp.int32, (CHUNK, HEAD_DIM), 0)

    n_lev = len(_LEVELS)
    heads = range(HEADS)
    hsl = [slice(h * HEAD_DIM, (h + 1) * HEAD_DIM) for h in heads]

    def level_mask(l):
        return masks_ref[l * CHUNK:(l + 1) * CHUNK, 0:CHUNK]

    def level_mask2(l):
        return masks_ref[l * CHUNK:(l + 1) * CHUNK, :]

    units = [(u, h) for u in range(PREP_UNROLL) for h in heads]
    n_units = range(len(units))

    def prepare_chunks(j):
        cs = [j * PREP_UNROLL + u for u in range(PREP_UNROLL)]
        rows = [slice(c * CHUNK, (c + 1) * CHUNK) for c in cs]
        wsum = wsum_ref[...]
        gb, eg, gcr = [], [], []
        for u in range(PREP_UNROLL):
            gb.append(gbuf[rows[u], :])
            g_hi, g_lo = _split(gb[u])
            eg.append(_dot(wsum, g_hi) + _dot(wsum, g_lo))
            gcr.append(gcrow[cs[u]])

        def ld(ref, i):
            u, h = units[i]
            return ref[h, rows[u], :]

        dkk = [ld(dk, i) for i in n_units]
        dkb = [_bf(k) for k in dkk]
        beta = [gb[u][:, h:h + 1] for u, h in units]
        gcol = [eg[u][0:CHUNK, HEADS + h:HEADS + h + 1] for u, h in units]
        decay = [causal * jnp.exp(jnp.minimum(gcol[i] - gcr[u][HEADS + h:HEADS + h + 1, :], 0.0))
                 for i, (u, h) in enumerate(units)]
        kbeta = [dkk[i] * beta[i] for i in n_units]
        gram = [_dot_nt(jnp.concatenate([_bf(kbeta[i]), _bf(ld(dq, i))], axis=0), dkb[i]) for i in n_units]
        lmat = [gram[i][0:CHUNK] * (strict * decay[i]) for i in n_units]
        qk = [gram[i][CHUNK:2 * CHUNK] * decay[i] for i in n_units]
        for i, (u, h) in enumerate(units):
            if h % 2 == 0:
                qkb[cs[u] * (HEADS // 2) + h // 2] = _bf(jnp.concatenate([qk[i], qk[i + 1]], axis=1))

        hqv = [ld(hq, i) for i in n_units]
        hkv = [ld(hk, i) for i in n_units]
        hqb = [_bf(q) for q in hqv]
        hkb = [_bf(k) for k in hkv]
        sc = [level_mask(n_lev) * _dot_nt(hqb[i], hkb[i]) for i in n_units]
        pairs = [(i, i + 1) for i in range(0, len(units), 2)]
        lane2 = lax.broadcasted_iota(jnp.int32, (CHUNK, 2 * CHUNK), 1)
        first_f = jnp.where(lane2 < CHUNK, 1.0, 0.0)
        second_f = 1.0 - first_f

        def block_diag(m2, lo, hi):
            return jnp.concatenate([_bf(m2 * lo), _bf(m2 * hi)], axis=0)

        lmat2 = [jnp.concatenate([lmat[a], lmat[b]], axis=1) for a, b in pairs]
        lbd = [block_diag(m2, first_f, second_f) for m2 in lmat2]
        nmat2 = [-(level_mask2(n_lev - 1) * m2) for m2 in lmat2]
        first_b, second_b = _bf(first_f), _bf(second_f)

        hbv = [ld(hb, i) for i in n_units]

        def level_factor(i, l):
            m = _LEVELS[l]
            if m >= SUBLANES // 2:
                ref = jnp.concatenate(
                    [jnp.broadcast_to(hbv[i][r:r + 1, :], (SUBLANES, HEAD_DIM))
                     for r in [(g * SUBLANES // (2 * m)) * (2 * m) + m - 1 for g in range(CHUNK // SUBLANES)]],
                    axis=0)
                return jnp.exp2(jnp.abs(hbv[i] - ref) * NEG_LOG2E)
            f = ld(hf, i)
            if m == 1:
                return jnp.where(trow % 2 == 1, f, 1.0)
            f8 = f.reshape(CHUNK // SUBLANES, SUBLANES, HEAD_DIM)
            f_prev = pltpu.roll(f8, 1, 1).reshape(CHUNK, HEAD_DIM)
            f_next = pltpu.roll(f8, SUBLANES - 1, 1).reshape(CHUNK, HEAD_DIM)
            return jnp.where(trow % 4 == 0, f_next,
                             jnp.where(trow % 4 == 1, 1.0, jnp.where(trow % 4 == 2, f, f_prev * f)))

        def score_level(l):
            for i in n_units:
                zb = _bf(level_factor(i, l))
                sc[i] = sc[i] + level_mask(l) * _dot_nt(hqb[i] * zb, hkb[i] * zb)

        score_level(n_lev - 1)
        n_pairs = range(len(pairs))
        for l in range(n_lev - 2, -1, -1):
            nb = [_bf(nmat2[p]) for p in n_pairs]
            xm = [level_mask2(l) * (lmat2[p] + _dot(nb[p], lbd[p])) for p in n_pairs]
            score_level(l)
            for p in n_pairs:
                nbd = jnp.concatenate([nb[p] * first_b, nb[p] * second_b], axis=0)
                nmat2[p] = nmat2[p] - (xm[p] + _dot(_bf(xm[p]), nbd))

        hvb = [_bf(ld(hv, i)) for i in n_units]
        for i, (u, h) in enumerate(units):
            oa[rows[u], hsl[h]] = _dot(_bf(sc[i]), hvb[i])
        for i, (u, h) in enumerate(units):
            kd = hkv[i] * jnp.exp(hbv[i][CHUNK - 1:CHUNK, :] - hbv[i])
            kvinc[cs[u] * HEADS + h] = _dot_tn(hvb[i], _bf(kd))
        for i, (u, h) in enumerate(units):
            qin[rows[u], hsl[h]] = _bf(hqv[i] * jnp.exp(hbv[i]))
            hdec[cs[u], :, hsl[h]] = jnp.exp(hbv[i][CHUNK - 1:CHUNK, :])

        for i, (u, h) in enumerate(units):
            eg_col = jnp.exp(gcol[i])
            rhs = jnp.concatenate([ld(dv, i) * beta[i], kbeta[i] * eg_col], axis=1)
            zero_rhs = jnp.zeros(rhs.shape, jnp.bfloat16)
            rhs_pad = jnp.concatenate([_bf(rhs), zero_rhs] if i % 2 == 0 else [zero_rhs, _bf(rhs)], axis=0)
            uw = rhs + _dot(_bf(nmat2[i // 2]), rhs_pad)
            du[rows[u], hsl[h]] = uw[:, 0:HEAD_DIM]
            wq[cs[u], 0:CHUNK, hsl[h]] = _bf(uw[:, HEAD_DIM:2 * HEAD_DIM])
            wq[cs[u], CHUNK:2 * CHUNK, hsl[h]] = _bf(ld(dq, i) * eg_col)
            kdec[rows[u], hsl[h]] = _bf(dkk[i] * jnp.exp(eg[u][CHUNK:2 * CHUNK, HEADS + h:HEADS + h + 1]))
        for u in range(PREP_UNROLL):
            gl[cs[u]] = jnp.exp(eg[u][CHUNK - 1:CHUNK, :])

    for j in range(n_chunks // PREP_UNROLL):
        prepare_chunks(j)

    for c in range(n_chunks):
        rows = slice(c * CHUNK, (c + 1) * CHUNK)
        s = [sdn[h] for h in heads]
        wqc = wq[c]
        zero_s = jnp.zeros((HEAD_DIM, HEAD_DIM), jnp.bfloat16)
        ws = []
        for h in range(0, HEADS, 2):
            s_bd = jnp.concatenate([jnp.concatenate([_bf(s[h]), zero_s], axis=1),
                                    jnp.concatenate([zero_s, _bf(s[h + 1])], axis=1)], axis=0)
            ws2 = _dot(wqc[:, h * HEAD_DIM:(h + 2) * HEAD_DIM], s_bd)
            ws += [ws2[:, 0:HEAD_DIM], ws2[:, HEAD_DIM:2 * HEAD_DIM]]
        st = [shg[h] for h in heads]
        dec = hdec[c]
        for h in heads:
            oa[rows, hsl[h]] = oa[rows, hsl[h]] + _dot_nt(qin[rows, hsl[h]], _bf(st[h]))
            shg[h] = st[h] * dec[:, hsl[h]] + kvinc[c * HEADS + h]
        issue_gate_tiles(-(-len(gate_tiles_todo) // (2 * (n_chunks - c))))
        vnb = [_bf(du[rows, hsl[h]] - ws[h][0:CHUNK]) for h in heads]
        glc = gl[c]
        zero_v = jnp.zeros((CHUNK, HEAD_DIM), jnp.bfloat16)
        for h in range(0, HEADS, 2):
            v_bd = jnp.concatenate([jnp.concatenate([vnb[h], zero_v], axis=1),
                                    jnp.concatenate([zero_v, vnb[h + 1]], axis=1)], axis=0)
            intra = _dot(qkb[c * (HEADS // 2) + h // 2], v_bd)
            ob[rows, hsl[h]] = ws[h][CHUNK:2 * CHUNK] + intra[:, 0:HEAD_DIM]
            ob[rows, hsl[h + 1]] = ws[h + 1][CHUNK:2 * CHUNK] + intra[:, HEAD_DIM:2 * HEAD_DIM]
        for h in heads:
            sdn[h] = s[h] * glc[:, HEADS + h:HEADS + h + 1] + _dot_tn(kdec[rows, hsl[h]], vnb[h])
        issue_gate_tiles(-(-len(gate_tiles_todo) // (2 * (n_chunks - c) - 1)))

    def gated_norm(o_ref, w_row, gate):
        parts = []
        for h in range(HEADS):
            oh = o_ref[:, h * HEAD_DIM:(h + 1) * HEAD_DIM]
            parts.append(oh * lax.rsqrt(jnp.mean(oh * oh, axis=-1, keepdims=True) + RMS_EPS))
        return jnp.concatenate(parts, axis=1) * w_row * gate

    na = gated_norm(oa, hgnw_ref[...], sg[...])
    nb_ = gated_norm(ob, dnnw_ref[...], sz[...])
    merged = sga[...] * _dot(_bf(na), wa_ref[...]) + sgb[...] * _dot(_bf(nb_), wb_ref[...])
    mix = _dot(_bf(merged), wo_ref[...])
    out_ref[...] = _layer_norm(ALPHA * x_ref[...] + mix, g1_ref[...], b1_ref[...])

    xnbuf[...] = _bf(xnext_ref[...])
    project_head(0, xnbuf)


def _mlp_kernel(h_ref, wup_ref, wdown_ref, g2_ref, b2_ref, out_ref):
    tm = h_ref.shape[0]
    hb = _bf(h_ref[...])
    n_ff = D_FF // FF_BLOCK
    assert n_ff >= 2 and tm % MLP_TAIL_SPLIT == 0
    acc = None
    for j in range(n_ff - 1):
        up = _dot(hb, wup_ref[:, j * FF_BLOCK:(j + 1) * FF_BLOCK])
        act = jnp.square(jnp.maximum(up, 0.0))
        part = _dot(_bf(act), wdown_ref[j * FF_BLOCK:(j + 1) * FF_BLOCK, :])
        acc = part if acc is None else acc + part
    up = _dot(hb, wup_ref[:, (n_ff - 1) * FF_BLOCK:n_ff * FF_BLOCK])
    actb = _bf(jnp.square(jnp.maximum(up, 0.0)))
    for r in range(MLP_TAIL_SPLIT):
        rows = slice(r * tm // MLP_TAIL_SPLIT, (r + 1) * tm // MLP_TAIL_SPLIT)
        part = _dot(actb[rows], wdown_ref[(n_ff - 1) * FF_BLOCK:n_ff * FF_BLOCK, :])
        out_ref[rows, :] = _layer_norm(ALPHA * h_ref[rows, :] + acc[rows] + part, g2_ref[...], b2_ref[...])


def _resident(shape):
    nd = len(shape)
    return pl.BlockSpec(shape, lambda *_: (0,) * nd, pipeline_mode=pl.Buffered(1))


def _mixer_call(x, wup, wdown, lbl, win, wgate, wab, convw, hgnw, alogr, dtbr, alogc, dtbc, dnnw,
                wa, wb, wo, g1, b1):
    bsz, seq, _ = x.shape
    tb = min(TOKEN_BLOCK, seq)
    assert seq % tb == 0 and tb % (CHUNK * PREP_UNROLL) == 0
    n_chunks = tb // CHUNK
    wsum = jnp.asarray(_WSUM, jnp.bfloat16)
    masks = jnp.asarray(np.concatenate([_MASKS, _MASKS], axis=1), jnp.float32)
    tt = np.arange(tb)
    bdtriu = jnp.asarray(((tt[:, None] <= tt[None, :]) & ((tt[:, None] // CHUNK) == (tt[None, :] // CHUNK))),
                         jnp.bfloat16)
    consts = (lbl, win, wgate, wab, convw, hgnw, alogr, dtbr, alogc, dtbc, dnnw, wa, wb, wo,
              g1, b1, wsum, masks, bdtriu, bdtriu.T)
    f32 = jnp.float32
    blk = functools.partial(pltpu.VMEM, (tb, KW))
    bf16 = jnp.bfloat16
    per_head = functools.partial(pltpu.VMEM, (HEADS, tb, HEAD_DIM))
    scratch = [pltpu.VMEM((tb, D_MODEL), bf16), pltpu.VMEM((tb, D_MODEL), bf16)]
    scratch += [per_head(f32) for _ in range(9)]
    scratch += [blk(f32) for _ in range(3)]
    scratch += [pltpu.VMEM((tb, LANES), f32),
                pltpu.VMEM((n_chunks, 8, CHUNK), f32),
                pltpu.VMEM((HEADS, tb + CONV_PAD, 3 * HEAD_DIM), f32),
                pltpu.VMEM((HEADS, tb, 3 * HEAD_DIM), f32),
                blk(bf16), blk(bf16),
                pltpu.VMEM((n_chunks, 2 * CHUNK, KW), bf16),
                pltpu.VMEM((n_chunks * HEADS // 2, CHUNK, 2 * CHUNK), bf16),
                pltpu.VMEM((n_chunks * HEADS, HEAD_DIM, HEAD_DIM), f32),
                pltpu.VMEM((n_chunks, 1, KW), f32),
                pltpu.VMEM((n_chunks, 1, LANES), f32),
                blk(f32), blk(f32),
                pltpu.VMEM((tb, D_MODEL), f32),
                pltpu.VMEM((tb, D_MODEL), f32),
                pltpu.VMEM((HEADS, HEAD_DIM, HEAD_DIM), f32),
                pltpu.VMEM((HEADS, HEAD_DIM, HEAD_DIM), f32)]
    n_blocks = seq // tb

    def next_block(b, i):
        flat = jnp.minimum(b * n_blocks + i + 1, bsz * n_blocks - 1)
        return flat // n_blocks, flat % n_blocks, 0

    n_steps = bsz * n_blocks
    up_rows, down_rows = wup.shape[0] // n_steps, wdown.shape[0] // n_steps
    assert up_rows * n_steps == wup.shape[0] and down_rows * n_steps == wdown.shape[0]
    assert up_rows % BF16_ROWS == 0 and down_rows % BF16_ROWS == 0

    def step_rows(b, i):
        return b * n_blocks + i, 0

    return pl.pallas_call(
        _mixer_kernel,
        grid=(bsz, n_blocks),
        in_specs=[pl.BlockSpec((None, tb, D_MODEL), lambda b, i: (b, i, 0)),
                  pl.BlockSpec((None, tb, D_MODEL), next_block),
                  pl.BlockSpec((up_rows, wup.shape[1]), step_rows),
                  pl.BlockSpec((down_rows, wdown.shape[1]), step_rows)]
                 + [_resident(c.shape) for c in consts],
        out_specs=[pl.BlockSpec((None, tb, D_MODEL), lambda b, i: (b, i, 0)),
                   pl.BlockSpec((up_rows, wup.shape[1]), step_rows),
                   pl.BlockSpec((down_rows, wdown.shape[1]), step_rows)],
        out_shape=[jax.ShapeDtypeStruct(x.shape, f32),
                   jax.ShapeDtypeStruct(wup.shape, bf16),
                   jax.ShapeDtypeStruct(wdown.shape, bf16)],
        scratch_shapes=scratch,
        compiler_params=pltpu.CompilerParams(dimension_semantics=("arbitrary", "arbitrary"),
                                             vmem_limit_bytes=VMEM_LIMIT_BYTES),
        name="token_mixer",
    )(x, x, wup, wdown, *consts)


def _mlp_call(h, wup, wdown, g2, b2):
    m = h.shape[0]
    tm = min(MLP_BLOCK, m)
    assert m % tm == 0
    consts = (wup, wdown, g2, b2)
    return pl.pallas_call(
        _mlp_kernel,
        grid=(m // tm,),
        in_specs=[pl.BlockSpec((tm, D_MODEL), lambda i: (i, 0))] + [_resident(c.shape) for c in consts],
        out_specs=pl.BlockSpec((tm, D_MODEL), lambda i: (i, 0)),
        out_shape=jax.ShapeDtypeStruct(h.shape, jnp.float32),
        compiler_params=pltpu.CompilerParams(dimension_semantics=("arbitrary",),
                                             vmem_limit_bytes=VMEM_LIMIT_BYTES),
        name="relu2_mlp",
    )(h, *consts)


def kernel(x, hg_lb_logits, w_in, conv_w, hg_norm_w, dn_A_log, dn_dt_bias, dn_norm_w, w_branch_a, w_branch_b, w_o, ln1_g, ln1_b, w_up, w_down, ln2_g, ln2_b):
    bsz, seq, _ = x.shape
    f32 = jnp.float32
    bf16 = jnp.bfloat16
    o = _OFFS
    l = LAYER
    win = w_in[l].astype(bf16)
    wgate = win[:, o[8]:o[10]]
    wab = jnp.pad(win[:, o[6]:o[8]], ((0, 0), (0, LANES - 2 * HEADS)))
    zeros4 = jnp.zeros((HEADS,), f32)
    alog8 = jnp.concatenate([zeros4, dn_A_log[l].astype(f32)])
    dtb8 = jnp.concatenate([zeros4, dn_dt_bias[l].astype(f32)])
    alogr = jnp.pad(alog8, (0, LANES - 2 * HEADS)).reshape(1, LANES)
    dtbr = jnp.pad(dtb8, (0, LANES - 2 * HEADS)).reshape(1, LANES)
    h1, wup_bf, wdown_bf = _mixer_call(
        x, w_up[l], w_down[l], hg_lb_logits.astype(f32), win, wgate, wab, conv_w[l].astype(f32),
        hg_norm_w[l].reshape(1, KW), alogr, dtbr, alog8.reshape(8, 1), dtb8.reshape(8, 1),
        jnp.tile(dn_norm_w[l], HEADS).reshape(1, KW),
        w_branch_a[l].astype(bf16), w_branch_b[l].astype(bf16), w_o[l].astype(bf16),
        ln1_g[l].reshape(1, D_MODEL), ln1_b[l].reshape(1, D_MODEL))
    out = _mlp_call(h1.reshape(bsz * seq, D_MODEL), wup_bf, wdown_bf,
                    ln2_g[l].reshape(1, D_MODEL), ln2_b[l].reshape(1, D_MODEL))
    return out.reshape(bsz, seq, D_MODEL)
```

```python
import functools

import numpy as np
import jax
import jax.numpy as jnp
from jax import lax
from jax.experimental import pallas as pl
from jax.experimental.pallas import tpu as pltpu

D_MODEL = 1024
HEADS = 4
HEAD_DIM = 128
KW = HEADS * HEAD_DIM
CHUNK = 64
CONV_W = 4
D_FF = 4 * D_MODEL
DEPTH = 1
LAYER = 0
ALPHA = (2 * DEPTH) ** 0.25
LN_EPS = 1e-5
RMS_EPS = 1e-6
L2_EPS = 1e-6
QK_SCALE = HEAD_DIM ** -0.5
NEG_LOG2E = -1.4426950408889634

TOKEN_BLOCK = 256
PREP_UNROLL = 4
GATE_TILE = 256
GATE_TILES_AFTER_PROJECTIONS = 4
MLP_BLOCK = 1024
MLP_TAIL_SPLIT = 4
FF_BLOCK = 1024
LANES = 128
SUBLANES = 8
BF16_ROWS = 16
CONV_PAD = 8
VMEM_LIMIT_BYTES = 56 * 1024 * 1024

_SIZES = (KW, KW, KW, KW, 3 * KW, KW, HEADS, HEADS, D_MODEL, D_MODEL)
_OFFS = np.concatenate([[0], np.cumsum(_SIZES)]).tolist()

COL_HG_QFI = _OFFS[0]
COL_HG_G = _OFFS[3]
COL_DN_QKV = _OFFS[4]
COL_DN_Z = _OFFS[5]

_LEVELS = (32, 16, 8, 4, 2, 1)


def _build_constants():
    t = np.arange(CHUNK)[:, None]
    r = np.arange(CHUNK)[None, :]
    wsum = np.concatenate([r <= t, r > t], 0).astype(np.float32)
    masks = [((t // (2 * m)) == (r // (2 * m))) & ((t % (2 * m)) >= m) & ((r % (2 * m)) < m) for m in _LEVELS]
    masks.append(t == r)
    masks = np.concatenate(masks, 0).astype(np.float32)
    return wsum, masks


_WSUM, _MASKS = _build_constants()


def _dot(a, b):
    return jnp.dot(a, b, preferred_element_type=jnp.float32)


def _dot_nt(a, b):
    return lax.dot_general(a, b, (((1,), (1,)), ((), ())), preferred_element_type=jnp.float32)


def _dot_tn(a, b):
    return lax.dot_general(a, b, (((0,), (0,)), ((), ())), preferred_element_type=jnp.float32)


def _bf(x):
    return x.astype(jnp.bfloat16)


def _split(x):
    hi = _bf(x)
    return hi, _bf(x - hi.astype(jnp.float32))


def _sigmoid(x):
    return 1.0 / (1.0 + jnp.exp2(x * NEG_LOG2E))


def _silu(x):
    return x * _sigmoid(x)


def _softplus(x):
    return jnp.maximum(x, 0.0) + jnp.log1p(jnp.exp(-jnp.abs(x)))


def _layer_norm(x, g, b):
    mu = jnp.mean(x, axis=-1, keepdims=True)
    xc = x - mu
    var = jnp.mean(xc * xc, axis=-1, keepdims=True)
    return xc * lax.rsqrt(var + LN_EPS) * g + b


def _mixer_kernel(x_ref, xnext_ref, wup32_ref, wdown32_ref, lbl_ref, win_ref, wgate_ref, wab_ref, convw_ref, hgnw_ref,
                  alogr_ref, dtbr_ref, alogc_ref, dtbc_ref, dnnw_ref, wa_ref, wb_ref, wo_ref,
                  g1_ref, b1_ref, wsum_ref, masks_ref, bdtriu_ref, bdtril_ref, out_ref, wupb_ref, wdownb_ref,
                  xbuf, xnbuf, hq, hk, hv, hlf, hf, hb, dq, dk, dv, oa, ob, du, gbuf, gcrow, convbuf, rawhg, qin, kdec, wq, qkb,
                  kvinc, hdec, gl, sg, sz, sga, sgb, shg, sdn):
    tb = x_ref.shape[0]
    n_chunks = tb // CHUNK

    wupb_ref[...] = _bf(wup32_ref[...])
    wdownb_ref[...] = _bf(wdown32_ref[...])

    @pl.when(pl.program_id(1) == 0)
    def _reset_carries():
        shg[...] = jnp.zeros_like(shg)
        sdn[...] = jnp.zeros_like(sdn)
        convbuf[:, 0:CONV_PAD, :] = jnp.zeros((HEADS, CONV_PAD, 3 * HEAD_DIM), jnp.float32)

    def project_head(c, xsrc=xbuf):
        cols = ([COL_DN_QKV + k * KW + c * HEAD_DIM for k in range(3)]
                + [COL_HG_QFI + k * KW + c * HEAD_DIM for k in range(3)])
        w_head = jnp.concatenate([win_ref[:, o:o + HEAD_DIM] for o in cols], axis=1)
        raw = _dot(xsrc[...], w_head)
        convbuf[c, CONV_PAD:CONV_PAD + tb, :] = raw[:, 0:3 * HEAD_DIM]
        rawhg[c] = raw[:, 3 * HEAD_DIM:6 * HEAD_DIM]

    def head_epilogue(c):
        cw = jnp.concatenate([convw_ref[:, k * KW + c * HEAD_DIM:k * KW + (c + 1) * HEAD_DIM]
                              for k in range(3)], axis=1)
        base = CONV_PAD - (CONV_W - 1)
        acc = convbuf[c, base:base + tb, :] * cw[0:1, :]
        for j in range(1, CONV_W):
            acc = acc + convbuf[c, base + j:base + j + tb, :] * cw[j:j + 1, :]
        convbuf[c, 0:CONV_PAD, :] = convbuf[c, tb:tb + CONV_PAD, :]
        qkv = _silu(acc)
        qh = qkv[:, 0:HEAD_DIM]
        kh = qkv[:, HEAD_DIM:2 * HEAD_DIM]
        dq[c] = qh * (lax.rsqrt(jnp.sum(qh * qh, axis=-1, keepdims=True) + L2_EPS) * QK_SCALE)
        dk[c] = kh * lax.rsqrt(jnp.sum(kh * kh, axis=-1, keepdims=True) + L2_EPS)
        dv[c] = qkv[:, 2 * HEAD_DIM:3 * HEAD_DIM]
        lg = lbl_ref[:, c * HEAD_DIM:(c + 1) * HEAD_DIM]
        e = jnp.exp(lg - jnp.max(lg, axis=0, keepdims=True))
        lb = e[0:1, :] / jnp.sum(e, axis=0, keepdims=True)
        f = lb + (1.0 - lb) * _sigmoid(rawhg[c, :, HEAD_DIM:2 * HEAD_DIM])
        hlf[c] = jnp.log(f)
        hf[c] = f
        hk[c] = 1.0 - f
        hq[c] = _silu(rawhg[c, :, 0:HEAD_DIM]) * QK_SCALE
        hv[c] = rawhg[c, :, 2 * HEAD_DIM:3 * HEAD_DIM]

    gate_tiles = ([(win_ref, COL_HG_G + t, sg, t, _silu) for t in range(0, KW, GATE_TILE)]
                  + [(win_ref, COL_DN_Z + t, sz, t, _silu) for t in range(0, KW, GATE_TILE)]
                  + [(wgate_ref, t, sga, t, _sigmoid) for t in range(0, D_MODEL, GATE_TILE)]
                  + [(wgate_ref, D_MODEL + t, sgb, t, _sigmoid) for t in range(0, D_MODEL, GATE_TILE)])
    gate_tiles_todo = list(range(len(gate_tiles)))

    def issue_gate_tiles(count):
        for _ in range(min(count, len(gate_tiles_todo))):
            w_ref, src, dst, off, act = gate_tiles[gate_tiles_todo.pop(0)]
            dst[:, off:off + GATE_TILE] = act(_dot(xbuf[...], w_ref[:, src:src + GATE_TILE]))

    @pl.when((pl.program_id(0) == 0) & (pl.program_id(1) == 0))
    def _first_projection():
        xnbuf[...] = _bf(x_ref[...])
        project_head(0, xnbuf)

    xbuf[...] = xnbuf[...]

    def cumulate_forget(h0):
        lf_hi, lf_lo = _split(jnp.concatenate([hlf[h0], hlf[h0 + 1]], axis=1))
        bc = _dot(bdtril_ref[...], lf_hi) + _dot(bdtril_ref[...], lf_lo)
        hb[h0] = bc[:, 0:HEAD_DIM]
        hb[h0 + 1] = bc[:, HEAD_DIM:2 * HEAD_DIM]

    for c in range(1, HEADS):
        head_epilogue(c - 1)
        project_head(c)
        if c % 2 == 0:
            cumulate_forget(c - 2)
    head_epilogue(HEADS - 1)
    issue_gate_tiles(GATE_TILES_AFTER_PROJECTIONS)
    cumulate_forget(HEADS - 2)

    pab = _dot(xbuf[...], wab_ref[...])
    g_col = -jnp.exp(alogr_ref[...]) * _softplus(pab + dtbr_ref[...])
    lane = lax.broadcasted_iota(jnp.int32, pab.shape, 1)
    gbuf[...] = jnp.where(lane < HEADS, _sigmoid(pab), g_col)
    pabt = jnp.transpose(pab)[0:2 * HEADS, :]
    g_row = -jnp.exp(alogc_ref[...]) * _softplus(pabt + dtbc_ref[...])
    gr_hi, gr_lo = _split(g_row)
    gc_row = _dot(gr_hi, bdtriu_ref[...]) + _dot(gr_lo, bdtriu_ref[...])
    for c in range(n_chunks):
        gcrow[c] = gc_row[:, c * CHUNK:(c + 1) * CHUNK]

    ri = lax.broadcasted_iota(jnp.int32, (CHUNK, CHUNK), 0)
    ci = lax.broadcasted_iota(jnp.int32, (CHUNK, CHUNK), 1)
    causal = jnp.where(ri >= ci, 1.0, 0.0)
    strict = jnp.where(ri > ci, 1.0, 0.0)
    trow = lax.broadcasted_iota(jnp.int32, (CHUNK, HEAD_DIM), 0)

    n_lev = len(_LEVELS)
    heads = range(HEADS)
    hsl = [slice(h * HEAD_DIM, (h + 1) * HEAD_DIM) for h in heads]

    def level_mask(l):
        return masks_ref[l * CHUNK:(l + 1) * CHUNK, 0:CHUNK]

    def level_mask2(l):
        return masks_ref[l * CHUNK:(l + 1) * CHUNK, :]

    units = [(u, h) for u in range(PREP_UNROLL) for h in heads]
    n_units = range(len(units))

    def prepare_chunks(j):
        cs = [j * PREP_UNROLL + u for u in range(PREP_UNROLL)]
        rows = [slice(c * CHUNK, (c + 1) * CHUNK) for c in cs]
        wsum = wsum_ref[...]
        gb, eg, gcr = [], [], []
        for u in range(PREP_UNROLL):
            gb.append(gbuf[rows[u], :])
            g_hi, g_lo = _split(gb[u])
            eg.append(_dot(wsum, g_hi) + _dot(wsum, g_lo))
            gcr.append(gcrow[cs[u]])

        def ld(ref, i):
            u, h = units[i]
            return ref[h, rows[u], :]

        dkk = [ld(dk, i) for i in n_units]
        dkb = [_bf(k) for k in dkk]
        beta = [gb[u][:, h:h + 1] for u, h in units]
        gcol = [eg[u][0:CHUNK, HEADS + h:HEADS + h + 1] for u, h in units]
        decay = [causal * jnp.exp(jnp.minimum(gcol[i] - gcr[u][HEADS + h:HEADS + h + 1, :], 0.0))
                 for i, (u, h) in enumerate(units)]
        kbeta = [dkk[i] * beta[i] for i in n_units]
        gram = [_dot_nt(jnp.concatenate([_bf(kbeta[i]), _bf(ld(dq, i))], axis=0), dkb[i]) for i in n_units]
        lmat = [gram[i][0:CHUNK] * (strict * decay[i]) for i in n_units]
        qk = [gram[i][CHUNK:2 * CHUNK] * decay[i] for i in n_units]
        for i, (u, h) in enumerate(units):
            if h % 2 == 0:
                qkb[cs[u] * (HEADS // 2) + h // 2] = _bf(jnp.concatenate([qk[i], qk[i + 1]], axis=1))

        hqv = [ld(hq, i) for i in n_units]
        hkv = [ld(hk, i) for i in n_units]
        hqb = [_bf(q) for q in hqv]
        hkb = [_bf(k) for k in hkv]
        sc = [level_mask(n_lev) * _dot_nt(hqb[i], hkb[i]) for i in n_units]
        pairs = [(i, i + 1) for i in range(0, len(units), 2)]
        lane2 = lax.broadcasted_iota(jnp.int32, (CHUNK, 2 * CHUNK), 1)
        first_f = jnp.where(lane2 < CHUNK, 1.0, 0.0)
        second_f = 1.0 - first_f

        def block_diag(m2, lo, hi):
            return jnp.concatenate([_bf(m2 * lo), _bf(m2 * hi)], axis=0)

        lmat2 = [jnp.concatenate([lmat[a], lmat[b]], axis=1) for a, b in pairs]
        lbd = [block_diag(m2, first_f, second_f) for m2 in lmat2]
        nmat2 = [-(level_mask2(n_lev - 1) * m2) for m2 in lmat2]
        first_b, second_b = _bf(first_f), _bf(second_f)

        hbv = [ld(hb, i) for i in n_units]

        def level_factor(i, l):
            m = _LEVELS[l]
            if m >= SUBLANES // 2:
                ref = jnp.concatenate(
                    [jnp.broadcast_to(hbv[i][r:r + 1, :], (SUBLANES, HEAD_DIM))
                     for r in [(g * SUBLANES // (2 * m)) * (2 * m) + m - 1 for g in range(CHUNK // SUBLANES)]],
                    axis=0)
                return jnp.exp2(jnp.abs(hbv[i] - ref) * NEG_LOG2E)
            f = ld(hf, i)
            if m == 1:
                return jnp.where(trow % 2 == 1, f, 1.0)
            f8 = f.reshape(CHUNK // SUBLANES, SUBLANES, HEAD_DIM)
            f_prev = pltpu.roll(f8, 1, 1).reshape(CHUNK, HEAD_DIM)
            f_next = pltpu.roll(f8, SUBLANES - 1, 1).reshape(CHUNK, HEAD_DIM)
            return jnp.where(trow % 4 == 0, f_next,
                             jnp.where(trow % 4 == 1, 1.0, jnp.where(trow % 4 == 2, f, f_prev * f)))

        def score_level(l):
            m = _LEVELS[l]
            for i in n_units:
                zb = _bf(level_factor(i, l))
                d = _dot_nt(hqb[i] * zb, hkb[i] * zb)
                if m < SUBLANES:
                    sc[i] = sc[i] + level_mask(l) * d
                    continue
                groups = []
                for g in range(CHUNK // SUBLANES):
                    r = slice(g * SUBLANES, (g + 1) * SUBLANES)
                    is_query = (g * SUBLANES) % (2 * m) >= m
                    groups.append(sc[i][r] + level_mask(l)[r] * d[r] if is_query else sc[i][r])
                sc[i] = jnp.concatenate(groups, axis=0)

        score_level(n_lev - 1)
        n_pairs = range(len(pairs))
        for l in range(n_lev - 2, -1, -1):
            nb = [_bf(nmat2[p]) for p in n_pairs]
            xm = [level_mask2(l) * (lmat2[p] + _dot(nb[p], lbd[p])) for p in n_pairs]
            score_level(l)
            for p in n_pairs:
                nbd = jnp.concatenate([nb[p] * first_b, nb[p] * second_b], axis=0)
                nmat2[p] = nmat2[p] - (xm[p] + _dot(_bf(xm[p]), nbd))

        hvb = [_bf(ld(hv, i)) for i in n_units]
        for i, (u, h) in enumerate(units):
            oa[rows[u], hsl[h]] = _dot(_bf(sc[i]), hvb[i])
        for i, (u, h) in enumerate(units):
            kd = hkv[i] * jnp.exp(hbv[i][CHUNK - 1:CHUNK, :] - hbv[i])
            kvinc[cs[u] * HEADS + h] = _dot_tn(hvb[i], _bf(kd))
        for i, (u, h) in enumerate(units):
            qin[rows[u], hsl[h]] = _bf(hqv[i] * jnp.exp(hbv[i]))
            hdec[cs[u], :, hsl[h]] = jnp.exp(hbv[i][CHUNK - 1:CHUNK, :])

        for i, (u, h) in enumerate(units):
            eg_col = jnp.exp(gcol[i])
            rhs = jnp.concatenate([ld(dv, i) * beta[i], kbeta[i] * eg_col], axis=1)
            zero_rhs = jnp.zeros(rhs.shape, jnp.bfloat16)
            rhs_pad = jnp.concatenate([_bf(rhs), zero_rhs] if i % 2 == 0 else [zero_rhs, _bf(rhs)], axis=0)
            uw = rhs + _dot(_bf(nmat2[i // 2]), rhs_pad)
            du[rows[u], hsl[h]] = uw[:, 0:HEAD_DIM]
            wq[cs[u], 0:CHUNK, hsl[h]] = _bf(uw[:, HEAD_DIM:2 * HEAD_DIM])
            wq[cs[u], CHUNK:2 * CHUNK, hsl[h]] = _bf(ld(dq, i) * eg_col)
            kdec[rows[u], hsl[h]] = _bf(dkk[i] * jnp.exp(eg[u][CHUNK:2 * CHUNK, HEADS + h:HEADS + h + 1]))
        for u in range(PREP_UNROLL):
            gl[cs[u]] = jnp.exp(eg[u][CHUNK - 1:CHUNK, :])

    for j in range(n_chunks // PREP_UNROLL):
        prepare_chunks(j)

    for c in range(n_chunks):
        rows = slice(c * CHUNK, (c + 1) * CHUNK)
        s = [sdn[h] for h in heads]
        wqc = wq[c]
        zero_s = jnp.zeros((HEAD_DIM, HEAD_DIM), jnp.bfloat16)
        ws = []
        for h in range(0, HEADS, 2):
            s_bd = jnp.concatenate([jnp.concatenate([_bf(s[h]), zero_s], axis=1),
                                    jnp.concatenate([zero_s, _bf(s[h + 1])], axis=1)], axis=0)
            ws2 = _dot(wqc[:, h * HEAD_DIM:(h + 2) * HEAD_DIM], s_bd)
            ws += [ws2[:, 0:HEAD_DIM], ws2[:, HEAD_DIM:2 * HEAD_DIM]]
        st = [shg[h] for h in heads]
        dec = hdec[c]
        for h in heads:
            oa[rows, hsl[h]] = oa[rows, hsl[h]] + _dot_nt(qin[rows, hsl[h]], _bf(st[h]))
            shg[h] = st[h] * dec[:, hsl[h]] + kvinc[c * HEADS + h]
        issue_gate_tiles(-(-len(gate_tiles_todo) // (2 * (n_chunks - c))))
        vnb = [_bf(du[rows, hsl[h]] - ws[h][0:CHUNK]) for h in heads]
        glc = gl[c]
        zero_v = jnp.zeros((CHUNK, HEAD_DIM), jnp.bfloat16)
        for h in range(0, HEADS, 2):
            v_bd = jnp.concatenate([jnp.concatenate([vnb[h], zero_v], axis=1),
                                    jnp.concatenate([zero_v, vnb[h + 1]], axis=1)], axis=0)
            intra = _dot(qkb[c * (HEADS // 2) + h // 2], v_bd)
            ob[rows, hsl[h]] = ws[h][CHUNK:2 * CHUNK] + intra[:, 0:HEAD_DIM]
            ob[rows, hsl[h + 1]] = ws[h + 1][CHUNK:2 * CHUNK] + intra[:, HEAD_DIM:2 * HEAD_DIM]
        for h in heads:
            sdn[h] = s[h] * glc[:, HEADS + h:HEADS + h + 1] + _dot_tn(kdec[rows, hsl[h]], vnb[h])
        issue_gate_tiles(-(-len(gate_tiles_todo) // (2 * (n_chunks - c) - 1)))

    def gated_norm(o_ref, w_row, gate):
        parts = []
        for h in range(HEADS):
            oh = o_ref[:, h * HEAD_DIM:(h + 1) * HEAD_DIM]
            parts.append(oh * lax.rsqrt(jnp.mean(oh * oh, axis=-1, keepdims=True) + RMS_EPS))
        return jnp.concatenate(parts, axis=1) * w_row * gate

    na = gated_norm(oa, hgnw_ref[...], sg[...])
    nb_ = gated_norm(ob, dnnw_ref[...], sz[...])
    merged = sga[...] * _dot(_bf(na), wa_ref[...]) + sgb[...] * _dot(_bf(nb_), wb_ref[...])
    mix = _dot(_bf(merged), wo_ref[...])
    out_ref[...] = _layer_norm(ALPHA * x_ref[...] + mix, g1_ref[...], b1_ref[...])

    xnbuf[...] = _bf(xnext_ref[...])
    project_head(0, xnbuf)


def _mlp_kernel(h_ref, wup_ref, wdown_ref, g2_ref, b2_ref, out_ref):
    tm = h_ref.shape[0]
    hb = _bf(h_ref[...])
    n_ff = D_FF // FF_BLOCK
    assert n_ff >= 2 and tm % MLP_TAIL_SPLIT == 0
    acc = None
    for j in range(n_ff - 1):
        up = _dot(hb, wup_ref[:, j * FF_BLOCK:(j + 1) * FF_BLOCK])
        act = jnp.square(jnp.maximum(up, 0.0))
        part = _dot(_bf(act), wdown_ref[j * FF_BLOCK:(j + 1) * FF_BLOCK, :])
        acc = part if acc is None else acc + part
    up = _dot(hb, wup_ref[:, (n_ff - 1) * FF_BLOCK:n_ff * FF_BLOCK])
    actb = _bf(jnp.square(jnp.maximum(up, 0.0)))
    for r in range(MLP_TAIL_SPLIT):
        rows = slice(r * tm // MLP_TAIL_SPLIT, (r + 1) * tm // MLP_TAIL_SPLIT)
        part = _dot(actb[rows], wdown_ref[(n_ff - 1) * FF_BLOCK:n_ff * FF_BLOCK, :])
        out_ref[rows, :] = _layer_norm(ALPHA * h_ref[rows, :] + acc[rows] + part, g2_ref[...], b2_ref[...])


def _resident(shape):
    nd = len(shape)
    return pl.BlockSpec(shape, lambda *_: (0,) * nd, pipeline_mode=pl.Buffered(1))


def _mixer_call(x, wup, wdown, lbl, win, wgate, wab, convw, hgnw, alogr, dtbr, alogc, dtbc, dnnw,
                wa, wb, wo, g1, b1):
    bsz, seq, _ = x.shape
    tb = min(TOKEN_BLOCK, seq)
    assert seq % tb == 0 and tb % (CHUNK * PREP_UNROLL) == 0
    n_chunks = tb // CHUNK
    wsum = jnp.asarray(_WSUM, jnp.bfloat16)
    masks = jnp.asarray(np.concatenate([_MASKS, _MASKS], axis=1), jnp.float32)
    tt = np.arange(tb)
    bdtriu = jnp.asarray(((tt[:, None] <= tt[None, :]) & ((tt[:, None] // CHUNK) == (tt[None, :] // CHUNK))),
                         jnp.bfloat16)
    consts = (lbl, win, wgate, wab, convw, hgnw, alogr, dtbr, alogc, dtbc, dnnw, wa, wb, wo,
              g1, b1, wsum, masks, bdtriu, bdtriu.T)
    f32 = jnp.float32
    blk = functools.partial(pltpu.VMEM, (tb, KW))
    bf16 = jnp.bfloat16
    per_head = functools.partial(pltpu.VMEM, (HEADS, tb, HEAD_DIM))
    scratch = [pltpu.VMEM((tb, D_MODEL), bf16), pltpu.VMEM((tb, D_MODEL), bf16)]
    scratch += [per_head(f32) for _ in range(9)]
    scratch += [blk(f32) for _ in range(3)]
    scratch += [pltpu.VMEM((tb, LANES), f32),
                pltpu.VMEM((n_chunks, 8, CHUNK), f32),
                pltpu.VMEM((HEADS, tb + CONV_PAD, 3 * HEAD_DIM), f32),
                pltpu.VMEM((HEADS, tb, 3 * HEAD_DIM), f32),
                blk(bf16), blk(bf16),
                pltpu.VMEM((n_chunks, 2 * CHUNK, KW), bf16),
                pltpu.VMEM((n_chunks * HEADS // 2, CHUNK, 2 * CHUNK), bf16),
                pltpu.VMEM((n_chunks * HEADS, HEAD_DIM, HEAD_DIM), f32),
                pltpu.VMEM((n_chunks, 1, KW), f32),
                pltpu.VMEM((n_chunks, 1, LANES), f32),
                blk(f32), blk(f32),
                pltpu.VMEM((tb, D_MODEL), f32),
                pltpu.VMEM((tb, D_MODEL), f32),
                pltpu.VMEM((HEADS, HEAD_DIM, HEAD_DIM), f32),
                pltpu.VMEM((HEADS, HEAD_DIM, HEAD_DIM), f32)]
    n_blocks = seq // tb

    def next_block(b, i):
        flat = jnp.minimum(b * n_blocks + i + 1, bsz * n_blocks - 1)
        return flat // n_blocks, flat % n_blocks, 0

    n_steps = bsz * n_blocks
    up_rows, down_rows = wup.shape[0] // n_steps, wdown.shape[0] // n_steps
    assert up_rows * n_steps == wup.shape[0] and down_rows * n_steps == wdown.shape[0]
    assert up_rows % BF16_ROWS == 0 and down_rows % BF16_ROWS == 0

    def step_rows(b, i):
        return b * n_blocks + i, 0

    return pl.pallas_call(
        _mixer_kernel,
        grid=(bsz, n_blocks),
        in_specs=[pl.BlockSpec((None, tb, D_MODEL), lambda b, i: (b, i, 0)),
                  pl.BlockSpec((None, tb, D_MODEL), next_block),
                  pl.BlockSpec((up_rows, wup.shape[1]), step_rows),
                  pl.BlockSpec((down_rows, wdown.shape[1]), step_rows)]
                 + [_resident(c.shape) for c in consts],
        out_specs=[pl.BlockSpec((None, tb, D_MODEL), lambda b, i: (b, i, 0)),
                   pl.BlockSpec((up_rows, wup.shape[1]), step_rows),
                   pl.BlockSpec((down_rows, wdown.shape[1]), step_rows)],
        out_shape=[jax.ShapeDtypeStruct(x.shape, f32),
                   jax.ShapeDtypeStruct(wup.shape, bf16),
                   jax.ShapeDtypeStruct(wdown.shape, bf16)],
        scratch_shapes=scratch,
        compiler_params=pltpu.CompilerParams(dimension_semantics=("arbitrary", "arbitrary"),
                                             vmem_limit_bytes=VMEM_LIMIT_BYTES),
        name="token_mixer",
    )(x, x, wup, wdown, *consts)


def _mlp_call(h, wup, wdown, g2, b2):
    m = h.shape[0]
    tm = min(MLP_BLOCK, m)
    assert m % tm == 0
    consts = (wup, wdown, g2, b2)
    return pl.pallas_call(
        _mlp_kernel,
        grid=(m // tm,),
        in_specs=[pl.BlockSpec((tm, D_MODEL), lambda i: (i, 0))] + [_resident(c.shape) for c in consts],
        out_specs=pl.BlockSpec((tm, D_MODEL), lambda i: (i, 0)),
        out_shape=jax.ShapeDtypeStruct(h.shape, jnp.float32),
        compiler_params=pltpu.CompilerParams(dimension_semantics=("arbitrary",),
                                             vmem_limit_bytes=VMEM_LIMIT_BYTES),
        name="relu2_mlp",
    )(h, *consts)


def kernel(x, hg_lb_logits, w_in, conv_w, hg_norm_w, dn_A_log, dn_dt_bias, dn_norm_w, w_branch_a, w_branch_b, w_o, ln1_g, ln1_b, w_up, w_down, ln2_g, ln2_b):
    bsz, seq, _ = x.shape
    f32 = jnp.float32
    bf16 = jnp.bfloat16
    o = _OFFS
    l = LAYER
    win = w_in[l].astype(bf16)
    wgate = win[:, o[8]:o[10]]
    wab = jnp.pad(win[:, o[6]:o[8]], ((0, 0), (0, LANES - 2 * HEADS)))
    zeros4 = jnp.zeros((HEADS,), f32)
    alog8 = jnp.concatenate([zeros4, dn_A_log[l].astype(f32)])
    dtb8 = jnp.concatenate([zeros4, dn_dt_bias[l].astype(f32)])
    alogr = jnp.pad(alog8, (0, LANES - 2 * HEADS)).reshape(1, LANES)
    dtbr = jnp.pad(dtb8, (0, LANES - 2 * HEADS)).reshape(1, LANES)
    h1, wup_bf, wdown_bf = _mixer_call(
        x, w_up[l], w_down[l], hg_lb_logits.astype(f32), win, wgate, wab, conv_w[l].astype(f32),
        hg_norm_w[l].reshape(1, KW), alogr, dtbr, alog8.reshape(8, 1), dtb8.reshape(8, 1),
        jnp.tile(dn_norm_w[l], HEADS).reshape(1, KW),
        w_branch_a[l].astype(bf16), w_branch_b[l].astype(bf16), w_o[l].astype(bf16),
        ln1_g[l].reshape(1, D_MODEL), ln1_b[l].reshape(1, D_MODEL))
    out = _mlp_call(h1.reshape(bsz * seq, D_MODEL), wup_bf, wdown_bf,
                    ln2_g[l].reshape(1, D_MODEL), ln2_b[l].reshape(1, D_MODEL))
    return out.reshape(bsz, seq, D_MODEL)
```

```python
import functools

import numpy as np
import jax
import jax.numpy as jnp
from jax import lax
from jax.experimental import pallas as pl
from jax.experimental.pallas import tpu as pltpu

D_MODEL = 1024
HEADS = 4
HEAD_DIM = 128
KW = HEADS * HEAD_DIM
CHUNK = 64
CONV_W = 4
D_FF = 4 * D_MODEL
DEPTH = 1
LAYER = 0
ALPHA = (2 * DEPTH) ** 0.25
LN_EPS = 1e-5
RMS_EPS = 1e-6
L2_EPS = 1e-6
QK_SCALE = HEAD_DIM ** -0.5
NEG_LOG2E = -1.4426950408889634

TOKEN_BLOCK = 256
PREP_UNROLL = 4
GATE_TILE = 256
GATE_TILES_AFTER_PROJECTIONS = 4
MLP_BLOCK = 1024
MLP_TAIL_SPLIT = 4
FF_BLOCK = 1024
LANES = 128
SUBLANES = 8
BF16_ROWS = 16
CONV_PAD = 8
VMEM_LIMIT_BYTES = 56 * 1024 * 1024

_SIZES = (KW, KW, KW, KW, 3 * KW, KW, HEADS, HEADS, D_MODEL, D_MODEL)
_OFFS = np.concatenate([[0], np.cumsum(_SIZES)]).tolist()

COL_HG_QFI = _OFFS[0]
COL_HG_G = _OFFS[3]
COL_DN_QKV = _OFFS[4]
COL_DN_Z = _OFFS[5]

_LEVELS = (32, 16, 8, 4, 2, 1)


def _build_constants():
    t = np.arange(CHUNK)[:, None]
    r = np.arange(CHUNK)[None, :]
    wsum = np.concatenate([r <= t, r > t], 0).astype(np.float32)
    masks = [((t // (2 * m)) == (r // (2 * m))) & ((t % (2 * m)) >= m) & ((r % (2 * m)) < m) for m in _LEVELS]
    masks.append(t == r)
    masks = np.concatenate(masks, 0).astype(np.float32)
    return wsum, masks


_WSUM, _MASKS = _build_constants()


def _dot(a, b):
    return jnp.dot(a, b, preferred_element_type=jnp.float32)


def _dot_nt(a, b):
    return lax.dot_general(a, b, (((1,), (1,)), ((), ())), preferred_element_type=jnp.float32)


def _dot_tn(a, b):
    return lax.dot_general(a, b, (((0,), (0,)), ((), ())), preferred_element_type=jnp.float32)


def _bf(x):
    return x.astype(jnp.bfloat16)


def _split(x):
    hi = _bf(x)
    return hi, _bf(x - hi.astype(jnp.float32))


def _sigmoid(x):
    return 1.0 / (1.0 + jnp.exp2(x * NEG_LOG2E))


def _silu(x):
    return x * _sigmoid(x)


def _softplus(x):
    return jnp.maximum(x, 0.0) + jnp.log1p(jnp.exp(-jnp.abs(x)))


def _layer_norm(x, g, b):
    mu = jnp.mean(x, axis=-1, keepdims=True)
    xc = x - mu
    var = jnp.mean(xc * xc, axis=-1, keepdims=True)
    return xc * lax.rsqrt(var + LN_EPS) * g + b


def _mixer_kernel(x_ref, xnext_ref, wup32_ref, wdown32_ref, lbl_ref, win_ref, wgate_ref, wab_ref, convw_ref, hgnw_ref,
                  alogr_ref, dtbr_ref, alogc_ref, dtbc_ref, dnnw_ref, wa_ref, wb_ref, wo_ref,
                  g1_ref, b1_ref, wsum_ref, masks_ref, bdtriu_ref, bdtril_ref, out_ref, wupb_ref, wdownb_ref,
                  xbuf, xnbuf, hq, hk, hv, hlf, hf, hb, dq, dk, dv, oa, ob, du, gbuf, gcrow, convbuf, rawhg, qin, kdec, wq, qkb,
                  kvinc, hdec, gl, sg, sz, sga, sgb, shg, sdn):
    tb = x_ref.shape[0]
    n_chunks = tb // CHUNK

    wupb_ref[...] = _bf(wup32_ref[...])
    wdownb_ref[...] = _bf(wdown32_ref[...])

    @pl.when(pl.program_id(1) == 0)
    def _reset_carries():
        shg[...] = jnp.zeros_like(shg)
        sdn[...] = jnp.zeros_like(sdn)
        convbuf[:, 0:CONV_PAD, :] = jnp.zeros((HEADS, CONV_PAD, 3 * HEAD_DIM), jnp.float32)

    def project_head(c, xsrc=xbuf):
        cols = ([COL_DN_QKV + k * KW + c * HEAD_DIM for k in range(3)]
                + [COL_HG_QFI + k * KW + c * HEAD_DIM for k in range(3)])
        w_head = jnp.concatenate([win_ref[:, o:o + HEAD_DIM] for o in cols], axis=1)
        raw = _dot(xsrc[...], w_head)
        convbuf[c, CONV_PAD:CONV_PAD + tb, :] = raw[:, 0:3 * HEAD_DIM]
        rawhg[c] = raw[:, 3 * HEAD_DIM:6 * HEAD_DIM]

    def head_epilogue(c):
        cw = jnp.concatenate([convw_ref[:, k * KW + c * HEAD_DIM:k * KW + (c + 1) * HEAD_DIM]
                              for k in range(3)], axis=1)
        base = CONV_PAD - (CONV_W - 1)
        acc = convbuf[c, base:base + tb, :] * cw[0:1, :]
        for j in range(1, CONV_W):
            acc = acc + convbuf[c, base + j:base + j + tb, :] * cw[j:j + 1, :]
        convbuf[c, 0:CONV_PAD, :] = convbuf[c, tb:tb + CONV_PAD, :]
        qkv = _silu(acc)
        qh = qkv[:, 0:HEAD_DIM]
        kh = qkv[:, HEAD_DIM:2 * HEAD_DIM]
        dq[c] = qh * (lax.rsqrt(jnp.sum(qh * qh, axis=-1, keepdims=True) + L2_EPS) * QK_SCALE)
        dk[c] = kh * lax.rsqrt(jnp.sum(kh * kh, axis=-1, keepdims=True) + L2_EPS)
        dv[c] = qkv[:, 2 * HEAD_DIM:3 * HEAD_DIM]
        lg = lbl_ref[:, c * HEAD_DIM:(c + 1) * HEAD_DIM]
        e = jnp.exp(lg - jnp.max(lg, axis=0, keepdims=True))
        lb = e[0:1, :] / jnp.sum(e, axis=0, keepdims=True)
        f = lb + (1.0 - lb) * _sigmoid(rawhg[c, :, HEAD_DIM:2 * HEAD_DIM])
        hlf[c] = jnp.log(f)
        hf[c] = f
        hk[c] = 1.0 - f
        hq[c] = _silu(rawhg[c, :, 0:HEAD_DIM]) * QK_SCALE
        hv[c] = rawhg[c, :, 2 * HEAD_DIM:3 * HEAD_DIM]

    gate_tiles = ([(win_ref, COL_HG_G + t, sg, t, _silu) for t in range(0, KW, GATE_TILE)]
                  + [(win_ref, COL_DN_Z + t, sz, t, _silu) for t in range(0, KW, GATE_TILE)]
                  + [(wgate_ref, t, sga, t, _sigmoid) for t in range(0, D_MODEL, GATE_TILE)]
                  + [(wgate_ref, D_MODEL + t, sgb, t, _sigmoid) for t in range(0, D_MODEL, GATE_TILE)])
    gate_tiles_todo = list(range(len(gate_tiles)))

    def issue_gate_tiles(count):
        for _ in range(min(count, len(gate_tiles_todo))):
            w_ref, src, dst, off, act = gate_tiles[gate_tiles_todo.pop(0)]
            dst[:, off:off + GATE_TILE] = act(_dot(xbuf[...], w_ref[:, src:src + GATE_TILE]))

    @pl.when((pl.program_id(0) == 0) & (pl.program_id(1) == 0))
    def _first_projection():
        xnbuf[...] = _bf(x_ref[...])
        project_head(0, xnbuf)

    xbuf[...] = xnbuf[...]

    def cumulate_forget(h0):
        lf_hi, lf_lo = _split(jnp.concatenate([hlf[h0], hlf[h0 + 1]], axis=1))
        bc = _dot(bdtril_ref[...], lf_hi) + _dot(bdtril_ref[...], lf_lo)
        hb[h0] = bc[:, 0:HEAD_DIM]
        hb[h0 + 1] = bc[:, HEAD_DIM:2 * HEAD_DIM]

    for c in range(1, HEADS):
        head_epilogue(c - 1)
        project_head(c)
        if c % 2 == 0:
            cumulate_forget(c - 2)
    head_epilogue(HEADS - 1)
    issue_gate_tiles(GATE_TILES_AFTER_PROJECTIONS)
    cumulate_forget(HEADS - 2)

    pab = _dot(xbuf[...], wab_ref[...])
    g_col = -jnp.exp(alogr_ref[...]) * _softplus(pab + dtbr_ref[...])
    lane = lax.broadcasted_iota(jnp.int32, pab.shape, 1)
    gbuf[...] = jnp.where(lane < HEADS, _sigmoid(pab), g_col)
    pabt = jnp.transpose(pab)[0:2 * HEADS, :]
    g_row = -jnp.exp(alogc_ref[...]) * _softplus(pabt + dtbc_ref[...])
    gr_hi, gr_lo = _split(g_row)
    gc_row = _dot(gr_hi, bdtriu_ref[...]) + _dot(gr_lo, bdtriu_ref[...])
    for c in range(n_chunks):
        gcrow[c] = gc_row[:, c * CHUNK:(c + 1) * CHUNK]

    ri = lax.broadcasted_iota(jnp.int32, (CHUNK, CHUNK), 0)
    ci = lax.broadcasted_iota(jnp.int32, (CHUNK, CHUNK), 1)
    causal = jnp.where(ri >= ci, 1.0, 0.0)
    strict = jnp.where(ri > ci, 1.0, 0.0)
    trow = lax.broadcasted_iota(jnp.int32, (CHUNK, HEAD_DIM), 0)

    n_lev = len(_LEVELS)
    heads = range(HEADS)
    hsl = [slice(h * HEAD_DIM, (h + 1) * HEAD_DIM) for h in heads]

    def level_mask(l):
        return masks_ref[l * CHUNK:(l + 1) * CHUNK, 0:CHUNK]

    def level_mask2(l):
        return masks_ref[l * CHUNK:(l + 1) * CHUNK, :]

    units = [(u, h) for u in range(PREP_UNROLL) for h in heads]
    n_units = range(len(units))

    def prepare_chunks(j):
        cs = [j * PREP_UNROLL + u for u in range(PREP_UNROLL)]
        rows = [slice(c * CHUNK, (c + 1) * CHUNK) for c in cs]
        wsum = wsum_ref[...]
        gb, eg, gcr = [], [], []
        for u in range(PREP_UNROLL):
            gb.append(gbuf[rows[u], :])
            g_hi, g_lo = _split(gb[u])
            eg.append(_dot(wsum, g_hi) + _dot(wsum, g_lo))
            gcr.append(gcrow[cs[u]])

        def ld(ref, i):
            u, h = units[i]
            return ref[h, rows[u], :]

        dkk = [ld(dk, i) for i in n_units]
        dkb = [_bf(k) for k in dkk]
        beta = [gb[u][:, h:h + 1] for u, h in units]
        gcol = [eg[u][0:CHUNK, HEADS + h:HEADS + h + 1] for u, h in units]
        decay = [causal * jnp.exp(jnp.minimum(gcol[i] - gcr[u][HEADS + h:HEADS + h + 1, :], 0.0))
                 for i, (u, h) in enumerate(units)]
        kbeta = [dkk[i] * beta[i] for i in n_units]
        gram = [_dot_nt(jnp.concatenate([_bf(kbeta[i]), _bf(ld(dq, i))], axis=0), dkb[i]) for i in n_units]
        lmat = [gram[i][0:CHUNK] * (strict * decay[i]) for i in n_units]
        qk = [gram[i][CHUNK:2 * CHUNK] * decay[i] for i in n_units]
        for i, (u, h) in enumerate(units):
            if h % 2 == 0:
                qkb[cs[u] * (HEADS // 2) + h // 2] = _bf(jnp.concatenate([qk[i], qk[i + 1]], axis=1))

        hqv = [ld(hq, i) for i in n_units]
        hkv = [ld(hk, i) for i in n_units]
        hqb = [_bf(q) for q in hqv]
        hkb = [_bf(k) for k in hkv]
        sc = [level_mask(n_lev) * _dot_nt(hqb[i], hkb[i]) for i in n_units]
        pairs = [(i, i + 1) for i in range(0, len(units), 2)]
        lane2 = lax.broadcasted_iota(jnp.int32, (CHUNK, 2 * CHUNK), 1)
        first_f = jnp.where(lane2 < CHUNK, 1.0, 0.0)
        second_f = 1.0 - first_f

        def block_diag(m2, lo, hi):
            return jnp.concatenate([_bf(m2 * lo), _bf(m2 * hi)], axis=0)

        lmat2 = [jnp.concatenate([lmat[a], lmat[b]], axis=1) for a, b in pairs]
        lbd = [block_diag(m2, first_f, second_f) for m2 in lmat2]
        nmat2 = [-(level_mask2(n_lev - 1) * m2) for m2 in lmat2]
        first_b, second_b = _bf(first_f), _bf(second_f)

        hbv = [ld(hb, i) for i in n_units]

        def level_factor(i, l):
            m = _LEVELS[l]
            if m >= SUBLANES // 2:
                ref = jnp.concatenate(
                    [jnp.broadcast_to(hbv[i][r:r + 1, :], (SUBLANES, HEAD_DIM))
                     for r in [(g * SUBLANES // (2 * m)) * (2 * m) + m - 1 for g in range(CHUNK // SUBLANES)]],
                    axis=0)
                return jnp.exp2(jnp.abs(hbv[i] - ref) * NEG_LOG2E)
            f = ld(hf, i)
            if m == 1:
                return jnp.where(trow % 2 == 1, f, 1.0)
            f8 = f.reshape(CHUNK // SUBLANES, SUBLANES, HEAD_DIM)
            f_prev = pltpu.roll(f8, 1, 1).reshape(CHUNK, HEAD_DIM)
            f_next = pltpu.roll(f8, SUBLANES - 1, 1).reshape(CHUNK, HEAD_DIM)
            return jnp.where(trow % 4 == 0, f_next,
                             jnp.where(trow % 4 == 1, 1.0, jnp.where(trow % 4 == 2, f, f_prev * f)))

        def score_level(l):
            m = _LEVELS[l]
            for i in n_units:
                zb = _bf(level_factor(i, l))
                d = _dot_nt(hqb[i] * zb, hkb[i] * zb)
                if m < SUBLANES:
                    sc[i] = sc[i] + level_mask(l) * d
                    continue
                groups = []
                for g in range(CHUNK // SUBLANES):
                    r = slice(g * SUBLANES, (g + 1) * SUBLANES)
                    is_query = (g * SUBLANES) % (2 * m) >= m
                    groups.append(sc[i][r] + level_mask(l)[r] * d[r] if is_query else sc[i][r])
                sc[i] = jnp.concatenate(groups, axis=0)

        score_level(n_lev - 1)
        n_pairs = range(len(pairs))
        for l in range(n_lev - 2, -1, -1):
            nb = [_bf(nmat2[p]) for p in n_pairs]
            xm = [level_mask2(l) * (lmat2[p] + _dot(nb[p], lbd[p])) for p in n_pairs]
            score_level(l)
            for p in n_pairs:
                nbd = jnp.concatenate([nb[p] * first_b, nb[p] * second_b], axis=0)
                nmat2[p] = nmat2[p] - (xm[p] + _dot(_bf(xm[p]), nbd))

        hvb = [_bf(ld(hv, i)) for i in n_units]
        for i, (u, h) in enumerate(units):
            oa[rows[u], hsl[h]] = _dot(_bf(sc[i]), hvb[i])
        for i, (u, h) in enumerate(units):
            kd = hkv[i] * jnp.exp(hbv[i][CHUNK - 1:CHUNK, :] - hbv[i])
            kvinc[cs[u] * HEADS + h] = _dot_tn(hvb[i], _bf(kd))
        for i, (u, h) in enumerate(units):
            qin[rows[u], hsl[h]] = _bf(hqv[i] * jnp.exp(hbv[i]))
            hdec[cs[u], :, hsl[h]] = jnp.exp(hbv[i][CHUNK - 1:CHUNK, :])

        for i, (u, h) in enumerate(units):
            eg_col = jnp.exp(gcol[i])
            rhs = jnp.concatenate([ld(dv, i) * beta[i], kbeta[i] * eg_col], axis=1)
            zero_rhs = jnp.zeros(rhs.shape, jnp.bfloat16)
            rhs_pad = jnp.concatenate([_bf(rhs), zero_rhs] if i % 2 == 0 else [zero_rhs, _bf(rhs)], axis=0)
            uw = rhs + _dot(_bf(nmat2[i // 2]), rhs_pad)
            du[rows[u], hsl[h]] = uw[:, 0:HEAD_DIM]
            wq[cs[u], 0:CHUNK, hsl[h]] = _bf(uw[:, HEAD_DIM:2 * HEAD_DIM])
            wq[cs[u], CHUNK:2 * CHUNK, hsl[h]] = _bf(ld(dq, i) * eg_col)
            kdec[rows[u], hsl[h]] = _bf(dkk[i] * jnp.exp(eg[u][CHUNK:2 * CHUNK, HEADS + h:HEADS + h + 1]))
        for u in range(PREP_UNROLL):
            gl[cs[u]] = jnp.exp(eg[u][CHUNK - 1:CHUNK, :])

    for j in range(n_chunks // PREP_UNROLL):
        prepare_chunks(j)

    for c in range(n_chunks):
        rows = slice(c * CHUNK, (c + 1) * CHUNK)
        s = [sdn[h] for h in heads]
        wqc = wq[c]
        zero_s = jnp.zeros((HEAD_DIM, HEAD_DIM), jnp.bfloat16)
        ws = []
        for h in range(0, HEADS, 2):
            s_bd = jnp.concatenate([jnp.concatenate([_bf(s[h]), zero_s], axis=1),
                                    jnp.concatenate([zero_s, _bf(s[h + 1])], axis=1)], axis=0)
            ws2 = _dot(wqc[:, h * HEAD_DIM:(h + 2) * HEAD_DIM], s_bd)
            ws += [ws2[:, 0:HEAD_DIM], ws2[:, HEAD_DIM:2 * HEAD_DIM]]
        st = [shg[h] for h in heads]
        dec = hdec[c]
        for h in heads:
            oa[rows, hsl[h]] = oa[rows, hsl[h]] + _dot_nt(qin[rows, hsl[h]], _bf(st[h]))
            shg[h] = st[h] * dec[:, hsl[h]] + kvinc[c * HEADS + h]
        issue_gate_tiles(-(-len(gate_tiles_todo) // (2 * (n_chunks - c))))
        vnb = [_bf(du[rows, hsl[h]] - ws[h][0:CHUNK]) for h in heads]
        glc = gl[c]
        zero_v = jnp.zeros((CHUNK, HEAD_DIM), jnp.bfloat16)
        for h in range(0, HEADS, 2):
            v_bd = jnp.concatenate([jnp.concatenate([vnb[h], zero_v], axis=1),
                                    jnp.concatenate([zero_v, vnb[h + 1]], axis=1)], axis=0)
            intra = _dot(qkb[c * (HEADS // 2) + h // 2], v_bd)
            ob[rows, hsl[h]] = ws[h][CHUNK:2 * CHUNK] + intra[:, 0:HEAD_DIM]
            ob[rows, hsl[h + 1]] = ws[h + 1][CHUNK:2 * CHUNK] + intra[:, HEAD_DIM:2 * HEAD_DIM]
        for h in heads:
            sdn[h] = s[h] * glc[:, HEADS + h:HEADS + h + 1] + _dot_tn(kdec[rows, hsl[h]], vnb[h])
        issue_gate_tiles(-(-len(gate_tiles_todo) // (2 * (n_chunks - c) - 1)))

    def gated_norm(o_ref, w_row, gate):
        parts = []
        for h in range(HEADS):
            oh = o_ref[:, h * HEAD_DIM:(h + 1) * HEAD_DIM]
            parts.append(oh * lax.rsqrt(jnp.mean(oh * oh, axis=-1, keepdims=True) + RMS_EPS))
        return jnp.concatenate(parts, axis=1) * w_row * gate

    na = gated_norm(oa, hgnw_ref[...], sg[...])
    nb_ = gated_norm(ob, dnnw_ref[...], sz[...])
    merged = sga[...] * _dot(_bf(na), wa_ref[...]) + sgb[...] * _dot(_bf(nb_), wb_ref[...])
    mix = _dot(_bf(merged), wo_ref[...])
    out_ref[...] = _layer_norm(ALPHA * x_ref[...] + mix, g1_ref[...], b1_ref[...])

    xnbuf[...] = _bf(xnext_ref[...])
    project_head(0, xnbuf)


def _mlp_kernel(h_ref, wup_ref, wdown_ref, g2_ref, b2_ref, out_ref):
    tm = h_ref.shape[0]
    hb = _bf(h_ref[...])
    n_ff = D_FF // FF_BLOCK
    assert n_ff >= 2 and tm % MLP_TAIL_SPLIT == 0
    acc = None
    for j in range(n_ff - 1):
        up = _dot(hb, wup_ref[:, j * FF_BLOCK:(j + 1) * FF_BLOCK])
        act = jnp.square(jnp.maximum(up, 0.0))
        part = _dot(_bf(act), wdown_ref[j * FF_BLOCK:(j + 1) * FF_BLOCK, :])
        acc = part if acc is None else acc + part
    up = _dot(hb, wup_ref[:, (n_ff - 1) * FF_BLOCK:n_ff * FF_BLOCK])
    actb = _bf(jnp.square(jnp.maximum(up, 0.0)))
    for r in range(MLP_TAIL_SPLIT):
        rows = slice(r * tm // MLP_TAIL_SPLIT, (r + 1) * tm // MLP_TAIL_SPLIT)
        part = _dot(actb[rows], wdown_ref[(n_ff - 1) * FF_BLOCK:n_ff * FF_BLOCK, :])
        out_ref[rows, :] = _layer_norm(ALPHA * h_ref[rows, :] + acc[rows] + part, g2_ref[...], b2_ref[...])


def _resident(shape):
    nd = len(shape)
    return pl.BlockSpec(shape, lambda *_: (0,) * nd, pipeline_mode=pl.Buffered(1))


def _mixer_call(x, wup, wdown, lbl, win, wgate, wab, convw, hgnw, alogr, dtbr, alogc, dtbc, dnnw,
                wa, wb, wo, g1, b1):
    bsz, seq, _ = x.shape
    tb = min(TOKEN_BLOCK, seq)
    assert seq % tb == 0 and tb % (CHUNK * PREP_UNROLL) == 0
    n_chunks = tb // CHUNK
    wsum = jnp.asarray(_WSUM, jnp.bfloat16)
    masks = jnp.asarray(np.concatenate([_MASKS, _MASKS], axis=1), jnp.float32)
    tt = np.arange(tb)
    bdtriu = jnp.asarray(((tt[:, None] <= tt[None, :]) & ((tt[:, None] // CHUNK) == (tt[None, :] // CHUNK))),
                         jnp.bfloat16)
    consts = (lbl, win, wgate, wab, convw, hgnw, alogr, dtbr, alogc, dtbc, dnnw, wa, wb, wo,
              g1, b1, wsum, masks, bdtriu, bdtriu.T)
    f32 = jnp.float32
    blk = functools.partial(pltpu.VMEM, (tb, KW))
    bf16 = jnp.bfloat16
    per_head = functools.partial(pltpu.VMEM, (HEADS, tb, HEAD_DIM))
    scratch = [pltpu.VMEM((tb, D_MODEL), bf16), pltpu.VMEM((tb, D_MODEL), bf16)]
    scratch += [per_head(f32) for _ in range(9)]
    scratch += [blk(f32) for _ in range(3)]
    scratch += [pltpu.VMEM((tb, LANES), f32),
                pltpu.VMEM((n_chunks, 8, CHUNK), f32),
                pltpu.VMEM((HEADS, tb + CONV_PAD, 3 * HEAD_DIM), f32),
                pltpu.VMEM((HEADS, tb, 3 * HEAD_DIM), f32),
                blk(bf16), blk(bf16),
                pltpu.VMEM((n_chunks, 2 * CHUNK, KW), bf16),
                pltpu.VMEM((n_chunks * HEADS // 2, CHUNK, 2 * CHUNK), bf16),
                pltpu.VMEM((n_chunks * HEADS, HEAD_DIM, HEAD_DIM), f32),
                pltpu.VMEM((n_chunks, 1, KW), f32),
                pltpu.VMEM((n_chunks, 1, LANES), f32),
                blk(f32), blk(f32),
                pltpu.VMEM((tb, D_MODEL), f32),
                pltpu.VMEM((tb, D_MODEL), f32),
                pltpu.VMEM((HEADS, HEAD_DIM, HEAD_DIM), f32),
                pltpu.VMEM((HEADS, HEAD_DIM, HEAD_DIM), f32)]
    n_blocks = seq // tb

    def next_block(b, i):
        flat = jnp.minimum(b * n_blocks + i + 1, bsz * n_blocks - 1)
        return flat // n_blocks, flat % n_blocks, 0

    n_steps = bsz * n_blocks
    up_rows, down_rows = wup.shape[0] // n_steps, wdown.shape[0] // n_steps
    assert up_rows * n_steps == wup.shape[0] and down_rows * n_steps == wdown.shape[0]
    assert up_rows % BF16_ROWS == 0 and down_rows % BF16_ROWS == 0

    def step_rows(b, i):
        return b * n_blocks + i, 0

    return pl.pallas_call(
        _mixer_kernel,
        grid=(bsz, n_blocks),
        in_specs=[pl.BlockSpec((None, tb, D_MODEL), lambda b, i: (b, i, 0)),
                  pl.BlockSpec((None, tb, D_MODEL), next_block),
                  pl.BlockSpec((up_rows, wup.shape[1]), step_rows),
                  pl.BlockSpec((down_rows, wdown.shape[1]), step_rows)]
                 + [_resident(c.shape) for c in consts],
        out_specs=[pl.BlockSpec((None, tb, D_MODEL), lambda b, i: (b, i, 0)),
                   pl.BlockSpec((up_rows, wup.shape[1]), step_rows),
                   pl.BlockSpec((down_rows, wdown.shape[1]), step_rows)],
        out_shape=[jax.ShapeDtypeStruct(x.shape, f32),
                   jax.ShapeDtypeStruct(wup.shape, bf16),
                   jax.ShapeDtypeStruct(wdown.shape, bf16)],
        scratch_shapes=scratch,
        compiler_params=pltpu.CompilerParams(dimension_semantics=("arbitrary", "arbitrary"),
                                             vmem_limit_bytes=VMEM_LIMIT_BYTES),
        name="token_mixer",
    )(x, x, wup, wdown, *consts)


def _mlp_call(h, wup, wdown, g2, b2):
    m = h.shape[0]
    tm = min(MLP_BLOCK, m)
    assert m % tm == 0
    consts = (wup, wdown, g2, b2)
    return pl.pallas_call(
        _mlp_kernel,
        grid=(m // tm,),
        in_specs=[pl.BlockSpec((tm, D_MODEL), lambda i: (i, 0))] + [_resident(c.shape) for c in consts],
        out_specs=pl.BlockSpec((tm, D_MODEL), lambda i: (i, 0)),
        out_shape=jax.ShapeDtypeStruct(h.shape, jnp.float32),
        compiler_params=pltpu.CompilerParams(dimension_semantics=("arbitrary",),
                                             vmem_limit_bytes=VMEM_LIMIT_BYTES),
        name="relu2_mlp",
    )(h, *consts)


def kernel(x, hg_lb_logits, w_in, conv_w, hg_norm_w, dn_A_log, dn_dt_bias, dn_norm_w, w_branch_a, w_branch_b, w_o, ln1_g, ln1_b, w_up, w_down, ln2_g, ln2_b):
    bsz, seq, _ = x.shape
    f32 = jnp.float32
    bf16 = jnp.bfloat16
    o = _OFFS
    l = LAYER
    win = w_in[l][:, :o[6]].astype(bf16)
    wgate = w_in[l][:, o[8]:o[10]].astype(bf16)
    wab = jnp.pad(w_in[l][:, o[6]:o[8]].astype(bf16), ((0, 0), (0, LANES - 2 * HEADS)))
    zeros4 = jnp.zeros((HEADS,), f32)
    alog8 = jnp.concatenate([zeros4, dn_A_log[l].astype(f32)])
    dtb8 = jnp.concatenate([zeros4, dn_dt_bias[l].astype(f32)])
    alogr = jnp.pad(alog8, (0, LANES - 2 * HEADS)).reshape(1, LANES)
    dtbr = jnp.pad(dtb8, (0, LANES - 2 * HEADS)).reshape(1, LANES)
    h1, wup_bf, wdown_bf = _mixer_call(
        x, w_up[l], w_down[l], hg_lb_logits.astype(f32), win, wgate, wab, conv_w[l].astype(f32),
        hg_norm_w[l].reshape(1, KW), alogr, dtbr, alog8.reshape(8, 1), dtb8.reshape(8, 1),
        jnp.tile(dn_norm_w[l], HEADS).reshape(1, KW),
        w_branch_a[l].astype(bf16), w_branch_b[l].astype(bf16), w_o[l].astype(bf16),
        ln1_g[l].reshape(1, D_MODEL), ln1_b[l].reshape(1, D_MODEL))
    out = _mlp_call(h1.reshape(bsz * seq, D_MODEL), wup_bf, wdown_bf,
                    ln2_g[l].reshape(1, D_MODEL), ln2_b[l].reshape(1, D_MODEL))
    return out.reshape(bsz, seq, D_MODEL)
```

```python
import functools

import numpy as np
import jax
import jax.numpy as jnp
from jax import lax
from jax.experimental import pallas as pl
from jax.experimental.pallas import tpu as pltpu

D_MODEL = 1024
HEADS = 4
HEAD_DIM = 128
KW = HEADS * HEAD_DIM
CHUNK = 64
CONV_W = 4
D_FF = 4 * D_MODEL
DEPTH = 1
LAYER = 0
ALPHA = (2 * DEPTH) ** 0.25
LN_EPS = 1e-5
RMS_EPS = 1e-6
L2_EPS = 1e-6
QK_SCALE = HEAD_DIM ** -0.5
NEG_LOG2E = -1.4426950408889634

TOKEN_BLOCK = 256
PREP_UNROLL = 4
GATE_TILE = 256
GATE_TILES_AFTER_PROJECTIONS = 4
MLP_BLOCK = 1024
MLP_TAIL_SPLIT = 4
FF_BLOCK = 1024
LANES = 128
SUBLANES = 8
BF16_ROWS = 16
CONV_PAD = 8
VMEM_LIMIT_BYTES = 56 * 1024 * 1024

_SIZES = (KW, KW, KW, KW, 3 * KW, KW, HEADS, HEADS, D_MODEL, D_MODEL)
_OFFS = np.concatenate([[0], np.cumsum(_SIZES)]).tolist()

COL_HG_QFI = _OFFS[0]
COL_HG_G = _OFFS[3]
COL_DN_QKV = _OFFS[4]
COL_DN_Z = _OFFS[5]

_LEVELS = (32, 16, 8, 4, 2, 1)


def _build_constants():
    t = np.arange(CHUNK)[:, None]
    r = np.arange(CHUNK)[None, :]
    wsum = np.concatenate([r <= t, r > t], 0).astype(np.float32)
    masks = [((t // (2 * m)) == (r // (2 * m))) & ((t % (2 * m)) >= m) & ((r % (2 * m)) < m) for m in _LEVELS]
    masks.append(t == r)
    masks = np.concatenate(masks, 0).astype(np.float32)
    return wsum, masks


_WSUM, _MASKS = _build_constants()


def _dot(a, b):
    return jnp.dot(a, b, preferred_element_type=jnp.float32)


def _dot_nt(a, b):
    return lax.dot_general(a, b, (((1,), (1,)), ((), ())), preferred_element_type=jnp.float32)


def _dot_tn(a, b):
    return lax.dot_general(a, b, (((0,), (0,)), ((), ())), preferred_element_type=jnp.float32)


def _bf(x):
    return x.astype(jnp.bfloat16)


def _split(x):
    hi = _bf(x)
    return hi, _bf(x - hi.astype(jnp.float32))


def _sigmoid(x):
    return 1.0 / (1.0 + jnp.exp2(x * NEG_LOG2E))


def _silu(x):
    return x * _sigmoid(x)


def _softplus(x):
    return jnp.maximum(x, 0.0) + jnp.log1p(jnp.exp(-jnp.abs(x)))


def _layer_norm(x, g, b):
    mu = jnp.mean(x, axis=-1, keepdims=True)
    xc = x - mu
    var = jnp.mean(xc * xc, axis=-1, keepdims=True)
    return xc * lax.rsqrt(var + LN_EPS) * g + b


def _mixer_kernel(x_ref, xnext_ref, wup32_ref, wdown32_ref, lbl_ref, win_ref, wgate_ref, wab_ref, convw_ref, hgnw_ref,
                  alogr_ref, dtbr_ref, alogc_ref, dtbc_ref, dnnw_ref, wa_ref, wb_ref, wo_ref,
                  g1_ref, b1_ref, wsum_ref, masks_ref, bdtriu_ref, bdtril_ref, out_ref, wupb_ref, wdownb_ref,
                  xbuf, xnbuf, hq, hk, hv, hlf, hf, hb, dq, dk, dv, oa, ob, du, gbuf, gcrow, convbuf, rawhg, qin, kdec, wq, qkb,
                  kvinc, hdec, gl, sg, sz, sga, sgb, shg, sdn, wa16, wb16, wo16):
    tb = x_ref.shape[0]
    n_chunks = tb // CHUNK

    wupb_ref[...] = _bf(wup32_ref[...])
    wdownb_ref[...] = _bf(wdown32_ref[...])

    @pl.when(pl.program_id(1) == 0)
    def _reset_carries():
        shg[...] = jnp.zeros_like(shg)
        sdn[...] = jnp.zeros_like(sdn)
        convbuf[:, 0:CONV_PAD, :] = jnp.zeros((HEADS, CONV_PAD, 3 * HEAD_DIM), jnp.float32)
        wa16[...] = _bf(wa_ref[...])
        wb16[...] = _bf(wb_ref[...])
        wo16[...] = _bf(wo_ref[...])

    def project_head(c, xsrc=xbuf):
        cols = ([COL_DN_QKV + k * KW + c * HEAD_DIM for k in range(3)]
                + [COL_HG_QFI + k * KW + c * HEAD_DIM for k in range(3)])
        w_head = jnp.concatenate([win_ref[:, o:o + HEAD_DIM] for o in cols], axis=1)
        raw = _dot(xsrc[...], w_head)
        convbuf[c, CONV_PAD:CONV_PAD + tb, :] = raw[:, 0:3 * HEAD_DIM]
        rawhg[c] = raw[:, 3 * HEAD_DIM:6 * HEAD_DIM]

    def head_epilogue(c):
        cw = jnp.concatenate([convw_ref[:, k * KW + c * HEAD_DIM:k * KW + (c + 1) * HEAD_DIM]
                              for k in range(3)], axis=1)
        base = CONV_PAD - (CONV_W - 1)
        acc = convbuf[c, base:base + tb, :] * cw[0:1, :]
        for j in range(1, CONV_W):
            acc = acc + convbuf[c, base + j:base + j + tb, :] * cw[j:j + 1, :]
        convbuf[c, 0:CONV_PAD, :] = convbuf[c, tb:tb + CONV_PAD, :]
        qkv = _silu(acc)
        qh = qkv[:, 0:HEAD_DIM]
        kh = qkv[:, HEAD_DIM:2 * HEAD_DIM]
        dq[c] = qh * (lax.rsqrt(jnp.sum(qh * qh, axis=-1, keepdims=True) + L2_EPS) * QK_SCALE)
        dk[c] = kh * lax.rsqrt(jnp.sum(kh * kh, axis=-1, keepdims=True) + L2_EPS)
        dv[c] = qkv[:, 2 * HEAD_DIM:3 * HEAD_DIM]
        lg = lbl_ref[:, c * HEAD_DIM:(c + 1) * HEAD_DIM]
        e = jnp.exp(lg - jnp.max(lg, axis=0, keepdims=True))
        lb = e[0:1, :] / jnp.sum(e, axis=0, keepdims=True)
        f = lb + (1.0 - lb) * _sigmoid(rawhg[c, :, HEAD_DIM:2 * HEAD_DIM])
        hlf[c] = jnp.log(f)
        hf[c] = f
        hk[c] = 1.0 - f
        hq[c] = _silu(rawhg[c, :, 0:HEAD_DIM]) * QK_SCALE
        hv[c] = rawhg[c, :, 2 * HEAD_DIM:3 * HEAD_DIM]

    gate_tiles = ([(win_ref, COL_HG_G + t, sg, t, _silu) for t in range(0, KW, GATE_TILE)]
                  + [(win_ref, COL_DN_Z + t, sz, t, _silu) for t in range(0, KW, GATE_TILE)]
                  + [(wgate_ref, t, sga, t, _sigmoid) for t in range(0, D_MODEL, GATE_TILE)]
                  + [(wgate_ref, D_MODEL + t, sgb, t, _sigmoid) for t in range(0, D_MODEL, GATE_TILE)])
    gate_tiles_todo = list(range(len(gate_tiles)))

    def issue_gate_tiles(count):
        for _ in range(min(count, len(gate_tiles_todo))):
            w_ref, src, dst, off, act = gate_tiles[gate_tiles_todo.pop(0)]
            dst[:, off:off + GATE_TILE] = act(_dot(xbuf[...], w_ref[:, src:src + GATE_TILE]))

    @pl.when((pl.program_id(0) == 0) & (pl.program_id(1) == 0))
    def _first_projection():
        xnbuf[...] = _bf(x_ref[...])
        project_head(0, xnbuf)

    xbuf[...] = xnbuf[...]

    def cumulate_forget(h0):
        lf_hi, lf_lo = _split(jnp.concatenate([hlf[h0], hlf[h0 + 1]], axis=1))
        bc = _dot(bdtril_ref[...], lf_hi) + _dot(bdtril_ref[...], lf_lo)
        hb[h0] = bc[:, 0:HEAD_DIM]
        hb[h0 + 1] = bc[:, HEAD_DIM:2 * HEAD_DIM]

    for c in range(1, HEADS):
        head_epilogue(c - 1)
        project_head(c)
        if c % 2 == 0:
            cumulate_forget(c - 2)
    head_epilogue(HEADS - 1)
    issue_gate_tiles(GATE_TILES_AFTER_PROJECTIONS)
    cumulate_forget(HEADS - 2)

    pab = _dot(xbuf[...], wab_ref[...])
    g_col = -jnp.exp(alogr_ref[...]) * _softplus(pab + dtbr_ref[...])
    lane = lax.broadcasted_iota(jnp.int32, pab.shape, 1)
    gbuf[...] = jnp.where(lane < HEADS, _sigmoid(pab), g_col)
    pabt = jnp.transpose(pab)[0:2 * HEADS, :]
    g_row = -jnp.exp(alogc_ref[...]) * _softplus(pabt + dtbc_ref[...])
    gr_hi, gr_lo = _split(g_row)
    gc_row = _dot(gr_hi, bdtriu_ref[...]) + _dot(gr_lo, bdtriu_ref[...])
    for c in range(n_chunks):
        gcrow[c] = gc_row[:, c * CHUNK:(c + 1) * CHUNK]

    ri = lax.broadcasted_iota(jnp.int32, (CHUNK, CHUNK), 0)
    ci = lax.broadcasted_iota(jnp.int32, (CHUNK, CHUNK), 1)
    causal = jnp.where(ri >= ci, 1.0, 0.0)
    strict = jnp.where(ri > ci, 1.0, 0.0)
    trow = lax.broadcasted_iota(jnp.int32, (CHUNK, HEAD_DIM), 0)

    n_lev = len(_LEVELS)
    heads = range(HEADS)
    hsl = [slice(h * HEAD_DIM, (h + 1) * HEAD_DIM) for h in heads]

    def level_mask(l):
        return masks_ref[l * CHUNK:(l + 1) * CHUNK, 0:CHUNK]

    def level_mask2(l):
        return masks_ref[l * CHUNK:(l + 1) * CHUNK, :]

    units = [(u, h) for u in range(PREP_UNROLL) for h in heads]
    n_units = range(len(units))

    def prepare_chunks(j):
        cs = [j * PREP_UNROLL + u for u in range(PREP_UNROLL)]
        rows = [slice(c * CHUNK, (c + 1) * CHUNK) for c in cs]
        wsum = wsum_ref[...]
        gb, eg, gcr = [], [], []
        for u in range(PREP_UNROLL):
            gb.append(gbuf[rows[u], :])
            g_hi, g_lo = _split(gb[u])
            eg.append(_dot(wsum, g_hi) + _dot(wsum, g_lo))
            gcr.append(gcrow[cs[u]])

        def ld(ref, i):
            u, h = units[i]
            return ref[h, rows[u], :]

        dkk = [ld(dk, i) for i in n_units]
        dkb = [_bf(k) for k in dkk]
        beta = [gb[u][:, h:h + 1] for u, h in units]
        gcol = [eg[u][0:CHUNK, HEADS + h:HEADS + h + 1] for u, h in units]
        decay = [causal * jnp.exp(jnp.minimum(gcol[i] - gcr[u][HEADS + h:HEADS + h + 1, :], 0.0))
                 for i, (u, h) in enumerate(units)]
        kbeta = [dkk[i] * beta[i] for i in n_units]
        gram = [_dot_nt(jnp.concatenate([_bf(kbeta[i]), _bf(ld(dq, i))], axis=0), dkb[i]) for i in n_units]
        lmat = [gram[i][0:CHUNK] * (strict * decay[i]) for i in n_units]
        qk = [gram[i][CHUNK:2 * CHUNK] * decay[i] for i in n_units]
        for i, (u, h) in enumerate(units):
            if h % 2 == 0:
                qkb[cs[u] * (HEADS // 2) + h // 2] = _bf(jnp.concatenate([qk[i], qk[i + 1]], axis=1))

        hqv = [ld(hq, i) for i in n_units]
        hkv = [ld(hk, i) for i in n_units]
        hqb = [_bf(q) for q in hqv]
        hkb = [_bf(k) for k in hkv]
        sc = [level_mask(n_lev) * _dot_nt(hqb[i], hkb[i]) for i in n_units]
        pairs = [(i, i + 1) for i in range(0, len(units), 2)]
        lane2 = lax.broadcasted_iota(jnp.int32, (CHUNK, 2 * CHUNK), 1)
        first_f = jnp.where(lane2 < CHUNK, 1.0, 0.0)
        second_f = 1.0 - first_f

        def block_diag(m2, lo, hi):
            return jnp.concatenate([_bf(m2 * lo), _bf(m2 * hi)], axis=0)

        lmat2 = [jnp.concatenate([lmat[a], lmat[b]], axis=1) for a, b in pairs]
        lbd = [block_diag(m2, first_f, second_f) for m2 in lmat2]
        nmat2 = [-(level_mask2(n_lev - 1) * m2) for m2 in lmat2]
        first_b, second_b = _bf(first_f), _bf(second_f)

        hbv = [ld(hb, i) for i in n_units]

        def level_factor(i, l):
            m = _LEVELS[l]
            if m >= SUBLANES // 2:
                ref = jnp.concatenate(
                    [jnp.broadcast_to(hbv[i][r:r + 1, :], (SUBLANES, HEAD_DIM))
                     for r in [(g * SUBLANES // (2 * m)) * (2 * m) + m - 1 for g in range(CHUNK // SUBLANES)]],
                    axis=0)
                return jnp.exp2(jnp.abs(hbv[i] - ref) * NEG_LOG2E)
            f = ld(hf, i)
            if m == 1:
                return jnp.where(trow % 2 == 1, f, 1.0)
            f8 = f.reshape(CHUNK // SUBLANES, SUBLANES, HEAD_DIM)
            f_prev = pltpu.roll(f8, 1, 1).reshape(CHUNK, HEAD_DIM)
            f_next = pltpu.roll(f8, SUBLANES - 1, 1).reshape(CHUNK, HEAD_DIM)
            return jnp.where(trow % 4 == 0, f_next,
                             jnp.where(trow % 4 == 1, 1.0, jnp.where(trow % 4 == 2, f, f_prev * f)))

        def score_level(l):
            m = _LEVELS[l]
            for i in n_units:
                zb = _bf(level_factor(i, l))
                d = _dot_nt(hqb[i] * zb, hkb[i] * zb)
                if m < SUBLANES:
                    sc[i] = sc[i] + level_mask(l) * d
                    continue
                groups = []
                for g in range(CHUNK // SUBLANES):
                    r = slice(g * SUBLANES, (g + 1) * SUBLANES)
                    is_query = (g * SUBLANES) % (2 * m) >= m
                    groups.append(sc[i][r] + level_mask(l)[r] * d[r] if is_query else sc[i][r])
                sc[i] = jnp.concatenate(groups, axis=0)

        score_level(n_lev - 1)
        n_pairs = range(len(pairs))
        for l in range(n_lev - 2, -1, -1):
            nb = [_bf(nmat2[p]) for p in n_pairs]
            xm = [level_mask2(l) * (lmat2[p] + _dot(nb[p], lbd[p])) for p in n_pairs]
            score_level(l)
            for p in n_pairs:
                nbd = jnp.concatenate([nb[p] * first_b, nb[p] * second_b], axis=0)
                nmat2[p] = nmat2[p] - (xm[p] + _dot(_bf(xm[p]), nbd))

        hvb = [_bf(ld(hv, i)) for i in n_units]
        for i, (u, h) in enumerate(units):
            oa[rows[u], hsl[h]] = _dot(_bf(sc[i]), hvb[i])
        for i, (u, h) in enumerate(units):
            kd = hkv[i] * jnp.exp(hbv[i][CHUNK - 1:CHUNK, :] - hbv[i])
            kvinc[cs[u] * HEADS + h] = _dot_tn(hvb[i], _bf(kd))
        for i, (u, h) in enumerate(units):
            qin[rows[u], hsl[h]] = _bf(hqv[i] * jnp.exp(hbv[i]))
            hdec[cs[u], :, hsl[h]] = jnp.exp(hbv[i][CHUNK - 1:CHUNK, :])

        for i, (u, h) in enumerate(units):
            eg_col = jnp.exp(gcol[i])
            rhs = jnp.concatenate([ld(dv, i) * beta[i], kbeta[i] * eg_col], axis=1)
            zero_rhs = jnp.zeros(rhs.shape, jnp.bfloat16)
            rhs_pad = jnp.concatenate([_bf(rhs), zero_rhs] if i % 2 == 0 else [zero_rhs, _bf(rhs)], axis=0)
            uw = rhs + _dot(_bf(nmat2[i // 2]), rhs_pad)
            du[rows[u], hsl[h]] = uw[:, 0:HEAD_DIM]
            wq[cs[u], 0:CHUNK, hsl[h]] = _bf(uw[:, HEAD_DIM:2 * HEAD_DIM])
            wq[cs[u], CHUNK:2 * CHUNK, hsl[h]] = _bf(ld(dq, i) * eg_col)
            kdec[rows[u], hsl[h]] = _bf(dkk[i] * jnp.exp(eg[u][CHUNK:2 * CHUNK, HEADS + h:HEADS + h + 1]))
        for u in range(PREP_UNROLL):
            gl[cs[u]] = jnp.exp(eg[u][CHUNK - 1:CHUNK, :])

    for j in range(n_chunks // PREP_UNROLL):
        prepare_chunks(j)

    for c in range(n_chunks):
        rows = slice(c * CHUNK, (c + 1) * CHUNK)
        s = [sdn[h] for h in heads]
        wqc = wq[c]
        zero_s = jnp.zeros((HEAD_DIM, HEAD_DIM), jnp.bfloat16)
        ws = []
        for h in range(0, HEADS, 2):
            s_bd = jnp.concatenate([jnp.concatenate([_bf(s[h]), zero_s], axis=1),
                                    jnp.concatenate([zero_s, _bf(s[h + 1])], axis=1)], axis=0)
            ws2 = _dot(wqc[:, h * HEAD_DIM:(h + 2) * HEAD_DIM], s_bd)
            ws += [ws2[:, 0:HEAD_DIM], ws2[:, HEAD_DIM:2 * HEAD_DIM]]
        st = [shg[h] for h in heads]
        dec = hdec[c]
        for h in heads:
            oa[rows, hsl[h]] = oa[rows, hsl[h]] + _dot_nt(qin[rows, hsl[h]], _bf(st[h]))
            shg[h] = st[h] * dec[:, hsl[h]] + kvinc[c * HEADS + h]
        issue_gate_tiles(-(-len(gate_tiles_todo) // (2 * (n_chunks - c))))
        vnb = [_bf(du[rows, hsl[h]] - ws[h][0:CHUNK]) for h in heads]
        glc = gl[c]
        zero_v = jnp.zeros((CHUNK, HEAD_DIM), jnp.bfloat16)
        for h in range(0, HEADS, 2):
            v_bd = jnp.concatenate([jnp.concatenate([vnb[h], zero_v], axis=1),
                                    jnp.concatenate([zero_v, vnb[h + 1]], axis=1)], axis=0)
            intra = _dot(qkb[c * (HEADS // 2) + h // 2], v_bd)
            ob[rows, hsl[h]] = ws[h][CHUNK:2 * CHUNK] + intra[:, 0:HEAD_DIM]
            ob[rows, hsl[h + 1]] = ws[h + 1][CHUNK:2 * CHUNK] + intra[:, HEAD_DIM:2 * HEAD_DIM]
        for h in heads:
            sdn[h] = s[h] * glc[:, HEADS + h:HEADS + h + 1] + _dot_tn(kdec[rows, hsl[h]], vnb[h])
        issue_gate_tiles(-(-len(gate_tiles_todo) // (2 * (n_chunks - c) - 1)))

    def gated_norm(o_ref, w_row, gate):
        parts = []
        for h in range(HEADS):
            oh = o_ref[:, h * HEAD_DIM:(h + 1) * HEAD_DIM]
            parts.append(oh * lax.rsqrt(jnp.mean(oh * oh, axis=-1, keepdims=True) + RMS_EPS))
        return jnp.concatenate(parts, axis=1) * w_row * gate

    na = gated_norm(oa, hgnw_ref[...], sg[...])
    nb_ = gated_norm(ob, dnnw_ref[...], sz[...])
    merged = sga[...] * _dot(_bf(na), wa16[...]) + sgb[...] * _dot(_bf(nb_), wb16[...])
    mix = _dot(_bf(merged), wo16[...])
    out_ref[...] = _layer_norm(ALPHA * x_ref[...] + mix, g1_ref[...], b1_ref[...])

    xnbuf[...] = _bf(xnext_ref[...])
    project_head(0, xnbuf)


def _mlp_kernel(h_ref, wup_ref, wdown_ref, g2_ref, b2_ref, out_ref):
    tm = h_ref.shape[0]
    hb = _bf(h_ref[...])
    n_ff = D_FF // FF_BLOCK
    assert n_ff >= 2 and tm % MLP_TAIL_SPLIT == 0
    acc = None
    for j in range(n_ff - 1):
        up = _dot(hb, wup_ref[:, j * FF_BLOCK:(j + 1) * FF_BLOCK])
        act = jnp.square(jnp.maximum(up, 0.0))
        part = _dot(_bf(act), wdown_ref[j * FF_BLOCK:(j + 1) * FF_BLOCK, :])
        acc = part if acc is None else acc + part
    up = _dot(hb, wup_ref[:, (n_ff - 1) * FF_BLOCK:n_ff * FF_BLOCK])
    actb = _bf(jnp.square(jnp.maximum(up, 0.0)))
    for r in range(MLP_TAIL_SPLIT):
        rows = slice(r * tm // MLP_TAIL_SPLIT, (r + 1) * tm // MLP_TAIL_SPLIT)
        part = _dot(actb[rows], wdown_ref[(n_ff - 1) * FF_BLOCK:n_ff * FF_BLOCK, :])
        out_ref[rows, :] = _layer_norm(ALPHA * h_ref[rows, :] + acc[rows] + part, g2_ref[...], b2_ref[...])


def _resident(shape):
    nd = len(shape)
    return pl.BlockSpec(shape, lambda *_: (0,) * nd, pipeline_mode=pl.Buffered(1))


def _mixer_call(x, wup, wdown, lbl, win, wgate, wab, convw, hgnw, alogr, dtbr, alogc, dtbc, dnnw,
                wa, wb, wo, g1, b1):
    bsz, seq, _ = x.shape
    tb = min(TOKEN_BLOCK, seq)
    assert seq % tb == 0 and tb % (CHUNK * PREP_UNROLL) == 0
    n_chunks = tb // CHUNK
    wsum = jnp.asarray(_WSUM, jnp.bfloat16)
    masks = jnp.asarray(np.concatenate([_MASKS, _MASKS], axis=1), jnp.float32)
    tt = np.arange(tb)
    bdtriu = jnp.asarray(((tt[:, None] <= tt[None, :]) & ((tt[:, None] // CHUNK) == (tt[None, :] // CHUNK))),
                         jnp.bfloat16)
    consts = (lbl, win, wgate, wab, convw, hgnw, alogr, dtbr, alogc, dtbc, dnnw, wa, wb, wo,
              g1, b1, wsum, masks, bdtriu, bdtriu.T)
    f32 = jnp.float32
    blk = functools.partial(pltpu.VMEM, (tb, KW))
    bf16 = jnp.bfloat16
    per_head = functools.partial(pltpu.VMEM, (HEADS, tb, HEAD_DIM))
    scratch = [pltpu.VMEM((tb, D_MODEL), bf16), pltpu.VMEM((tb, D_MODEL), bf16)]
    scratch += [per_head(f32) for _ in range(9)]
    scratch += [blk(f32) for _ in range(3)]
    scratch += [pltpu.VMEM((tb, LANES), f32),
                pltpu.VMEM((n_chunks, 8, CHUNK), f32),
                pltpu.VMEM((HEADS, tb + CONV_PAD, 3 * HEAD_DIM), f32),
                pltpu.VMEM((HEADS, tb, 3 * HEAD_DIM), f32),
                blk(bf16), blk(bf16),
                pltpu.VMEM((n_chunks, 2 * CHUNK, KW), bf16),
                pltpu.VMEM((n_chunks * HEADS // 2, CHUNK, 2 * CHUNK), bf16),
                pltpu.VMEM((n_chunks * HEADS, HEAD_DIM, HEAD_DIM), f32),
                pltpu.VMEM((n_chunks, 1, KW), f32),
                pltpu.VMEM((n_chunks, 1, LANES), f32),
                blk(f32), blk(f32),
                pltpu.VMEM((tb, D_MODEL), f32),
                pltpu.VMEM((tb, D_MODEL), f32),
                pltpu.VMEM((HEADS, HEAD_DIM, HEAD_DIM), f32),
                pltpu.VMEM((HEADS, HEAD_DIM, HEAD_DIM), f32)]
    scratch += [pltpu.VMEM(w.shape, bf16) for w in (wa, wb, wo)]
    n_blocks = seq // tb

    def next_block(b, i):
        flat = jnp.minimum(b * n_blocks + i + 1, bsz * n_blocks - 1)
        return flat // n_blocks, flat % n_blocks, 0

    n_steps = bsz * n_blocks
    up_rows, down_rows = wup.shape[0] // n_steps, wdown.shape[0] // n_steps
    assert up_rows * n_steps == wup.shape[0] and down_rows * n_steps == wdown.shape[0]
    assert up_rows % BF16_ROWS == 0 and down_rows % BF16_ROWS == 0

    def step_rows(b, i):
        return b * n_blocks + i, 0

    return pl.pallas_call(
        _mixer_kernel,
        grid=(bsz, n_blocks),
        in_specs=[pl.BlockSpec((None, tb, D_MODEL), lambda b, i: (b, i, 0)),
                  pl.BlockSpec((None, tb, D_MODEL), next_block),
                  pl.BlockSpec((up_rows, wup.shape[1]), step_rows),
                  pl.BlockSpec((down_rows, wdown.shape[1]), step_rows)]
                 + [_resident(c.shape) for c in consts],
        out_specs=[pl.BlockSpec((None, tb, D_MODEL), lambda b, i: (b, i, 0)),
                   pl.BlockSpec((up_rows, wup.shape[1]), step_rows),
                   pl.BlockSpec((down_rows, wdown.shape[1]), step_rows)],
        out_shape=[jax.ShapeDtypeStruct(x.shape, f32),
                   jax.ShapeDtypeStruct(wup.shape, bf16),
                   jax.ShapeDtypeStruct(wdown.shape, bf16)],
        scratch_shapes=scratch,
        compiler_params=pltpu.CompilerParams(dimension_semantics=("arbitrary", "arbitrary"),
                                             vmem_limit_bytes=VMEM_LIMIT_BYTES),
        name="token_mixer",
    )(x, x, wup, wdown, *consts)


def _mlp_call(h, wup, wdown, g2, b2):
    m = h.shape[0]
    tm = min(MLP_BLOCK, m)
    assert m % tm == 0
    consts = (wup, wdown, g2, b2)
    return pl.pallas_call(
        _mlp_kernel,
        grid=(m // tm,),
        in_specs=[pl.BlockSpec((tm, D_MODEL), lambda i: (i, 0))] + [_resident(c.shape) for c in consts],
        out_specs=pl.BlockSpec((tm, D_MODEL), lambda i: (i, 0)),
        out_shape=jax.ShapeDtypeStruct(h.shape, jnp.float32),
        compiler_params=pltpu.CompilerParams(dimension_semantics=("arbitrary",),
                                             vmem_limit_bytes=VMEM_LIMIT_BYTES),
        name="relu2_mlp",
    )(h, *consts)


def kernel(x, hg_lb_logits, w_in, conv_w, hg_norm_w, dn_A_log, dn_dt_bias, dn_norm_w, w_branch_a, w_branch_b, w_o, ln1_g, ln1_b, w_up, w_down, ln2_g, ln2_b):
    bsz, seq, _ = x.shape
    f32 = jnp.float32
    bf16 = jnp.bfloat16
    o = _OFFS
    l = LAYER
    win = w_in[l].astype(bf16)
    wgate = win[:, o[8]:o[10]]
    wab = jnp.pad(win[:, o[6]:o[8]], ((0, 0), (0, LANES - 2 * HEADS)))
    zeros4 = jnp.zeros((HEADS,), f32)
    alog8 = jnp.concatenate([zeros4, dn_A_log[l].astype(f32)])
    dtb8 = jnp.concatenate([zeros4, dn_dt_bias[l].astype(f32)])
    alogr = jnp.pad(alog8, (0, LANES - 2 * HEADS)).reshape(1, LANES)
    dtbr = jnp.pad(dtb8, (0, LANES - 2 * HEADS)).reshape(1, LANES)
    h1, wup_bf, wdown_bf = _mixer_call(
        x, w_up[l], w_down[l], hg_lb_logits.astype(f32), win, wgate, wab, conv_w[l].astype(f32),
        hg_norm_w[l].reshape(1, KW), alogr, dtbr, alog8.reshape(8, 1), dtb8.reshape(8, 1),
        jnp.tile(dn_norm_w[l], HEADS).reshape(1, KW),
        w_branch_a[l].astype(f32), w_branch_b[l].astype(f32), w_o[l].astype(f32),
        ln1_g[l].reshape(1, D_MODEL), ln1_b[l].reshape(1, D_MODEL))
    out = _mlp_call(h1.reshape(bsz * seq, D_MODEL), wup_bf, wdown_bf,
                    ln2_g[l].reshape(1, D_MODEL), ln2_b[l].reshape(1, D_MODEL))
    return out.reshape(bsz, seq, D_MODEL)
```

```python
import functools

import numpy as np
import jax
import jax.numpy as jnp
from jax import lax
from jax.experimental import pallas as pl
from jax.experimental.pallas import tpu as pltpu

D_MODEL = 1024
HEADS = 4
HEAD_DIM = 128
KW = HEADS * HEAD_DIM
CHUNK = 64
CONV_W = 4
D_FF = 4 * D_MODEL
DEPTH = 1
LAYER = 0
ALPHA = (2 * DEPTH) ** 0.25
LN_EPS = 1e-5
RMS_EPS = 1e-6
L2_EPS = 1e-6
QK_SCALE = HEAD_DIM ** -0.5
NEG_LOG2E = -1.4426950408889634

TOKEN_BLOCK = 256
PREP_UNROLL = 4
GATE_TILE = 256
GATE_TILES_AFTER_PROJECTIONS = 4
MLP_BLOCK = 1024
MLP_TAIL_SPLIT = 4
FF_BLOCK = 1024
LANES = 128
SUBLANES = 8
BF16_ROWS = 16
CONV_PAD = 8
VMEM_LIMIT_BYTES = 56 * 1024 * 1024

_SIZES = (KW, KW, KW, KW, 3 * KW, KW, HEADS, HEADS, D_MODEL, D_MODEL)
_OFFS = np.concatenate([[0], np.cumsum(_SIZES)]).tolist()

COL_HG_QFI = _OFFS[0]
COL_HG_G = _OFFS[3]
COL_DN_QKV = _OFFS[4]
COL_DN_Z = _OFFS[5]
COL_BRANCH_GATES = _OFFS[8]

_LEVELS = (32, 16, 8, 4, 2, 1)


def _build_constants():
    t = np.arange(CHUNK)[:, None]
    r = np.arange(CHUNK)[None, :]
    wsum = np.concatenate([r <= t, r > t], 0).astype(np.float32)
    masks = [((t // (2 * m)) == (r // (2 * m))) & ((t % (2 * m)) >= m) & ((r % (2 * m)) < m) for m in _LEVELS]
    masks.append(t == r)
    masks = np.concatenate(masks, 0).astype(np.float32)
    return wsum, masks


_WSUM, _MASKS = _build_constants()


def _dot(a, b):
    return jnp.dot(a, b, preferred_element_type=jnp.float32)


def _dot_nt(a, b):
    return lax.dot_general(a, b, (((1,), (1,)), ((), ())), preferred_element_type=jnp.float32)


def _dot_tn(a, b):
    return lax.dot_general(a, b, (((0,), (0,)), ((), ())), preferred_element_type=jnp.float32)


def _bf(x):
    return x.astype(jnp.bfloat16)


def _split(x):
    hi = _bf(x)
    return hi, _bf(x - hi.astype(jnp.float32))


def _sigmoid(x):
    return 1.0 / (1.0 + jnp.exp2(x * NEG_LOG2E))


def _silu(x):
    return x * _sigmoid(x)


def _softplus(x):
    return jnp.maximum(x, 0.0) + jnp.log1p(jnp.exp(-jnp.abs(x)))


def _layer_norm(x, g, b):
    mu = jnp.mean(x, axis=-1, keepdims=True)
    xc = x - mu
    var = jnp.mean(xc * xc, axis=-1, keepdims=True)
    return xc * lax.rsqrt(var + LN_EPS) * g + b


def _mixer_kernel(x_ref, xnext_ref, wup32_ref, wdown32_ref, lbl_ref, win_ref, wab_ref, convw_ref, hgnw_ref,
                  alogr_ref, dtbr_ref, alogc_ref, dtbc_ref, dnnw_ref, wa_ref, wb_ref, wo_ref,
                  g1_ref, b1_ref, wsum_ref, masks_ref, bdtriu_ref, bdtril_ref, out_ref, wupb_ref, wdownb_ref,
                  xbuf, xnbuf, hq, hk, hv, hlf, hf, hb, dq, dk, dv, oa, ob, du, gbuf, gcrow, convbuf, rawhg, qin, kdec, wq, qkb,
                  kvinc, hdec, gl, sg, sz, sga, sgb, shg, sdn, wa16, wb16, wo16, wgate16):
    tb = x_ref.shape[0]
    n_chunks = tb // CHUNK

    wupb_ref[...] = _bf(wup32_ref[...])
    wdownb_ref[...] = _bf(wdown32_ref[...])

    @pl.when(pl.program_id(1) == 0)
    def _reset_carries():
        shg[...] = jnp.zeros_like(shg)
        sdn[...] = jnp.zeros_like(sdn)
        convbuf[:, 0:CONV_PAD, :] = jnp.zeros((HEADS, CONV_PAD, 3 * HEAD_DIM), jnp.float32)

    def project_head(c, xsrc=xbuf):
        cols = ([COL_DN_QKV + k * KW + c * HEAD_DIM for k in range(3)]
                + [COL_HG_QFI + k * KW + c * HEAD_DIM for k in range(3)])
        w_head = jnp.concatenate([win_ref[:, o:o + HEAD_DIM] for o in cols], axis=1)
        raw = _dot(xsrc[...], w_head)
        convbuf[c, CONV_PAD:CONV_PAD + tb, :] = raw[:, 0:3 * HEAD_DIM]
        rawhg[c] = raw[:, 3 * HEAD_DIM:6 * HEAD_DIM]

    def head_epilogue(c):
        cw = jnp.concatenate([convw_ref[:, k * KW + c * HEAD_DIM:k * KW + (c + 1) * HEAD_DIM]
                              for k in range(3)], axis=1)
        base = CONV_PAD - (CONV_W - 1)
        acc = convbuf[c, base:base + tb, :] * cw[0:1, :]
        for j in range(1, CONV_W):
            acc = acc + convbuf[c, base + j:base + j + tb, :] * cw[j:j + 1, :]
        convbuf[c, 0:CONV_PAD, :] = convbuf[c, tb:tb + CONV_PAD, :]
        qkv = _silu(acc)
        qh = qkv[:, 0:HEAD_DIM]
        kh = qkv[:, HEAD_DIM:2 * HEAD_DIM]
        dq[c] = qh * (lax.rsqrt(jnp.sum(qh * qh, axis=-1, keepdims=True) + L2_EPS) * QK_SCALE)
        dk[c] = kh * lax.rsqrt(jnp.sum(kh * kh, axis=-1, keepdims=True) + L2_EPS)
        dv[c] = qkv[:, 2 * HEAD_DIM:3 * HEAD_DIM]
        lg = lbl_ref[:, c * HEAD_DIM:(c + 1) * HEAD_DIM]
        e = jnp.exp(lg - jnp.max(lg, axis=0, keepdims=True))
        lb = e[0:1, :] / jnp.sum(e, axis=0, keepdims=True)
        f = lb + (1.0 - lb) * _sigmoid(rawhg[c, :, HEAD_DIM:2 * HEAD_DIM])
        hlf[c] = jnp.log(f)
        hf[c] = f
        hk[c] = 1.0 - f
        hq[c] = _silu(rawhg[c, :, 0:HEAD_DIM]) * QK_SCALE
        hv[c] = rawhg[c, :, 2 * HEAD_DIM:3 * HEAD_DIM]

    gate_tiles = ([(win_ref, COL_HG_G + t, sg, t, _silu) for t in range(0, KW, GATE_TILE)]
                  + [(win_ref, COL_DN_Z + t, sz, t, _silu) for t in range(0, KW, GATE_TILE)]
                  + [(wgate16, t, sga, t, _sigmoid) for t in range(0, D_MODEL, GATE_TILE)]
                  + [(wgate16, D_MODEL + t, sgb, t, _sigmoid) for t in range(0, D_MODEL, GATE_TILE)])
    gate_tiles_todo = list(range(len(gate_tiles)))

    def issue_gate_tiles(count):
        for _ in range(min(count, len(gate_tiles_todo))):
            w_ref, src, dst, off, act = gate_tiles[gate_tiles_todo.pop(0)]
            dst[:, off:off + GATE_TILE] = act(_dot(xbuf[...], w_ref[:, src:src + GATE_TILE]))

    @pl.when((pl.program_id(0) == 0) & (pl.program_id(1) == 0))
    def _first_projection():
        wgate16[...] = win_ref[:, COL_BRANCH_GATES:COL_BRANCH_GATES + 2 * D_MODEL]
        wa16[...] = _bf(wa_ref[...])
        wb16[...] = _bf(wb_ref[...])
        wo16[...] = _bf(wo_ref[...])
        xnbuf[...] = _bf(x_ref[...])
        project_head(0, xnbuf)

    xbuf[...] = xnbuf[...]

    def cumulate_forget(h0):
        lf_hi, lf_lo = _split(jnp.concatenate([hlf[h0], hlf[h0 + 1]], axis=1))
        bc = _dot(bdtril_ref[...], lf_hi) + _dot(bdtril_ref[...], lf_lo)
        hb[h0] = bc[:, 0:HEAD_DIM]
        hb[h0 + 1] = bc[:, HEAD_DIM:2 * HEAD_DIM]

    for c in range(1, HEADS):
        head_epilogue(c - 1)
        project_head(c)
        if c % 2 == 0:
            cumulate_forget(c - 2)
    head_epilogue(HEADS - 1)
    issue_gate_tiles(GATE_TILES_AFTER_PROJECTIONS)
    cumulate_forget(HEADS - 2)

    pab = _dot(xbuf[...], wab_ref[...])
    g_col = -jnp.exp(alogr_ref[...]) * _softplus(pab + dtbr_ref[...])
    lane = lax.broadcasted_iota(jnp.int32, pab.shape, 1)
    gbuf[...] = jnp.where(lane < HEADS, _sigmoid(pab), g_col)
    pabt = jnp.transpose(pab)[0:2 * HEADS, :]
    g_row = -jnp.exp(alogc_ref[...]) * _softplus(pabt + dtbc_ref[...])
    gr_hi, gr_lo = _split(g_row)
    gc_row = _dot(gr_hi, bdtriu_ref[...]) + _dot(gr_lo, bdtriu_ref[...])
    for c in range(n_chunks):
        gcrow[c] = gc_row[:, c * CHUNK:(c + 1) * CHUNK]

    ri = lax.broadcasted_iota(jnp.int32, (CHUNK, CHUNK), 0)
    ci = lax.broadcasted_iota(jnp.int32, (CHUNK, CHUNK), 1)
    causal = jnp.where(ri >= ci, 1.0, 0.0)
    strict = jnp.where(ri > ci, 1.0, 0.0)
    trow = lax.broadcasted_iota(jnp.int32, (CHUNK, HEAD_DIM), 0)

    n_lev = len(_LEVELS)
    heads = range(HEADS)
    hsl = [slice(h * HEAD_DIM, (h + 1) * HEAD_DIM) for h in heads]

    def level_mask(l):
        return masks_ref[l * CHUNK:(l + 1) * CHUNK, 0:CHUNK]

    def level_mask2(l):
        return masks_ref[l * CHUNK:(l + 1) * CHUNK, :]

    units = [(u, h) for u in range(PREP_UNROLL) for h in heads]
    n_units = range(len(units))

    def prepare_chunks(j):
        cs = [j * PREP_UNROLL + u for u in range(PREP_UNROLL)]
        rows = [slice(c * CHUNK, (c + 1) * CHUNK) for c in cs]
        wsum = wsum_ref[...]
        gb, eg, gcr = [], [], []
        for u in range(PREP_UNROLL):
            gb.append(gbuf[rows[u], :])
            g_hi, g_lo = _split(gb[u])
            eg.append(_dot(wsum, g_hi) + _dot(wsum, g_lo))
            gcr.append(gcrow[cs[u]])

        def ld(ref, i):
            u, h = units[i]
            return ref[h, rows[u], :]

        dkk = [ld(dk, i) for i in n_units]
        dkb = [_bf(k) for k in dkk]
        beta = [gb[u][:, h:h + 1] for u, h in units]
        gcol = [eg[u][0:CHUNK, HEADS + h:HEADS + h + 1] for u, h in units]
        decay = [causal * jnp.exp(jnp.minimum(gcol[i] - gcr[u][HEADS + h:HEADS + h + 1, :], 0.0))
                 for i, (u, h) in enumerate(units)]
        kbeta = [dkk[i] * beta[i] for i in n_units]
        gram = [_dot_nt(jnp.concatenate([_bf(kbeta[i]), _bf(ld(dq, i))], axis=0), dkb[i]) for i in n_units]
        lmat = [gram[i][0:CHUNK] * (strict * decay[i]) for i in n_units]
        qk = [gram[i][CHUNK:2 * CHUNK] * decay[i] for i in n_units]
        for i, (u, h) in enumerate(units):
            if h % 2 == 0:
                qkb[cs[u] * (HEADS // 2) + h // 2] = _bf(jnp.concatenate([qk[i], qk[i + 1]], axis=1))

        hqv = [ld(hq, i) for i in n_units]
        hkv = [ld(hk, i) for i in n_units]
        hqb = [_bf(q) for q in hqv]
        hkb = [_bf(k) for k in hkv]
        sc = [level_mask(n_lev) * _dot_nt(hqb[i], hkb[i]) for i in n_units]
        pairs = [(i, i + 1) for i in range(0, len(units), 2)]
        lane2 = lax.broadcasted_iota(jnp.int32, (CHUNK, 2 * CHUNK), 1)
        first_f = jnp.where(lane2 < CHUNK, 1.0, 0.0)
        second_f = 1.0 - first_f

        def block_diag(m2, lo, hi):
            return jnp.concatenate([_bf(m2 * lo), _bf(m2 * hi)], axis=0)

        lmat2 = [jnp.concatenate([lmat[a], lmat[b]], axis=1) for a, b in pairs]
        lbd = [block_diag(m2, first_f, second_f) for m2 in lmat2]
        nmat2 = [-(level_mask2(n_lev - 1) * m2) for m2 in lmat2]
        first_b, second_b = _bf(first_f), _bf(second_f)

        hbv = [ld(hb, i) for i in n_units]

        def level_factor(i, l):
            m = _LEVELS[l]
            if m >= SUBLANES // 2:
                ref = jnp.concatenate(
                    [jnp.broadcast_to(hbv[i][r:r + 1, :], (SUBLANES, HEAD_DIM))
                     for r in [(g * SUBLANES // (2 * m)) * (2 * m) + m - 1 for g in range(CHUNK // SUBLANES)]],
                    axis=0)
                return jnp.exp2(jnp.abs(hbv[i] - ref) * NEG_LOG2E)
            f = ld(hf, i)
            if m == 1:
                return jnp.where(trow % 2 == 1, f, 1.0)
            f8 = f.reshape(CHUNK // SUBLANES, SUBLANES, HEAD_DIM)
            f_prev = pltpu.roll(f8, 1, 1).reshape(CHUNK, HEAD_DIM)
            f_next = pltpu.roll(f8, SUBLANES - 1, 1).reshape(CHUNK, HEAD_DIM)
            return jnp.where(trow % 4 == 0, f_next,
                             jnp.where(trow % 4 == 1, 1.0, jnp.where(trow % 4 == 2, f, f_prev * f)))

        def score_level(l):
            m = _LEVELS[l]
            for i in n_units:
                zb = _bf(level_factor(i, l))
                d = _dot_nt(hqb[i] * zb, hkb[i] * zb)
                if m < SUBLANES:
                    sc[i] = sc[i] + level_mask(l) * d
                    continue
                groups = []
                for g in range(CHUNK // SUBLANES):
                    r = slice(g * SUBLANES, (g + 1) * SUBLANES)
                    is_query = (g * SUBLANES) % (2 * m) >= m
                    groups.append(sc[i][r] + level_mask(l)[r] * d[r] if is_query else sc[i][r])
                sc[i] = jnp.concatenate(groups, axis=0)

        score_level(n_lev - 1)
        n_pairs = range(len(pairs))
        for l in range(n_lev - 2, -1, -1):
            nb = [_bf(nmat2[p]) for p in n_pairs]
            xm = [level_mask2(l) * (lmat2[p] + _dot(nb[p], lbd[p])) for p in n_pairs]
            score_level(l)
            for p in n_pairs:
                nbd = jnp.concatenate([nb[p] * first_b, nb[p] * second_b], axis=0)
                nmat2[p] = nmat2[p] - (xm[p] + _dot(_bf(xm[p]), nbd))

        hvb = [_bf(ld(hv, i)) for i in n_units]
        for i, (u, h) in enumerate(units):
            oa[rows[u], hsl[h]] = _dot(_bf(sc[i]), hvb[i])
        for i, (u, h) in enumerate(units):
            kd = hkv[i] * jnp.exp(hbv[i][CHUNK - 1:CHUNK, :] - hbv[i])
            kvinc[cs[u] * HEADS + h] = _dot_tn(hvb[i], _bf(kd))
        for i, (u, h) in enumerate(units):
            qin[rows[u], hsl[h]] = _bf(hqv[i] * jnp.exp(hbv[i]))
            hdec[cs[u], :, hsl[h]] = jnp.exp(hbv[i][CHUNK - 1:CHUNK, :])

        for i, (u, h) in enumerate(units):
            eg_col = jnp.exp(gcol[i])
            rhs = jnp.concatenate([ld(dv, i) * beta[i], kbeta[i] * eg_col], axis=1)
            zero_rhs = jnp.zeros(rhs.shape, jnp.bfloat16)
            rhs_pad = jnp.concatenate([_bf(rhs), zero_rhs] if i % 2 == 0 else [zero_rhs, _bf(rhs)], axis=0)
            uw = rhs + _dot(_bf(nmat2[i // 2]), rhs_pad)
            du[rows[u], hsl[h]] = uw[:, 0:HEAD_DIM]
            wq[cs[u], 0:CHUNK, hsl[h]] = _bf(uw[:, HEAD_DIM:2 * HEAD_DIM])
            wq[cs[u], CHUNK:2 * CHUNK, hsl[h]] = _bf(ld(dq, i) * eg_col)
            kdec[rows[u], hsl[h]] = _bf(dkk[i] * jnp.exp(eg[u][CHUNK:2 * CHUNK, HEADS + h:HEADS + h + 1]))
        for u in range(PREP_UNROLL):
            gl[cs[u]] = jnp.exp(eg[u][CHUNK - 1:CHUNK, :])

    for j in range(n_chunks // PREP_UNROLL):
        prepare_chunks(j)

    for c in range(n_chunks):
        rows = slice(c * CHUNK, (c + 1) * CHUNK)
        s = [sdn[h] for h in heads]
        wqc = wq[c]
        zero_s = jnp.zeros((HEAD_DIM, HEAD_DIM), jnp.bfloat16)
        ws = []
        for h in range(0, HEADS, 2):
            s_bd = jnp.concatenate([jnp.concatenate([_bf(s[h]), zero_s], axis=1),
                                    jnp.concatenate([zero_s, _bf(s[h + 1])], axis=1)], axis=0)
            ws2 = _dot(wqc[:, h * HEAD_DIM:(h + 2) * HEAD_DIM], s_bd)
            ws += [ws2[:, 0:HEAD_DIM], ws2[:, HEAD_DIM:2 * HEAD_DIM]]
        st = [shg[h] for h in heads]
        dec = hdec[c]
        for h in heads:
            oa[rows, hsl[h]] = oa[rows, hsl[h]] + _dot_nt(qin[rows, hsl[h]], _bf(st[h]))
            shg[h] = st[h] * dec[:, hsl[h]] + kvinc[c * HEADS + h]
        issue_gate_tiles(-(-len(gate_tiles_todo) // (2 * (n_chunks - c))))
        vnb = [_bf(du[rows, hsl[h]] - ws[h][0:CHUNK]) for h in heads]
        glc = gl[c]
        zero_v = jnp.zeros((CHUNK, HEAD_DIM), jnp.bfloat16)
        for h in range(0, HEADS, 2):
            v_bd = jnp.concatenate([jnp.concatenate([vnb[h], zero_v], axis=1),
                                    jnp.concatenate([zero_v, vnb[h + 1]], axis=1)], axis=0)
            intra = _dot(qkb[c * (HEADS // 2) + h // 2], v_bd)
            ob[rows, hsl[h]] = ws[h][CHUNK:2 * CHUNK] + intra[:, 0:HEAD_DIM]
            ob[rows, hsl[h + 1]] = ws[h + 1][CHUNK:2 * CHUNK] + intra[:, HEAD_DIM:2 * HEAD_DIM]
        for h in heads:
            sdn[h] = s[h] * glc[:, HEADS + h:HEADS + h + 1] + _dot_tn(kdec[rows, hsl[h]], vnb[h])
        issue_gate_tiles(-(-len(gate_tiles_todo) // (2 * (n_chunks - c) - 1)))

    def gated_norm(o_ref, w_row, gate):
        parts = []
        for h in range(HEADS):
            oh = o_ref[:, h * HEAD_DIM:(h + 1) * HEAD_DIM]
            parts.append(oh * lax.rsqrt(jnp.mean(oh * oh, axis=-1, keepdims=True) + RMS_EPS))
        return jnp.concatenate(parts, axis=1) * w_row * gate

    na = gated_norm(oa, hgnw_ref[...], sg[...])
    nb_ = gated_norm(ob, dnnw_ref[...], sz[...])
    merged = sga[...] * _dot(_bf(na), wa16[...]) + sgb[...] * _dot(_bf(nb_), wb16[...])
    mix = _dot(_bf(merged), wo16[...])
    out_ref[...] = _layer_norm(ALPHA * x_ref[...] + mix, g1_ref[...], b1_ref[...])

    xnbuf[...] = _bf(xnext_ref[...])
    project_head(0, xnbuf)


def _mlp_kernel(h_ref, wup_ref, wdown_ref, g2_ref, b2_ref, out_ref):
    tm = h_ref.shape[0]
    hb = _bf(h_ref[...])
    n_ff = D_FF // FF_BLOCK
    assert n_ff >= 2 and tm % MLP_TAIL_SPLIT == 0
    acc = None
    for j in range(n_ff - 1):
        up = _dot(hb, wup_ref[:, j * FF_BLOCK:(j + 1) * FF_BLOCK])
        act = jnp.square(jnp.maximum(up, 0.0))
        part = _dot(_bf(act), wdown_ref[j * FF_BLOCK:(j + 1) * FF_BLOCK, :])
        acc = part if acc is None else acc + part
    up = _dot(hb, wup_ref[:, (n_ff - 1) * FF_BLOCK:n_ff * FF_BLOCK])
    actb = _bf(jnp.square(jnp.maximum(up, 0.0)))
    for r in range(MLP_TAIL_SPLIT):
        rows = slice(r * tm // MLP_TAIL_SPLIT, (r + 1) * tm // MLP_TAIL_SPLIT)
        part = _dot(actb[rows], wdown_ref[(n_ff - 1) * FF_BLOCK:n_ff * FF_BLOCK, :])
        out_ref[rows, :] = _layer_norm(ALPHA * h_ref[rows, :] + acc[rows] + part, g2_ref[...], b2_ref[...])


def _resident(shape):
    nd = len(shape)
    return pl.BlockSpec(shape, lambda *_: (0,) * nd, pipeline_mode=pl.Buffered(1))


def _mixer_call(x, wup, wdown, lbl, win, wab, convw, hgnw, alogr, dtbr, alogc, dtbc, dnnw,
                wa, wb, wo, g1, b1):
    bsz, seq, _ = x.shape
    tb = min(TOKEN_BLOCK, seq)
    assert seq % tb == 0 and tb % (CHUNK * PREP_UNROLL) == 0
    n_chunks = tb // CHUNK
    wsum = jnp.asarray(_WSUM, jnp.bfloat16)
    masks = jnp.asarray(np.concatenate([_MASKS, _MASKS], axis=1), jnp.float32)
    tt = np.arange(tb)
    bdtriu = jnp.asarray(((tt[:, None] <= tt[None, :]) & ((tt[:, None] // CHUNK) == (tt[None, :] // CHUNK))),
                         jnp.bfloat16)
    consts = (lbl, win, wab, convw, hgnw, alogr, dtbr, alogc, dtbc, dnnw, wa, wb, wo,
              g1, b1, wsum, masks, bdtriu, bdtriu.T)
    f32 = jnp.float32
    blk = functools.partial(pltpu.VMEM, (tb, KW))
    bf16 = jnp.bfloat16
    per_head = functools.partial(pltpu.VMEM, (HEADS, tb, HEAD_DIM))
    scratch = [pltpu.VMEM((tb, D_MODEL), bf16), pltpu.VMEM((tb, D_MODEL), bf16)]
    scratch += [per_head(f32) for _ in range(9)]
    scratch += [blk(f32) for _ in range(3)]
    scratch += [pltpu.VMEM((tb, LANES), f32),
                pltpu.VMEM((n_chunks, 8, CHUNK), f32),
                pltpu.VMEM((HEADS, tb + CONV_PAD, 3 * HEAD_DIM), f32),
                pltpu.VMEM((HEADS, tb, 3 * HEAD_DIM), f32),
                blk(bf16), blk(bf16),
                pltpu.VMEM((n_chunks, 2 * CHUNK, KW), bf16),
                pltpu.VMEM((n_chunks * HEADS // 2, CHUNK, 2 * CHUNK), bf16),
                pltpu.VMEM((n_chunks * HEADS, HEAD_DIM, HEAD_DIM), f32),
                pltpu.VMEM((n_chunks, 1, KW), f32),
                pltpu.VMEM((n_chunks, 1, LANES), f32),
                blk(f32), blk(f32),
                pltpu.VMEM((tb, D_MODEL), f32),
                pltpu.VMEM((tb, D_MODEL), f32),
                pltpu.VMEM((HEADS, HEAD_DIM, HEAD_DIM), f32),
                pltpu.VMEM((HEADS, HEAD_DIM, HEAD_DIM), f32)]
    scratch += [pltpu.VMEM(w.shape, bf16) for w in (wa, wb, wo)]
    scratch += [pltpu.VMEM((D_MODEL, 2 * D_MODEL), bf16)]
    n_blocks = seq // tb

    def next_block(b, i):
        flat = jnp.minimum(b * n_blocks + i + 1, bsz * n_blocks - 1)
        return flat // n_blocks, flat % n_blocks, 0

    n_steps = bsz * n_blocks
    up_rows, down_rows = wup.shape[0] // n_steps, wdown.shape[0] // n_steps
    assert up_rows * n_steps == wup.shape[0] and down_rows * n_steps == wdown.shape[0]
    assert up_rows % BF16_ROWS == 0 and down_rows % BF16_ROWS == 0

    def step_rows(b, i):
        return b * n_blocks + i, 0

    return pl.pallas_call(
        _mixer_kernel,
        grid=(bsz, n_blocks),
        in_specs=[pl.BlockSpec((None, tb, D_MODEL), lambda b, i: (b, i, 0)),
                  pl.BlockSpec((None, tb, D_MODEL), next_block),
                  pl.BlockSpec((up_rows, wup.shape[1]), step_rows),
                  pl.BlockSpec((down_rows, wdown.shape[1]), step_rows)]
                 + [_resident(c.shape) for c in consts],
        out_specs=[pl.BlockSpec((None, tb, D_MODEL), lambda b, i: (b, i, 0)),
                   pl.BlockSpec((up_rows, wup.shape[1]), step_rows),
                   pl.BlockSpec((down_rows, wdown.shape[1]), step_rows)],
        out_shape=[jax.ShapeDtypeStruct(x.shape, f32),
                   jax.ShapeDtypeStruct(wup.shape, bf16),
                   jax.ShapeDtypeStruct(wdown.shape, bf16)],
        scratch_shapes=scratch,
        compiler_params=pltpu.CompilerParams(dimension_semantics=("arbitrary", "arbitrary"),
                                             vmem_limit_bytes=VMEM_LIMIT_BYTES),
        name="token_mixer",
    )(x, x, wup, wdown, *consts)


def _mlp_call(h, wup, wdown, g2, b2):
    m = h.shape[0]
    tm = min(MLP_BLOCK, m)
    assert m % tm == 0
    consts = (wup, wdown, g2, b2)
    return pl.pallas_call(
        _mlp_kernel,
        grid=(m // tm,),
        in_specs=[pl.BlockSpec((tm, D_MODEL), lambda i: (i, 0))] + [_resident(c.shape) for c in consts],
        out_specs=pl.BlockSpec((tm, D_MODEL), lambda i: (i, 0)),
        out_shape=jax.ShapeDtypeStruct(h.shape, jnp.float32),
        compiler_params=pltpu.CompilerParams(dimension_semantics=("arbitrary",),
                                             vmem_limit_bytes=VMEM_LIMIT_BYTES),
        name="relu2_mlp",
    )(h, *consts)


def kernel(x, hg_lb_logits, w_in, conv_w, hg_norm_w, dn_A_log, dn_dt_bias, dn_norm_w, w_branch_a, w_branch_b, w_o, ln1_g, ln1_b, w_up, w_down, ln2_g, ln2_b):
    bsz, seq, _ = x.shape
    f32 = jnp.float32
    bf16 = jnp.bfloat16
    o = _OFFS
    l = LAYER
    win = w_in[l].astype(bf16)
    wab = jnp.pad(win[:, o[6]:o[8]], ((0, 0), (0, LANES - 2 * HEADS)))
    zeros4 = jnp.zeros((HEADS,), f32)
    alog8 = jnp.concatenate([zeros4, dn_A_log[l].astype(f32)])
    dtb8 = jnp.concatenate([zeros4, dn_dt_bias[l].astype(f32)])
    alogr = jnp.pad(alog8, (0, LANES - 2 * HEADS)).reshape(1, LANES)
    dtbr = jnp.pad(dtb8, (0, LANES - 2 * HEADS)).reshape(1, LANES)
    h1, wup_bf, wdown_bf = _mixer_call(
        x, w_up[l], w_down[l], hg_lb_logits.astype(f32), win, wab, conv_w[l].astype(f32),
        hg_norm_w[l].reshape(1, KW), alogr, dtbr, alog8.reshape(8, 1), dtb8.reshape(8, 1),
        jnp.tile(dn_norm_w[l], HEADS).reshape(1, KW),
        w_branch_a[l].astype(f32), w_branch_b[l].astype(f32), w_o[l].astype(f32),
        ln1_g[l].reshape(1, D_MODEL), ln1_b[l].reshape(1, D_MODEL))
    out = _mlp_call(h1.reshape(bsz * seq, D_MODEL), wup_bf, wdown_bf,
                    ln2_g[l].reshape(1, D_MODEL), ln2_b[l].reshape(1, D_MODEL))
    return out.reshape(bsz, seq, D_MODEL)
```
